```python
import math
import jax, jax.numpy as jnp
from jax import lax
import numpy as np

D_MODEL = 2048
BATCH = 8
SEQ = 4096
DEPTH = 4

HEAD_DIM = 128
N_QK_HEADS_A = 8
N_V_HEADS_A = 16
QK_WIDTH_A = N_QK_HEADS_A * HEAD_DIM
V_WIDTH_A = N_V_HEADS_A * HEAD_DIM
CONV_A = 4
CHUNK_A = 64
N_GROUPS_B = 8
GROUP_DIM_B = 128
WIDTH_B = N_GROUPS_B * GROUP_DIM_B
CHUNK_B = 128
N_BRANCH = 2
D_FF = 5632
CONV_FFN = 3
PLE_DIM = 256
EPS = 1e-6

SPLIT_SIZES = (QK_WIDTH_A, QK_WIDTH_A, V_WIDTH_A, V_WIDTH_A, N_V_HEADS_A, N_V_HEADS_A,
               WIDTH_B, WIDTH_B, N_BRANCH * D_MODEL)
N_IN = sum(SPLIT_SIZES)

kernel_name = "hybrid_deltanet_sgu_convglu_ple"


def rms_norm(x, gain):
    xf = x.astype(jnp.float32)
    y = xf * lax.rsqrt(jnp.mean(xf * xf, axis=-1, keepdims=True) + EPS)
    return (y * gain.astype(jnp.float32)).astype(x.dtype)


def l2_normalize(x):
    return x * lax.rsqrt(jnp.sum(x * x, axis=-1, keepdims=True) + EPS)


def causal_depthwise_conv(x, w):
    k = w.shape[0]
    return lax.conv_general_dilated(
        x, w[:, None, :].astype(x.dtype), window_strides=(1,), padding=((k - 1, 0),),
        dimension_numbers=("NWC", "WIO", "NWC"), feature_group_count=x.shape[-1])


def gated_delta_rule(q, k, v, beta, g):
    b, s, h, dk = q.shape
    dv = v.shape[-1]
    n = s // CHUNK_A

    def chunks(t):
        t = t.reshape((b, n, CHUNK_A, h) + t.shape[3:])
        return jnp.moveaxis(t, (1, 3), (0, 2))

    qc, kc, vc = chunks(q), chunks(k), chunks(v)
    bc, gc = chunks(beta), chunks(g)
    gam = jnp.cumsum(gc, axis=-1)
    causal = jnp.tril(jnp.ones((CHUNK_A, CHUNK_A), dtype=bool))
    strict = jnp.tril(jnp.ones((CHUNK_A, CHUNK_A), dtype=bool), -1)
    decay = jnp.exp(jnp.where(causal, gam[..., :, None] - gam[..., None, :], -jnp.inf))
    kb = kc * bc[..., None]
    a_mat = jnp.where(strict, jnp.einsum("nbhcd,nbhsd->nbhcs", kb, kc) * decay, 0.0)
    eye = jnp.eye(CHUNK_A, dtype=jnp.float32)
    rhs = jnp.concatenate([vc * bc[..., None], kb * jnp.exp(gam)[..., None]], axis=-1)
    sol = lax.linalg.triangular_solve(a_mat + eye, rhs, left_side=True, lower=True,
                                      unit_diagonal=True)
    u, w = sol[..., :dv], sol[..., dv:]
    attn = jnp.einsum("nbhcd,nbhsd->nbhcs", qc, kc) * decay
    q_dec = qc * jnp.exp(gam)[..., None]
    k_dec = kc * jnp.exp(gam[..., -1:] - gam)[..., None]
    chunk_decay = jnp.exp(gam[..., -1])

    def step(state, inp):
        q_i, k_i, u_i, w_i, a_i, d_i = inp
        v_new = u_i - jnp.einsum("bhcd,bhde->bhce", w_i, state)
        o = jnp.einsum("bhcd,bhde->bhce", q_i, state) + jnp.einsum("bhcs,bhse->bhce", a_i, v_new)
        state = state * d_i[..., None, None] + jnp.einsum("bhcd,bhce->bhde", k_i, v_new)
        return state, o

    s0 = jnp.zeros((b, h, dk, dv), jnp.float32)
    _, o = lax.scan(step, s0, (q_dec, k_dec, u, w, attn, chunk_decay))
    return jnp.moveaxis(o, (0, 2), (1, 3)).reshape(b, s, h, dv)


def delta_mixer(q, k, v, z, beta_logit, a_logit, conv_w, a_log, dt_bias, head_gain):
    b, s, _ = q.shape
    f32 = jnp.float32
    qkv = jax.nn.silu(causal_depthwise_conv(jnp.concatenate([q, k, v], axis=-1), conv_w))
    q, k, v = jnp.split(qkv, [QK_WIDTH_A, 2 * QK_WIDTH_A], axis=-1)
    rep = N_V_HEADS_A // N_QK_HEADS_A
    q = l2_normalize(q.reshape(b, s, N_QK_HEADS_A, HEAD_DIM).astype(f32)) * (HEAD_DIM ** -0.5)
    k = l2_normalize(k.reshape(b, s, N_QK_HEADS_A, HEAD_DIM).astype(f32))
    q = jnp.repeat(q, rep, axis=2)
    k = jnp.repeat(k, rep, axis=2)
    v = v.reshape(b, s, N_V_HEADS_A, HEAD_DIM).astype(f32)
    beta = jax.nn.sigmoid(beta_logit.astype(f32))
    g = -jnp.exp(a_log.astype(f32)) * jax.nn.softplus(a_logit.astype(f32) + dt_bias.astype(f32))
    o = gated_delta_rule(q, k, v, beta, g)
    o = rms_norm(o, head_gain) * jax.nn.silu(z.reshape(b, s, N_V_HEADS_A, HEAD_DIM).astype(f32))
    return o.reshape(b, s, V_WIDTH_A).astype(z.dtype)


def spatial_gating_mixer(u, v, norm_gain, w_s, b_s):
    b, s, _ = u.shape
    n = s // CHUNK_B
    u = jax.nn.gelu(u)
    v = rms_norm(jax.nn.gelu(v), norm_gain)
    causal = jnp.tril(jnp.ones((CHUNK_B, CHUNK_B), dtype=bool))
    w = jnp.where(causal, w_s, 0.0).astype(v.dtype)
    vc = v.reshape(b, n, CHUNK_B, N_GROUPS_B, GROUP_DIM_B)
    mixed = jnp.einsum("gts,bnsgc->bntgc", w, vc) + b_s.T[None, None, :, :, None].astype(v.dtype)
    return u * mixed.reshape(b, s, WIDTH_B)


def conv_glu_ffn(h, w_up, conv_w, conv_b, w_down):
    up = causal_depthwise_conv(h @ w_up, conv_w) + conv_b
    gate, val = jnp.split(up, 2, axis=-1)
    return (jax.nn.silu(gate) * val) @ w_down


def _fwd_setup_inputs(seed: int = 0) -> dict:
    key = jax.random.key(seed)
    ks = jax.random.split(key, 24)
    f32 = jnp.float32

    def nrm(k, shape, scale):
        return jax.random.normal(k, shape, f32) * scale

    def gain(k, shape):
        return 1.0 + 0.02 * jax.random.normal(k, shape, f32)

    dt = jnp.exp(jax.random.uniform(ks[5], (DEPTH, N_V_HEADS_A), f32, math.log(1e-3), math.log(1e-1)))
    return {
        "x": nrm(ks[0], (BATCH, SEQ, D_MODEL), 1.0),
        "p": nrm(ks[1], (DEPTH, BATCH, SEQ, PLE_DIM), 1.0),
        "norm_mix": gain(ks[2], (DEPTH, D_MODEL)),
        "w_in": nrm(ks[3], (DEPTH, D_MODEL, N_IN), D_MODEL ** -0.5),
        "conv_qkv": nrm(ks[6], (DEPTH, CONV_A, 2 * QK_WIDTH_A + V_WIDTH_A), CONV_A ** -0.5),
        "a_log": jnp.log(jax.random.uniform(ks[4], (DEPTH, N_V_HEADS_A), f32, 1.0, 16.0)),
        "dt_bias": dt + jnp.log(-jnp.expm1(-dt)),
        "head_norm": gain(ks[7], (DEPTH, HEAD_DIM)),
        "sgu_norm": gain(ks[8], (DEPTH, WIDTH_B)),
        "w_spatial": nrm(ks[9], (DEPTH, N_GROUPS_B, CHUNK_B, CHUNK_B), CHUNK_B ** -0.5),
        "b_spatial": 1.0 + nrm(ks[10], (DEPTH, N_GROUPS_B, CHUNK_B), 0.1),
        "w_branch_a": nrm(ks[11], (DEPTH, V_WIDTH_A, D_MODEL), V_WIDTH_A ** -0.5),
        "w_branch_b": nrm(ks[12], (DEPTH, WIDTH_B, D_MODEL), WIDTH_B ** -0.5),
        "w_out": nrm(ks[13], (DEPTH, D_MODEL, D_MODEL), D_MODEL ** -0.5),
        "norm_ffn": gain(ks[14], (DEPTH, D_MODEL)),
        "w_ffn_up": nrm(ks[15], (DEPTH, D_MODEL, 2 * D_FF), D_MODEL ** -0.5),
        "conv_ffn": nrm(ks[16], (DEPTH, CONV_FFN, 2 * D_FF), CONV_FFN ** -0.5),
        "b_conv_ffn": nrm(ks[17], (DEPTH, 2 * D_FF), 0.02),
        "w_ffn_down": nrm(ks[18], (DEPTH, D_FF, D_MODEL), D_FF ** -0.5),
        "norm_ple": gain(ks[19], (DEPTH, D_MODEL)),
        "w_ple_gate": nrm(ks[20], (DEPTH, D_MODEL, D_MODEL), D_MODEL ** -0.5),
        "w_ple_proj": nrm(ks[21], (DEPTH, PLE_DIM, D_MODEL), PLE_DIM ** -0.5),
        "norm_final": gain(ks[22], (D_MODEL,)),
    }


def _fwd_reference(x, p, norm_mix, w_in, conv_qkv, a_log, dt_bias, head_norm, sgu_norm, w_spatial,
              b_spatial, w_branch_a, w_branch_b, w_out, norm_ffn, w_ffn_up, conv_ffn, b_conv_ffn,
              w_ffn_down, norm_ple, w_ple_gate, w_ple_proj, norm_final):
    split_idx = np.cumsum(SPLIT_SIZES)[:-1].tolist()
    for i in range(DEPTH):
        h = rms_norm(x, norm_mix[i])
        proj = h @ w_in[i]
        q, k, v, z, b_logit, a_logit, u_b, v_b, gates = jnp.split(proj, split_idx, axis=-1)
        y_a = delta_mixer(q, k, v, z, b_logit, a_logit, conv_qkv[i], a_log[i], dt_bias[i],
                          head_norm[i]) @ w_branch_a[i]
        y_b = spatial_gating_mixer(u_b, v_b, sgu_norm[i], w_spatial[i], b_spatial[i]) @ w_branch_b[i]
        g_a, g_b = jnp.split(gates, N_BRANCH, axis=-1)
        merged = jax.nn.sigmoid(g_a) * y_a + jax.nn.sigmoid(g_b) * y_b
        x = x + merged @ w_out[i]
        h = rms_norm(x, norm_ffn[i])
        x = x + conv_glu_ffn(h, w_ffn_up[i], conv_ffn[i], b_conv_ffn[i], w_ffn_down[i])
        h = rms_norm(x, norm_ple[i])
        x = x + jax.nn.sigmoid(h @ w_ple_gate[i]) * (p[i] @ w_ple_proj[i])
    return rms_norm(x, norm_final)


import jax as _jax
import jax.numpy as _jnp

TWIN_FORMAT = 'train_step'
FWD_PARAMS = ['x', 'p', 'norm_mix', 'w_in', 'conv_qkv', 'a_log', 'dt_bias', 'head_norm', 'sgu_norm', 'w_spatial', 'b_spatial', 'w_branch_a', 'w_branch_b', 'w_out', 'norm_ffn', 'w_ffn_up', 'conv_ffn', 'b_conv_ffn', 'w_ffn_down', 'norm_ple', 'w_ple_gate', 'w_ple_proj', 'norm_final']
TWIN_WEIGHTS = ['norm_mix', 'w_in', 'conv_qkv', 'a_log', 'dt_bias', 'head_norm', 'sgu_norm', 'w_spatial', 'b_spatial', 'w_branch_a', 'w_branch_b', 'w_out', 'norm_ffn', 'w_ffn_up', 'conv_ffn', 'b_conv_ffn', 'w_ffn_down', 'norm_ple', 'w_ple_gate', 'w_ple_proj', 'norm_final']
TWIN_DIFF_INPUT = 'x'
TWIN_INPUTS = ['x', 'p', 'norm_mix', 'w_in', 'conv_qkv', 'a_log', 'dt_bias', 'head_norm', 'sgu_norm', 'w_spatial', 'b_spatial', 'w_branch_a', 'w_branch_b', 'w_out', 'norm_ffn', 'w_ffn_up', 'conv_ffn', 'b_conv_ffn', 'w_ffn_down', 'norm_ple', 'w_ple_gate', 'w_ple_proj', 'norm_final', 'loss_target', 'm_norm_mix', 'm_w_in', 'm_conv_qkv', 'm_a_log', 'm_dt_bias', 'm_head_norm', 'm_sgu_norm', 'm_w_spatial', 'm_b_spatial', 'm_w_branch_a', 'm_w_branch_b', 'm_w_out', 'm_norm_ffn', 'm_w_ffn_up', 'm_conv_ffn', 'm_b_conv_ffn', 'm_w_ffn_down', 'm_norm_ple', 'm_w_ple_gate', 'm_w_ple_proj', 'm_norm_final', 'v_norm_mix', 'v_w_in', 'v_conv_qkv', 'v_a_log', 'v_dt_bias', 'v_head_norm', 'v_sgu_norm', 'v_w_spatial', 'v_b_spatial', 'v_w_branch_a', 'v_w_branch_b', 'v_w_out', 'v_norm_ffn', 'v_w_ffn_up', 'v_conv_ffn', 'v_b_conv_ffn', 'v_w_ffn_down', 'v_norm_ple', 'v_w_ple_gate', 'v_w_ple_proj', 'v_norm_final']
TWIN_OUTPUTS = ['loss', 'grad_x', 'grad_norm_mix', 'grad_w_in', 'grad_conv_qkv', 'grad_a_log', 'grad_dt_bias', 'grad_head_norm', 'grad_sgu_norm', 'grad_w_spatial', 'grad_b_spatial', 'grad_w_branch_a', 'grad_w_branch_b', 'grad_w_out', 'grad_norm_ffn', 'grad_w_ffn_up', 'grad_conv_ffn', 'grad_b_conv_ffn', 'grad_w_ffn_down', 'grad_norm_ple', 'grad_w_ple_gate', 'grad_w_ple_proj', 'grad_norm_final', 'delta_norm_mix', 'delta_w_in', 'delta_conv_qkv', 'delta_a_log', 'delta_dt_bias', 'delta_head_norm', 'delta_sgu_norm', 'delta_w_spatial', 'delta_b_spatial', 'delta_w_branch_a', 'delta_w_branch_b', 'delta_w_out', 'delta_norm_ffn', 'delta_w_ffn_up', 'delta_conv_ffn', 'delta_b_conv_ffn', 'delta_w_ffn_down', 'delta_norm_ple', 'delta_w_ple_gate', 'delta_w_ple_proj', 'delta_norm_final', 'new_m_norm_mix', 'new_m_w_in', 'new_m_conv_qkv', 'new_m_a_log', 'new_m_dt_bias', 'new_m_head_norm', 'new_m_sgu_norm', 'new_m_w_spatial', 'new_m_b_spatial', 'new_m_w_branch_a', 'new_m_w_branch_b', 'new_m_w_out', 'new_m_norm_ffn', 'new_m_w_ffn_up', 'new_m_conv_ffn', 'new_m_b_conv_ffn', 'new_m_w_ffn_down', 'new_m_norm_ple', 'new_m_w_ple_gate', 'new_m_w_ple_proj', 'new_m_norm_final', 'new_v_norm_mix', 'new_v_w_in', 'new_v_conv_qkv', 'new_v_a_log', 'new_v_dt_bias', 'new_v_head_norm', 'new_v_sgu_norm', 'new_v_w_spatial', 'new_v_b_spatial', 'new_v_w_branch_a', 'new_v_w_branch_b', 'new_v_w_out', 'new_v_norm_ffn', 'new_v_w_ffn_up', 'new_v_conv_ffn', 'new_v_b_conv_ffn', 'new_v_w_ffn_down', 'new_v_norm_ple', 'new_v_w_ple_gate', 'new_v_w_ple_proj', 'new_v_norm_final']
TWIN_LEAF_KINDS = {'loss': 'loss', 'grad_x': 'grad_x', 'grad_norm_mix': 'grad_w', 'grad_w_in': 'grad_w', 'grad_conv_qkv': 'grad_w', 'grad_a_log': 'grad_w', 'grad_dt_bias': 'grad_w', 'grad_head_norm': 'grad_w', 'grad_sgu_norm': 'grad_w', 'grad_w_spatial': 'grad_w', 'grad_b_spatial': 'grad_w', 'grad_w_branch_a': 'grad_w', 'grad_w_branch_b': 'grad_w', 'grad_w_out': 'grad_w', 'grad_norm_ffn': 'grad_w', 'grad_w_ffn_up': 'grad_w', 'grad_conv_ffn': 'grad_w', 'grad_b_conv_ffn': 'grad_w', 'grad_w_ffn_down': 'grad_w', 'grad_norm_ple': 'grad_w', 'grad_w_ple_gate': 'grad_w', 'grad_w_ple_proj': 'grad_w', 'grad_norm_final': 'grad_w', 'delta_norm_mix': 'delta_w', 'delta_w_in': 'delta_w', 'delta_conv_qkv': 'delta_w', 'delta_a_log': 'delta_w', 'delta_dt_bias': 'delta_w', 'delta_head_norm': 'delta_w', 'delta_sgu_norm': 'delta_w', 'delta_w_spatial': 'delta_w', 'delta_b_spatial': 'delta_w', 'delta_w_branch_a': 'delta_w', 'delta_w_branch_b': 'delta_w', 'delta_w_out': 'delta_w', 'delta_norm_ffn': 'delta_w', 'delta_w_ffn_up': 'delta_w', 'delta_conv_ffn': 'delta_w', 'delta_b_conv_ffn': 'delta_w', 'delta_w_ffn_down': 'delta_w', 'delta_norm_ple': 'delta_w', 'delta_w_ple_gate': 'delta_w', 'delta_w_ple_proj': 'delta_w', 'delta_norm_final': 'delta_w', 'new_m_norm_mix': 'new_m', 'new_m_w_in': 'new_m', 'new_m_conv_qkv': 'new_m', 'new_m_a_log': 'new_m', 'new_m_dt_bias': 'new_m', 'new_m_head_norm': 'new_m', 'new_m_sgu_norm': 'new_m', 'new_m_w_spatial': 'new_m', 'new_m_b_spatial': 'new_m', 'new_m_w_branch_a': 'new_m', 'new_m_w_branch_b': 'new_m', 'new_m_w_out': 'new_m', 'new_m_norm_ffn': 'new_m', 'new_m_w_ffn_up': 'new_m', 'new_m_conv_ffn': 'new_m', 'new_m_b_conv_ffn': 'new_m', 'new_m_w_ffn_down': 'new_m', 'new_m_norm_ple': 'new_m', 'new_m_w_ple_gate': 'new_m', 'new_m_w_ple_proj': 'new_m', 'new_m_norm_final': 'new_m', 'new_v_norm_mix': 'new_v', 'new_v_w_in': 'new_v', 'new_v_conv_qkv': 'new_v', 'new_v_a_log': 'new_v', 'new_v_dt_bias': 'new_v', 'new_v_head_norm': 'new_v', 'new_v_sgu_norm': 'new_v', 'new_v_w_spatial': 'new_v', 'new_v_b_spatial': 'new_v', 'new_v_w_branch_a': 'new_v', 'new_v_w_branch_b': 'new_v', 'new_v_w_out': 'new_v', 'new_v_norm_ffn': 'new_v', 'new_v_w_ffn_up': 'new_v', 'new_v_conv_ffn': 'new_v', 'new_v_b_conv_ffn': 'new_v', 'new_v_w_ffn_down': 'new_v', 'new_v_norm_ple': 'new_v', 'new_v_w_ple_gate': 'new_v', 'new_v_w_ple_proj': 'new_v', 'new_v_norm_final': 'new_v'}


def _forward(args):
    return _fwd_reference(*[args[k] for k in FWD_PARAMS])


def _output_shape():
    def fwd():
        inp = _fwd_setup_inputs(0)
        return _fwd_reference(*[inp[k] for k in FWD_PARAMS])
    out = _jax.eval_shape(fwd)
    return out.shape, out.dtype

N_MICROBATCH = 1
ADAM_LR = 0.001
ADAM_B1 = 0.9
ADAM_B2 = 0.999
ADAM_EPS = 1e-08
ADAM_WD = 0.01
ADAM_STEP = 10
PER_EXAMPLE_BATCH_AXIS = {'x': 0, 'p': 1, 'loss_target': 0}
SHARED_INPUTS = []
_WEIGHT_DTYPES = {'norm_mix': _jnp.float32, 'w_in': _jnp.float32, 'conv_qkv': _jnp.float32, 'a_log': _jnp.float32, 'dt_bias': _jnp.float32, 'head_norm': _jnp.float32, 'sgu_norm': _jnp.float32, 'w_spatial': _jnp.float32, 'b_spatial': _jnp.float32, 'w_branch_a': _jnp.float32, 'w_branch_b': _jnp.float32, 'w_out': _jnp.float32, 'norm_ffn': _jnp.float32, 'w_ffn_up': _jnp.float32, 'conv_ffn': _jnp.float32, 'b_conv_ffn': _jnp.float32, 'w_ffn_down': _jnp.float32, 'norm_ple': _jnp.float32, 'w_ple_gate': _jnp.float32, 'w_ple_proj': _jnp.float32, 'norm_final': _jnp.float32}
MOMENT_SCALE = {'norm_mix': 5.507198e-02, 'w_in': 2.206234e-02, 'conv_qkv': 2.112173e-02, 'a_log': 9.241096e-02, 'dt_bias': 8.774067e-02, 'head_norm': 9.221452e-02, 'sgu_norm': 2.392104e-02, 'w_spatial': 2.449984e-02, 'b_spatial': 3.454066e-02, 'w_branch_a': 2.211615e-02, 'w_branch_b': 3.099517e-02, 'w_out': 3.800048e-02, 'norm_ffn': 5.247372e-02, 'w_ffn_up': 2.204326e-02, 'conv_ffn': 2.212793e-02, 'b_conv_ffn': 2.197942e-02, 'w_ffn_down': 3.599244e-02, 'norm_ple': 1.233372e-02, 'w_ple_gate': 1.230665e-02, 'w_ple_proj': 3.142556e-02, 'norm_final': 1.600932e+01}


def _to_microbatches(a, axis):
    t = _jnp.moveaxis(a, axis, 0)
    t = t.reshape((N_MICROBATCH, t.shape[0] // N_MICROBATCH) + t.shape[1:])
    return _jnp.moveaxis(t, 1, axis + 1)


def setup_inputs(seed: int = 0) -> dict:
    inp = _fwd_setup_inputs(seed)
    key = _jax.random.fold_in(_jax.random.key(seed), 7919)
    shape, _ = _output_shape()
    out = dict(inp)
    out["loss_target"] = _jax.random.normal(_jax.random.fold_in(key, 0), shape, _jnp.float32)
    for i, name in enumerate(TWIN_WEIGHTS):
        w = inp[name].astype(_jnp.float32)
        if MOMENT_SCALE is None:
            s = _jnp.sqrt(_jnp.mean(_jnp.square(w)) + 1e-30)
        else:
            s = MOMENT_SCALE[name]
        km, kv = _jax.random.split(_jax.random.fold_in(key, i + 1))
        out[name] = w
        out["m_" + name] = s * _jax.random.normal(km, w.shape, _jnp.float32)
        out["v_" + name] = (s * s) * _jax.random.uniform(kv, w.shape, _jnp.float32, 0.5, 1.5)
    if N_MICROBATCH > 1:
        for name, axis in PER_EXAMPLE_BATCH_AXIS.items():
            out[name] = _to_microbatches(out[name], axis)
    return {'x': out['x'], 'p': out['p'], 'norm_mix': out['norm_mix'], 'w_in': out['w_in'], 'conv_qkv': out['conv_qkv'], 'a_log': out['a_log'], 'dt_bias': out['dt_bias'], 'head_norm': out['head_norm'], 'sgu_norm': out['sgu_norm'], 'w_spatial': out['w_spatial'], 'b_spatial': out['b_spatial'], 'w_branch_a': out['w_branch_a'], 'w_branch_b': out['w_branch_b'], 'w_out': out['w_out'], 'norm_ffn': out['norm_ffn'], 'w_ffn_up': out['w_ffn_up'], 'conv_ffn': out['conv_ffn'], 'b_conv_ffn': out['b_conv_ffn'], 'w_ffn_down': out['w_ffn_down'], 'norm_ple': out['norm_ple'], 'w_ple_gate': out['w_ple_gate'], 'w_ple_proj': out['w_ple_proj'], 'norm_final': out['norm_final'], 'loss_target': out['loss_target'], 'm_norm_mix': out['m_norm_mix'], 'm_w_in': out['m_w_in'], 'm_conv_qkv': out['m_conv_qkv'], 'm_a_log': out['m_a_log'], 'm_dt_bias': out['m_dt_bias'], 'm_head_norm': out['m_head_norm'], 'm_sgu_norm': out['m_sgu_norm'], 'm_w_spatial': out['m_w_spatial'], 'm_b_spatial': out['m_b_spatial'], 'm_w_branch_a': out['m_w_branch_a'], 'm_w_branch_b': out['m_w_branch_b'], 'm_w_out': out['m_w_out'], 'm_norm_ffn': out['m_norm_ffn'], 'm_w_ffn_up': out['m_w_ffn_up'], 'm_conv_ffn': out['m_conv_ffn'], 'm_b_conv_ffn': out['m_b_conv_ffn'], 'm_w_ffn_down': out['m_w_ffn_down'], 'm_norm_ple': out['m_norm_ple'], 'm_w_ple_gate': out['m_w_ple_gate'], 'm_w_ple_proj': out['m_w_ple_proj'], 'm_norm_final': out['m_norm_final'], 'v_norm_mix': out['v_norm_mix'], 'v_w_in': out['v_w_in'], 'v_conv_qkv': out['v_conv_qkv'], 'v_a_log': out['v_a_log'], 'v_dt_bias': out['v_dt_bias'], 'v_head_norm': out['v_head_norm'], 'v_sgu_norm': out['v_sgu_norm'], 'v_w_spatial': out['v_w_spatial'], 'v_b_spatial': out['v_b_spatial'], 'v_w_branch_a': out['v_w_branch_a'], 'v_w_branch_b': out['v_w_branch_b'], 'v_w_out': out['v_w_out'], 'v_norm_ffn': out['v_norm_ffn'], 'v_w_ffn_up': out['v_w_ffn_up'], 'v_conv_ffn': out['v_conv_ffn'], 'v_b_conv_ffn': out['v_b_conv_ffn'], 'v_w_ffn_down': out['v_w_ffn_down'], 'v_norm_ple': out['v_norm_ple'], 'v_w_ple_gate': out['v_w_ple_gate'], 'v_w_ple_proj': out['v_w_ple_proj'], 'v_norm_final': out['v_norm_final']}


def _loss(weights, diff, rest, loss_target):
    with _jax.named_scope("forward"):
        args = {**rest, TWIN_DIFF_INPUT: diff, **{k: w.astype(_WEIGHT_DTYPES[k]) for k, w in weights.items()}}
        y = _forward(args)
    with _jax.named_scope("loss_head"):
        err = _jnp.square(y.astype(_jnp.float32) - loss_target)
        return 0.5 * _jnp.sum(_jnp.mean(err, axis=-1)) if err.ndim else 0.5 * err


def _adamw(w, g, m, v):
    m = ADAM_B1 * m + (1.0 - ADAM_B1) * g
    v = ADAM_B2 * v + (1.0 - ADAM_B2) * _jnp.square(g)
    m_hat = m / (1.0 - ADAM_B1 ** ADAM_STEP)
    v_hat = v / (1.0 - ADAM_B2 ** ADAM_STEP)
    delta = -ADAM_LR * (m_hat / (_jnp.sqrt(v_hat) + ADAM_EPS) + ADAM_WD * w)
    return delta, m, v


def reference(x, p, norm_mix, w_in, conv_qkv, a_log, dt_bias, head_norm, sgu_norm, w_spatial, b_spatial, w_branch_a, w_branch_b, w_out, norm_ffn, w_ffn_up, conv_ffn, b_conv_ffn, w_ffn_down, norm_ple, w_ple_gate, w_ple_proj, norm_final, loss_target, m_norm_mix, m_w_in, m_conv_qkv, m_a_log, m_dt_bias, m_head_norm, m_sgu_norm, m_w_spatial, m_b_spatial, m_w_branch_a, m_w_branch_b, m_w_out, m_norm_ffn, m_w_ffn_up, m_conv_ffn, m_b_conv_ffn, m_w_ffn_down, m_norm_ple, m_w_ple_gate, m_w_ple_proj, m_norm_final, v_norm_mix, v_w_in, v_conv_qkv, v_a_log, v_dt_bias, v_head_norm, v_sgu_norm, v_w_spatial, v_b_spatial, v_w_branch_a, v_w_branch_b, v_w_out, v_norm_ffn, v_w_ffn_up, v_conv_ffn, v_b_conv_ffn, v_w_ffn_down, v_norm_ple, v_w_ple_gate, v_w_ple_proj, v_norm_final):
    given = dict(x=x, p=p, norm_mix=norm_mix, w_in=w_in, conv_qkv=conv_qkv, a_log=a_log, dt_bias=dt_bias, head_norm=head_norm, sgu_norm=sgu_norm, w_spatial=w_spatial, b_spatial=b_spatial, w_branch_a=w_branch_a, w_branch_b=w_branch_b, w_out=w_out, norm_ffn=norm_ffn, w_ffn_up=w_ffn_up, conv_ffn=conv_ffn, b_conv_ffn=b_conv_ffn, w_ffn_down=w_ffn_down, norm_ple=norm_ple, w_ple_gate=w_ple_gate, w_ple_proj=w_ple_proj, norm_final=norm_final, loss_target=loss_target, m_norm_mix=m_norm_mix, m_w_in=m_w_in, m_conv_qkv=m_conv_qkv, m_a_log=m_a_log, m_dt_bias=m_dt_bias, m_head_norm=m_head_norm, m_sgu_norm=m_sgu_norm, m_w_spatial=m_w_spatial, m_b_spatial=m_b_spatial, m_w_branch_a=m_w_branch_a, m_w_branch_b=m_w_branch_b, m_w_out=m_w_out, m_norm_ffn=m_norm_ffn, m_w_ffn_up=m_w_ffn_up, m_conv_ffn=m_conv_ffn, m_b_conv_ffn=m_b_conv_ffn, m_w_ffn_down=m_w_ffn_down, m_norm_ple=m_norm_ple, m_w_ple_gate=m_w_ple_gate, m_w_ple_proj=m_w_ple_proj, m_norm_final=m_norm_final, v_norm_mix=v_norm_mix, v_w_in=v_w_in, v_conv_qkv=v_conv_qkv, v_a_log=v_a_log, v_dt_bias=v_dt_bias, v_head_norm=v_head_norm, v_sgu_norm=v_sgu_norm, v_w_spatial=v_w_spatial, v_b_spatial=v_b_spatial, v_w_branch_a=v_w_branch_a, v_w_branch_b=v_w_branch_b, v_w_out=v_w_out, v_norm_ffn=v_norm_ffn, v_w_ffn_up=v_w_ffn_up, v_conv_ffn=v_conv_ffn, v_b_conv_ffn=v_b_conv_ffn, v_w_ffn_down=v_w_ffn_down, v_norm_ple=v_norm_ple, v_w_ple_gate=v_w_ple_gate, v_w_ple_proj=v_w_ple_proj, v_norm_final=v_norm_final)
    weights = {n: given[n] for n in TWIN_WEIGHTS}
    shared = {n: given[n] for n in SHARED_INPUTS}
    per_example = {n: given[n] for n in ['x', 'p']}
    grad_fn = _jax.value_and_grad(_loss, argnums=(0, 1))

    def one_microbatch(ex, loss_target):
        ex = dict(ex)
        diff = ex.pop(TWIN_DIFF_INPUT)
        return grad_fn(weights, diff, {**shared, **ex}, loss_target)

    if N_MICROBATCH == 1:
        loss, (grad_w, grad_x) = one_microbatch(per_example, given["loss_target"])
    else:
        def body(carry, xs):
            loss_sum, grad_sum = carry
            l_k, (gw_k, gx_k) = one_microbatch(xs[0], xs[1])
            with _jax.named_scope("update"):
                return (loss_sum + l_k, _jax.tree.map(_jnp.add, grad_sum, gw_k)), gx_k

        init = (_jnp.zeros((), _jnp.float32), _jax.tree.map(_jnp.zeros_like, weights))
        (loss, grad_w), grad_x = _jax.lax.scan(body, init, (per_example, given["loss_target"]))
    with _jax.named_scope("update"):
        delta_w, new_m, new_v = {}, {}, {}
        for n in TWIN_WEIGHTS:
            delta_w[n], new_m[n], new_v[n] = _adamw(weights[n], grad_w[n], given["m_" + n], given["v_" + n])
    return (loss, grad_x, *[grad_w[n] for n in TWIN_WEIGHTS], *[delta_w[n] for n in TWIN_WEIGHTS],
            *[new_m[n] for n in TWIN_WEIGHTS], *[new_v[n] for n in TWIN_WEIGHTS])
```

```python
import functools
import math

import jax
import jax.numpy as jnp
from jax import lax
from jax.experimental import pallas as pl
from jax.experimental.pallas import tpu as pltpu

F32 = jnp.float32
BF16 = jnp.bfloat16
EPS = 1e-6
LANES = 128
HEAD = 128
CA = 64
N_DEV = 8
VMEM_LIMIT = 48 * 1024 * 1024
ADAM_VMEM_BUDGET = 16 * 1024 * 1024
MESH = pl.DeviceIdType.MESH
HI = lax.Precision.HIGHEST

ADAM_LR = 0.001
ADAM_B1 = 0.9
ADAM_B2 = 0.999
ADAM_EPS = 1e-08
ADAM_WD = 0.01
ADAM_STEP = 10


def _tile(n, cap, mult=LANES):
    best = None
    for t in range(mult, min(n, cap) + 1, mult):
        if n % t == 0:
            best = t
    return n if best is None else best


def _params(sem):
    return pltpu.CompilerParams(dimension_semantics=sem, vmem_limit_bytes=VMEM_LIMIT)


def _sigmoid(v):
    return jax.nn.sigmoid(v)


def _silu_grad(c):
    s = _sigmoid(c)
    return s + c * s * (1.0 - s)


_GELU_C = math.sqrt(2.0 / math.pi)


def _gelu(v):
    return 0.5 * v * (1.0 + jnp.tanh(_GELU_C * (v + 0.044715 * v * v * v)))


def _gelu_grad(v):
    t = jnp.tanh(_GELU_C * (v + 0.044715 * v * v * v))
    return 0.5 * (1.0 + t) + 0.5 * v * (1.0 - t * t) * _GELU_C * (1.0 + 3.0 * 0.044715 * v * v)


def _bdot(a, b, dims=((1,), (0,))):
    return lax.dot_general(a.astype(BF16), b.astype(BF16), (dims, ((), ())), preferred_element_type=F32)


_NT = ((1,), (1,))
_TN = ((0,), (0,))


def _hdot(a, b, dims=((1,), (0,))):
    return lax.dot_general(a, b, (dims, ((), ())), precision=HI, preferred_element_type=F32)


def _matmul(a, b, mode, out_dtype, name, res=None, tm_cap=1024, tn_cap=1024, tk_cap=512):
    if mode == "tn":
        kdim, m = a.shape
    else:
        m, kdim = a.shape
    n = b.shape[0] if mode == "nt" else b.shape[1]
    tm, tn, tk = _tile(m, tm_cap), _tile(n, tn_cap), _tile(kdim, tk_cap)
    nk = kdim // tk
    dims = {"nn": ((1,), (0,)), "nt": _NT, "tn": _TN}[mode]

    def body(*refs):
        if res is None:
            a_ref, b_ref, o_ref, acc = refs
            r_ref = None
        else:
            a_ref, b_ref, r_ref, o_ref, acc = refs
        k = pl.program_id(2)

        @pl.when(k == 0)
        def _():
            acc[...] = jnp.zeros_like(acc)

        acc[...] += _bdot(a_ref[...], b_ref[...], dims)

        @pl.when(k == nk - 1)
        def _():
            r = acc[...]
            if r_ref is not None:
                r = r + r_ref[...].astype(F32)
            o_ref[...] = r.astype(out_dtype)

    a_spec = pl.BlockSpec((tk, tm), lambda i, j, k: (k, i)) if mode == "tn" else pl.BlockSpec((tm, tk), lambda i, j, k: (i, k))
    b_spec = pl.BlockSpec((tn, tk), lambda i, j, k: (j, k)) if mode == "nt" else pl.BlockSpec((tk, tn), lambda i, j, k: (k, j))
    o_spec = pl.BlockSpec((tm, tn), lambda i, j, k: (i, j))
    in_specs = [a_spec, b_spec] + ([o_spec] if res is not None else [])
    args = (a, b) + ((res,) if res is not None else ())
    return pl.pallas_call(
        body, name=name, grid=(m // tm, n // tn, nk), in_specs=in_specs, out_specs=o_spec,
        out_shape=jax.ShapeDtypeStruct((m, n), out_dtype), scratch_shapes=[pltpu.VMEM((tm, tn), F32)],
        compiler_params=_params(("parallel", "parallel", "arbitrary")))(*args)


def _rms_fwd(x, gain, name):
    t, d = x.shape
    tb = _tile(t, 256, 8)

    def body(x_ref, g_ref, h_ref):
        xv = x_ref[...]
        r = lax.rsqrt(jnp.mean(xv * xv, axis=-1, keepdims=True) + EPS)
        h_ref[...] = (xv * r * g_ref[...]).astype(BF16)

    return pl.pallas_call(
        body, name=name, grid=(t // tb,),
        in_specs=[pl.BlockSpec((tb, d), lambda i: (i, 0)), pl.BlockSpec((1, d), lambda i: (0, 0))],
        out_specs=pl.BlockSpec((tb, d), lambda i: (i, 0)), out_shape=jax.ShapeDtypeStruct((t, d), BF16),
        compiler_params=_params(("parallel",)))(x, gain.reshape(1, d))


def _rms_bwd(x, dh, gain, dres, name):
    t, d = x.shape
    tb = _tile(t, 256, 8)

    def body(x_ref, dh_ref, g_ref, dr_ref, dx_ref, dg_ref):
        @pl.when(pl.program_id(0) == 0)
        def _():
            dg_ref[...] = jnp.zeros_like(dg_ref)

        xv = x_ref[...]
        dy = dh_ref[...].astype(F32)
        r = lax.rsqrt(jnp.mean(xv * xv, axis=-1, keepdims=True) + EPS)
        xh = xv * r
        dxh = dy * g_ref[...]
        dx_ref[...] = dr_ref[...] + r * (dxh - xh * jnp.mean(dxh * xh, axis=-1, keepdims=True))
        dg_ref[...] += jnp.sum(dy * xh, axis=0, keepdims=True)

    row = pl.BlockSpec((tb, d), lambda i: (i, 0))
    vec = pl.BlockSpec((1, d), lambda i: (0, 0))
    return pl.pallas_call(
        body, name=name, grid=(t // tb,), in_specs=[row, row, vec, row], out_specs=[row, vec],
        out_shape=[jax.ShapeDtypeStruct((t, d), F32), jax.ShapeDtypeStruct((1, d), F32)],
        compiler_params=_params(("arbitrary",)))(x, dh, gain.reshape(1, d), dres)


def _loss_head(x, target, gain):
    t, d = x.shape
    tb = _tile(t, 256, 8)

    def body(x_ref, t_ref, g_ref, dx_ref, dg_ref, loss_ref):
        @pl.when(pl.program_id(0) == 0)
        def _():
            dg_ref[...] = jnp.zeros_like(dg_ref)
            loss_ref[...] = jnp.zeros_like(loss_ref)

        xv = x_ref[...]
        r = lax.rsqrt(jnp.mean(xv * xv, axis=-1, keepdims=True) + EPS)
        xh = xv * r
        err = xh * g_ref[...] - t_ref[...]
        per_row = jnp.mean(err * err, axis=-1, keepdims=True)
        loss_ref[...] += 0.5 * jnp.sum(per_row, axis=0, keepdims=True)
        dy = err * (1.0 / d)
        dxh = dy * g_ref[...]
        dx_ref[...] = r * (dxh - xh * jnp.mean(dxh * xh, axis=-1, keepdims=True))
        dg_ref[...] += jnp.sum(dy * xh, axis=0, keepdims=True)

    row = pl.BlockSpec((tb, d), lambda i: (i, 0))
    vec = pl.BlockSpec((1, d), lambda i: (0, 0))
    return pl.pallas_call(
        body, name="loss_head", grid=(t // tb,), in_specs=[row, row, vec],
        out_specs=[row, vec, pl.BlockSpec((1, LANES), lambda i: (0, 0))],
        out_shape=[jax.ShapeDtypeStruct((t, d), F32), jax.ShapeDtypeStruct((1, d), F32),
                   jax.ShapeDtypeStruct((1, LANES), F32)],
        compiler_params=_params(("arbitrary",)))(x, target, gain.reshape(1, d))


def _shift_down(v, s, rows):
    if s == 0:
        return v
    return jnp.where(rows >= s, pltpu.roll(v, s, 0), 0.0)


def _shift_up(v, s, rows):
    if s == 0:
        return v
    t = v.shape[0]
    return jnp.where(rows < t - s, pltpu.roll(v, t - s, 0), 0.0)


def _conv(v, w, rows):
    k = w.shape[0]
    out = v * w[k - 1:k, :]
    for s in range(1, k):
        out = out + _shift_down(v, s, rows) * w[k - 1 - s:k - s, :]
    return out


def _qkv_fwd(proj, conv_w, nq, nk):
    t = proj.shape[0]
    cw = conv_w.shape[1]
    nblk = cw // HEAD

    def body(p_ref, w_ref, o_ref):
        j = pl.program_id(0)
        rows = lax.broadcasted_iota(jnp.int32, (t, HEAD), 0)
        c = _conv(p_ref[...].astype(F32), w_ref[...], rows)
        a = c * _sigmoid(c)
        nrm = a * lax.rsqrt(jnp.sum(a * a, axis=-1, keepdims=True) + EPS)
        nrm = nrm * jnp.where(j < nq, HEAD ** -0.5, 1.0)
        o_ref[...] = jnp.where(j < nq + nk, nrm, a).astype(BF16)

    return pl.pallas_call(
        body, name="qkv_fwd", grid=(nblk,),
        in_specs=[pl.BlockSpec((t, HEAD), lambda j: (0, j)), pl.BlockSpec((conv_w.shape[0], HEAD), lambda j: (0, j))],
        out_specs=pl.BlockSpec((t, HEAD), lambda j: (0, j)), out_shape=jax.ShapeDtypeStruct((t, cw), BF16),
        compiler_params=_params(("parallel",)))(proj, conv_w)


def _qkv_bwd(proj, dqkv, conv_w, nq, nk):
    t = proj.shape[0]
    kw, cw = conv_w.shape
    nblk = cw // HEAD

    def body(p_ref, d_ref, w_ref, dp_ref, dw_ref):
        j = pl.program_id(0)
        rows = lax.broadcasted_iota(jnp.int32, (t, HEAD), 0)
        xv = p_ref[...].astype(F32)
        w = w_ref[...]
        c = _conv(xv, w, rows)
        a = c * _sigmoid(c)
        dy = d_ref[...].astype(F32)
        r = lax.rsqrt(jnp.sum(a * a, axis=-1, keepdims=True) + EPS)
        y = a * r
        scale = jnp.where(j < nq, HEAD ** -0.5, 1.0)
        da_n = scale * r * (dy - y * jnp.sum(dy * y, axis=-1, keepdims=True))
        da = jnp.where(j < nq + nk, da_n, dy)
        dc = da * _silu_grad(c)
        dx = dc * w[kw - 1:kw, :]
        dw_ref[kw - 1:kw, :] = jnp.sum(dc * xv, axis=0, keepdims=True)
        for s in range(1, kw):
            dx = dx + _shift_up(dc, s, rows) * w[kw - 1 - s:kw - s, :]
            dw_ref[kw - 1 - s:kw - s, :] = jnp.sum(dc * _shift_down(xv, s, rows), axis=0, keepdims=True)
        dp_ref[...] = dx.astype(BF16)

    blk = pl.BlockSpec((t, HEAD), lambda j: (0, j))
    wblk = pl.BlockSpec((kw, HEAD), lambda j: (0, j))
    return pl.pallas_call(
        body, name="qkv_bwd", grid=(nblk,), in_specs=[blk, blk, wblk], out_specs=[blk, wblk],
        out_shape=[jax.ShapeDtypeStruct((t, cw), BF16), jax.ShapeDtypeStruct((kw, cw), F32)],
        compiler_params=_params(("parallel",)))(proj, dqkv, conv_w)


def _softplus(v):
    return jnp.where(v < -15.0, jnp.exp(v), jnp.maximum(v, 0.0) + jnp.log(1.0 + jnp.exp(-jnp.abs(v))))


def _gate_fwd(ba, alog_pad, dtb_pad, hv):
    t = ba.shape[0]
    tb = _tile(t, 512, CA)

    def body(ba_ref, al_ref, dt_ref, o_ref):
        v = ba_ref[...]
        beta = _sigmoid(v)
        g = -jnp.exp(al_ref[...]) * _softplus(v + dt_ref[...])
        pos = lax.broadcasted_iota(jnp.int32, (tb, LANES), 0) % CA
        s = 1
        while s < CA:
            g = g + jnp.where(pos >= s, pltpu.roll(g, s, 0), 0.0)
            s *= 2
        lane = lax.broadcasted_iota(jnp.int32, (tb, LANES), 1)
        o_ref[...] = jnp.where(lane < hv, beta, g)

    row = pl.BlockSpec((tb, LANES), lambda i: (i, 0))
    vec = pl.BlockSpec((1, LANES), lambda i: (0, 0))
    return pl.pallas_call(
        body, name="gate_fwd", grid=(t // tb,), in_specs=[row, vec, vec], out_specs=row,
        out_shape=jax.ShapeDtypeStruct((t, LANES), F32), compiler_params=_params(("parallel",)))(ba, alog_pad, dtb_pad)


def _gate_bwd(ba, dbg, alog_pad, dtb_pad, hv):
    t = ba.shape[0]
    tb = _tile(t, 512, CA)

    def body(ba_ref, d_ref, al_ref, dt_ref, dba_ref, dal_ref, ddt_ref):
        @pl.when(pl.program_id(0) == 0)
        def _():
            dal_ref[...] = jnp.zeros_like(dal_ref)
            ddt_ref[...] = jnp.zeros_like(ddt_ref)

        v = ba_ref[...]
        d = d_ref[...]
        pos = lax.broadcasted_iota(jnp.int32, (tb, LANES), 0) % CA
        dg = d
        s = 1
        while s < CA:
            dg = dg + jnp.where(pos < CA - s, pltpu.roll(dg, tb - s, 0), 0.0)
            s *= 2
        beta = _sigmoid(v)
        na = -jnp.exp(al_ref[...])
        z = v + dt_ref[...]
        da = dg * na * _sigmoid(z)
        lane = lax.broadcasted_iota(jnp.int32, (tb, LANES), 1)
        in_a = jnp.logical_and(lane >= hv, lane < 2 * hv)
        da = jnp.where(in_a, da, 0.0)
        dba_ref[...] = jnp.where(lane < hv, d * beta * (1.0 - beta), da)
        ddt_ref[...] += jnp.sum(da, axis=0, keepdims=True)
        dal_ref[...] += jnp.sum(jnp.where(in_a, dg * na * _softplus(z), 0.0), axis=0, keepdims=True)

    row = pl.BlockSpec((tb, LANES), lambda i: (i, 0))
    vec = pl.BlockSpec((1, LANES), lambda i: (0, 0))
    return pl.pallas_call(
        body, name="gate_bwd", grid=(t // tb,), in_specs=[row, row, vec, vec], out_specs=[row, vec, vec],
        out_shape=[jax.ShapeDtypeStruct((t, LANES), F32), jax.ShapeDtypeStruct((1, LANES), F32),
                   jax.ShapeDtypeStruct((1, LANES), F32)],
        compiler_params=_params(("arbitrary",)))(ba, dbg, alog_pad, dtb_pad)


def _chunk_masks():
    r = lax.broadcasted_iota(jnp.int32, (CA, CA), 0)
    c = lax.broadcasted_iota(jnp.int32, (CA, CA), 1)
    return r >= c, r > c, (r == c).astype(F32)


def _inv_unit_lower(a, eye):
    x = eye - a
    p = a
    n = 2
    while n < CA:
        p = _hdot(p, p)
        x = x + _hdot(x, p)
        n *= 2
    return x


def _delta_pre(q, k, v, bcol, gc, gr, gl, causal, strict):
    eg = jnp.exp(gc)
    dm = jnp.exp(jnp.where(causal, gc[:, :CA] - gr, -jnp.inf))
    kb = k * bcol
    kkb = _bdot(kb, k, _NT)
    a = jnp.where(strict, kkb * dm, 0.0)
    rhs = jnp.concatenate([v * bcol, kb * eg], axis=1)
    qk = _bdot(q, k, _NT)
    ekd = jnp.exp(gl - gc)
    return dict(eg=eg, dm=dm, kb=kb, kkb=kkb, a=a, rhs=rhs, p=qk * dm, qd=q * eg, ekd=ekd, kd=k * ekd, cd=jnp.exp(gl))


def _delta_fwd(qkvn, beta_b, gam_b, gam_r, hqk, hv):
    t = qkvn.shape[0]
    rep = hv // hqk
    rb = _tile(t, 512, CA)
    nb = t // rb
    ncb = rb // CA
    nc = t // CA

    def body(q_ref, k_ref, v_ref, b_ref, gc_ref, gr_ref, o_ref, s_ref, tm_ref, state):
        @pl.when(pl.program_id(1) == 0)
        def _():
            state[...] = jnp.zeros_like(state)

        causal, strict, eye = _chunk_masks()

        def chunk(n, carry):
            base = pl.multiple_of(n * CA, CA)
            rows = pl.ds(base, CA)
            q = q_ref[rows, :].astype(F32)
            k = k_ref[rows, :].astype(F32)
            for h in range(rep):
                v = v_ref[rows, h * HEAD:(h + 1) * HEAD].astype(F32)
                gl = gc_ref[h, pl.ds(base + CA - 1, 1), :]
                pre = _delta_pre(q, k, v, b_ref[h, rows, :], gc_ref[h, rows, :], gr_ref[h, pl.ds(n, 1), :], gl,
                                 causal, strict)
                tm = _inv_unit_lower(pre["a"], eye)
                sol = _hdot(tm, pre["rhs"])
                s = state[h]
                v_new = sol[:, :HEAD] - _bdot(sol[:, HEAD:], s)
                o_ref[rows, h * HEAD:(h + 1) * HEAD] = _bdot(pre["qd"], s) + _bdot(pre["p"], v_new)
                s_ref[h, n] = s.astype(BF16)
                tm_ref[h, rows, :] = tm
                state[h] = s * pre["cd"] + _bdot(pre["kd"], v_new, _TN)
            return carry

        lax.fori_loop(0, ncb, chunk, 0)

    qoff, koff, voff = 0, hqk, 2 * hqk // rep
    return pl.pallas_call(
        body, name="delta_fwd", grid=(hqk, nb),
        in_specs=[pl.BlockSpec((rb, HEAD), lambda j, i: (i, qoff + j)),
                  pl.BlockSpec((rb, HEAD), lambda j, i: (i, koff + j)),
                  pl.BlockSpec((rb, rep * HEAD), lambda j, i: (i, voff + j)),
                  pl.BlockSpec((rep, rb, LANES), lambda j, i: (j, i, 0)),
                  pl.BlockSpec((rep, rb, LANES), lambda j, i: (j, i, 0)),
                  pl.BlockSpec((rep, ncb, CA), lambda j, i: (j, i, 0))],
        out_specs=[pl.BlockSpec((rb, rep * HEAD), lambda j, i: (i, j)),
                   pl.BlockSpec((rep, ncb, HEAD, HEAD), lambda j, i: (j, i, 0, 0)),
                   pl.BlockSpec((rep, rb, CA), lambda j, i: (j, i, 0))],
        out_shape=[jax.ShapeDtypeStruct((t, hv * HEAD), F32), jax.ShapeDtypeStruct((hv, nc, HEAD, HEAD), BF16),
                   jax.ShapeDtypeStruct((hv, t, CA), F32)],
        scratch_shapes=[pltpu.VMEM((rep, HEAD, HEAD), F32)],
        compiler_params=_params(("parallel", "arbitrary")))(qkvn, qkvn, qkvn, beta_b, gam_b, gam_r)


def _delta_bwd(qkvn, beta_b, gam_b, gam_r, s_all, tm_all, do, hqk, hv):
    t = qkvn.shape[0]
    rep = hv // hqk
    rb = _tile(t, 512, CA)
    nb = t // rb
    ncb = rb // CA

    def body(q_ref, k_ref, v_ref, b_ref, gc_ref, gr_ref, s_ref, tm_ref, do_ref,
             dq_ref, dk_ref, dv_ref, db_ref, dg_ref, dstate):
        @pl.when(pl.program_id(1) == 0)
        def _():
            dstate[...] = jnp.zeros_like(dstate)

        causal, strict, _ = _chunk_masks()
        ones = jnp.ones((CA, LANES), F32)
        last = lax.broadcasted_iota(jnp.int32, (CA, LANES), 0) == CA - 1

        def rowsum(m):
            return jnp.sum(m, axis=1, keepdims=True)

        def chunk(it, carry):
            n = ncb - 1 - it
            base = pl.multiple_of(n * CA, CA)
            rows = pl.ds(base, CA)
            q = q_ref[rows, :].astype(F32)
            k = k_ref[rows, :].astype(F32)
            kkr = _bdot(k, k, _NT)
            dq = jnp.zeros((CA, HEAD), F32)
            dk = jnp.zeros((CA, HEAD), F32)
            for h in range(rep):
                v = v_ref[rows, h * HEAD:(h + 1) * HEAD].astype(F32)
                bcol = b_ref[h, rows, :]
                gl = gc_ref[h, pl.ds(base + CA - 1, 1), :]
                pre = _delta_pre(q, k, v, bcol, gc_ref[h, rows, :], gr_ref[h, pl.ds(n, 1), :], gl, causal, strict)
                eg, dm, kb, p, qd, kd, cd = pre["eg"], pre["dm"], pre["kb"], pre["p"], pre["qd"], pre["kd"], pre["cd"]
                tm = tm_ref[h, rows, :]
                sol = _hdot(tm, pre["rhs"])
                w = sol[:, HEAD:]
                s = s_ref[h, n].astype(F32)
                ds = dstate[h]
                dov = do_ref[rows, h * HEAD:(h + 1) * HEAD].astype(F32)
                v_new = sol[:, :HEAD] - _bdot(w, s)

                dvn = _bdot(p, dov, _TN) + _bdot(kd, ds)
                dp = jnp.where(causal, _bdot(dov, v_new, _NT), 0.0)
                dqd = _bdot(dov, s, _NT)
                dkd = _bdot(v_new, ds, _NT)
                dcd = jnp.sum(rowsum(s * ds), axis=0, keepdims=True)
                dw = -_bdot(dvn, s, _NT)
                dstate[h] = ds * cd + _bdot(qd, dov, _TN) - _bdot(w, dvn, _TN)

                drhs = _hdot(tm, jnp.concatenate([dvn, dw], axis=1), _TN)
                dbv, dbke = drhs[:, :HEAD], drhs[:, HEAD:]
                da = -jnp.where(strict, _bdot(drhs, sol, _NT), 0.0)
                m = da * dm
                e = m * pre["kkb"] + dp * p
                dgam = rowsum(e) - _hdot(e, ones, _TN) + rowsum(dbke * kb * eg) + rowsum(dqd * qd)
                r = rowsum(dkd * kd)
                tot = jnp.sum(r, axis=0, keepdims=True) + dcd * cd
                dgam = dgam - r + jnp.where(last, tot, 0.0)
                dbeta = rowsum(m * kkr) + rowsum(dbv * v) + rowsum(dbke * eg * k)
                nm = m * bcol[:, :CA]
                dqk = dp * dm
                dq = dq + _bdot(dqk, k) + eg * dqd
                dk = dk + _bdot(nm, k) + _bdot(nm, k, _TN) + _bdot(dqk, q, _TN) + bcol * eg * dbke + pre["ekd"] * dkd
                dv_ref[rows, h * HEAD:(h + 1) * HEAD] = bcol * dbv
                db_ref[h, rows, :] = jnp.broadcast_to(dbeta, (CA, LANES))
                dg_ref[h, rows, :] = jnp.broadcast_to(dgam, (CA, LANES))
            dq_ref[rows, :] = dq
            dk_ref[rows, :] = dk
            return carry

        lax.fori_loop(0, ncb, chunk, 0)

    qoff, koff, voff = 0, hqk, 2 * hqk // rep
    rv = lambda i: nb - 1 - i
    hd = pl.BlockSpec((rep, rb, LANES), lambda j, i: (j, rv(i), 0))
    qk_out = pl.BlockSpec((rb, HEAD), lambda j, i: (rv(i), j))
    v_blk = pl.BlockSpec((rb, rep * HEAD), lambda j, i: (rv(i), j))
    dq, dk, dv, db, dg = pl.pallas_call(
        body, name="delta_bwd", grid=(hqk, nb),
        in_specs=[pl.BlockSpec((rb, HEAD), lambda j, i: (rv(i), qoff + j)),
                  pl.BlockSpec((rb, HEAD), lambda j, i: (rv(i), koff + j)),
                  pl.BlockSpec((rb, rep * HEAD), lambda j, i: (rv(i), voff + j)),
                  hd, hd,
                  pl.BlockSpec((rep, ncb, CA), lambda j, i: (j, rv(i), 0)),
                  pl.BlockSpec((rep, ncb, HEAD, HEAD), lambda j, i: (j, rv(i), 0, 0)),
                  pl.BlockSpec((rep, rb, CA), lambda j, i: (j, rv(i), 0)),
                  v_blk],
        out_specs=[qk_out, qk_out, v_blk, hd, hd],
        out_shape=[jax.ShapeDtypeStruct((t, hqk * HEAD), F32), jax.ShapeDtypeStruct((t, hqk * HEAD), F32),
                   jax.ShapeDtypeStruct((t, hv * HEAD), F32),
                   jax.ShapeDtypeStruct((hv, t, LANES), F32), jax.ShapeDtypeStruct((hv, t, LANES), F32)],
        scratch_shapes=[pltpu.VMEM((rep, HEAD, HEAD), F32)],
        compiler_params=_params(("parallel", "arbitrary")))(qkvn, qkvn, qkvn, beta_b, gam_b, gam_r, s_all, tm_all, do)
    return dq, dk, dv, db, dg


def _apost_fwd(o, proj, gain, zoff, hv):
    t = o.shape[0]
    tb = _tile(t, 1024, 8)
    zb = zoff // HEAD

    def body(o_ref, z_ref, g_ref, y_ref):
        ov = o_ref[...]
        z = z_ref[...].astype(F32)
        r = lax.rsqrt(jnp.mean(ov * ov, axis=-1, keepdims=True) + EPS)
        y_ref[...] = (ov * r * g_ref[...] * (z * _sigmoid(z))).astype(BF16)

    blk = pl.BlockSpec((tb, HEAD), lambda i, h: (i, h))
    return pl.pallas_call(
        body, name="apost_fwd", grid=(t // tb, hv),
        in_specs=[blk, pl.BlockSpec((tb, HEAD), lambda i, h: (i, zb + h)), pl.BlockSpec((1, HEAD), lambda i, h: (0, 0))],
        out_specs=blk, out_shape=jax.ShapeDtypeStruct((t, hv * HEAD), BF16),
        compiler_params=_params(("parallel", "parallel")))(o, proj, gain.reshape(1, HEAD))


def _apost_bwd(o, proj, gain, dy, zoff, hv):
    t = o.shape[0]
    tb = _tile(t, 1024, 8)
    zb = zoff // HEAD

    def body(o_ref, z_ref, g_ref, dy_ref, do_ref, dz_ref, dg_ref):
        @pl.when(jnp.logical_and(pl.program_id(0) == 0, pl.program_id(1) == 0))
        def _():
            dg_ref[...] = jnp.zeros_like(dg_ref)

        ov = o_ref[...]
        z = z_ref[...].astype(F32)
        d = dy_ref[...].astype(F32)
        r = lax.rsqrt(jnp.mean(ov * ov, axis=-1, keepdims=True) + EPS)
        oh = ov * r
        sz = z * _sigmoid(z)
        dn = d * sz
        dz_ref[...] = (d * oh * g_ref[...] * _silu_grad(z)).astype(BF16)
        doh = dn * g_ref[...]
        do_ref[...] = r * (doh - oh * jnp.mean(doh * oh, axis=-1, keepdims=True))
        dg_ref[...] += jnp.sum(dn * oh, axis=0, keepdims=True)

    blk = pl.BlockSpec((tb, HEAD), lambda i, h: (i, h))
    vec = pl.BlockSpec((1, HEAD), lambda i, h: (0, 0))
    return pl.pallas_call(
        body, name="apost_bwd", grid=(t // tb, hv),
        in_specs=[blk, pl.BlockSpec((tb, HEAD), lambda i, h: (i, zb + h)), vec, blk],
        out_specs=[blk, blk, vec],
        out_shape=[jax.ShapeDtypeStruct((t, hv * HEAD), F32), jax.ShapeDtypeStruct((t, hv * HEAD), BF16),
                   jax.ShapeDtypeStruct((1, HEAD), F32)],
        compiler_params=_params(("arbitrary", "arbitrary")))(o, proj, gain.reshape(1, HEAD), dy)


def _sgu_fwd(proj, gain, w_s, b_t, uoff, wb):
    t = proj.shape[0]
    ng = wb // HEAD

    def body(u_ref, v_ref, g_ref, w_ref, b_ref, o_ref):
        r_i = lax.broadcasted_iota(jnp.int32, (HEAD, HEAD), 0)
        c_i = lax.broadcasted_iota(jnp.int32, (HEAD, HEAD), 1)
        u = _gelu(u_ref[...].astype(F32))
        vg = _gelu(v_ref[...].astype(F32))
        vn = vg * lax.rsqrt(jnp.mean(vg * vg, axis=-1, keepdims=True) + EPS) * g_ref[...]
        for g in range(ng):
            cols = slice(g * HEAD, (g + 1) * HEAD)
            wg = jnp.where(r_i >= c_i, w_ref[g], 0.0)
            mixed = _bdot(wg, vn[:, cols]) + b_ref[:, g:g + 1]
            o_ref[:, cols] = (u[:, cols] * mixed).astype(BF16)

    ub, vb = uoff // wb, uoff // wb + 1
    return pl.pallas_call(
        body, name="sgu_fwd", grid=(t // HEAD,),
        in_specs=[pl.BlockSpec((HEAD, wb), lambda i: (i, ub)), pl.BlockSpec((HEAD, wb), lambda i: (i, vb)),
                  pl.BlockSpec((1, wb), lambda i: (0, 0)), pl.BlockSpec((ng, HEAD, HEAD), lambda i: (0, 0, 0)),
                  pl.BlockSpec((HEAD, ng), lambda i: (0, 0))],
        out_specs=pl.BlockSpec((HEAD, wb), lambda i: (i, 0)), out_shape=jax.ShapeDtypeStruct((t, wb), BF16),
        compiler_params=_params(("parallel",)))(proj, proj, gain.reshape(1, wb), w_s, b_t)


def _sgu_bwd(proj, gain, w_s, b_t, dout, uoff, wb):
    t = proj.shape[0]
    ng = wb // HEAD

    def body(u_ref, v_ref, g_ref, w_ref, b_ref, d_ref, du_ref, dv_ref, dw_ref, db_ref, dg_ref, dvn_ref):
        @pl.when(pl.program_id(0) == 0)
        def _():
            dw_ref[...] = jnp.zeros_like(dw_ref)
            db_ref[...] = jnp.zeros_like(db_ref)
            dg_ref[...] = jnp.zeros_like(dg_ref)

        r_i = lax.broadcasted_iota(jnp.int32, (HEAD, HEAD), 0)
        c_i = lax.broadcasted_iota(jnp.int32, (HEAD, HEAD), 1)
        tril = r_i >= c_i
        ub = u_ref[...].astype(F32)
        vb = v_ref[...].astype(F32)
        u = _gelu(ub)
        vg = _gelu(vb)
        r = lax.rsqrt(jnp.mean(vg * vg, axis=-1, keepdims=True) + EPS)
        vh = vg * r
        vn = vh * g_ref[...]
        d = d_ref[...].astype(F32)
        for g in range(ng):
            cols = slice(g * HEAD, (g + 1) * HEAD)
            wg = jnp.where(tril, w_ref[g], 0.0)
            mixed = _bdot(wg, vn[:, cols]) + b_ref[:, g:g + 1]
            du_ref[:, cols] = (d[:, cols] * mixed * _gelu_grad(ub[:, cols])).astype(BF16)
            dmix = d[:, cols] * u[:, cols]
            dw_ref[g] += jnp.where(tril, _bdot(dmix, vn[:, cols], _NT), 0.0)
            db_ref[g] += jnp.broadcast_to(jnp.sum(dmix, axis=1, keepdims=True), (HEAD, HEAD))
            dvn_ref[:, cols] = _bdot(wg, dmix, _TN)
        dvn = dvn_ref[...]
        dg_ref[...] += jnp.sum(dvn * vh, axis=0, keepdims=True)
        dvh = dvn * g_ref[...]
        dvg = r * (dvh - vh * jnp.mean(dvh * vh, axis=-1, keepdims=True))
        dv_ref[...] = (dvg * _gelu_grad(vb)).astype(BF16)

    ub_i, vb_i = uoff // wb, uoff // wb + 1
    row = pl.BlockSpec((HEAD, wb), lambda i: (i, 0))
    mat = pl.BlockSpec((ng, HEAD, HEAD), lambda i: (0, 0, 0))
    vec = pl.BlockSpec((1, wb), lambda i: (0, 0))
    return pl.pallas_call(
        body, name="sgu_bwd", grid=(t // HEAD,),
        in_specs=[pl.BlockSpec((HEAD, wb), lambda i: (i, ub_i)), pl.BlockSpec((HEAD, wb), lambda i: (i, vb_i)),
                  vec, mat, pl.BlockSpec((HEAD, ng), lambda i: (0, 0)), row],
        out_specs=[row, row, mat, mat, vec],
        out_shape=[jax.ShapeDtypeStruct((t, wb), BF16), jax.ShapeDtypeStruct((t, wb), BF16),
                   jax.ShapeDtypeStruct((ng, HEAD, HEAD), F32), jax.ShapeDtypeStruct((ng, HEAD, HEAD), F32),
                   jax.ShapeDtypeStruct((1, wb), F32)],
        scratch_shapes=[pltpu.VMEM((HEAD, wb), F32)],
        compiler_params=_params(("arbitrary",)))(proj, proj, gain.reshape(1, wb), w_s, b_t, dout)


def _merge_specs(t, d, goff):
    tb = _tile(t, 512, 8)
    tc = _tile(d, 512)
    gb = goff // tc
    nd = d // tc
    blk = pl.BlockSpec((tb, tc), lambda i, j: (i, j))
    ga = pl.BlockSpec((tb, tc), lambda i, j: (i, gb + j))
    gbs = pl.BlockSpec((tb, tc), lambda i, j: (i, gb + nd + j))
    return (t // tb, nd), blk, ga, gbs


def _merge_fwd(ya, yb, proj, goff):
    t, d = ya.shape
    grid, blk, ga, gbs = _merge_specs(t, d, goff)

    def body(ya_ref, yb_ref, ga_ref, gb_ref, o_ref):
        o_ref[...] = (_sigmoid(ga_ref[...].astype(F32)) * ya_ref[...].astype(F32)
                      + _sigmoid(gb_ref[...].astype(F32)) * yb_ref[...].astype(F32)).astype(BF16)

    return pl.pallas_call(
        body, name="merge_fwd", grid=grid, in_specs=[blk, blk, ga, gbs], out_specs=blk,
        out_shape=jax.ShapeDtypeStruct((t, d), BF16),
        compiler_params=_params(("parallel", "parallel")))(ya, yb, proj, proj)


def _merge_bwd(dm, ya, yb, proj, goff):
    t, d = ya.shape
    grid, blk, ga, gbs = _merge_specs(t, d, goff)

    def body(dm_ref, ya_ref, yb_ref, ga_ref, gb_ref, dya_ref, dyb_ref, dga_ref, dgb_ref):
        dmv = dm_ref[...].astype(F32)
        sa = _sigmoid(ga_ref[...].astype(F32))
        sb = _sigmoid(gb_ref[...].astype(F32))
        dya_ref[...] = (dmv * sa).astype(BF16)
        dyb_ref[...] = (dmv * sb).astype(BF16)
        dga_ref[...] = (dmv * ya_ref[...].astype(F32) * sa * (1.0 - sa)).astype(BF16)
        dgb_ref[...] = (dmv * yb_ref[...].astype(F32) * sb * (1.0 - sb)).astype(BF16)

    shp = jax.ShapeDtypeStruct((t, d), BF16)
    return pl.pallas_call(
        body, name="merge_bwd", grid=grid, in_specs=[blk, blk, blk, ga, gbs], out_specs=[blk] * 4,
        out_shape=[shp] * 4, compiler_params=_params(("parallel", "parallel")))(dm, ya, yb, proj, proj)


def _ffn_act_fwd(up, conv_w, bias, dff):
    t = up.shape[0]
    nblk = dff // HEAD
    kw = conv_w.shape[0]

    def body(g_ref, v_ref, wg_ref, wv_ref, bg_ref, bv_ref, o_ref):
        rows = lax.broadcasted_iota(jnp.int32, (t, HEAD), 0)
        cg = _conv(g_ref[...].astype(F32), wg_ref[...], rows) + bg_ref[...]
        cv = _conv(v_ref[...].astype(F32), wv_ref[...], rows) + bv_ref[...]
        o_ref[...] = (cg * _sigmoid(cg) * cv).astype(BF16)

    return pl.pallas_call(
        body, name="ffn_act_fwd", grid=(nblk,),
        in_specs=[pl.BlockSpec((t, HEAD), lambda j: (0, j)), pl.BlockSpec((t, HEAD), lambda j: (0, nblk + j)),
                  pl.BlockSpec((kw, HEAD), lambda j: (0, j)), pl.BlockSpec((kw, HEAD), lambda j: (0, nblk + j)),
                  pl.BlockSpec((1, HEAD), lambda j: (0, j)), pl.BlockSpec((1, HEAD), lambda j: (0, nblk + j))],
        out_specs=pl.BlockSpec((t, HEAD), lambda j: (0, j)), out_shape=jax.ShapeDtypeStruct((t, dff), BF16),
        compiler_params=_params(("parallel",)))(up, up, conv_w, conv_w, bias, bias)


def _ffn_act_bwd(up, dact, conv_w, bias, dff):
    t = up.shape[0]
    nblk = dff // HEAD
    kw = conv_w.shape[0]

    def body(me_ref, pa_ref, d_ref, wm_ref, wp_ref, bm_ref, bp_ref, dup_ref, dw_ref, db_ref):
        is_gate = pl.program_id(0) < nblk
        rows = lax.broadcasted_iota(jnp.int32, (t, HEAD), 0)
        xv = me_ref[...].astype(F32)
        w = wm_ref[...]
        cm = _conv(xv, w, rows) + bm_ref[...]
        cp = _conv(pa_ref[...].astype(F32), wp_ref[...], rows) + bp_ref[...]
        d = d_ref[...].astype(F32)
        dc = jnp.where(is_gate, d * cp * _silu_grad(cm), d * (cp * _sigmoid(cp)))
        db_ref[...] = jnp.sum(dc, axis=0, keepdims=True)
        dx = dc * w[kw - 1:kw, :]
        dw_ref[kw - 1:kw, :] = jnp.sum(dc * xv, axis=0, keepdims=True)
        for s in range(1, kw):
            dx = dx + _shift_up(dc, s, rows) * w[kw - 1 - s:kw - s, :]
            dw_ref[kw - 1 - s:kw - s, :] = jnp.sum(dc * _shift_down(xv, s, rows), axis=0, keepdims=True)
        dup_ref[...] = dx.astype(BF16)

    part = lambda j: (j + nblk) % (2 * nblk)
    me = pl.BlockSpec((t, HEAD), lambda j: (0, j))
    wme = pl.BlockSpec((kw, HEAD), lambda j: (0, j))
    bme = pl.BlockSpec((1, HEAD), lambda j: (0, j))
    return pl.pallas_call(
        body, name="ffn_act_bwd", grid=(2 * nblk,),
        in_specs=[me, pl.BlockSpec((t, HEAD), lambda j: (0, part(j))), pl.BlockSpec((t, HEAD), lambda j: (0, j % nblk)),
                  wme, pl.BlockSpec((kw, HEAD), lambda j: (0, part(j))),
                  bme, pl.BlockSpec((1, HEAD), lambda j: (0, part(j)))],
        out_specs=[me, wme, bme],
        out_shape=[jax.ShapeDtypeStruct((t, 2 * dff), BF16), jax.ShapeDtypeStruct((kw, 2 * dff), F32),
                   jax.ShapeDtypeStruct((1, 2 * dff), F32)],
        compiler_params=_params(("parallel",)))(up, up, dact, conv_w, conv_w, bias, bias)


def _ple_fwd(x, gt, pp):
    t, d = x.shape
    tb, tc = _tile(t, 512, 8), _tile(d, 1024)

    def body(x_ref, g_ref, p_ref, o_ref):
        o_ref[...] = x_ref[...] + _sigmoid(g_ref[...].astype(F32)) * p_ref[...].astype(F32)

    blk = pl.BlockSpec((tb, tc), lambda i, j: (i, j))
    return pl.pallas_call(
        body, name="ple_fwd", grid=(t // tb, d // tc), in_specs=[blk, blk, blk], out_specs=blk,
        out_shape=jax.ShapeDtypeStruct((t, d), F32), compiler_params=_params(("parallel", "parallel")))(x, gt, pp)


def _ple_bwd(dx, gt, pp):
    t, d = dx.shape
    tb, tc = _tile(t, 512, 8), _tile(d, 1024)

    def body(dx_ref, g_ref, p_ref, dg_ref, dp_ref):
        dv = dx_ref[...]
        s = _sigmoid(g_ref[...].astype(F32))
        dg_ref[...] = (dv * p_ref[...].astype(F32) * s * (1.0 - s)).astype(BF16)
        dp_ref[...] = (dv * s).astype(BF16)

    blk = pl.BlockSpec((tb, tc), lambda i, j: (i, j))
    shp = jax.ShapeDtypeStruct((t, d), BF16)
    return pl.pallas_call(
        body, name="ple_bwd", grid=(t // tb, d // tc), in_specs=[blk, blk, blk], out_specs=[blk, blk],
        out_shape=[shp, shp], compiler_params=_params(("parallel", "parallel")))(dx, gt, pp)


def _adam(parts, w, m, v, name):
    npart, r, c = parts.shape
    row_bytes = 2 * c * (npart * parts.dtype.itemsize + 7 * 4)
    tr = _tile(r, max(16, min(512, ADAM_VMEM_BUDGET // row_bytes)), 16)
    c1 = 1.0 - ADAM_B1 ** ADAM_STEP
    c2 = 1.0 - ADAM_B2 ** ADAM_STEP

    def body(p_ref, w_ref, m_ref, v_ref, g_ref, d_ref, mo_ref, vo_ref):
        g = p_ref[0].astype(F32)
        for i in range(1, npart):
            g = g + p_ref[i].astype(F32)
        mn = ADAM_B1 * m_ref[...] + (1.0 - ADAM_B1) * g
        vn = ADAM_B2 * v_ref[...] + (1.0 - ADAM_B2) * (g * g)
        g_ref[...] = g
        mo_ref[...] = mn
        vo_ref[...] = vn
        d_ref[...] = -ADAM_LR * ((mn / c1) / (jnp.sqrt(vn / c2) + ADAM_EPS) + ADAM_WD * w_ref[...])

    blk = pl.BlockSpec((tr, c), lambda i: (i, 0))
    shp = jax.ShapeDtypeStruct((r, c), F32)
    return pl.pallas_call(
        body, name=name, grid=(r // tr,),
        in_specs=[pl.BlockSpec((npart, tr, c), lambda i: (0, i, 0)), blk, blk, blk], out_specs=[blk] * 4,
        out_shape=[shp] * 4, compiler_params=_params(("parallel",)))(parts, w, m, v)


def _exchange(srcs, scatter, name):
    n = len(srcs)
    out_shape = [jax.ShapeDtypeStruct(s.shape if scatter else (N_DEV,) + s.shape, s.dtype) for s in srcs]

    def body(*refs):
        src, out = refs[:n], refs[n:2 * n]
        send_sems, recv_sems, local_sems = refs[2 * n:]
        x, y, c = lax.axis_index("x"), lax.axis_index("y"), lax.axis_index("c")
        me = 4 * x + 2 * y + c
        local = []
        for k in range(n):
            cp = pltpu.make_async_copy(src[k].at[me] if scatter else src[k], out[k].at[me], local_sems.at[k])
            cp.start()
            local.append(cp)
        sends, recvs = [], []
        for m in range(1, N_DEV):
            px = lax.rem(x + ((m >> 2) & 1), 2)
            py = lax.rem(y + ((m >> 1) & 1), 2)
            pc = lax.rem(c + (m & 1), 2)
            peer = 4 * px + 2 * py + pc
            for k in range(n):
                cp = pltpu.make_async_remote_copy(
                    src_ref=src[k].at[peer] if scatter else src[k], dst_ref=out[k].at[me],
                    send_sem=send_sems.at[k, m - 1], recv_sem=recv_sems.at[k, m - 1],
                    device_id=(px, py, pc), device_id_type=MESH)
                cp.start()
                sends.append(cp)
                recvs.append(pltpu.make_async_remote_copy(
                    src_ref=src[k].at[peer] if scatter else src[k], dst_ref=out[k].at[peer],
                    send_sem=send_sems.at[k, m - 1], recv_sem=recv_sems.at[k, m - 1],
                    device_id=(px, py, pc), device_id_type=MESH))
        for cp in recvs:
            cp.wait_recv()
        for cp in sends:
            cp.wait_send()
        for cp in local:
            cp.wait()

    any_spec = pl.BlockSpec(memory_space=pl.ANY)
    return pl.pallas_call(
        body, name=name, in_specs=[any_spec] * n, out_specs=[any_spec] * n, out_shape=out_shape,
        scratch_shapes=[pltpu.SemaphoreType.DMA((n, N_DEV - 1)), pltpu.SemaphoreType.DMA((n, N_DEV - 1)),
                        pltpu.SemaphoreType.DMA((n,))],
        compiler_params=pltpu.CompilerParams(has_side_effects=True))(*srcs)


_BIG = ("w_in", "w_branch_a", "w_branch_b", "w_out", "w_ffn_up", "w_ffn_down", "w_ple_gate", "w_ple_proj")
_COL_SHARDED = ("w_in", "w_branch_b", "w_ffn_up", "w_ple_proj")
_CONVS = ("conv_qkv", "conv_ffn")
_SMALL = ("norm_mix", "conv_qkv", "a_log", "dt_bias", "head_norm", "sgu_norm", "w_spatial", "b_spatial", "norm_ffn",
          "conv_ffn", "b_conv_ffn", "norm_ple", "norm_final")
_WEIGHTS = ("norm_mix", "w_in", "conv_qkv", "a_log", "dt_bias", "head_norm", "sgu_norm", "w_spatial", "b_spatial",
            "w_branch_a", "w_branch_b", "w_out", "norm_ffn", "w_ffn_up", "conv_ffn", "b_conv_ffn", "w_ffn_down",
            "norm_ple", "w_ple_gate", "w_ple_proj", "norm_final")


def _full_cols(g):
    return jnp.transpose(g, (1, 0, 2)).reshape(g.shape[1], N_DEV * g.shape[2])


def _full_rows(g):
    return g.reshape(N_DEV * g.shape[1], g.shape[2])


def _split_cols(dw):
    k, n = dw.shape
    return jnp.transpose(dw.reshape(k, N_DEV, n // N_DEV), (1, 0, 2))


def _split_rows(dw):
    k, n = dw.shape
    return dw.reshape(N_DEV, k // N_DEV, n)


def _pad_lanes(v, width=LANES, offset=0):
    return jnp.pad(v, ((0, 0), (offset, width - offset - v.shape[1])))


def kernel(x, p, norm_mix, w_in, conv_qkv, a_log, dt_bias, head_norm, sgu_norm, w_spatial, b_spatial, w_branch_a, w_branch_b, w_out, norm_ffn, w_ffn_up, conv_ffn, b_conv_ffn, w_ffn_down, norm_ple, w_ple_gate, w_ple_proj, norm_final, loss_target, m_norm_mix, m_w_in, m_conv_qkv, m_a_log, m_dt_bias, m_head_norm, m_sgu_norm, m_w_spatial, m_b_spatial, m_w_branch_a, m_w_branch_b, m_w_out, m_norm_ffn, m_w_ffn_up, m_conv_ffn, m_b_conv_ffn, m_w_ffn_down, m_norm_ple, m_w_ple_gate, m_w_ple_proj, m_norm_final, v_norm_mix, v_w_in, v_conv_qkv, v_a_log, v_dt_bias, v_head_norm, v_sgu_norm, v_w_spatial, v_b_spatial, v_w_branch_a, v_w_branch_b, v_w_out, v_norm_ffn, v_w_ffn_up, v_conv_ffn, v_b_conv_ffn, v_w_ffn_down, v_norm_ple, v_w_ple_gate, v_w_ple_proj, v_norm_final):
    env = dict(locals())
    wts = {n: env[n] for n in _WEIGHTS}
    mom_m = {n: env["m_" + n] for n in _WEIGHTS}
    mom_v = {n: env["v_" + n] for n in _WEIGHTS}

    xin = x[0]
    tgt = loss_target[0]
    t, d = xin.shape
    depth = w_in.shape[0]
    hv = a_log.shape[1]
    vw = hv * HEAD
    wb = sgu_norm.shape[1]
    ng = w_spatial.shape[1]
    n_in = w_in.shape[2] * N_DEV
    qk = (n_in - 2 * vw - 2 * hv - 2 * wb - 2 * d) // 2
    hqk = qk // HEAD
    dff = w_ffn_down.shape[1] * N_DEV
    cw = 2 * qk + vw
    o_z, o_ba = 2 * qk + vw, 2 * qk + 2 * vw
    o_ub = o_ba
    o_ga = o_ub + 2 * wb
    me = 4 * lax.axis_index("x") + 2 * lax.axis_index("y") + lax.axis_index("c")

    full = []
    for i in range(depth):
        srcs = [wts[n][i].astype(BF16) for n in _BIG] + [wts[n][i] for n in _CONVS]
        got = dict(zip(_BIG + _CONVS, _exchange(srcs, False, "gather_weights")))
        fw = {n: (_full_cols(got[n]) if n in _COL_SHARDED or n in _CONVS else _full_rows(got[n])) for n in got}
        w_full = fw.pop("w_in")
        fw["w_main"] = jnp.concatenate([w_full[:, :o_ba], w_full[:, o_ba + 2 * hv:]], axis=1)
        fw["w_ba"] = _pad_lanes(w_full[:, o_ba:o_ba + 2 * hv])
        full.append(fw)

    saved = []
    xc = xin
    for i in range(depth):
        fw = full[i]
        s = {"x0": xc}
        s["h1"] = _rms_fwd(xc, norm_mix[i], "rms_fwd")
        s["proj"] = _matmul(s["h1"], fw["w_main"], "nn", BF16, "mm_proj")
        s["ba"] = _matmul(s["h1"], fw["w_ba"], "nn", F32, "mm_ba")
        s["qkvn"] = _qkv_fwd(s["proj"], fw["conv_qkv"], hqk, hqk)
        s["alog"] = _pad_lanes(a_log[i][None, :], offset=hv)
        s["dtb"] = _pad_lanes(dt_bias[i][None, :], offset=hv)
        bg = _gate_fwd(s["ba"], s["alog"], s["dtb"], hv)
        beta_t = bg[:, :hv].T
        gam_t = bg[:, hv:2 * hv].T
        s["beta_b"] = jnp.broadcast_to(beta_t[:, :, None], (hv, t, LANES))
        s["gam_b"] = jnp.broadcast_to(gam_t[:, :, None], (hv, t, LANES))
        s["gam_r"] = gam_t.reshape(hv, t // CA, CA)
        s["o"], s["s_all"], s["tm_all"] = _delta_fwd(s["qkvn"], s["beta_b"], s["gam_b"], s["gam_r"], hqk, hv)
        s["outa"] = _apost_fwd(s["o"], s["proj"], head_norm[i], o_z, hv)
        s["b_t"] = b_spatial[i].T
        s["outb"] = _sgu_fwd(s["proj"], sgu_norm[i], w_spatial[i], s["b_t"], o_ub, wb)
        s["ya"] = _matmul(s["outa"], fw["w_branch_a"], "nn", BF16, "mm_ya")
        s["yb"] = _matmul(s["outb"], fw["w_branch_b"], "nn", BF16, "mm_yb")
        s["mg"] = _merge_fwd(s["ya"], s["yb"], s["proj"], o_ga)
        s["x1"] = _matmul(s["mg"], fw["w_out"], "nn", F32, "mm_out", res=xc)
        s["h2"] = _rms_fwd(s["x1"], norm_ffn[i], "rms_fwd")
        s["up"] = _matmul(s["h2"], fw["w_ffn_up"], "nn", BF16, "mm_up")
        s["bias"] = b_conv_ffn[i][None, :]
        s["act"] = _ffn_act_fwd(s["up"], fw["conv_ffn"], s["bias"], dff)
        s["x2"] = _matmul(s["act"], fw["w_ffn_down"], "nn", F32, "mm_down", res=s["x1"])
        s["h3"] = _rms_fwd(s["x2"], norm_ple[i], "rms_fwd")
        s["gt"] = _matmul(s["h3"], fw["w_ple_gate"], "nn", BF16, "mm_gt")
        s["pp"] = _matmul(p[i, 0], fw["w_ple_proj"], "nn", BF16, "mm_pp")
        xc = _ple_fwd(s["x2"], s["gt"], s["pp"])
        saved.append(s)

    dx, g_norm_final, loss_part = _loss_head(xc, tgt, norm_final)

    small = {n: [None] * depth for n in _SMALL if n != "norm_final"}
    recv = {n: [None] * depth for n in _BIG}
    for i in reversed(range(depth)):
        fw, s = full[i], saved[i]
        dgt, dpp = _ple_bwd(dx, s["gt"], s["pp"])
        gw = {"w_ple_gate": _matmul(s["h3"], dgt, "tn", BF16, "mm_dw_gt"),
              "w_ple_proj": _matmul(p[i, 0], dpp, "tn", BF16, "mm_dw_pp")}
        dh3 = _matmul(dgt, fw["w_ple_gate"], "nt", F32, "mm_dh3")
        dx, small["norm_ple"][i] = _rms_bwd(s["x2"], dh3, norm_ple[i], dx, "rms_bwd")

        dact = _matmul(dx, fw["w_ffn_down"], "nt", BF16, "mm_dact")
        gw["w_ffn_down"] = _matmul(s["act"], dx, "tn", BF16, "mm_dw_down")
        dup, small["conv_ffn"][i], small["b_conv_ffn"][i] = _ffn_act_bwd(s["up"], dact, fw["conv_ffn"], s["bias"], dff)
        gw["w_ffn_up"] = _matmul(s["h2"], dup, "tn", BF16, "mm_dw_up")
        dh2 = _matmul(dup, fw["w_ffn_up"], "nt", F32, "mm_dh2")
        dx, small["norm_ffn"][i] = _rms_bwd(s["x1"], dh2, norm_ffn[i], dx, "rms_bwd")

        dmg = _matmul(dx, fw["w_out"], "nt", BF16, "mm_dmg")
        gw["w_out"] = _matmul(s["mg"], dx, "tn", BF16, "mm_dw_out")
        dya, dyb, dga, dgb = _merge_bwd(dmg, s["ya"], s["yb"], s["proj"], o_ga)
        gw["w_branch_a"] = _matmul(s["outa"], dya, "tn", BF16, "mm_dw_a")
        gw["w_branch_b"] = _matmul(s["outb"], dyb, "tn", BF16, "mm_dw_b")
        douta = _matmul(dya, fw["w_branch_a"], "nt", BF16, "mm_douta")
        doutb = _matmul(dyb, fw["w_branch_b"], "nt", BF16, "mm_doutb")
        dub, dvb, small["w_spatial"][i], db_s, dsg = _sgu_bwd(s["proj"], sgu_norm[i], w_spatial[i], s["b_t"], doutb, o_ub, wb)
        small["b_spatial"][i] = db_s[:, :, 0]
        small["sgu_norm"][i] = dsg
        do, dz, small["head_norm"][i] = _apost_bwd(s["o"], s["proj"], head_norm[i], douta, o_z, hv)
        dq, dk, dv, db_b, dg_b = _delta_bwd(s["qkvn"], s["beta_b"], s["gam_b"], s["gam_r"], s["s_all"], s["tm_all"], do, hqk, hv)
        dbg = _pad_lanes(jnp.concatenate([db_b[:, :, 0].T, dg_b[:, :, 0].T], axis=1))
        dba, dal, ddt = _gate_bwd(s["ba"], dbg, s["alog"], s["dtb"], hv)
        small["a_log"][i] = dal[:, hv:2 * hv]
        small["dt_bias"][i] = ddt[:, hv:2 * hv]
        dqkv_pre, small["conv_qkv"][i] = _qkv_bwd(s["proj"], jnp.concatenate([dq, dk, dv], axis=1), fw["conv_qkv"], hqk, hqk)
        dproj = jnp.concatenate([dqkv_pre, dz, dub, dvb, dga, dgb], axis=1)
        dw_main = _matmul(s["h1"], dproj, "tn", BF16, "mm_dw_main")
        dw_ba = _matmul(s["h1"], dba, "tn", BF16, "mm_dw_ba")
        gw["w_in"] = jnp.concatenate([dw_main[:, :o_ba], dw_ba[:, :2 * hv], dw_main[:, o_ba:]], axis=1)
        dh1 = _matmul(dproj, fw["w_main"], "nt", F32, "mm_dh1")
        dh1 = _matmul(dba, fw["w_ba"], "nt", F32, "mm_dh1_ba", res=dh1)
        dx, small["norm_mix"][i] = _rms_bwd(s["x0"], dh1, norm_mix[i], dx, "rms_bwd")

        parts = [(_split_cols(gw[n]) if n in _COL_SHARDED else _split_rows(gw[n])) for n in _BIG]
        for n, r in zip(_BIG, _exchange(parts, True, "scatter_grads")):
            recv[n][i] = r

    outs_g, outs_d, outs_m, outs_v = {}, {}, {}, {}

    for n in _BIG:
        shp = wts[n].shape
        parts = jnp.stack(recv[n], axis=1)
        parts = parts.reshape(N_DEV, shp[0] * shp[1], shp[2])
        two_d = lambda a: a.reshape(shp[0] * shp[1], shp[2])
        res = _adam(parts, two_d(wts[n]), two_d(mom_m[n]), two_d(mom_v[n]), "adam_" + n)
        outs_g[n], outs_d[n], outs_m[n], outs_v[n] = [r.reshape(shp) for r in res]

    stacked = {n: jnp.stack([jnp.reshape(a, (-1,)) for a in small[n]]) for n in small}
    stacked["norm_final"] = g_norm_final.reshape(-1)
    flat = [stacked[n].reshape(-1) for n in _SMALL] + [loss_part[0, :1]]
    sizes = [f.shape[0] for f in flat]
    total = sum(sizes)
    rows = -(-total // (LANES * 8)) * 8
    packed = jnp.pad(jnp.concatenate(flat), (0, rows * LANES - total)).reshape(rows, LANES)
    (got,) = _exchange([packed], False, "gather_small")
    got = got.reshape(N_DEV, rows * LANES)

    pieces_g, pieces_w, pieces_m, pieces_v = [], [], [], []
    off = 0
    for n, size in zip(_SMALL, sizes[:-1]):
        part = got[:, off:off + size]
        off += size
        if n in _CONVS:
            kw, cl = wts[n].shape[1], wts[n].shape[2]
            part = part.reshape(N_DEV, depth, kw, cl * N_DEV)
            part = lax.dynamic_slice_in_dim(part, me * cl, cl, axis=3).reshape(N_DEV, -1)
        pieces_g.append(part)
        pieces_w.append(wts[n].reshape(-1))
        pieces_m.append(mom_m[n].reshape(-1))
        pieces_v.append(mom_v[n].reshape(-1))
    pieces_g.append(got[:, off:off + 1])
    for lst in (pieces_w, pieces_m, pieces_v):
        lst.append(jnp.zeros((1,), F32))
    sizes2 = [a.shape[0] for a in pieces_w]
    total2 = sum(sizes2)
    rows2 = -(-total2 // (LANES * 16)) * 16
    pad2 = rows2 * LANES - total2
    pk = lambda lst: jnp.pad(jnp.concatenate(lst), (0, pad2)).reshape(rows2, LANES)
    parts = jnp.pad(jnp.concatenate(pieces_g, axis=1), ((0, 0), (0, pad2))).reshape(N_DEV, rows2, LANES)
    res = [r.reshape(-1) for r in _adam(parts, pk(pieces_w), pk(pieces_m), pk(pieces_v), "adam_small")]
    off = 0
    for n, size in zip(_SMALL, sizes2[:-1]):
        shp = wts[n].shape
        outs_g[n], outs_d[n], outs_m[n], outs_v[n] = [r[off:off + size].reshape(shp) for r in res]
        off += size
    loss = res[0][off]

    return (loss, dx[None], *[outs_g[n] for n in _WEIGHTS], *[outs_d[n] for n in _WEIGHTS],
            *[outs_m[n] for n in _WEIGHTS], *[outs_v[n] for n in _WEIGHTS])
```

```python
import functools
import math

import jax
import jax.numpy as jnp
from jax import lax
from jax.experimental import pallas as pl
from jax.experimental.pallas import tpu as pltpu

F32 = jnp.float32
BF16 = jnp.bfloat16
EPS = 1e-6
LANES = 128
HEAD = 128
CA = 64
N_DEV = 8
VMEM_LIMIT = 48 * 1024 * 1024
ADAM_VMEM_BUDGET = 16 * 1024 * 1024
MESH = pl.DeviceIdType.MESH
HI = lax.Precision.HIGHEST

ADAM_LR = 0.001
ADAM_B1 = 0.9
ADAM_B2 = 0.999
ADAM_EPS = 1e-08
ADAM_WD = 0.01
ADAM_STEP = 10


def _tile(n, cap, mult=LANES):
    best = None
    for t in range(mult, min(n, cap) + 1, mult):
        if n % t == 0:
            best = t
    return n if best is None else best


def _params(sem):
    return pltpu.CompilerParams(dimension_semantics=sem, vmem_limit_bytes=VMEM_LIMIT)


def _sigmoid(v):
    return jax.nn.sigmoid(v)


def _silu_grad(c):
    s = _sigmoid(c)
    return s + c * s * (1.0 - s)


_GELU_C = math.sqrt(2.0 / math.pi)


def _gelu(v):
    return 0.5 * v * (1.0 + jnp.tanh(_GELU_C * (v + 0.044715 * v * v * v)))


def _gelu_grad(v):
    t = jnp.tanh(_GELU_C * (v + 0.044715 * v * v * v))
    return 0.5 * (1.0 + t) + 0.5 * v * (1.0 - t * t) * _GELU_C * (1.0 + 3.0 * 0.044715 * v * v)


def _bdot(a, b, dims=((1,), (0,))):
    return lax.dot_general(a.astype(BF16), b.astype(BF16), (dims, ((), ())), preferred_element_type=F32)


_NT = ((1,), (1,))
_TN = ((0,), (0,))


def _hdot(a, b, dims=((1,), (0,))):
    return lax.dot_general(a, b, (dims, ((), ())), precision=HI, preferred_element_type=F32)


def _xchg_out_shapes(jobs):
    return [jax.ShapeDtypeStruct(a.shape if sc else (N_DEV,) + a.shape, a.dtype) for a, sc in jobs]


def _xchg_scratch(jobs):
    n = len(jobs)
    return [pltpu.SemaphoreType.DMA((n, N_DEV - 1)), pltpu.SemaphoreType.DMA((n, N_DEV - 1)), pltpu.SemaphoreType.DMA((n,))]


def _xchg_copies(scatter, src, out, sems):
    send_sems, recv_sems, local_sems = sems
    x, y, c = lax.axis_index("x"), lax.axis_index("y"), lax.axis_index("c")
    me = 4 * x + 2 * y + c
    local, sends, recvs = [], [], []
    for k, sc in enumerate(scatter):
        local.append(pltpu.make_async_copy(src[k].at[me] if sc else src[k], out[k].at[me], local_sems.at[k]))
    for m in range(1, N_DEV):
        px = lax.rem(x + ((m >> 2) & 1), 2)
        py = lax.rem(y + ((m >> 1) & 1), 2)
        pc = lax.rem(c + (m & 1), 2)
        peer = 4 * px + 2 * py + pc
        for k, sc in enumerate(scatter):
            mine = src[k].at[peer] if sc else src[k]
            for dst, lst in ((out[k].at[me], sends), (out[k].at[peer], recvs)):
                lst.append(pltpu.make_async_remote_copy(
                    src_ref=mine, dst_ref=dst, send_sem=send_sems.at[k, m - 1], recv_sem=recv_sems.at[k, m - 1],
                    device_id=(px, py, pc), device_id_type=MESH))
    return local, sends, recvs


def _xchg_start(scatter, src, out, sems):
    local, sends, _ = _xchg_copies(scatter, src, out, sems)
    for cp in local + sends:
        cp.start()


def _xchg_wait(scatter, src, out, sems):
    local, sends, recvs = _xchg_copies(scatter, src, out, sems)
    for cp in recvs:
        cp.wait_recv()
    for cp in sends:
        cp.wait_send()
    for cp in local:
        cp.wait()


_ANY = pl.BlockSpec(memory_space=pl.ANY)


def _exchange(jobs, name):
    n = len(jobs)
    scatter = [sc for _, sc in jobs]

    def body(*refs):
        src, out, sems = refs[:n], refs[n:2 * n], refs[2 * n:]
        _xchg_start(scatter, src, out, sems)
        _xchg_wait(scatter, src, out, sems)

    return pl.pallas_call(
        body, name=name, in_specs=[_ANY] * n, out_specs=[_ANY] * n, out_shape=_xchg_out_shapes(jobs),
        scratch_shapes=_xchg_scratch(jobs),
        compiler_params=pltpu.CompilerParams(has_side_effects=True))(*[a for a, _ in jobs])


def _carried(body, n_in, n_out, jobs, grid):
    if not jobs:
        return body
    nj = len(jobs)
    scatter = [sc for _, sc in jobs]

    def wrapped(*refs):
        ins, src = refs[:n_in], refs[n_in:n_in + nj]
        outs, got = refs[n_in + nj:n_in + nj + n_out], refs[n_in + nj + n_out:n_in + 2 * nj + n_out]
        rest = refs[n_in + 2 * nj + n_out:]
        scratch, sems = rest[:len(rest) - 3], rest[len(rest) - 3:]
        ids = [pl.program_id(a) for a in range(len(grid))]
        first = functools.reduce(jnp.logical_and, [i == 0 for i in ids])
        last = functools.reduce(jnp.logical_and, [i == g - 1 for i, g in zip(ids, grid)])

        @pl.when(first)
        def _():
            _xchg_start(scatter, src, got, sems)

        body(*ins, *outs, *scratch)

        @pl.when(last)
        def _():
            _xchg_wait(scatter, src, got, sems)

    return wrapped


def _call(body, name, grid, in_specs, out_specs, out_shape, args, scratch=(), sem=None, jobs=()):
    jobs = list(jobs)
    nj = len(jobs)
    sem = ("arbitrary",) * len(grid) if jobs or sem is None else sem
    res = pl.pallas_call(
        _carried(body, len(in_specs), len(out_specs), jobs, grid), name=name, grid=grid,
        in_specs=list(in_specs) + [_ANY] * nj, out_specs=list(out_specs) + [_ANY] * nj,
        out_shape=list(out_shape) + _xchg_out_shapes(jobs),
        scratch_shapes=list(scratch) + (_xchg_scratch(jobs) if jobs else []),
        compiler_params=_params(sem))(*args, *[a for a, _ in jobs])
    return res[:len(out_specs)], res[len(out_specs):]


MATMUL_OPERAND_VMEM = 20 * 1024 * 1024


def _matmul(a, b, mode, out_dtype, name, res=None, jobs=()):
    if mode == "tn":
        kdim, m = a.shape
    else:
        m, kdim = a.shape
    n = b.shape[0] if mode == "nt" else b.shape[1]
    tm, tn = _tile(m, 1024), _tile(n, 1024)
    per_k = 2 * (tm * a.dtype.itemsize + tn * b.dtype.itemsize)
    tk = _tile(kdim, max(LANES, MATMUL_OPERAND_VMEM // per_k))
    nk = kdim // tk
    dims = {"nn": ((1,), (0,)), "nt": _NT, "tn": _TN}[mode]

    def body(*refs):
        a_ref, b_ref = refs[:2]
        r_ref = refs[2] if res is not None else None
        o_ref = refs[3] if res is not None else refs[2]
        acc = refs[-1] if nk > 1 else None

        def write(r):
            if r_ref is not None:
                r = r + r_ref[...].astype(F32)
            o_ref[...] = r.astype(out_dtype)

        prod = _bdot(a_ref[...], b_ref[...], dims)
        if nk == 1:
            write(prod)
        else:
            k = pl.program_id(2)

            @pl.when(k == 0)
            def _():
                acc[...] = prod

            @pl.when(jnp.logical_and(k > 0, k < nk - 1))
            def _():
                acc[...] += prod

            @pl.when(k == nk - 1)
            def _():
                write(acc[...] + prod)

    a_spec = pl.BlockSpec((tk, tm), lambda i, j, k: (k, i)) if mode == "tn" else pl.BlockSpec((tm, tk), lambda i, j, k: (i, k))
    b_spec = pl.BlockSpec((tn, tk), lambda i, j, k: (j, k)) if mode == "nt" else pl.BlockSpec((tk, tn), lambda i, j, k: (k, j))
    o_spec = pl.BlockSpec((tm, tn), lambda i, j, k: (i, j))
    in_specs = [a_spec, b_spec] + ([o_spec] if res is not None else [])
    args = (a, b) + ((res,) if res is not None else ())
    (out,), got = _call(body, name, (m // tm, n // tn, nk), in_specs, [o_spec], [jax.ShapeDtypeStruct((m, n), out_dtype)],
                        args, scratch=[pltpu.VMEM((tm, tn), F32)] if nk > 1 else [],
                        sem=("parallel", "parallel", "arbitrary"), jobs=jobs)
    return (out, got) if jobs else out


def _rms_fwd(x, gain, name):
    t, d = x.shape
    tb = _tile(t, 256, 8)

    def body(x_ref, g_ref, h_ref):
        xv = x_ref[...]
        r = lax.rsqrt(jnp.mean(xv * xv, axis=-1, keepdims=True) + EPS)
        h_ref[...] = (xv * r * g_ref[...]).astype(BF16)

    return pl.pallas_call(
        body, name=name, grid=(t // tb,),
        in_specs=[pl.BlockSpec((tb, d), lambda i: (i, 0)), pl.BlockSpec((1, d), lambda i: (0, 0))],
        out_specs=pl.BlockSpec((tb, d), lambda i: (i, 0)), out_shape=jax.ShapeDtypeStruct((t, d), BF16),
        compiler_params=_params(("parallel",)))(x, gain.reshape(1, d))


def _rms_bwd(x, dh, gain, dres, name):
    t, d = x.shape
    tb = _tile(t, 256, 8)

    def body(x_ref, dh_ref, g_ref, dr_ref, dx_ref, dg_ref):
        @pl.when(pl.program_id(0) == 0)
        def _():
            dg_ref[...] = jnp.zeros_like(dg_ref)

        xv = x_ref[...]
        dy = dh_ref[...].astype(F32)
        r = lax.rsqrt(jnp.mean(xv * xv, axis=-1, keepdims=True) + EPS)
        xh = xv * r
        dxh = dy * g_ref[...]
        dx_ref[...] = dr_ref[...] + r * (dxh - xh * jnp.mean(dxh * xh, axis=-1, keepdims=True))
        dg_ref[...] += jnp.sum(dy * xh, axis=0, keepdims=True)

    row = pl.BlockSpec((tb, d), lambda i: (i, 0))
    vec = pl.BlockSpec((1, d), lambda i: (0, 0))
    return pl.pallas_call(
        body, name=name, grid=(t // tb,), in_specs=[row, row, vec, row], out_specs=[row, vec],
        out_shape=[jax.ShapeDtypeStruct((t, d), F32), jax.ShapeDtypeStruct((1, d), F32)],
        compiler_params=_params(("arbitrary",)))(x, dh, gain.reshape(1, d), dres)


def _loss_head(x, target, gain):
    t, d = x.shape
    tb = _tile(t, 256, 8)

    def body(x_ref, t_ref, g_ref, dx_ref, dg_ref, loss_ref):
        @pl.when(pl.program_id(0) == 0)
        def _():
            dg_ref[...] = jnp.zeros_like(dg_ref)
            loss_ref[...] = jnp.zeros_like(loss_ref)

        xv = x_ref[...]
        r = lax.rsqrt(jnp.mean(xv * xv, axis=-1, keepdims=True) + EPS)
        xh = xv * r
        err = xh * g_ref[...] - t_ref[...]
        per_row = jnp.mean(err * err, axis=-1, keepdims=True)
        loss_ref[...] += 0.5 * jnp.sum(per_row, axis=0, keepdims=True)
        dy = err * (1.0 / d)
        dxh = dy * g_ref[...]
        dx_ref[...] = r * (dxh - xh * jnp.mean(dxh * xh, axis=-1, keepdims=True))
        dg_ref[...] += jnp.sum(dy * xh, axis=0, keepdims=True)

    row = pl.BlockSpec((tb, d), lambda i: (i, 0))
    vec = pl.BlockSpec((1, d), lambda i: (0, 0))
    return pl.pallas_call(
        body, name="loss_head", grid=(t // tb,), in_specs=[row, row, vec],
        out_specs=[row, vec, pl.BlockSpec((1, LANES), lambda i: (0, 0))],
        out_shape=[jax.ShapeDtypeStruct((t, d), F32), jax.ShapeDtypeStruct((1, d), F32),
                   jax.ShapeDtypeStruct((1, LANES), F32)],
        compiler_params=_params(("arbitrary",)))(x, target, gain.reshape(1, d))


def _shift_down(v, s, rows):
    if s == 0:
        return v
    return jnp.where(rows >= s, pltpu.roll(v, s, 0), 0.0)


def _shift_up(v, s, rows):
    if s == 0:
        return v
    t = v.shape[0]
    return jnp.where(rows < t - s, pltpu.roll(v, t - s, 0), 0.0)


def _conv(v, w, rows):
    k = w.shape[0]
    out = v * w[k - 1:k, :]
    for s in range(1, k):
        out = out + _shift_down(v, s, rows) * w[k - 1 - s:k - s, :]
    return out


def _qkv_fwd(proj, conv_w, nq, nk):
    t = proj.shape[0]
    cw = conv_w.shape[1]
    nblk = cw // HEAD

    def body(p_ref, w_ref, o_ref):
        j = pl.program_id(0)
        rows = lax.broadcasted_iota(jnp.int32, (t, HEAD), 0)
        c = _conv(p_ref[...].astype(F32), w_ref[...], rows)
        a = c * _sigmoid(c)
        nrm = a * lax.rsqrt(jnp.sum(a * a, axis=-1, keepdims=True) + EPS)
        nrm = nrm * jnp.where(j < nq, HEAD ** -0.5, 1.0)
        o_ref[...] = jnp.where(j < nq + nk, nrm, a).astype(BF16)

    return pl.pallas_call(
        body, name="qkv_fwd", grid=(nblk,),
        in_specs=[pl.BlockSpec((t, HEAD), lambda j: (0, j)), pl.BlockSpec((conv_w.shape[0], HEAD), lambda j: (0, j))],
        out_specs=pl.BlockSpec((t, HEAD), lambda j: (0, j)), out_shape=jax.ShapeDtypeStruct((t, cw), BF16),
        compiler_params=_params(("parallel",)))(proj, conv_w)


def _qkv_bwd(proj, dqkv, conv_w, nq, nk):
    t = proj.shape[0]
    kw, cw = conv_w.shape
    nblk = cw // HEAD

    def body(p_ref, d_ref, w_ref, dp_ref, dw_ref):
        j = pl.program_id(0)
        rows = lax.broadcasted_iota(jnp.int32, (t, HEAD), 0)
        xv = p_ref[...].astype(F32)
        w = w_ref[...]
        c = _conv(xv, w, rows)
        a = c * _sigmoid(c)
        dy = d_ref[...].astype(F32)
        r = lax.rsqrt(jnp.sum(a * a, axis=-1, keepdims=True) + EPS)
        y = a * r
        scale = jnp.where(j < nq, HEAD ** -0.5, 1.0)
        da_n = scale * r * (dy - y * jnp.sum(dy * y, axis=-1, keepdims=True))
        da = jnp.where(j < nq + nk, da_n, dy)
        dc = da * _silu_grad(c)
        dx = dc * w[kw - 1:kw, :]
        dw_ref[kw - 1:kw, :] = jnp.sum(dc * xv, axis=0, keepdims=True)
        for s in range(1, kw):
            dx = dx + _shift_up(dc, s, rows) * w[kw - 1 - s:kw - s, :]
            dw_ref[kw - 1 - s:kw - s, :] = jnp.sum(dc * _shift_down(xv, s, rows), axis=0, keepdims=True)
        dp_ref[...] = dx.astype(BF16)

    blk = pl.BlockSpec((t, HEAD), lambda j: (0, j))
    wblk = pl.BlockSpec((kw, HEAD), lambda j: (0, j))
    return pl.pallas_call(
        body, name="qkv_bwd", grid=(nblk,), in_specs=[blk, blk, wblk], out_specs=[blk, wblk],
        out_shape=[jax.ShapeDtypeStruct((t, cw), BF16), jax.ShapeDtypeStruct((kw, cw), F32)],
        compiler_params=_params(("parallel",)))(proj, dqkv, conv_w)


def _softplus(v):
    return jnp.where(v < -15.0, jnp.exp(v), jnp.maximum(v, 0.0) + jnp.log(1.0 + jnp.exp(-jnp.abs(v))))


def _gate_fwd(ba, alog_pad, dtb_pad, hv):
    t = ba.shape[0]
    tb = _tile(t, 512, CA)

    def body(ba_ref, al_ref, dt_ref, o_ref):
        v = ba_ref[...]
        beta = _sigmoid(v)
        g = -jnp.exp(al_ref[...]) * _softplus(v + dt_ref[...])
        pos = lax.broadcasted_iota(jnp.int32, (tb, LANES), 0) % CA
        s = 1
        while s < CA:
            g = g + jnp.where(pos >= s, pltpu.roll(g, s, 0), 0.0)
            s *= 2
        lane = lax.broadcasted_iota(jnp.int32, (tb, LANES), 1)
        o_ref[...] = jnp.where(lane < hv, beta, g)

    row = pl.BlockSpec((tb, LANES), lambda i: (i, 0))
    vec = pl.BlockSpec((1, LANES), lambda i: (0, 0))
    return pl.pallas_call(
        body, name="gate_fwd", grid=(t // tb,), in_specs=[row, vec, vec], out_specs=row,
        out_shape=jax.ShapeDtypeStruct((t, LANES), F32), compiler_params=_params(("parallel",)))(ba, alog_pad, dtb_pad)


def _gate_bwd(ba, dbg, alog_pad, dtb_pad, hv):
    t = ba.shape[0]
    tb = _tile(t, 512, CA)

    def body(ba_ref, d_ref, al_ref, dt_ref, dba_ref, dal_ref, ddt_ref):
        @pl.when(pl.program_id(0) == 0)
        def _():
            dal_ref[...] = jnp.zeros_like(dal_ref)
            ddt_ref[...] = jnp.zeros_like(ddt_ref)

        v = ba_ref[...]
        d = d_ref[...]
        pos = lax.broadcasted_iota(jnp.int32, (tb, LANES), 0) % CA
        dg = d
        s = 1
        while s < CA:
            dg = dg + jnp.where(pos < CA - s, pltpu.roll(dg, tb - s, 0), 0.0)
            s *= 2
        beta = _sigmoid(v)
        na = -jnp.exp(al_ref[...])
        z = v + dt_ref[...]
        da = dg * na * _sigmoid(z)
        lane = lax.broadcasted_iota(jnp.int32, (tb, LANES), 1)
        in_a = jnp.logical_and(lane >= hv, lane < 2 * hv)
        da = jnp.where(in_a, da, 0.0)
        dba_ref[...] = jnp.where(lane < hv, d * beta * (1.0 - beta), da)
        ddt_ref[...] += jnp.sum(da, axis=0, keepdims=True)
        dal_ref[...] += jnp.sum(jnp.where(in_a, dg * na * _softplus(z), 0.0), axis=0, keepdims=True)

    row = pl.BlockSpec((tb, LANES), lambda i: (i, 0))
    vec = pl.BlockSpec((1, LANES), lambda i: (0, 0))
    return pl.pallas_call(
        body, name="gate_bwd", grid=(t // tb,), in_specs=[row, row, vec, vec], out_specs=[row, vec, vec],
        out_shape=[jax.ShapeDtypeStruct((t, LANES), F32), jax.ShapeDtypeStruct((1, LANES), F32),
                   jax.ShapeDtypeStruct((1, LANES), F32)],
        compiler_params=_params(("arbitrary",)))(ba, dbg, alog_pad, dtb_pad)


def _chunk_masks():
    r = lax.broadcasted_iota(jnp.int32, (CA, CA), 0)
    c = lax.broadcasted_iota(jnp.int32, (CA, CA), 1)
    return r >= c, r > c, (r == c).astype(F32)


def _inv_unit_lower(a, eye):
    x = eye - a
    p = a
    n = 2
    while n < CA:
        p = _hdot(p, p)
        x = x + _hdot(x, p)
        n *= 2
    return x


def _delta_pre(q, k, v, bcol, gc, gr, gl, causal, strict):
    eg = jnp.exp(gc)
    dm = jnp.exp(jnp.where(causal, gc[:, :CA] - gr, -jnp.inf))
    kb = k * bcol
    kkb = _bdot(kb, k, _NT)
    a = jnp.where(strict, kkb * dm, 0.0)
    rhs = jnp.concatenate([v * bcol, kb * eg], axis=1)
    qk = _bdot(q, k, _NT)
    ekd = jnp.exp(gl - gc)
    return dict(eg=eg, dm=dm, kb=kb, kkb=kkb, a=a, rhs=rhs, p=qk * dm, qd=q * eg, ekd=ekd, kd=k * ekd, cd=jnp.exp(gl))


def _delta_fwd(qkvn, beta_b, gam_b, gam_r, hqk, hv, jobs=()):
    t = qkvn.shape[0]
    rep = hv // hqk
    rb = _tile(t, 512, CA)
    nb = t // rb
    ncb = rb // CA
    nc = t // CA

    def body(q_ref, k_ref, v_ref, b_ref, gc_ref, gr_ref, o_ref, s_ref, tm_ref, state):
        @pl.when(pl.program_id(1) == 0)
        def _():
            state[...] = jnp.zeros_like(state)

        causal, strict, eye = _chunk_masks()

        def chunk(n, carry):
            base = pl.multiple_of(n * CA, CA)
            rows = pl.ds(base, CA)
            q = q_ref[rows, :].astype(F32)
            k = k_ref[rows, :].astype(F32)
            for h in range(rep):
                v = v_ref[rows, h * HEAD:(h + 1) * HEAD].astype(F32)
                gl = gc_ref[h, pl.ds(base + CA - 1, 1), :]
                pre = _delta_pre(q, k, v, b_ref[h, rows, :], gc_ref[h, rows, :], gr_ref[h, pl.ds(n, 1), :], gl,
                                 causal, strict)
                tm = _inv_unit_lower(pre["a"], eye)
                sol = _hdot(tm, pre["rhs"])
                s = state[h]
                v_new = sol[:, :HEAD] - _bdot(sol[:, HEAD:], s)
                o_ref[rows, h * HEAD:(h + 1) * HEAD] = _bdot(pre["qd"], s) + _bdot(pre["p"], v_new)
                s_ref[h, n] = s.astype(BF16)
                tm_ref[h, rows, :] = tm
                state[h] = s * pre["cd"] + _bdot(pre["kd"], v_new, _TN)
            return carry

        lax.fori_loop(0, ncb, chunk, 0)

    qoff, koff, voff = 0, hqk, 2 * hqk // rep
    return _call(
        body, "delta_fwd", (hqk, nb),
        [pl.BlockSpec((rb, HEAD), lambda j, i: (i, qoff + j)),
         pl.BlockSpec((rb, HEAD), lambda j, i: (i, koff + j)),
         pl.BlockSpec((rb, rep * HEAD), lambda j, i: (i, voff + j)),
         pl.BlockSpec((rep, rb, LANES), lambda j, i: (j, i, 0)),
         pl.BlockSpec((rep, rb, LANES), lambda j, i: (j, i, 0)),
         pl.BlockSpec((rep, ncb, CA), lambda j, i: (j, i, 0))],
        [pl.BlockSpec((rb, rep * HEAD), lambda j, i: (i, j)),
         pl.BlockSpec((rep, ncb, HEAD, HEAD), lambda j, i: (j, i, 0, 0)),
         pl.BlockSpec((rep, rb, CA), lambda j, i: (j, i, 0))],
        [jax.ShapeDtypeStruct((t, hv * HEAD), F32), jax.ShapeDtypeStruct((hv, nc, HEAD, HEAD), BF16),
         jax.ShapeDtypeStruct((hv, t, CA), F32)],
        (qkvn, qkvn, qkvn, beta_b, gam_b, gam_r), scratch=[pltpu.VMEM((rep, HEAD, HEAD), F32)],
        sem=("parallel", "arbitrary"), jobs=jobs)


def _delta_bwd(qkvn, beta_b, gam_b, gam_r, s_all, tm_all, do, hqk, hv, jobs=()):
    t = qkvn.shape[0]
    rep = hv // hqk
    rb = _tile(t, 512, CA)
    nb = t // rb
    ncb = rb // CA

    def body(q_ref, k_ref, v_ref, b_ref, gc_ref, gr_ref, s_ref, tm_ref, do_ref,
             dq_ref, dk_ref, dv_ref, db_ref, dg_ref, dstate):
        @pl.when(pl.program_id(1) == 0)
        def _():
            dstate[...] = jnp.zeros_like(dstate)

        causal, strict, _ = _chunk_masks()
        ones = jnp.ones((CA, LANES), F32)
        last = lax.broadcasted_iota(jnp.int32, (CA, LANES), 0) == CA - 1

        def rowsum(m):
            return jnp.sum(m, axis=1, keepdims=True)

        def chunk(it, carry):
            n = ncb - 1 - it
            base = pl.multiple_of(n * CA, CA)
            rows = pl.ds(base, CA)
            q = q_ref[rows, :].astype(F32)
            k = k_ref[rows, :].astype(F32)
            kkr = _bdot(k, k, _NT)
            dq = jnp.zeros((CA, HEAD), F32)
            dk = jnp.zeros((CA, HEAD), F32)
            for h in range(rep):
                v = v_ref[rows, h * HEAD:(h + 1) * HEAD].astype(F32)
                bcol = b_ref[h, rows, :]
                gl = gc_ref[h, pl.ds(base + CA - 1, 1), :]
                pre = _delta_pre(q, k, v, bcol, gc_ref[h, rows, :], gr_ref[h, pl.ds(n, 1), :], gl, causal, strict)
                eg, dm, kb, p, qd, kd, cd = pre["eg"], pre["dm"], pre["kb"], pre["p"], pre["qd"], pre["kd"], pre["cd"]
                tm = tm_ref[h, rows, :]
                sol = _hdot(tm, pre["rhs"])
                w = sol[:, HEAD:]
                s = s_ref[h, n].astype(F32)
                ds = dstate[h]
                dov = do_ref[rows, h * HEAD:(h + 1) * HEAD].astype(F32)
                v_new = sol[:, :HEAD] - _bdot(w, s)

                dvn = _bdot(p, dov, _TN) + _bdot(kd, ds)
                dp = jnp.where(causal, _bdot(dov, v_new, _NT), 0.0)
                dqd = _bdot(dov, s, _NT)
                dkd = _bdot(v_new, ds, _NT)
                dcd = jnp.sum(rowsum(s * ds), axis=0, keepdims=True)
                dw = -_bdot(dvn, s, _NT)
                dstate[h] = ds * cd + _bdot(qd, dov, _TN) - _bdot(w, dvn, _TN)

                drhs = _hdot(tm, jnp.concatenate([dvn, dw], axis=1), _TN)
                dbv, dbke = drhs[:, :HEAD], drhs[:, HEAD:]
                da = -jnp.where(strict, _bdot(drhs, sol, _NT), 0.0)
                m = da * dm
                e = m * pre["kkb"] + dp * p
                dgam = rowsum(e) - _hdot(e, ones, _TN) + rowsum(dbke * kb * eg) + rowsum(dqd * qd)
                r = rowsum(dkd * kd)
                tot = jnp.sum(r, axis=0, keepdims=True) + dcd * cd
                dgam = dgam - r + jnp.where(last, tot, 0.0)
                dbeta = rowsum(m * kkr) + rowsum(dbv * v) + rowsum(dbke * eg * k)
                nm = m * bcol[:, :CA]
                dqk = dp * dm
                dq = dq + _bdot(dqk, k) + eg * dqd
                dk = dk + _bdot(nm, k) + _bdot(nm, k, _TN) + _bdot(dqk, q, _TN) + bcol * eg * dbke + pre["ekd"] * dkd
                dv_ref[rows, h * HEAD:(h + 1) * HEAD] = bcol * dbv
                db_ref[h, rows, :] = jnp.broadcast_to(dbeta, (CA, LANES))
                dg_ref[h, rows, :] = jnp.broadcast_to(dgam, (CA, LANES))
            dq_ref[rows, :] = dq
            dk_ref[rows, :] = dk
            return carry

        lax.fori_loop(0, ncb, chunk, 0)

    qoff, koff, voff = 0, hqk, 2 * hqk // rep
    rv = lambda i: nb - 1 - i
    hd = pl.BlockSpec((rep, rb, LANES), lambda j, i: (j, rv(i), 0))
    qk_out = pl.BlockSpec((rb, HEAD), lambda j, i: (rv(i), j))
    v_blk = pl.BlockSpec((rb, rep * HEAD), lambda j, i: (rv(i), j))
    return _call(
        body, "delta_bwd", (hqk, nb),
        [pl.BlockSpec((rb, HEAD), lambda j, i: (rv(i), qoff + j)),
         pl.BlockSpec((rb, HEAD), lambda j, i: (rv(i), koff + j)),
         pl.BlockSpec((rb, rep * HEAD), lambda j, i: (rv(i), voff + j)),
         hd, hd,
         pl.BlockSpec((rep, ncb, CA), lambda j, i: (j, rv(i), 0)),
         pl.BlockSpec((rep, ncb, HEAD, HEAD), lambda j, i: (j, rv(i), 0, 0)),
         pl.BlockSpec((rep, rb, CA), lambda j, i: (j, rv(i), 0)),
         v_blk],
        [qk_out, qk_out, v_blk, hd, hd],
        [jax.ShapeDtypeStruct((t, hqk * HEAD), F32), jax.ShapeDtypeStruct((t, hqk * HEAD), F32),
         jax.ShapeDtypeStruct((t, hv * HEAD), F32),
         jax.ShapeDtypeStruct((hv, t, LANES), F32), jax.ShapeDtypeStruct((hv, t, LANES), F32)],
        (qkvn, qkvn, qkvn, beta_b, gam_b, gam_r, s_all, tm_all, do), scratch=[pltpu.VMEM((rep, HEAD, HEAD), F32)],
        sem=("parallel", "arbitrary"), jobs=jobs)


def _apost_fwd(o, proj, gain, zoff, hv):
    t = o.shape[0]
    tb = _tile(t, 1024, 8)
    zb = zoff // HEAD

    def body(o_ref, z_ref, g_ref, y_ref):
        ov = o_ref[...]
        z = z_ref[...].astype(F32)
        r = lax.rsqrt(jnp.mean(ov * ov, axis=-1, keepdims=True) + EPS)
        y_ref[...] = (ov * r * g_ref[...] * (z * _sigmoid(z))).astype(BF16)

    blk = pl.BlockSpec((tb, HEAD), lambda i, h: (i, h))
    return pl.pallas_call(
        body, name="apost_fwd", grid=(t // tb, hv),
        in_specs=[blk, pl.BlockSpec((tb, HEAD), lambda i, h: (i, zb + h)), pl.BlockSpec((1, HEAD), lambda i, h: (0, 0))],
        out_specs=blk, out_shape=jax.ShapeDtypeStruct((t, hv * HEAD), BF16),
        compiler_params=_params(("parallel", "parallel")))(o, proj, gain.reshape(1, HEAD))


def _apost_bwd(o, proj, gain, dy, zoff, hv):
    t = o.shape[0]
    tb = _tile(t, 1024, 8)
    zb = zoff // HEAD

    def body(o_ref, z_ref, g_ref, dy_ref, do_ref, dz_ref, dg_ref):
        @pl.when(jnp.logical_and(pl.program_id(0) == 0, pl.program_id(1) == 0))
        def _():
            dg_ref[...] = jnp.zeros_like(dg_ref)

        ov = o_ref[...]
        z = z_ref[...].astype(F32)
        d = dy_ref[...].astype(F32)
        r = lax.rsqrt(jnp.mean(ov * ov, axis=-1, keepdims=True) + EPS)
        oh = ov * r
        sz = z * _sigmoid(z)
        dn = d * sz
        dz_ref[...] = (d * oh * g_ref[...] * _silu_grad(z)).astype(BF16)
        doh = dn * g_ref[...]
        do_ref[...] = r * (doh - oh * jnp.mean(doh * oh, axis=-1, keepdims=True))
        dg_ref[...] += jnp.sum(dn * oh, axis=0, keepdims=True)

    blk = pl.BlockSpec((tb, HEAD), lambda i, h: (i, h))
    vec = pl.BlockSpec((1, HEAD), lambda i, h: (0, 0))
    return pl.pallas_call(
        body, name="apost_bwd", grid=(t // tb, hv),
        in_specs=[blk, pl.BlockSpec((tb, HEAD), lambda i, h: (i, zb + h)), vec, blk],
        out_specs=[blk, blk, vec],
        out_shape=[jax.ShapeDtypeStruct((t, hv * HEAD), F32), jax.ShapeDtypeStruct((t, hv * HEAD), BF16),
                   jax.ShapeDtypeStruct((1, HEAD), F32)],
        compiler_params=_params(("arbitrary", "arbitrary")))(o, proj, gain.reshape(1, HEAD), dy)


def _sgu_fwd(proj, gain, w_s, b_t, uoff, wb):
    t = proj.shape[0]
    ng = wb // HEAD

    def body(u_ref, v_ref, g_ref, w_ref, b_ref, o_ref):
        r_i = lax.broadcasted_iota(jnp.int32, (HEAD, HEAD), 0)
        c_i = lax.broadcasted_iota(jnp.int32, (HEAD, HEAD), 1)
        u = _gelu(u_ref[...].astype(F32))
        vg = _gelu(v_ref[...].astype(F32))
        vn = vg * lax.rsqrt(jnp.mean(vg * vg, axis=-1, keepdims=True) + EPS) * g_ref[...]
        for g in range(ng):
            cols = slice(g * HEAD, (g + 1) * HEAD)
            wg = jnp.where(r_i >= c_i, w_ref[g], 0.0)
            mixed = _bdot(wg, vn[:, cols]) + b_ref[:, g:g + 1]
            o_ref[:, cols] = (u[:, cols] * mixed).astype(BF16)

    ub, vb = uoff // wb, uoff // wb + 1
    return pl.pallas_call(
        body, name="sgu_fwd", grid=(t // HEAD,),
        in_specs=[pl.BlockSpec((HEAD, wb), lambda i: (i, ub)), pl.BlockSpec((HEAD, wb), lambda i: (i, vb)),
                  pl.BlockSpec((1, wb), lambda i: (0, 0)), pl.BlockSpec((ng, HEAD, HEAD), lambda i: (0, 0, 0)),
                  pl.BlockSpec((HEAD, ng), lambda i: (0, 0))],
        out_specs=pl.BlockSpec((HEAD, wb), lambda i: (i, 0)), out_shape=jax.ShapeDtypeStruct((t, wb), BF16),
        compiler_params=_params(("parallel",)))(proj, proj, gain.reshape(1, wb), w_s, b_t)


def _sgu_bwd(proj, gain, w_s, b_t, dout, uoff, wb):
    t = proj.shape[0]
    ng = wb // HEAD

    def body(u_ref, v_ref, g_ref, w_ref, b_ref, d_ref, du_ref, dv_ref, dw_ref, db_ref, dg_ref, dvn_ref):
        @pl.when(pl.program_id(0) == 0)
        def _():
            dw_ref[...] = jnp.zeros_like(dw_ref)
            db_ref[...] = jnp.zeros_like(db_ref)
            dg_ref[...] = jnp.zeros_like(dg_ref)

        r_i = lax.broadcasted_iota(jnp.int32, (HEAD, HEAD), 0)
        c_i = lax.broadcasted_iota(jnp.int32, (HEAD, HEAD), 1)
        tril = r_i >= c_i
        ub = u_ref[...].astype(F32)
        vb = v_ref[...].astype(F32)
        u = _gelu(ub)
        vg = _gelu(vb)
        r = lax.rsqrt(jnp.mean(vg * vg, axis=-1, keepdims=True) + EPS)
        vh = vg * r
        vn = vh * g_ref[...]
        d = d_ref[...].astype(F32)
        for g in range(ng):
            cols = slice(g * HEAD, (g + 1) * HEAD)
            wg = jnp.where(tril, w_ref[g], 0.0)
            mixed = _bdot(wg, vn[:, cols]) + b_ref[:, g:g + 1]
            du_ref[:, cols] = (d[:, cols] * mixed * _gelu_grad(ub[:, cols])).astype(BF16)
            dmix = d[:, cols] * u[:, cols]
            dw_ref[g] += jnp.where(tril, _bdot(dmix, vn[:, cols], _NT), 0.0)
            db_ref[g] += jnp.broadcast_to(jnp.sum(dmix, axis=1, keepdims=True), (HEAD, HEAD))
            dvn_ref[:, cols] = _bdot(wg, dmix, _TN)
        dvn = dvn_ref[...]
        dg_ref[...] += jnp.sum(dvn * vh, axis=0, keepdims=True)
        dvh = dvn * g_ref[...]
        dvg = r * (dvh - vh * jnp.mean(dvh * vh, axis=-1, keepdims=True))
        dv_ref[...] = (dvg * _gelu_grad(vb)).astype(BF16)

    ub_i, vb_i = uoff // wb, uoff // wb + 1
    row = pl.BlockSpec((HEAD, wb), lambda i: (i, 0))
    mat = pl.BlockSpec((ng, HEAD, HEAD), lambda i: (0, 0, 0))
    vec = pl.BlockSpec((1, wb), lambda i: (0, 0))
    return pl.pallas_call(
        body, name="sgu_bwd", grid=(t // HEAD,),
        in_specs=[pl.BlockSpec((HEAD, wb), lambda i: (i, ub_i)), pl.BlockSpec((HEAD, wb), lambda i: (i, vb_i)),
                  vec, mat, pl.BlockSpec((HEAD, ng), lambda i: (0, 0)), row],
        out_specs=[row, row, mat, mat, vec],
        out_shape=[jax.ShapeDtypeStruct((t, wb), BF16), jax.ShapeDtypeStruct((t, wb), BF16),
                   jax.ShapeDtypeStruct((ng, HEAD, HEAD), F32), jax.ShapeDtypeStruct((ng, HEAD, HEAD), F32),
                   jax.ShapeDtypeStruct((1, wb), F32)],
        scratch_shapes=[pltpu.VMEM((HEAD, wb), F32)],
        compiler_params=_params(("arbitrary",)))(proj, proj, gain.reshape(1, wb), w_s, b_t, dout)


def _merge_specs(t, d, goff):
    tb = _tile(t, 512, 8)
    tc = _tile(d, 512)
    gb = goff // tc
    nd = d // tc
    blk = pl.BlockSpec((tb, tc), lambda i, j: (i, j))
    ga = pl.BlockSpec((tb, tc), lambda i, j: (i, gb + j))
    gbs = pl.BlockSpec((tb, tc), lambda i, j: (i, gb + nd + j))
    return (t // tb, nd), blk, ga, gbs


def _merge_fwd(ya, yb, proj, goff):
    t, d = ya.shape
    grid, blk, ga, gbs = _merge_specs(t, d, goff)

    def body(ya_ref, yb_ref, ga_ref, gb_ref, o_ref):
        o_ref[...] = (_sigmoid(ga_ref[...].astype(F32)) * ya_ref[...].astype(F32)
                      + _sigmoid(gb_ref[...].astype(F32)) * yb_ref[...].astype(F32)).astype(BF16)

    return pl.pallas_call(
        body, name="merge_fwd", grid=grid, in_specs=[blk, blk, ga, gbs], out_specs=blk,
        out_shape=jax.ShapeDtypeStruct((t, d), BF16),
        compiler_params=_params(("parallel", "parallel")))(ya, yb, proj, proj)


def _merge_bwd(dm, ya, yb, proj, goff):
    t, d = ya.shape
    grid, blk, ga, gbs = _merge_specs(t, d, goff)

    def body(dm_ref, ya_ref, yb_ref, ga_ref, gb_ref, dya_ref, dyb_ref, dga_ref, dgb_ref):
        dmv = dm_ref[...].astype(F32)
        sa = _sigmoid(ga_ref[...].astype(F32))
        sb = _sigmoid(gb_ref[...].astype(F32))
        dya_ref[...] = (dmv * sa).astype(BF16)
        dyb_ref[...] = (dmv * sb).astype(BF16)
        dga_ref[...] = (dmv * ya_ref[...].astype(F32) * sa * (1.0 - sa)).astype(BF16)
        dgb_ref[...] = (dmv * yb_ref[...].astype(F32) * sb * (1.0 - sb)).astype(BF16)

    shp = jax.ShapeDtypeStruct((t, d), BF16)
    return pl.pallas_call(
        body, name="merge_bwd", grid=grid, in_specs=[blk, blk, blk, ga, gbs], out_specs=[blk] * 4,
        out_shape=[shp] * 4, compiler_params=_params(("parallel", "parallel")))(dm, ya, yb, proj, proj)


def _ffn_act_fwd(up, conv_w, bias, dff):
    t = up.shape[0]
    nblk = dff // HEAD
    kw = conv_w.shape[0]

    def body(g_ref, v_ref, wg_ref, wv_ref, bg_ref, bv_ref, o_ref):
        rows = lax.broadcasted_iota(jnp.int32, (t, HEAD), 0)
        cg = _conv(g_ref[...].astype(F32), wg_ref[...], rows) + bg_ref[...]
        cv = _conv(v_ref[...].astype(F32), wv_ref[...], rows) + bv_ref[...]
        o_ref[...] = (cg * _sigmoid(cg) * cv).astype(BF16)

    return pl.pallas_call(
        body, name="ffn_act_fwd", grid=(nblk,),
        in_specs=[pl.BlockSpec((t, HEAD), lambda j: (0, j)), pl.BlockSpec((t, HEAD), lambda j: (0, nblk + j)),
                  pl.BlockSpec((kw, HEAD), lambda j: (0, j)), pl.BlockSpec((kw, HEAD), lambda j: (0, nblk + j)),
                  pl.BlockSpec((1, HEAD), lambda j: (0, j)), pl.BlockSpec((1, HEAD), lambda j: (0, nblk + j))],
        out_specs=pl.BlockSpec((t, HEAD), lambda j: (0, j)), out_shape=jax.ShapeDtypeStruct((t, dff), BF16),
        compiler_params=_params(("parallel",)))(up, up, conv_w, conv_w, bias, bias)


def _ffn_act_bwd(up, dact, conv_w, bias, dff, jobs=()):
    t = up.shape[0]
    nblk = dff // HEAD
    kw = conv_w.shape[0]

    def body(me_ref, pa_ref, d_ref, wm_ref, wp_ref, bm_ref, bp_ref, dup_ref, dw_ref, db_ref):
        is_gate = pl.program_id(0) < nblk
        rows = lax.broadcasted_iota(jnp.int32, (t, HEAD), 0)
        xv = me_ref[...].astype(F32)
        w = wm_ref[...]
        cm = _conv(xv, w, rows) + bm_ref[...]
        cp = _conv(pa_ref[...].astype(F32), wp_ref[...], rows) + bp_ref[...]
        d = d_ref[...].astype(F32)
        dc = jnp.where(is_gate, d * cp * _silu_grad(cm), d * (cp * _sigmoid(cp)))
        db_ref[...] = jnp.sum(dc, axis=0, keepdims=True)
        dx = dc * w[kw - 1:kw, :]
        dw_ref[kw - 1:kw, :] = jnp.sum(dc * xv, axis=0, keepdims=True)
        for s in range(1, kw):
            dx = dx + _shift_up(dc, s, rows) * w[kw - 1 - s:kw - s, :]
            dw_ref[kw - 1 - s:kw - s, :] = jnp.sum(dc * _shift_down(xv, s, rows), axis=0, keepdims=True)
        dup_ref[...] = dx.astype(BF16)

    part = lambda j: (j + nblk) % (2 * nblk)
    me = pl.BlockSpec((t, HEAD), lambda j: (0, j))
    wme = pl.BlockSpec((kw, HEAD), lambda j: (0, j))
    bme = pl.BlockSpec((1, HEAD), lambda j: (0, j))
    return _call(
        body, "ffn_act_bwd", (2 * nblk,),
        [me, pl.BlockSpec((t, HEAD), lambda j: (0, part(j))), pl.BlockSpec((t, HEAD), lambda j: (0, j % nblk)),
         wme, pl.BlockSpec((kw, HEAD), lambda j: (0, part(j))),
         bme, pl.BlockSpec((1, HEAD), lambda j: (0, part(j)))],
        [me, wme, bme],
        [jax.ShapeDtypeStruct((t, 2 * dff), BF16), jax.ShapeDtypeStruct((kw, 2 * dff), F32),
         jax.ShapeDtypeStruct((1, 2 * dff), F32)],
        (up, up, dact, conv_w, conv_w, bias, bias), sem=("parallel",), jobs=jobs)


def _ple_fwd(x, gt, pp):
    t, d = x.shape
    tb, tc = _tile(t, 512, 8), _tile(d, 1024)

    def body(x_ref, g_ref, p_ref, o_ref):
        o_ref[...] = x_ref[...] + _sigmoid(g_ref[...].astype(F32)) * p_ref[...].astype(F32)

    blk = pl.BlockSpec((tb, tc), lambda i, j: (i, j))
    return pl.pallas_call(
        body, name="ple_fwd", grid=(t // tb, d // tc), in_specs=[blk, blk, blk], out_specs=blk,
        out_shape=jax.ShapeDtypeStruct((t, d), F32), compiler_params=_params(("parallel", "parallel")))(x, gt, pp)


def _ple_bwd(dx, gt, pp):
    t, d = dx.shape
    tb, tc = _tile(t, 512, 8), _tile(d, 1024)

    def body(dx_ref, g_ref, p_ref, dg_ref, dp_ref):
        dv = dx_ref[...]
        s = _sigmoid(g_ref[...].astype(F32))
        dg_ref[...] = (dv * p_ref[...].astype(F32) * s * (1.0 - s)).astype(BF16)
        dp_ref[...] = (dv * s).astype(BF16)

    blk = pl.BlockSpec((tb, tc), lambda i, j: (i, j))
    shp = jax.ShapeDtypeStruct((t, d), BF16)
    return pl.pallas_call(
        body, name="ple_bwd", grid=(t // tb, d // tc), in_specs=[blk, blk, blk], out_specs=[blk, blk],
        out_shape=[shp, shp], compiler_params=_params(("parallel", "parallel")))(dx, gt, pp)


def _adam(parts, w, m, v, name):
    npart, r, c = parts.shape
    row_bytes = 2 * c * (npart * parts.dtype.itemsize + 7 * 4)
    tr = _tile(r, max(16, min(512, ADAM_VMEM_BUDGET // row_bytes)), 16)
    c1 = 1.0 - ADAM_B1 ** ADAM_STEP
    c2 = 1.0 - ADAM_B2 ** ADAM_STEP

    def body(p_ref, w_ref, m_ref, v_ref, g_ref, d_ref, mo_ref, vo_ref):
        g = p_ref[0].astype(F32)
        for i in range(1, npart):
            g = g + p_ref[i].astype(F32)
        mn = ADAM_B1 * m_ref[...] + (1.0 - ADAM_B1) * g
        vn = ADAM_B2 * v_ref[...] + (1.0 - ADAM_B2) * (g * g)
        g_ref[...] = g
        mo_ref[...] = mn
        vo_ref[...] = vn
        d_ref[...] = -ADAM_LR * ((mn / c1) / (jnp.sqrt(vn / c2) + ADAM_EPS) + ADAM_WD * w_ref[...])

    blk = pl.BlockSpec((tr, c), lambda i: (i, 0))
    shp = jax.ShapeDtypeStruct((r, c), F32)
    return pl.pallas_call(
        body, name=name, grid=(r // tr,),
        in_specs=[pl.BlockSpec((npart, tr, c), lambda i: (0, i, 0)), blk, blk, blk], out_specs=[blk] * 4,
        out_shape=[shp] * 4, compiler_params=_params(("parallel",)))(parts, w, m, v)


_BIG = ("w_in", "w_branch_a", "w_branch_b", "w_out", "w_ffn_up", "w_ffn_down", "w_ple_gate", "w_ple_proj")
_COL_SHARDED = ("w_in", "w_branch_b", "w_ffn_up", "w_ple_proj")
_CONVS = ("conv_qkv", "conv_ffn")
_GATHER_AHEAD = ("w_in", "conv_qkv")
_GATHER_ON_PROJ = ("w_branch_a", "w_branch_b", "w_out", "conv_ffn")
_GATHER_ON_DELTA = ("w_ffn_up",)
_GATHER_ON_UP = ("w_ffn_down", "w_ple_gate", "w_ple_proj")
_SCATTER_ON_FFN = ("w_ple_gate", "w_ple_proj", "w_ffn_down")
_SCATTER_ON_DELTA = ("w_ffn_up", "w_out", "w_branch_a", "w_branch_b")
_SMALL = ("norm_mix", "conv_qkv", "a_log", "dt_bias", "head_norm", "sgu_norm", "w_spatial", "b_spatial", "norm_ffn",
          "conv_ffn", "b_conv_ffn", "norm_ple", "norm_final")
_WEIGHTS = ("norm_mix", "w_in", "conv_qkv", "a_log", "dt_bias", "head_norm", "sgu_norm", "w_spatial", "b_spatial",
            "w_branch_a", "w_branch_b", "w_out", "norm_ffn", "w_ffn_up", "conv_ffn", "b_conv_ffn", "w_ffn_down",
            "norm_ple", "w_ple_gate", "w_ple_proj", "norm_final")


def _full_cols(g):
    return jnp.transpose(g, (1, 0, 2)).reshape(g.shape[1], N_DEV * g.shape[2])


def _full_rows(g):
    return g.reshape(N_DEV * g.shape[1], g.shape[2])


def _split_cols(dw):
    k, n = dw.shape
    return jnp.transpose(dw.reshape(k, N_DEV, n // N_DEV), (1, 0, 2))


def _split_rows(dw):
    k, n = dw.shape
    return dw.reshape(N_DEV, k // N_DEV, n)


def _pad_lanes(v, width=LANES, offset=0):
    return jnp.pad(v, ((0, 0), (offset, width - offset - v.shape[1])))


def kernel(x, p, norm_mix, w_in, conv_qkv, a_log, dt_bias, head_norm, sgu_norm, w_spatial, b_spatial, w_branch_a, w_branch_b, w_out, norm_ffn, w_ffn_up, conv_ffn, b_conv_ffn, w_ffn_down, norm_ple, w_ple_gate, w_ple_proj, norm_final, loss_target, m_norm_mix, m_w_in, m_conv_qkv, m_a_log, m_dt_bias, m_head_norm, m_sgu_norm, m_w_spatial, m_b_spatial, m_w_branch_a, m_w_branch_b, m_w_out, m_norm_ffn, m_w_ffn_up, m_conv_ffn, m_b_conv_ffn, m_w_ffn_down, m_norm_ple, m_w_ple_gate, m_w_ple_proj, m_norm_final, v_norm_mix, v_w_in, v_conv_qkv, v_a_log, v_dt_bias, v_head_norm, v_sgu_norm, v_w_spatial, v_b_spatial, v_w_branch_a, v_w_branch_b, v_w_out, v_norm_ffn, v_w_ffn_up, v_conv_ffn, v_b_conv_ffn, v_w_ffn_down, v_norm_ple, v_w_ple_gate, v_w_ple_proj, v_norm_final):
    env = dict(locals())
    wts = {n: env[n] for n in _WEIGHTS}
    mom_m = {n: env["m_" + n] for n in _WEIGHTS}
    mom_v = {n: env["v_" + n] for n in _WEIGHTS}

    xin = x[0]
    tgt = loss_target[0]
    t, d = xin.shape
    depth = w_in.shape[0]
    hv = a_log.shape[1]
    vw = hv * HEAD
    wb = sgu_norm.shape[1]
    ng = w_spatial.shape[1]
    n_in = w_in.shape[2] * N_DEV
    qk = (n_in - 2 * vw - 2 * hv - 2 * wb - 2 * d) // 2
    hqk = qk // HEAD
    dff = w_ffn_down.shape[1] * N_DEV
    cw = 2 * qk + vw
    o_z, o_ba = 2 * qk + vw, 2 * qk + 2 * vw
    o_ub = o_ba
    o_ga = o_ub + 2 * wb
    me = 4 * lax.axis_index("x") + 2 * lax.axis_index("y") + lax.axis_index("c")

    full = [dict() for _ in range(depth)]

    def gather_jobs(i, names):
        return [(wts[n][i].astype(BF16) if n in _BIG else wts[n][i], False) for n in names]

    def take(i, names, results):
        for n, g in zip(names, results):
            full[i][n] = _full_cols(g) if n in _COL_SHARDED or n in _CONVS else _full_rows(g)

    take(0, _GATHER_AHEAD, _exchange(gather_jobs(0, _GATHER_AHEAD), "gather_first"))

    saved = []
    xc = xin
    for i in range(depth):
        fw = full[i]
        w_full = fw["w_in"]
        fw["w_main"] = jnp.concatenate([w_full[:, :o_ba], w_full[:, o_ba + 2 * hv:]], axis=1)
        fw["w_ba"] = _pad_lanes(w_full[:, o_ba:o_ba + 2 * hv])
        s = {"x0": xc}
        s["h1"] = _rms_fwd(xc, norm_mix[i], "rms_fwd")
        s["proj"], got = _matmul(s["h1"], fw["w_main"], "nn", BF16, "mm_proj", jobs=gather_jobs(i, _GATHER_ON_PROJ))
        take(i, _GATHER_ON_PROJ, got)
        s["ba"] = _matmul(s["h1"], fw["w_ba"], "nn", F32, "mm_ba")
        s["qkvn"] = _qkv_fwd(s["proj"], fw["conv_qkv"], hqk, hqk)
        s["alog"] = _pad_lanes(a_log[i][None, :], offset=hv)
        s["dtb"] = _pad_lanes(dt_bias[i][None, :], offset=hv)
        bg = _gate_fwd(s["ba"], s["alog"], s["dtb"], hv)
        beta_t = bg[:, :hv].T
        gam_t = bg[:, hv:2 * hv].T
        s["beta_b"] = jnp.broadcast_to(beta_t[:, :, None], (hv, t, LANES))
        s["gam_b"] = jnp.broadcast_to(gam_t[:, :, None], (hv, t, LANES))
        s["gam_r"] = gam_t.reshape(hv, t // CA, CA)
        ahead = gather_jobs(i + 1, _GATHER_AHEAD) if i + 1 < depth else []
        (s["o"], s["s_all"], s["tm_all"]), got = _delta_fwd(
            s["qkvn"], s["beta_b"], s["gam_b"], s["gam_r"], hqk, hv, jobs=gather_jobs(i, _GATHER_ON_DELTA) + ahead)
        take(i, _GATHER_ON_DELTA, got)
        if ahead:
            take(i + 1, _GATHER_AHEAD, got[len(_GATHER_ON_DELTA):])
        s["outa"] = _apost_fwd(s["o"], s["proj"], head_norm[i], o_z, hv)
        s["b_t"] = b_spatial[i].T
        s["outb"] = _sgu_fwd(s["proj"], sgu_norm[i], w_spatial[i], s["b_t"], o_ub, wb)
        s["ya"] = _matmul(s["outa"], fw["w_branch_a"], "nn", BF16, "mm_ya")
        s["yb"] = _matmul(s["outb"], fw["w_branch_b"], "nn", BF16, "mm_yb")
        s["mg"] = _merge_fwd(s["ya"], s["yb"], s["proj"], o_ga)
        s["x1"] = _matmul(s["mg"], fw["w_out"], "nn", F32, "mm_out", res=xc)
        s["h2"] = _rms_fwd(s["x1"], norm_ffn[i], "rms_fwd")
        s["up"], got = _matmul(s["h2"], fw["w_ffn_up"], "nn", BF16, "mm_up", jobs=gather_jobs(i, _GATHER_ON_UP))
        take(i, _GATHER_ON_UP, got)
        s["bias"] = b_conv_ffn[i][None, :]
        s["act"] = _ffn_act_fwd(s["up"], fw["conv_ffn"], s["bias"], dff)
        s["x2"] = _matmul(s["act"], fw["w_ffn_down"], "nn", F32, "mm_down", res=s["x1"])
        s["h3"] = _rms_fwd(s["x2"], norm_ple[i], "rms_fwd")
        s["gt"] = _matmul(s["h3"], fw["w_ple_gate"], "nn", BF16, "mm_gt")
        s["pp"] = _matmul(p[i, 0], fw["w_ple_proj"], "nn", BF16, "mm_pp")
        xc = _ple_fwd(s["x2"], s["gt"], s["pp"])
        saved.append(s)

    dx, g_norm_final, loss_part = _loss_head(xc, tgt, norm_final)

    small = {n: [None] * depth for n in _SMALL if n != "norm_final"}
    recv = {n: [None] * depth for n in _BIG}

    def scatter_jobs(gw, names):
        return [(_split_cols(gw[n]) if n in _COL_SHARDED else _split_rows(gw[n]), True) for n in names]

    def keep(i, names, results):
        for n, r in zip(names, results):
            recv[n][i] = r

    later = []
    for i in reversed(range(depth)):
        fw, s = full[i], saved[i]
        dgt, dpp = _ple_bwd(dx, s["gt"], s["pp"])
        gw = {"w_ple_gate": _matmul(s["h3"], dgt, "tn", BF16, "mm_dw_gt"),
              "w_ple_proj": _matmul(p[i, 0], dpp, "tn", BF16, "mm_dw_pp")}
        dh3 = _matmul(dgt, fw["w_ple_gate"], "nt", F32, "mm_dh3")
        dx, small["norm_ple"][i] = _rms_bwd(s["x2"], dh3, norm_ple[i], dx, "rms_bwd")

        dact = _matmul(dx, fw["w_ffn_down"], "nt", BF16, "mm_dact")
        gw["w_ffn_down"] = _matmul(s["act"], dx, "tn", BF16, "mm_dw_down")
        (dup, small["conv_ffn"][i], small["b_conv_ffn"][i]), got = _ffn_act_bwd(
            s["up"], dact, fw["conv_ffn"], s["bias"], dff, jobs=scatter_jobs(gw, _SCATTER_ON_FFN))
        keep(i, _SCATTER_ON_FFN, got)
        gw["w_ffn_up"] = _matmul(s["h2"], dup, "tn", BF16, "mm_dw_up")
        dh2 = _matmul(dup, fw["w_ffn_up"], "nt", F32, "mm_dh2")
        dx, small["norm_ffn"][i] = _rms_bwd(s["x1"], dh2, norm_ffn[i], dx, "rms_bwd")

        dmg = _matmul(dx, fw["w_out"], "nt", BF16, "mm_dmg")
        gw["w_out"] = _matmul(s["mg"], dx, "tn", BF16, "mm_dw_out")
        dya, dyb, dga, dgb = _merge_bwd(dmg, s["ya"], s["yb"], s["proj"], o_ga)
        gw["w_branch_a"] = _matmul(s["outa"], dya, "tn", BF16, "mm_dw_a")
        gw["w_branch_b"] = _matmul(s["outb"], dyb, "tn", BF16, "mm_dw_b")
        douta = _matmul(dya, fw["w_branch_a"], "nt", BF16, "mm_douta")
        doutb = _matmul(dyb, fw["w_branch_b"], "nt", BF16, "mm_doutb")
        dub, dvb, small["w_spatial"][i], db_s, dsg = _sgu_bwd(s["proj"], sgu_norm[i], w_spatial[i], s["b_t"], doutb, o_ub, wb)
        small["b_spatial"][i] = db_s[:, :, 0]
        small["sgu_norm"][i] = dsg
        do, dz, small["head_norm"][i] = _apost_bwd(s["o"], s["proj"], head_norm[i], douta, o_z, hv)
        (dq, dk, dv, db_b, dg_b), got = _delta_bwd(
            s["qkvn"], s["beta_b"], s["gam_b"], s["gam_r"], s["s_all"], s["tm_all"], do, hqk, hv,
            jobs=scatter_jobs(gw, _SCATTER_ON_DELTA) + later)
        keep(i, _SCATTER_ON_DELTA, got)
        if later:
            keep(i + 1, ("w_in",), got[len(_SCATTER_ON_DELTA):])
        dbg = _pad_lanes(jnp.concatenate([db_b[:, :, 0].T, dg_b[:, :, 0].T], axis=1))
        dba, dal, ddt = _gate_bwd(s["ba"], dbg, s["alog"], s["dtb"], hv)
        small["a_log"][i] = dal[:, hv:2 * hv]
        small["dt_bias"][i] = ddt[:, hv:2 * hv]
        dqkv_pre, small["conv_qkv"][i] = _qkv_bwd(s["proj"], jnp.concatenate([dq, dk, dv], axis=1), fw["conv_qkv"], hqk, hqk)
        dproj = jnp.concatenate([dqkv_pre, dz, dub, dvb, dga, dgb], axis=1)
        dw_main = _matmul(s["h1"], dproj, "tn", BF16, "mm_dw_main")
        dw_ba = _matmul(s["h1"], dba, "tn", BF16, "mm_dw_ba")
        gw["w_in"] = jnp.concatenate([dw_main[:, :o_ba], dw_ba[:, :2 * hv], dw_main[:, o_ba:]], axis=1)
        dh1 = _matmul(dproj, fw["w_main"], "nt", F32, "mm_dh1")
        dh1 = _matmul(dba, fw["w_ba"], "nt", F32, "mm_dh1_ba", res=dh1)
        dx, small["norm_mix"][i] = _rms_bwd(s["x0"], dh1, norm_mix[i], dx, "rms_bwd")

        later = scatter_jobs(gw, ("w_in",))

    keep(0, ("w_in",), _exchange(later, "scatter_last"))

    outs_g, outs_d, outs_m, outs_v = {}, {}, {}, {}

    for n in _BIG:
        shp = wts[n].shape
        parts = jnp.stack(recv[n], axis=1)
        parts = parts.reshape(N_DEV, shp[0] * shp[1], shp[2])
        two_d = lambda a: a.reshape(shp[0] * shp[1], shp[2])
        res = _adam(parts, two_d(wts[n]), two_d(mom_m[n]), two_d(mom_v[n]), "adam_" + n)
        outs_g[n], outs_d[n], outs_m[n], outs_v[n] = [r.reshape(shp) for r in res]

    stacked = {n: jnp.stack([jnp.reshape(a, (-1,)) for a in small[n]]) for n in small}
    stacked["norm_final"] = g_norm_final.reshape(-1)
    flat = [stacked[n].reshape(-1) for n in _SMALL] + [loss_part[0, :1]]
    sizes = [f.shape[0] for f in flat]
    total = sum(sizes)
    rows = -(-total // (LANES * 8)) * 8
    packed = jnp.pad(jnp.concatenate(flat), (0, rows * LANES - total)).reshape(rows, LANES)
    (got,) = _exchange([(packed, False)], "gather_small")
    got = got.reshape(N_DEV, rows * LANES)

    pieces_g, pieces_w, pieces_m, pieces_v = [], [], [], []
    off = 0
    for n, size in zip(_SMALL, sizes[:-1]):
        part = got[:, off:off + size]
        off += size
        if n in _CONVS:
            kw, cl = wts[n].shape[1], wts[n].shape[2]
            part = part.reshape(N_DEV, depth, kw, cl * N_DEV)
            part = lax.dynamic_slice_in_dim(part, me * cl, cl, axis=3).reshape(N_DEV, -1)
        pieces_g.append(part)
        pieces_w.append(wts[n].reshape(-1))
        pieces_m.append(mom_m[n].reshape(-1))
        pieces_v.append(mom_v[n].reshape(-1))
    pieces_g.append(got[:, off:off + 1])
    for lst in (pieces_w, pieces_m, pieces_v):
        lst.append(jnp.zeros((1,), F32))
    sizes2 = [a.shape[0] for a in pieces_w]
    total2 = sum(sizes2)
    rows2 = -(-total2 // (LANES * 16)) * 16
    pad2 = rows2 * LANES - total2
    pk = lambda lst: jnp.pad(jnp.concatenate(lst), (0, pad2)).reshape(rows2, LANES)
    parts = jnp.pad(jnp.concatenate(pieces_g, axis=1), ((0, 0), (0, pad2))).reshape(N_DEV, rows2, LANES)
    res = [r.reshape(-1) for r in _adam(parts, pk(pieces_w), pk(pieces_m), pk(pieces_v), "adam_small")]
    off = 0
    for n, size in zip(_SMALL, sizes2[:-1]):
        shp = wts[n].shape
        outs_g[n], outs_d[n], outs_m[n], outs_v[n] = [r[off:off + size].reshape(shp) for r in res]
        off += size
    loss = res[0][off]

    return (loss, dx[None], *[outs_g[n] for n in _WEIGHTS], *[outs_d[n] for n in _WEIGHTS],
            *[outs_m[n] for n in _WEIGHTS], *[outs_v[n] for n in _WEIGHTS])
```

```python
import functools
import math

import jax
import jax.numpy as jnp
from jax import lax
from jax.experimental import pallas as pl
from jax.experimental.pallas import tpu as pltpu

F32 = jnp.float32
BF16 = jnp.bfloat16
EPS = 1e-6
LANES = 128
HEAD = 128
CA = 64
N_DEV = 8
VMEM_LIMIT = 48 * 1024 * 1024
ADAM_VMEM_BUDGET = 16 * 1024 * 1024
MESH = pl.DeviceIdType.MESH

ADAM_LR = 0.001
ADAM_B1 = 0.9
ADAM_B2 = 0.999
ADAM_EPS = 1e-08
ADAM_WD = 0.01
ADAM_STEP = 10


def _tile(n, cap, mult=LANES):
    best = None
    for t in range(mult, min(n, cap) + 1, mult):
        if n % t == 0:
            best = t
    return n if best is None else best


def _params(sem):
    return pltpu.CompilerParams(dimension_semantics=sem, vmem_limit_bytes=VMEM_LIMIT)


def _sigmoid(v):
    return jax.nn.sigmoid(v)


def _silu_grad(c):
    s = _sigmoid(c)
    return s + c * s * (1.0 - s)


_GELU_C = math.sqrt(2.0 / math.pi)


def _gelu(v):
    return 0.5 * v * (1.0 + jnp.tanh(_GELU_C * (v + 0.044715 * v * v * v)))


def _gelu_grad(v):
    t = jnp.tanh(_GELU_C * (v + 0.044715 * v * v * v))
    return 0.5 * (1.0 + t) + 0.5 * v * (1.0 - t * t) * _GELU_C * (1.0 + 3.0 * 0.044715 * v * v)


def _bdot(a, b, dims=((1,), (0,))):
    return lax.dot_general(a.astype(BF16), b.astype(BF16), (dims, ((), ())), preferred_element_type=F32)


_NT = ((1,), (1,))
_TN = ((0,), (0,))


def _split(a):
    hi = a.astype(BF16)
    return hi, (a - hi.astype(F32)).astype(BF16)


def _dot3(ah, al, bh, bl, dims=((1,), (0,))):
    def d(u, v):
        return lax.dot_general(u, v, (dims, ((), ())), preferred_element_type=F32)
    return d(ah, bh) + (d(al, bh) + d(ah, bl))


def _hdot(a, b, dims=((1,), (0,))):
    return _dot3(*_split(a), *_split(b), dims)


def _xchg_out_shapes(jobs):
    return [jax.ShapeDtypeStruct(a.shape if sc else (N_DEV,) + a.shape, a.dtype) for a, sc in jobs]


def _xchg_scratch(jobs):
    n = len(jobs)
    return [pltpu.SemaphoreType.DMA((n, N_DEV - 1)), pltpu.SemaphoreType.DMA((n, N_DEV - 1)), pltpu.SemaphoreType.DMA((n,))]


def _xchg_copies(scatter, src, out, sems):
    send_sems, recv_sems, local_sems = sems
    x, y, c = lax.axis_index("x"), lax.axis_index("y"), lax.axis_index("c")
    me = 4 * x + 2 * y + c
    local, sends, recvs = [], [], []
    for k, sc in enumerate(scatter):
        local.append(pltpu.make_async_copy(src[k].at[me] if sc else src[k], out[k].at[me], local_sems.at[k]))
    for m in range(1, N_DEV):
        px = lax.rem(x + ((m >> 2) & 1), 2)
        py = lax.rem(y + ((m >> 1) & 1), 2)
        pc = lax.rem(c + (m & 1), 2)
        peer = 4 * px + 2 * py + pc
        for k, sc in enumerate(scatter):
            mine = src[k].at[peer] if sc else src[k]
            for dst, lst in ((out[k].at[me], sends), (out[k].at[peer], recvs)):
                lst.append(pltpu.make_async_remote_copy(
                    src_ref=mine, dst_ref=dst, send_sem=send_sems.at[k, m - 1], recv_sem=recv_sems.at[k, m - 1],
                    device_id=(px, py, pc), device_id_type=MESH))
    return local, sends, recvs


def _xchg_start(scatter, src, out, sems):
    local, sends, _ = _xchg_copies(scatter, src, out, sems)
    for cp in local + sends:
        cp.start()


def _xchg_wait(scatter, src, out, sems):
    local, sends, recvs = _xchg_copies(scatter, src, out, sems)
    for cp in recvs:
        cp.wait_recv()
    for cp in sends:
        cp.wait_send()
    for cp in local:
        cp.wait()


_ANY = pl.BlockSpec(memory_space=pl.ANY)


def _exchange(jobs, name):
    n = len(jobs)
    scatter = [sc for _, sc in jobs]

    def body(*refs):
        src, out, sems = refs[:n], refs[n:2 * n], refs[2 * n:]
        _xchg_start(scatter, src, out, sems)
        _xchg_wait(scatter, src, out, sems)

    return pl.pallas_call(
        body, name=name, in_specs=[_ANY] * n, out_specs=[_ANY] * n, out_shape=_xchg_out_shapes(jobs),
        scratch_shapes=_xchg_scratch(jobs),
        compiler_params=pltpu.CompilerParams(has_side_effects=True))(*[a for a, _ in jobs])


def _carried(body, n_in, n_out, jobs, grid):
    if not jobs:
        return body
    nj = len(jobs)
    scatter = [sc for _, sc in jobs]

    def wrapped(*refs):
        ins, src = refs[:n_in], refs[n_in:n_in + nj]
        outs, got = refs[n_in + nj:n_in + nj + n_out], refs[n_in + nj + n_out:n_in + 2 * nj + n_out]
        rest = refs[n_in + 2 * nj + n_out:]
        scratch, sems = rest[:len(rest) - 3], rest[len(rest) - 3:]
        ids = [pl.program_id(a) for a in range(len(grid))]
        first = functools.reduce(jnp.logical_and, [i == 0 for i in ids])
        last = functools.reduce(jnp.logical_and, [i == g - 1 for i, g in zip(ids, grid)])

        @pl.when(first)
        def _():
            _xchg_start(scatter, src, got, sems)

        body(*ins, *outs, *scratch)

        @pl.when(last)
        def _():
            _xchg_wait(scatter, src, got, sems)

    return wrapped


def _call(body, name, grid, in_specs, out_specs, out_shape, args, scratch=(), sem=None, jobs=()):
    jobs = list(jobs)
    nj = len(jobs)
    sem = ("arbitrary",) * len(grid) if jobs or sem is None else sem
    res = pl.pallas_call(
        _carried(body, len(in_specs), len(out_specs), jobs, grid), name=name, grid=grid,
        in_specs=list(in_specs) + [_ANY] * nj, out_specs=list(out_specs) + [_ANY] * nj,
        out_shape=list(out_shape) + _xchg_out_shapes(jobs),
        scratch_shapes=list(scratch) + (_xchg_scratch(jobs) if jobs else []),
        compiler_params=_params(sem))(*args, *[a for a, _ in jobs])
    return res[:len(out_specs)], res[len(out_specs):]


MATMUL_OPERAND_VMEM = 20 * 1024 * 1024


def _matmul(a, b, mode, out_dtype, name, res=None, jobs=()):
    if mode == "tn":
        kdim, m = a.shape
    else:
        m, kdim = a.shape
    n = b.shape[0] if mode == "nt" else b.shape[1]
    tm, tn = _tile(m, 1024), _tile(n, 1024)
    per_k = 2 * (tm * a.dtype.itemsize + tn * b.dtype.itemsize)
    tk = _tile(kdim, max(LANES, MATMUL_OPERAND_VMEM // per_k))
    nk = kdim // tk
    dims = {"nn": ((1,), (0,)), "nt": _NT, "tn": _TN}[mode]

    def body(*refs):
        a_ref, b_ref = refs[:2]
        r_ref = refs[2] if res is not None else None
        o_ref = refs[3] if res is not None else refs[2]
        acc = refs[-1] if nk > 1 else None

        def write(r):
            if r_ref is not None:
                r = r + r_ref[...].astype(F32)
            o_ref[...] = r.astype(out_dtype)

        prod = _bdot(a_ref[...], b_ref[...], dims)
        if nk == 1:
            write(prod)
        else:
            k = pl.program_id(2)

            @pl.when(k == 0)
            def _():
                acc[...] = prod

            @pl.when(jnp.logical_and(k > 0, k < nk - 1))
            def _():
                acc[...] += prod

            @pl.when(k == nk - 1)
            def _():
                write(acc[...] + prod)

    a_spec = pl.BlockSpec((tk, tm), lambda i, j, k: (k, i)) if mode == "tn" else pl.BlockSpec((tm, tk), lambda i, j, k: (i, k))
    b_spec = pl.BlockSpec((tn, tk), lambda i, j, k: (j, k)) if mode == "nt" else pl.BlockSpec((tk, tn), lambda i, j, k: (k, j))
    o_spec = pl.BlockSpec((tm, tn), lambda i, j, k: (i, j))
    in_specs = [a_spec, b_spec] + ([o_spec] if res is not None else [])
    args = (a, b) + ((res,) if res is not None else ())
    (out,), got = _call(body, name, (m // tm, n // tn, nk), in_specs, [o_spec], [jax.ShapeDtypeStruct((m, n), out_dtype)],
                        args, scratch=[pltpu.VMEM((tm, tn), F32)] if nk > 1 else [],
                        sem=("parallel", "parallel", "arbitrary"), jobs=jobs)
    return (out, got) if jobs else out


def _rms_fwd(x, gain, name):
    t, d = x.shape
    tb = _tile(t, 256, 8)

    def body(x_ref, g_ref, h_ref):
        xv = x_ref[...]
        r = lax.rsqrt(jnp.mean(xv * xv, axis=-1, keepdims=True) + EPS)
        h_ref[...] = (xv * r * g_ref[...]).astype(BF16)

    return pl.pallas_call(
        body, name=name, grid=(t // tb,),
        in_specs=[pl.BlockSpec((tb, d), lambda i: (i, 0)), pl.BlockSpec((1, d), lambda i: (0, 0))],
        out_specs=pl.BlockSpec((tb, d), lambda i: (i, 0)), out_shape=jax.ShapeDtypeStruct((t, d), BF16),
        compiler_params=_params(("parallel",)))(x, gain.reshape(1, d))


def _rms_bwd(x, dh, gain, dres, name):
    t, d = x.shape
    tb = _tile(t, 256, 8)

    def body(x_ref, dh_ref, g_ref, dr_ref, dx_ref, dg_ref):
        @pl.when(pl.program_id(0) == 0)
        def _():
            dg_ref[...] = jnp.zeros_like(dg_ref)

        xv = x_ref[...]
        dy = dh_ref[...].astype(F32)
        r = lax.rsqrt(jnp.mean(xv * xv, axis=-1, keepdims=True) + EPS)
        xh = xv * r
        dxh = dy * g_ref[...]
        dx_ref[...] = dr_ref[...] + r * (dxh - xh * jnp.mean(dxh * xh, axis=-1, keepdims=True))
        dg_ref[...] += jnp.sum(dy * xh, axis=0, keepdims=True)

    row = pl.BlockSpec((tb, d), lambda i: (i, 0))
    vec = pl.BlockSpec((1, d), lambda i: (0, 0))
    return pl.pallas_call(
        body, name=name, grid=(t // tb,), in_specs=[row, row, vec, row], out_specs=[row, vec],
        out_shape=[jax.ShapeDtypeStruct((t, d), F32), jax.ShapeDtypeStruct((1, d), F32)],
        compiler_params=_params(("arbitrary",)))(x, dh, gain.reshape(1, d), dres)


def _loss_head(x, target, gain):
    t, d = x.shape
    tb = _tile(t, 256, 8)

    def body(x_ref, t_ref, g_ref, dx_ref, dg_ref, loss_ref):
        @pl.when(pl.program_id(0) == 0)
        def _():
            dg_ref[...] = jnp.zeros_like(dg_ref)
            loss_ref[...] = jnp.zeros_like(loss_ref)

        xv = x_ref[...]
        r = lax.rsqrt(jnp.mean(xv * xv, axis=-1, keepdims=True) + EPS)
        xh = xv * r
        err = xh * g_ref[...] - t_ref[...]
        per_row = jnp.mean(err * err, axis=-1, keepdims=True)
        loss_ref[...] += 0.5 * jnp.sum(per_row, axis=0, keepdims=True)
        dy = err * (1.0 / d)
        dxh = dy * g_ref[...]
        dx_ref[...] = r * (dxh - xh * jnp.mean(dxh * xh, axis=-1, keepdims=True))
        dg_ref[...] += jnp.sum(dy * xh, axis=0, keepdims=True)

    row = pl.BlockSpec((tb, d), lambda i: (i, 0))
    vec = pl.BlockSpec((1, d), lambda i: (0, 0))
    return pl.pallas_call(
        body, name="loss_head", grid=(t // tb,), in_specs=[row, row, vec],
        out_specs=[row, vec, pl.BlockSpec((1, LANES), lambda i: (0, 0))],
        out_shape=[jax.ShapeDtypeStruct((t, d), F32), jax.ShapeDtypeStruct((1, d), F32),
                   jax.ShapeDtypeStruct((1, LANES), F32)],
        compiler_params=_params(("arbitrary",)))(x, target, gain.reshape(1, d))


def _shift_down(v, s, rows):
    if s == 0:
        return v
    return jnp.where(rows >= s, pltpu.roll(v, s, 0), 0.0)


def _shift_up(v, s, rows):
    if s == 0:
        return v
    t = v.shape[0]
    return jnp.where(rows < t - s, pltpu.roll(v, t - s, 0), 0.0)


def _conv(v, w, rows):
    k = w.shape[0]
    out = v * w[k - 1:k, :]
    for s in range(1, k):
        out = out + _shift_down(v, s, rows) * w[k - 1 - s:k - s, :]
    return out


def _qkv_fwd(proj, conv_w, nq, nk):
    t = proj.shape[0]
    cw = conv_w.shape[1]
    nblk = cw // HEAD

    def body(p_ref, w_ref, o_ref):
        j = pl.program_id(0)
        rows = lax.broadcasted_iota(jnp.int32, (t, HEAD), 0)
        c = _conv(p_ref[...].astype(F32), w_ref[...], rows)
        a = c * _sigmoid(c)
        nrm = a * lax.rsqrt(jnp.sum(a * a, axis=-1, keepdims=True) + EPS)
        nrm = nrm * jnp.where(j < nq, HEAD ** -0.5, 1.0)
        o_ref[...] = jnp.where(j < nq + nk, nrm, a).astype(BF16)

    return pl.pallas_call(
        body, name="qkv_fwd", grid=(nblk,),
        in_specs=[pl.BlockSpec((t, HEAD), lambda j: (0, j)), pl.BlockSpec((conv_w.shape[0], HEAD), lambda j: (0, j))],
        out_specs=pl.BlockSpec((t, HEAD), lambda j: (0, j)), out_shape=jax.ShapeDtypeStruct((t, cw), BF16),
        compiler_params=_params(("parallel",)))(proj, conv_w)


def _qkv_bwd(proj, dqkv, conv_w, nq, nk):
    t = proj.shape[0]
    kw, cw = conv_w.shape
    nblk = cw // HEAD

    def body(p_ref, d_ref, w_ref, dp_ref, dw_ref):
        j = pl.program_id(0)
        rows = lax.broadcasted_iota(jnp.int32, (t, HEAD), 0)
        xv = p_ref[...].astype(F32)
        w = w_ref[...]
        c = _conv(xv, w, rows)
        a = c * _sigmoid(c)
        dy = d_ref[...].astype(F32)
        r = lax.rsqrt(jnp.sum(a * a, axis=-1, keepdims=True) + EPS)
        y = a * r
        scale = jnp.where(j < nq, HEAD ** -0.5, 1.0)
        da_n = scale * r * (dy - y * jnp.sum(dy * y, axis=-1, keepdims=True))
        da = jnp.where(j < nq + nk, da_n, dy)
        dc = da * _silu_grad(c)
        dx = dc * w[kw - 1:kw, :]
        dw_ref[kw - 1:kw, :] = jnp.sum(dc * xv, axis=0, keepdims=True)
        for s in range(1, kw):
            dx = dx + _shift_up(dc, s, rows) * w[kw - 1 - s:kw - s, :]
            dw_ref[kw - 1 - s:kw - s, :] = jnp.sum(dc * _shift_down(xv, s, rows), axis=0, keepdims=True)
        dp_ref[...] = dx.astype(BF16)

    blk = pl.BlockSpec((t, HEAD), lambda j: (0, j))
    wblk = pl.BlockSpec((kw, HEAD), lambda j: (0, j))
    return pl.pallas_call(
        body, name="qkv_bwd", grid=(nblk,), in_specs=[blk, blk, wblk], out_specs=[blk, wblk],
        out_shape=[jax.ShapeDtypeStruct((t, cw), BF16), jax.ShapeDtypeStruct((kw, cw), F32)],
        compiler_params=_params(("parallel",)))(proj, dqkv, conv_w)


def _softplus(v):
    return jnp.where(v < -15.0, jnp.exp(v), jnp.maximum(v, 0.0) + jnp.log(1.0 + jnp.exp(-jnp.abs(v))))


def _gate_fwd(ba, alog_pad, dtb_pad, hv):
    t = ba.shape[0]
    tb = _tile(t, 512, CA)

    def body(ba_ref, al_ref, dt_ref, o_ref):
        v = ba_ref[...]
        beta = _sigmoid(v)
        g = -jnp.exp(al_ref[...]) * _softplus(v + dt_ref[...])
        pos = lax.broadcasted_iota(jnp.int32, (tb, LANES), 0) % CA
        s = 1
        while s < CA:
            g = g + jnp.where(pos >= s, pltpu.roll(g, s, 0), 0.0)
            s *= 2
        lane = lax.broadcasted_iota(jnp.int32, (tb, LANES), 1)
        o_ref[...] = jnp.where(lane < hv, beta, g)

    row = pl.BlockSpec((tb, LANES), lambda i: (i, 0))
    vec = pl.BlockSpec((1, LANES), lambda i: (0, 0))
    return pl.pallas_call(
        body, name="gate_fwd", grid=(t // tb,), in_specs=[row, vec, vec], out_specs=row,
        out_shape=jax.ShapeDtypeStruct((t, LANES), F32), compiler_params=_params(("parallel",)))(ba, alog_pad, dtb_pad)


def _gate_bwd(ba, dbg, alog_pad, dtb_pad, hv):
    t = ba.shape[0]
    tb = _tile(t, 512, CA)

    def body(ba_ref, d_ref, al_ref, dt_ref, dba_ref, dal_ref, ddt_ref):
        @pl.when(pl.program_id(0) == 0)
        def _():
            dal_ref[...] = jnp.zeros_like(dal_ref)
            ddt_ref[...] = jnp.zeros_like(ddt_ref)

        v = ba_ref[...]
        d = d_ref[...]
        pos = lax.broadcasted_iota(jnp.int32, (tb, LANES), 0) % CA
        dg = d
        s = 1
        while s < CA:
            dg = dg + jnp.where(pos < CA - s, pltpu.roll(dg, tb - s, 0), 0.0)
            s *= 2
        beta = _sigmoid(v)
        na = -jnp.exp(al_ref[...])
        z = v + dt_ref[...]
        da = dg * na * _sigmoid(z)
        lane = lax.broadcasted_iota(jnp.int32, (tb, LANES), 1)
        in_a = jnp.logical_and(lane >= hv, lane < 2 * hv)
        da = jnp.where(in_a, da, 0.0)
        dba_ref[...] = jnp.where(lane < hv, d * beta * (1.0 - beta), da)
        ddt_ref[...] += jnp.sum(da, axis=0, keepdims=True)
        dal_ref[...] += jnp.sum(jnp.where(in_a, dg * na * _softplus(z), 0.0), axis=0, keepdims=True)

    row = pl.BlockSpec((tb, LANES), lambda i: (i, 0))
    vec = pl.BlockSpec((1, LANES), lambda i: (0, 0))
    return pl.pallas_call(
        body, name="gate_bwd", grid=(t // tb,), in_specs=[row, row, vec, vec], out_specs=[row, vec, vec],
        out_shape=[jax.ShapeDtypeStruct((t, LANES), F32), jax.ShapeDtypeStruct((1, LANES), F32),
                   jax.ShapeDtypeStruct((1, LANES), F32)],
        compiler_params=_params(("arbitrary",)))(ba, dbg, alog_pad, dtb_pad)


def _chunk_masks():
    r = lax.broadcasted_iota(jnp.int32, (CA, CA), 0)
    c = lax.broadcasted_iota(jnp.int32, (CA, CA), 1)
    return r >= c, r > c, (r == c).astype(F32)


def _inv_unit_lower(a, eye):
    x = eye - a
    ph, plo = _split(a)
    n = 1
    while n < CA // 2:
        ph, plo = _split(_dot3(ph, plo, ph, plo))
        x = x + _dot3(*_split(x), ph, plo)
        n *= 2
    return x


def _delta_pre(q, k, v, bcol, gc, gr, gl, causal, strict):
    eg = jnp.exp(gc)
    dm = jnp.exp(jnp.where(causal, gc[:, :CA] - gr, -jnp.inf))
    kb = k * bcol
    kkb = _bdot(kb, k, _NT)
    a = jnp.where(strict, kkb * dm, 0.0)
    rhs = jnp.concatenate([v * bcol, kb * eg], axis=1)
    qk = _bdot(q, k, _NT)
    ekd = jnp.exp(gl - gc)
    return dict(eg=eg, dm=dm, kb=kb, kkb=kkb, a=a, rhs=rhs, p=qk * dm, qd=q * eg, ekd=ekd, kd=k * ekd, cd=jnp.exp(gl))


def _delta_fwd(qkvn, beta_b, gam_b, gam_r, hqk, hv, jobs=()):
    t = qkvn.shape[0]
    rep = hv // hqk
    rb = _tile(t, 512, CA)
    nb = t // rb
    ncb = rb // CA
    nc = t // CA

    def body(q_ref, k_ref, v_ref, b_ref, gc_ref, gr_ref, o_ref, s_ref, tm_ref, state):
        @pl.when(pl.program_id(1) == 0)
        def _():
            state[...] = jnp.zeros_like(state)

        causal, strict, eye = _chunk_masks()

        def chunk(n, carry):
            base = pl.multiple_of(n * CA, CA)
            rows = pl.ds(base, CA)
            q = q_ref[rows, :].astype(F32)
            k = k_ref[rows, :].astype(F32)
            for h in range(rep):
                v = v_ref[rows, h * HEAD:(h + 1) * HEAD].astype(F32)
                gl = gc_ref[h, pl.ds(base + CA - 1, 1), :]
                pre = _delta_pre(q, k, v, b_ref[h, rows, :], gc_ref[h, rows, :], gr_ref[h, pl.ds(n, 1), :], gl,
                                 causal, strict)
                tm = _inv_unit_lower(pre["a"], eye)
                sol = _hdot(tm, pre["rhs"])
                s = state[h]
                v_new = sol[:, :HEAD] - _bdot(sol[:, HEAD:], s)
                o_ref[rows, h * HEAD:(h + 1) * HEAD] = _bdot(pre["qd"], s) + _bdot(pre["p"], v_new)
                s_ref[h, n] = s.astype(BF16)
                tm_ref[h, rows, :] = tm
                state[h] = s * pre["cd"] + _bdot(pre["kd"], v_new, _TN)
            return carry

        lax.fori_loop(0, ncb, chunk, 0)

    qoff, koff, voff = 0, hqk, 2 * hqk // rep
    return _call(
        body, "delta_fwd", (hqk, nb),
        [pl.BlockSpec((rb, HEAD), lambda j, i: (i, qoff + j)),
         pl.BlockSpec((rb, HEAD), lambda j, i: (i, koff + j)),
         pl.BlockSpec((rb, rep * HEAD), lambda j, i: (i, voff + j)),
         pl.BlockSpec((rep, rb, LANES), lambda j, i: (j, i, 0)),
         pl.BlockSpec((rep, rb, LANES), lambda j, i: (j, i, 0)),
         pl.BlockSpec((rep, ncb, CA), lambda j, i: (j, i, 0))],
        [pl.BlockSpec((rb, rep * HEAD), lambda j, i: (i, j)),
         pl.BlockSpec((rep, ncb, HEAD, HEAD), lambda j, i: (j, i, 0, 0)),
         pl.BlockSpec((rep, rb, CA), lambda j, i: (j, i, 0))],
        [jax.ShapeDtypeStruct((t, hv * HEAD), F32), jax.ShapeDtypeStruct((hv, nc, HEAD, HEAD), BF16),
         jax.ShapeDtypeStruct((hv, t, CA), F32)],
        (qkvn, qkvn, qkvn, beta_b, gam_b, gam_r), scratch=[pltpu.VMEM((rep, HEAD, HEAD), F32)],
        sem=("parallel", "arbitrary"), jobs=jobs)


def _delta_bwd(qkvn, beta_b, gam_b, gam_r, s_all, tm_all, do, hqk, hv, jobs=()):
    t = qkvn.shape[0]
    rep = hv // hqk
    rb = _tile(t, 512, CA)
    nb = t // rb
    ncb = rb // CA

    def body(q_ref, k_ref, v_ref, b_ref, gc_ref, gr_ref, s_ref, tm_ref, do_ref,
             dq_ref, dk_ref, dv_ref, db_ref, dg_ref, dstate):
        @pl.when(pl.program_id(1) == 0)
        def _():
            dstate[...] = jnp.zeros_like(dstate)

        causal, strict, _ = _chunk_masks()
        ones = jnp.ones((CA, LANES), BF16)
        last = lax.broadcasted_iota(jnp.int32, (CA, LANES), 0) == CA - 1

        def rowsum(m):
            return jnp.sum(m, axis=1, keepdims=True)

        def colsum(m):
            hi, lo = _split(m)
            return _bdot(hi, ones, _TN) + _bdot(lo, ones, _TN)

        def chunk(it, carry):
            n = ncb - 1 - it
            base = pl.multiple_of(n * CA, CA)
            rows = pl.ds(base, CA)
            q = q_ref[rows, :].astype(F32)
            k = k_ref[rows, :].astype(F32)
            kkr = _bdot(k, k, _NT)
            dq = jnp.zeros((CA, HEAD), F32)
            dk = jnp.zeros((CA, HEAD), F32)
            for h in range(rep):
                v = v_ref[rows, h * HEAD:(h + 1) * HEAD].astype(F32)
                bcol = b_ref[h, rows, :]
                gl = gc_ref[h, pl.ds(base + CA - 1, 1), :]
                pre = _delta_pre(q, k, v, bcol, gc_ref[h, rows, :], gr_ref[h, pl.ds(n, 1), :], gl, causal, strict)
                eg, dm, kb, p, qd, kd, cd = pre["eg"], pre["dm"], pre["kb"], pre["p"], pre["qd"], pre["kd"], pre["cd"]
                tm = tm_ref[h, rows, :]
                sol = _hdot(tm, pre["rhs"])
                w = sol[:, HEAD:]
                s = s_ref[h, n].astype(F32)
                ds = dstate[h]
                dov = do_ref[rows, h * HEAD:(h + 1) * HEAD].astype(F32)
                v_new = sol[:, :HEAD] - _bdot(w, s)

                dvn = _bdot(p, dov, _TN) + _bdot(kd, ds)
                dp = jnp.where(causal, _bdot(dov, v_new, _NT), 0.0)
                dqd = _bdot(dov, s, _NT)
                dkd = _bdot(v_new, ds, _NT)
                dcd = jnp.sum(rowsum(s * ds), axis=0, keepdims=True)
                dw = -_bdot(dvn, s, _NT)
                dstate[h] = ds * cd + _bdot(qd, dov, _TN) - _bdot(w, dvn, _TN)

                drhs = _hdot(tm, jnp.concatenate([dvn, dw], axis=1), _TN)
                dbv, dbke = drhs[:, :HEAD], drhs[:, HEAD:]
                da = -jnp.where(strict, _bdot(drhs, sol, _NT), 0.0)
                m = da * dm
                e = m * pre["kkb"] + dp * p
                dgam = rowsum(e) - colsum(e) + rowsum(dbke * kb * eg) + rowsum(dqd * qd)
                r = rowsum(dkd * kd)
                tot = jnp.sum(r, axis=0, keepdims=True) + dcd * cd
                dgam = dgam - r + jnp.where(last, tot, 0.0)
                dbeta = rowsum(m * kkr) + rowsum(dbv * v) + rowsum(dbke * eg * k)
                nm = m * bcol[:, :CA]
                dqk = dp * dm
                dq = dq + _bdot(dqk, k) + eg * dqd
                dk = dk + _bdot(nm, k) + _bdot(nm, k, _TN) + _bdot(dqk, q, _TN) + bcol * eg * dbke + pre["ekd"] * dkd
                dv_ref[rows, h * HEAD:(h + 1) * HEAD] = bcol * dbv
                db_ref[h, rows, :] = jnp.broadcast_to(dbeta, (CA, LANES))
                dg_ref[h, rows, :] = jnp.broadcast_to(dgam, (CA, LANES))
            dq_ref[rows, :] = dq
            dk_ref[rows, :] = dk
            return carry

        lax.fori_loop(0, ncb, chunk, 0)

    qoff, koff, voff = 0, hqk, 2 * hqk // rep
    rv = lambda i: nb - 1 - i
    hd = pl.BlockSpec((rep, rb, LANES), lambda j, i: (j, rv(i), 0))
    qk_out = pl.BlockSpec((rb, HEAD), lambda j, i: (rv(i), j))
    v_blk = pl.BlockSpec((rb, rep * HEAD), lambda j, i: (rv(i), j))
    return _call(
        body, "delta_bwd", (hqk, nb),
        [pl.BlockSpec((rb, HEAD), lambda j, i: (rv(i), qoff + j)),
         pl.BlockSpec((rb, HEAD), lambda j, i: (rv(i), koff + j)),
         pl.BlockSpec((rb, rep * HEAD), lambda j, i: (rv(i), voff + j)),
         hd, hd,
         pl.BlockSpec((rep, ncb, CA), lambda j, i: (j, rv(i), 0)),
         pl.BlockSpec((rep, ncb, HEAD, HEAD), lambda j, i: (j, rv(i), 0, 0)),
         pl.BlockSpec((rep, rb, CA), lambda j, i: (j, rv(i), 0)),
         v_blk],
        [qk_out, qk_out, v_blk, hd, hd],
        [jax.ShapeDtypeStruct((t, hqk * HEAD), F32), jax.ShapeDtypeStruct((t, hqk * HEAD), F32),
         jax.ShapeDtypeStruct((t, hv * HEAD), F32),
         jax.ShapeDtypeStruct((hv, t, LANES), F32), jax.ShapeDtypeStruct((hv, t, LANES), F32)],
        (qkvn, qkvn, qkvn, beta_b, gam_b, gam_r, s_all, tm_all, do), scratch=[pltpu.VMEM((rep, HEAD, HEAD), F32)],
        sem=("parallel", "arbitrary"), jobs=jobs)


def _apost_fwd(o, proj, gain, zoff, hv):
    t = o.shape[0]
    tb = _tile(t, 1024, 8)
    zb = zoff // HEAD

    def body(o_ref, z_ref, g_ref, y_ref):
        ov = o_ref[...]
        z = z_ref[...].astype(F32)
        r = lax.rsqrt(jnp.mean(ov * ov, axis=-1, keepdims=True) + EPS)
        y_ref[...] = (ov * r * g_ref[...] * (z * _sigmoid(z))).astype(BF16)

    blk = pl.BlockSpec((tb, HEAD), lambda i, h: (i, h))
    return pl.pallas_call(
        body, name="apost_fwd", grid=(t // tb, hv),
        in_specs=[blk, pl.BlockSpec((tb, HEAD), lambda i, h: (i, zb + h)), pl.BlockSpec((1, HEAD), lambda i, h: (0, 0))],
        out_specs=blk, out_shape=jax.ShapeDtypeStruct((t, hv * HEAD), BF16),
        compiler_params=_params(("parallel", "parallel")))(o, proj, gain.reshape(1, HEAD))


def _apost_bwd(o, proj, gain, dy, zoff, hv):
    t = o.shape[0]
    tb = _tile(t, 1024, 8)
    zb = zoff // HEAD

    def body(o_ref, z_ref, g_ref, dy_ref, do_ref, dz_ref, dg_ref):
        @pl.when(jnp.logical_and(pl.program_id(0) == 0, pl.program_id(1) == 0))
        def _():
            dg_ref[...] = jnp.zeros_like(dg_ref)

        ov = o_ref[...]
        z = z_ref[...].astype(F32)
        d = dy_ref[...].astype(F32)
        r = lax.rsqrt(jnp.mean(ov * ov, axis=-1, keepdims=True) + EPS)
        oh = ov * r
        sz = z * _sigmoid(z)
        dn = d * sz
        dz_ref[...] = (d * oh * g_ref[...] * _silu_grad(z)).astype(BF16)
        doh = dn * g_ref[...]
        do_ref[...] = r * (doh - oh * jnp.mean(doh * oh, axis=-1, keepdims=True))
        dg_ref[...] += jnp.sum(dn * oh, axis=0, keepdims=True)

    blk = pl.BlockSpec((tb, HEAD), lambda i, h: (i, h))
    vec = pl.BlockSpec((1, HEAD), lambda i, h: (0, 0))
    return pl.pallas_call(
        body, name="apost_bwd", grid=(t // tb, hv),
        in_specs=[blk, pl.BlockSpec((tb, HEAD), lambda i, h: (i, zb + h)), vec, blk],
        out_specs=[blk, blk, vec],
        out_shape=[jax.ShapeDtypeStruct((t, hv * HEAD), F32), jax.ShapeDtypeStruct((t, hv * HEAD), BF16),
                   jax.ShapeDtypeStruct((1, HEAD), F32)],
        compiler_params=_params(("arbitrary", "arbitrary")))(o, proj, gain.reshape(1, HEAD), dy)


def _sgu_fwd(proj, gain, w_s, b_t, uoff, wb):
    t = proj.shape[0]
    ng = wb // HEAD

    def body(u_ref, v_ref, g_ref, w_ref, b_ref, o_ref):
        r_i = lax.broadcasted_iota(jnp.int32, (HEAD, HEAD), 0)
        c_i = lax.broadcasted_iota(jnp.int32, (HEAD, HEAD), 1)
        u = _gelu(u_ref[...].astype(F32))
        vg = _gelu(v_ref[...].astype(F32))
        vn = vg * lax.rsqrt(jnp.mean(vg * vg, axis=-1, keepdims=True) + EPS) * g_ref[...]
        for g in range(ng):
            cols = slice(g * HEAD, (g + 1) * HEAD)
            wg = jnp.where(r_i >= c_i, w_ref[g], 0.0)
            mixed = _bdot(wg, vn[:, cols]) + b_ref[:, g:g + 1]
            o_ref[:, cols] = (u[:, cols] * mixed).astype(BF16)

    ub, vb = uoff // wb, uoff // wb + 1
    return pl.pallas_call(
        body, name="sgu_fwd", grid=(t // HEAD,),
        in_specs=[pl.BlockSpec((HEAD, wb), lambda i: (i, ub)), pl.BlockSpec((HEAD, wb), lambda i: (i, vb)),
                  pl.BlockSpec((1, wb), lambda i: (0, 0)), pl.BlockSpec((ng, HEAD, HEAD), lambda i: (0, 0, 0)),
                  pl.BlockSpec((HEAD, ng), lambda i: (0, 0))],
        out_specs=pl.BlockSpec((HEAD, wb), lambda i: (i, 0)), out_shape=jax.ShapeDtypeStruct((t, wb), BF16),
        compiler_params=_params(("parallel",)))(proj, proj, gain.reshape(1, wb), w_s, b_t)


def _sgu_bwd(proj, gain, w_s, b_t, dout, uoff, wb):
    t = proj.shape[0]
    ng = wb // HEAD

    def body(u_ref, v_ref, g_ref, w_ref, b_ref, d_ref, du_ref, dv_ref, dw_ref, db_ref, dg_ref, dvn_ref):
        @pl.when(pl.program_id(0) == 0)
        def _():
            dw_ref[...] = jnp.zeros_like(dw_ref)
            db_ref[...] = jnp.zeros_like(db_ref)
            dg_ref[...] = jnp.zeros_like(dg_ref)

        r_i = lax.broadcasted_iota(jnp.int32, (HEAD, HEAD), 0)
        c_i = lax.broadcasted_iota(jnp.int32, (HEAD, HEAD), 1)
        tril = r_i >= c_i
        ub = u_ref[...].astype(F32)
        vb = v_ref[...].astype(F32)
        u = _gelu(ub)
        vg = _gelu(vb)
        r = lax.rsqrt(jnp.mean(vg * vg, axis=-1, keepdims=True) + EPS)
        vh = vg * r
        vn = vh * g_ref[...]
        d = d_ref[...].astype(F32)
        for g in range(ng):
            cols = slice(g * HEAD, (g + 1) * HEAD)
            wg = jnp.where(tril, w_ref[g], 0.0)
            mixed = _bdot(wg, vn[:, cols]) + b_ref[:, g:g + 1]
            du_ref[:, cols] = (d[:, cols] * mixed * _gelu_grad(ub[:, cols])).astype(BF16)
            dmix = d[:, cols] * u[:, cols]
            dw_ref[g] += jnp.where(tril, _bdot(dmix, vn[:, cols], _NT), 0.0)
            db_ref[g] += jnp.broadcast_to(jnp.sum(dmix, axis=1, keepdims=True), (HEAD, HEAD))
            dvn_ref[:, cols] = _bdot(wg, dmix, _TN)
        dvn = dvn_ref[...]
        dg_ref[...] += jnp.sum(dvn * vh, axis=0, keepdims=True)
        dvh = dvn * g_ref[...]
        dvg = r * (dvh - vh * jnp.mean(dvh * vh, axis=-1, keepdims=True))
        dv_ref[...] = (dvg * _gelu_grad(vb)).astype(BF16)

    ub_i, vb_i = uoff // wb, uoff // wb + 1
    row = pl.BlockSpec((HEAD, wb), lambda i: (i, 0))
    mat = pl.BlockSpec((ng, HEAD, HEAD), lambda i: (0, 0, 0))
    vec = pl.BlockSpec((1, wb), lambda i: (0, 0))
    return pl.pallas_call(
        body, name="sgu_bwd", grid=(t // HEAD,),
        in_specs=[pl.BlockSpec((HEAD, wb), lambda i: (i, ub_i)), pl.BlockSpec((HEAD, wb), lambda i: (i, vb_i)),
                  vec, mat, pl.BlockSpec((HEAD, ng), lambda i: (0, 0)), row],
        out_specs=[row, row, mat, mat, vec],
        out_shape=[jax.ShapeDtypeStruct((t, wb), BF16), jax.ShapeDtypeStruct((t, wb), BF16),
                   jax.ShapeDtypeStruct((ng, HEAD, HEAD), F32), jax.ShapeDtypeStruct((ng, HEAD, HEAD), F32),
                   jax.ShapeDtypeStruct((1, wb), F32)],
        scratch_shapes=[pltpu.VMEM((HEAD, wb), F32)],
        compiler_params=_params(("arbitrary",)))(proj, proj, gain.reshape(1, wb), w_s, b_t, dout)


def _merge_specs(t, d, goff):
    tb = _tile(t, 512, 8)
    tc = _tile(d, 512)
    gb = goff // tc
    nd = d // tc
    blk = pl.BlockSpec((tb, tc), lambda i, j: (i, j))
    ga = pl.BlockSpec((tb, tc), lambda i, j: (i, gb + j))
    gbs = pl.BlockSpec((tb, tc), lambda i, j: (i, gb + nd + j))
    return (t // tb, nd), blk, ga, gbs


def _merge_fwd(ya, yb, proj, goff):
    t, d = ya.shape
    grid, blk, ga, gbs = _merge_specs(t, d, goff)

    def body(ya_ref, yb_ref, ga_ref, gb_ref, o_ref):
        o_ref[...] = (_sigmoid(ga_ref[...].astype(F32)) * ya_ref[...].astype(F32)
                      + _sigmoid(gb_ref[...].astype(F32)) * yb_ref[...].astype(F32)).astype(BF16)

    return pl.pallas_call(
        body, name="merge_fwd", grid=grid, in_specs=[blk, blk, ga, gbs], out_specs=blk,
        out_shape=jax.ShapeDtypeStruct((t, d), BF16),
        compiler_params=_params(("parallel", "parallel")))(ya, yb, proj, proj)


def _merge_bwd(dm, ya, yb, proj, goff):
    t, d = ya.shape
    grid, blk, ga, gbs = _merge_specs(t, d, goff)

    def body(dm_ref, ya_ref, yb_ref, ga_ref, gb_ref, dya_ref, dyb_ref, dga_ref, dgb_ref):
        dmv = dm_ref[...].astype(F32)
        sa = _sigmoid(ga_ref[...].astype(F32))
        sb = _sigmoid(gb_ref[...].astype(F32))
        dya_ref[...] = (dmv * sa).astype(BF16)
        dyb_ref[...] = (dmv * sb).astype(BF16)
        dga_ref[...] = (dmv * ya_ref[...].astype(F32) * sa * (1.0 - sa)).astype(BF16)
        dgb_ref[...] = (dmv * yb_ref[...].astype(F32) * sb * (1.0 - sb)).astype(BF16)

    shp = jax.ShapeDtypeStruct((t, d), BF16)
    return pl.pallas_call(
        body, name="merge_bwd", grid=grid, in_specs=[blk, blk, blk, ga, gbs], out_specs=[blk] * 4,
        out_shape=[shp] * 4, compiler_params=_params(("parallel", "parallel")))(dm, ya, yb, proj, proj)


def _ffn_act_fwd(up, conv_w, bias, dff):
    t = up.shape[0]
    nblk = dff // HEAD
    kw = conv_w.shape[0]

    def body(g_ref, v_ref, wg_ref, wv_ref, bg_ref, bv_ref, o_ref):
        rows = lax.broadcasted_iota(jnp.int32, (t, HEAD), 0)
        cg = _conv(g_ref[...].astype(F32), wg_ref[...], rows) + bg_ref[...]
        cv = _conv(v_ref[...].astype(F32), wv_ref[...], rows) + bv_ref[...]
        o_ref[...] = (cg * _sigmoid(cg) * cv).astype(BF16)

    return pl.pallas_call(
        body, name="ffn_act_fwd", grid=(nblk,),
        in_specs=[pl.BlockSpec((t, HEAD), lambda j: (0, j)), pl.BlockSpec((t, HEAD), lambda j: (0, nblk + j)),
                  pl.BlockSpec((kw, HEAD), lambda j: (0, j)), pl.BlockSpec((kw, HEAD), lambda j: (0, nblk + j)),
                  pl.BlockSpec((1, HEAD), lambda j: (0, j)), pl.BlockSpec((1, HEAD), lambda j: (0, nblk + j))],
        out_specs=pl.BlockSpec((t, HEAD), lambda j: (0, j)), out_shape=jax.ShapeDtypeStruct((t, dff), BF16),
        compiler_params=_params(("parallel",)))(up, up, conv_w, conv_w, bias, bias)


def _ffn_act_bwd(up, dact, conv_w, bias, dff, jobs=()):
    t = up.shape[0]
    nblk = dff // HEAD
    kw = conv_w.shape[0]

    def body(me_ref, pa_ref, d_ref, wm_ref, wp_ref, bm_ref, bp_ref, dup_ref, dw_ref, db_ref):
        is_gate = pl.program_id(0) < nblk
        rows = lax.broadcasted_iota(jnp.int32, (t, HEAD), 0)
        xv = me_ref[...].astype(F32)
        w = wm_ref[...]
        cm = _conv(xv, w, rows) + bm_ref[...]
        cp = _conv(pa_ref[...].astype(F32), wp_ref[...], rows) + bp_ref[...]
        d = d_ref[...].astype(F32)
        dc = jnp.where(is_gate, d * cp * _silu_grad(cm), d * (cp * _sigmoid(cp)))
        db_ref[...] = jnp.sum(dc, axis=0, keepdims=True)
        dx = dc * w[kw - 1:kw, :]
        dw_ref[kw - 1:kw, :] = jnp.sum(dc * xv, axis=0, keepdims=True)
        for s in range(1, kw):
            dx = dx + _shift_up(dc, s, rows) * w[kw - 1 - s:kw - s, :]
            dw_ref[kw - 1 - s:kw - s, :] = jnp.sum(dc * _shift_down(xv, s, rows), axis=0, keepdims=True)
        dup_ref[...] = dx.astype(BF16)

    part = lambda j: (j + nblk) % (2 * nblk)
    me = pl.BlockSpec((t, HEAD), lambda j: (0, j))
    wme = pl.BlockSpec((kw, HEAD), lambda j: (0, j))
    bme = pl.BlockSpec((1, HEAD), lambda j: (0, j))
    return _call(
        body, "ffn_act_bwd", (2 * nblk,),
        [me, pl.BlockSpec((t, HEAD), lambda j: (0, part(j))), pl.BlockSpec((t, HEAD), lambda j: (0, j % nblk)),
         wme, pl.BlockSpec((kw, HEAD), lambda j: (0, part(j))),
         bme, pl.BlockSpec((1, HEAD), lambda j: (0, part(j)))],
        [me, wme, bme],
        [jax.ShapeDtypeStruct((t, 2 * dff), BF16), jax.ShapeDtypeStruct((kw, 2 * dff), F32),
         jax.ShapeDtypeStruct((1, 2 * dff), F32)],
        (up, up, dact, conv_w, conv_w, bias, bias), sem=("parallel",), jobs=jobs)


def _ple_fwd(x, gt, pp):
    t, d = x.shape
    tb, tc = _tile(t, 512, 8), _tile(d, 1024)

    def body(x_ref, g_ref, p_ref, o_ref):
        o_ref[...] = x_ref[...] + _sigmoid(g_ref[...].astype(F32)) * p_ref[...].astype(F32)

    blk = pl.BlockSpec((tb, tc), lambda i, j: (i, j))
    return pl.pallas_call(
        body, name="ple_fwd", grid=(t // tb, d // tc), in_specs=[blk, blk, blk], out_specs=blk,
        out_shape=jax.ShapeDtypeStruct((t, d), F32), compiler_params=_params(("parallel", "parallel")))(x, gt, pp)


def _ple_bwd(dx, gt, pp):
    t, d = dx.shape
    tb, tc = _tile(t, 512, 8), _tile(d, 1024)

    def body(dx_ref, g_ref, p_ref, dg_ref, dp_ref):
        dv = dx_ref[...]
        s = _sigmoid(g_ref[...].astype(F32))
        dg_ref[...] = (dv * p_ref[...].astype(F32) * s * (1.0 - s)).astype(BF16)
        dp_ref[...] = (dv * s).astype(BF16)

    blk = pl.BlockSpec((tb, tc), lambda i, j: (i, j))
    shp = jax.ShapeDtypeStruct((t, d), BF16)
    return pl.pallas_call(
        body, name="ple_bwd", grid=(t // tb, d // tc), in_specs=[blk, blk, blk], out_specs=[blk, blk],
        out_shape=[shp, shp], compiler_params=_params(("parallel", "parallel")))(dx, gt, pp)


def _adam(pieces, w, m, v, name, jobs=()):
    nq = len(pieces)
    npart, rp, c = pieces[0].shape
    row_bytes = 2 * c * (nq * npart * pieces[0].dtype.itemsize + 7 * 4)
    tr = _tile(rp, max(16, min(512, ADAM_VMEM_BUDGET // row_bytes)), 16)
    nblk = rp // tr
    c1 = 1.0 - ADAM_B1 ** ADAM_STEP
    c2 = 1.0 - ADAM_B2 ** ADAM_STEP

    def body(*refs):
        p_refs = refs[:nq]
        w_ref, m_ref, v_ref, g_ref, d_ref, mo_ref, vo_ref = refs[nq:]
        for q in range(nq):
            @pl.when(pl.program_id(0) == q)
            def _(p_ref=p_refs[q]):
                g = p_ref[0].astype(F32)
                for i in range(1, npart):
                    g = g + p_ref[i].astype(F32)
                mn = ADAM_B1 * m_ref[...] + (1.0 - ADAM_B1) * g
                vn = ADAM_B2 * v_ref[...] + (1.0 - ADAM_B2) * (g * g)
                g_ref[...] = g
                mo_ref[...] = mn
                vo_ref[...] = vn
                d_ref[...] = -ADAM_LR * ((mn / c1) / (jnp.sqrt(vn / c2) + ADAM_EPS) + ADAM_WD * w_ref[...])

    def piece_spec(q):
        return pl.BlockSpec((npart, tr, c), lambda i, r: (0, jnp.where(i == q, r, jnp.where(i < q, 0, nblk - 1)), 0))

    blk = pl.BlockSpec((tr, c), lambda i, r: (i * nblk + r, 0))
    shp = jax.ShapeDtypeStruct((nq * rp, c), F32)
    return _call(body, name, (nq, nblk), [piece_spec(q) for q in range(nq)] + [blk] * 3, [blk] * 4, [shp] * 4,
                 (*pieces, w, m, v), sem=("parallel", "parallel"), jobs=jobs)


_BIG = ("w_in", "w_branch_a", "w_branch_b", "w_out", "w_ffn_up", "w_ffn_down", "w_ple_gate", "w_ple_proj")
_COL_SHARDED = ("w_in", "w_branch_b", "w_ffn_up", "w_ple_proj")
_CONVS = ("conv_qkv", "conv_ffn")
_GATHER_AHEAD = ("w_in", "conv_qkv")
_GATHER_ON_PROJ = ("w_branch_a", "w_branch_b", "w_out", "conv_ffn")
_GATHER_ON_DELTA = ("w_ffn_up",)
_GATHER_ON_UP = ("w_ffn_down", "w_ple_gate", "w_ple_proj")
_SCATTER_ON_FFN = ("w_ple_gate", "w_ple_proj", "w_ffn_down")
_SCATTER_ON_DELTA = ("w_ffn_up", "w_out", "w_branch_a", "w_branch_b")
_SMALL = ("norm_mix", "conv_qkv", "a_log", "dt_bias", "head_norm", "sgu_norm", "w_spatial", "b_spatial", "norm_ffn",
          "conv_ffn", "b_conv_ffn", "norm_ple", "norm_final")
_WEIGHTS = ("norm_mix", "w_in", "conv_qkv", "a_log", "dt_bias", "head_norm", "sgu_norm", "w_spatial", "b_spatial",
            "w_branch_a", "w_branch_b", "w_out", "norm_ffn", "w_ffn_up", "conv_ffn", "b_conv_ffn", "w_ffn_down",
            "norm_ple", "w_ple_gate", "w_ple_proj", "norm_final")


def _full_cols(g):
    return jnp.transpose(g, (1, 0, 2)).reshape(g.shape[1], N_DEV * g.shape[2])


def _full_rows(g):
    return g.reshape(N_DEV * g.shape[1], g.shape[2])


def _split_cols(dw):
    k, n = dw.shape
    return jnp.transpose(dw.reshape(k, N_DEV, n // N_DEV), (1, 0, 2))


def _split_rows(dw):
    k, n = dw.shape
    return dw.reshape(N_DEV, k // N_DEV, n)


def _pad_lanes(v, width=LANES, offset=0):
    return jnp.pad(v, ((0, 0), (offset, width - offset - v.shape[1])))


def kernel(x, p, norm_mix, w_in, conv_qkv, a_log, dt_bias, head_norm, sgu_norm, w_spatial, b_spatial, w_branch_a, w_branch_b, w_out, norm_ffn, w_ffn_up, conv_ffn, b_conv_ffn, w_ffn_down, norm_ple, w_ple_gate, w_ple_proj, norm_final, loss_target, m_norm_mix, m_w_in, m_conv_qkv, m_a_log, m_dt_bias, m_head_norm, m_sgu_norm, m_w_spatial, m_b_spatial, m_w_branch_a, m_w_branch_b, m_w_out, m_norm_ffn, m_w_ffn_up, m_conv_ffn, m_b_conv_ffn, m_w_ffn_down, m_norm_ple, m_w_ple_gate, m_w_ple_proj, m_norm_final, v_norm_mix, v_w_in, v_conv_qkv, v_a_log, v_dt_bias, v_head_norm, v_sgu_norm, v_w_spatial, v_b_spatial, v_w_branch_a, v_w_branch_b, v_w_out, v_norm_ffn, v_w_ffn_up, v_conv_ffn, v_b_conv_ffn, v_w_ffn_down, v_norm_ple, v_w_ple_gate, v_w_ple_proj, v_norm_final):
    env = dict(locals())
    wts = {n: env[n] for n in _WEIGHTS}
    mom_m = {n: env["m_" + n] for n in _WEIGHTS}
    mom_v = {n: env["v_" + n] for n in _WEIGHTS}

    xin = x[0]
    tgt = loss_target[0]
    t, d = xin.shape
    depth = w_in.shape[0]
    hv = a_log.shape[1]
    vw = hv * HEAD
    wb = sgu_norm.shape[1]
    ng = w_spatial.shape[1]
    n_in = w_in.shape[2] * N_DEV
    qk = (n_in - 2 * vw - 2 * hv - 2 * wb - 2 * d) // 2
    hqk = qk // HEAD
    dff = w_ffn_down.shape[1] * N_DEV
    cw = 2 * qk + vw
    o_z, o_ba = 2 * qk + vw, 2 * qk + 2 * vw
    o_ub = o_ba
    o_ga = o_ub + 2 * wb
    me = 4 * lax.axis_index("x") + 2 * lax.axis_index("y") + lax.axis_index("c")

    full = [dict() for _ in range(depth)]

    def gather_jobs(i, names):
        return [(wts[n][i].astype(BF16) if n in _BIG else wts[n][i], False) for n in names]

    def take(i, names, results):
        for n, g in zip(names, results):
            full[i][n] = _full_cols(g) if n in _COL_SHARDED or n in _CONVS else _full_rows(g)

    take(0, _GATHER_AHEAD, _exchange(gather_jobs(0, _GATHER_AHEAD), "gather_first"))

    saved = []
    xc = xin
    for i in range(depth):
        fw = full[i]
        w_full = fw["w_in"]
        fw["w_main"] = jnp.concatenate([w_full[:, :o_ba], w_full[:, o_ba + 2 * hv:]], axis=1)
        fw["w_ba"] = _pad_lanes(w_full[:, o_ba:o_ba + 2 * hv])
        s = {"x0": xc}
        s["h1"] = _rms_fwd(xc, norm_mix[i], "rms_fwd")
        s["proj"], got = _matmul(s["h1"], fw["w_main"], "nn", BF16, "mm_proj", jobs=gather_jobs(i, _GATHER_ON_PROJ))
        take(i, _GATHER_ON_PROJ, got)
        s["ba"] = _matmul(s["h1"], fw["w_ba"], "nn", F32, "mm_ba")
        s["qkvn"] = _qkv_fwd(s["proj"], fw["conv_qkv"], hqk, hqk)
        s["alog"] = _pad_lanes(a_log[i][None, :], offset=hv)
        s["dtb"] = _pad_lanes(dt_bias[i][None, :], offset=hv)
        bg = _gate_fwd(s["ba"], s["alog"], s["dtb"], hv)
        beta_t = bg[:, :hv].T
        gam_t = bg[:, hv:2 * hv].T
        s["beta_b"] = jnp.broadcast_to(beta_t[:, :, None], (hv, t, LANES))
        s["gam_b"] = jnp.broadcast_to(gam_t[:, :, None], (hv, t, LANES))
        s["gam_r"] = gam_t.reshape(hv, t // CA, CA)
        ahead = gather_jobs(i + 1, _GATHER_AHEAD) if i + 1 < depth else []
        (s["o"], s["s_all"], s["tm_all"]), got = _delta_fwd(
            s["qkvn"], s["beta_b"], s["gam_b"], s["gam_r"], hqk, hv, jobs=gather_jobs(i, _GATHER_ON_DELTA) + ahead)
        take(i, _GATHER_ON_DELTA, got)
        if ahead:
            take(i + 1, _GATHER_AHEAD, got[len(_GATHER_ON_DELTA):])
        s["outa"] = _apost_fwd(s["o"], s["proj"], head_norm[i], o_z, hv)
        s["b_t"] = b_spatial[i].T
        s["outb"] = _sgu_fwd(s["proj"], sgu_norm[i], w_spatial[i], s["b_t"], o_ub, wb)
        s["ya"] = _matmul(s["outa"], fw["w_branch_a"], "nn", BF16, "mm_ya")
        s["yb"] = _matmul(s["outb"], fw["w_branch_b"], "nn", BF16, "mm_yb")
        s["mg"] = _merge_fwd(s["ya"], s["yb"], s["proj"], o_ga)
        s["x1"] = _matmul(s["mg"], fw["w_out"], "nn", F32, "mm_out", res=xc)
        s["h2"] = _rms_fwd(s["x1"], norm_ffn[i], "rms_fwd")
        s["up"], got = _matmul(s["h2"], fw["w_ffn_up"], "nn", BF16, "mm_up", jobs=gather_jobs(i, _GATHER_ON_UP))
        take(i, _GATHER_ON_UP, got)
        s["bias"] = b_conv_ffn[i][None, :]
        s["act"] = _ffn_act_fwd(s["up"], fw["conv_ffn"], s["bias"], dff)
        s["x2"] = _matmul(s["act"], fw["w_ffn_down"], "nn", F32, "mm_down", res=s["x1"])
        s["h3"] = _rms_fwd(s["x2"], norm_ple[i], "rms_fwd")
        s["gt"] = _matmul(s["h3"], fw["w_ple_gate"], "nn", BF16, "mm_gt")
        s["pp"] = _matmul(p[i, 0], fw["w_ple_proj"], "nn", BF16, "mm_pp")
        xc = _ple_fwd(s["x2"], s["gt"], s["pp"])
        saved.append(s)

    dx, g_norm_final, loss_part = _loss_head(xc, tgt, norm_final)

    small = {n: [None] * depth for n in _SMALL if n != "norm_final"}
    recv = {n: [None] * depth for n in _BIG}
    recv["w_in"] = [None] * (2 * depth)

    def scatter_jobs(gw, names):
        return [(_split_cols(gw[n]) if n in _COL_SHARDED else _split_rows(gw[n]), True) for n in names]

    def keep(i, names, results):
        for n, r in zip(names, results):
            recv[n][i] = r

    def carry(jobs, *args):
        return _matmul(*args, jobs=jobs) if jobs else (_matmul(*args), [])

    later = []
    for i in reversed(range(depth)):
        fw, s = full[i], saved[i]
        dgt, dpp = _ple_bwd(dx, s["gt"], s["pp"])
        gw = {"w_ple_gate": _matmul(s["h3"], dgt, "tn", BF16, "mm_dw_gt"),
              "w_ple_proj": _matmul(p[i, 0], dpp, "tn", BF16, "mm_dw_pp")}
        dh3 = _matmul(dgt, fw["w_ple_gate"], "nt", F32, "mm_dh3")
        dx, small["norm_ple"][i] = _rms_bwd(s["x2"], dh3, norm_ple[i], dx, "rms_bwd")

        dact = _matmul(dx, fw["w_ffn_down"], "nt", BF16, "mm_dact")
        gw["w_ffn_down"] = _matmul(s["act"], dx, "tn", BF16, "mm_dw_down")
        (dup, small["conv_ffn"][i], small["b_conv_ffn"][i]), got = _ffn_act_bwd(
            s["up"], dact, fw["conv_ffn"], s["bias"], dff, jobs=scatter_jobs(gw, _SCATTER_ON_FFN))
        keep(i, _SCATTER_ON_FFN, got)
        gw["w_ffn_up"], got = carry(later[:1], s["h2"], dup, "tn", BF16, "mm_dw_up")
        keep(2 * i + 2, ("w_in",), got)
        dh2, got = carry(later[1:], dup, fw["w_ffn_up"], "nt", F32, "mm_dh2")
        keep(2 * i + 3, ("w_in",), got)
        dx, small["norm_ffn"][i] = _rms_bwd(s["x1"], dh2, norm_ffn[i], dx, "rms_bwd")

        dmg = _matmul(dx, fw["w_out"], "nt", BF16, "mm_dmg")
        gw["w_out"] = _matmul(s["mg"], dx, "tn", BF16, "mm_dw_out")
        dya, dyb, dga, dgb = _merge_bwd(dmg, s["ya"], s["yb"], s["proj"], o_ga)
        gw["w_branch_a"] = _matmul(s["outa"], dya, "tn", BF16, "mm_dw_a")
        gw["w_branch_b"] = _matmul(s["outb"], dyb, "tn", BF16, "mm_dw_b")
        douta = _matmul(dya, fw["w_branch_a"], "nt", BF16, "mm_douta")
        doutb = _matmul(dyb, fw["w_branch_b"], "nt", BF16, "mm_doutb")
        dub, dvb, small["w_spatial"][i], db_s, dsg = _sgu_bwd(s["proj"], sgu_norm[i], w_spatial[i], s["b_t"], doutb, o_ub, wb)
        small["b_spatial"][i] = db_s[:, :, 0]
        small["sgu_norm"][i] = dsg
        do, dz, small["head_norm"][i] = _apost_bwd(s["o"], s["proj"], head_norm[i], douta, o_z, hv)
        (dq, dk, dv, db_b, dg_b), got = _delta_bwd(
            s["qkvn"], s["beta_b"], s["gam_b"], s["gam_r"], s["s_all"], s["tm_all"], do, hqk, hv,
            jobs=scatter_jobs(gw, _SCATTER_ON_DELTA))
        keep(i, _SCATTER_ON_DELTA, got)
        dbg = _pad_lanes(jnp.concatenate([db_b[:, :, 0].T, dg_b[:, :, 0].T], axis=1))
        dba, dal, ddt = _gate_bwd(s["ba"], dbg, s["alog"], s["dtb"], hv)
        small["a_log"][i] = dal[:, hv:2 * hv]
        small["dt_bias"][i] = ddt[:, hv:2 * hv]
        dqkv_pre, small["conv_qkv"][i] = _qkv_bwd(s["proj"], jnp.concatenate([dq, dk, dv], axis=1), fw["conv_qkv"], hqk, hqk)
        dproj = jnp.concatenate([dqkv_pre, dz, dub, dvb, dga, dgb], axis=1)
        dw_main = _matmul(s["h1"], dproj, "tn", BF16, "mm_dw_main")
        dw_ba = _matmul(s["h1"], dba, "tn", BF16, "mm_dw_ba")
        dw_in = jnp.concatenate([dw_main[:, :o_ba], dw_ba[:, :2 * hv], dw_main[:, o_ba:]], axis=1)
        dh1 = _matmul(dproj, fw["w_main"], "nt", F32, "mm_dh1")
        dh1 = _matmul(dba, fw["w_ba"], "nt", F32, "mm_dh1_ba", res=dh1)
        dx, small["norm_mix"][i] = _rms_bwd(s["x0"], dh1, norm_mix[i], dx, "rms_bwd")

        later = [(_split_cols(dw_in[:d // 2]), True), (_split_cols(dw_in[d // 2:]), True)]

    recv["w_in"][0], recv["w_in"][1] = _exchange(later, "scatter_last")

    rep_names = tuple(n for n in _SMALL if n not in _CONVS)
    stacked = {n: jnp.concatenate([jnp.reshape(a, (-1,)) for a in small[n]]) for n in small}
    stacked["norm_final"] = g_norm_final.reshape(-1)

    def padded(parts, mult, axis=0):
        flat = jnp.concatenate(parts, axis=axis)
        pad = -flat.shape[axis] % mult
        return jnp.pad(flat, [(0, 0)] * axis + [(0, pad)])

    rep_flat = padded([stacked[n] for n in rep_names] + [loss_part[0, :1]], 16 * LANES)
    conv_flat = [padded([stacked[n]], 8 * LANES) for n in _CONVS]
    packed = jnp.concatenate([rep_flat] + conv_flat).reshape(-1, LANES)

    outs_g, outs_d, outs_m, outs_v = {}, {}, {}, {}

    for n in _BIG:
        shp = wts[n].shape
        two_d = lambda a: a.reshape(shp[0] * shp[1], shp[2])
        res, got = _adam(recv[n], two_d(wts[n]), two_d(mom_m[n]), two_d(mom_v[n]), "adam_" + n,
                         jobs=[(packed, False)] if n == "w_in" else [])
        if got:
            small_all = got[0].reshape(N_DEV, -1)
        outs_g[n], outs_d[n], outs_m[n], outs_v[n] = [r.reshape(shp) for r in res]

    n_rep = rep_flat.shape[0]
    pk = lambda src: padded([src[n].reshape(-1) for n in rep_names] + [jnp.zeros((1,), F32)], 16 * LANES).reshape(-1, LANES)
    res, _ = _adam([small_all[:, :n_rep].reshape(N_DEV, -1, LANES)], pk(wts), pk(mom_m), pk(mom_v), "adam_small")
    res = [r.reshape(-1) for r in res]
    off = 0
    for n in rep_names:
        shp = wts[n].shape
        size = math.prod(shp)
        outs_g[n], outs_d[n], outs_m[n], outs_v[n] = [r[off:off + size].reshape(shp) for r in res]
        off += size
    loss = res[0][off]

    off = n_rep
    for n, cf in zip(_CONVS, conv_flat):
        _, kw, cl = wts[n].shape
        part = small_all[:, off:off + depth * kw * cl * N_DEV].reshape(N_DEV, depth * kw, N_DEV, cl)
        off += cf.shape[0]
        part = lax.dynamic_index_in_dim(part, me, axis=2, keepdims=False)
        two_d = lambda a: a.reshape(depth * kw, cl)
        res, _ = _adam([part], two_d(wts[n]), two_d(mom_m[n]), two_d(mom_v[n]), "adam_" + n)
        outs_g[n], outs_d[n], outs_m[n], outs_v[n] = [r.reshape(wts[n].shape) for r in res]

    return (loss, dx[None], *[outs_g[n] for n in _WEIGHTS], *[outs_d[n] for n in _WEIGHTS],
            *[outs_m[n] for n in _WEIGHTS], *[outs_v[n] for n in _WEIGHTS])
```

```python
import functools
import math

import jax
import jax.numpy as jnp
from jax import lax
from jax.experimental import pallas as pl
from jax.experimental.pallas import tpu as pltpu

F32 = jnp.float32
BF16 = jnp.bfloat16
EPS = 1e-6
LANES = 128
HEAD = 128
CA = 64
N_DEV = 8
VMEM_LIMIT = 48 * 1024 * 1024
ADAM_VMEM_BUDGET = 16 * 1024 * 1024
MESH = pl.DeviceIdType.MESH

ADAM_LR = 0.001
ADAM_B1 = 0.9
ADAM_B2 = 0.999
ADAM_EPS = 1e-08
ADAM_WD = 0.01
ADAM_STEP = 10


def _tile(n, cap, mult=LANES):
    best = None
    for t in range(mult, min(n, cap) + 1, mult):
        if n % t == 0:
            best = t
    return n if best is None else best


def _params(sem):
    return pltpu.CompilerParams(dimension_semantics=sem, vmem_limit_bytes=VMEM_LIMIT)


def _sigmoid(v):
    return jax.nn.sigmoid(v)


def _silu_grad(c):
    s = _sigmoid(c)
    return s + c * s * (1.0 - s)


_GELU_C = math.sqrt(2.0 / math.pi)


def _gelu(v):
    return 0.5 * v * (1.0 + jnp.tanh(_GELU_C * (v + 0.044715 * v * v * v)))


def _gelu_grad(v):
    t = jnp.tanh(_GELU_C * (v + 0.044715 * v * v * v))
    return 0.5 * (1.0 + t) + 0.5 * v * (1.0 - t * t) * _GELU_C * (1.0 + 3.0 * 0.044715 * v * v)


_NN = (((1,), (0,)), ((), ()))
_NT = (((1,), (1,)), ((), ()))
_TN = (((0,), (0,)), ((), ()))
_BNN = (((2,), (1,)), ((0,), (0,)))
_BNT = (((2,), (2,)), ((0,), (0,)))
_BTN = (((1,), (1,)), ((0,), (0,)))


def _bdot(a, b, dn=_NN):
    return lax.dot_general(a.astype(BF16), b.astype(BF16), dn, preferred_element_type=F32)


def _split(a):
    hi = a.astype(BF16)
    return hi, (a - hi.astype(F32)).astype(BF16)


def _dot3(ah, al, bh, bl, dn=_NN):
    def d(u, v):
        return lax.dot_general(u, v, dn, preferred_element_type=F32)
    return d(ah, bh) + (d(al, bh) + d(ah, bl))


def _hdot(a, b, dn=_NN):
    return _dot3(*_split(a), *_split(b), dn)


def _xchg_out_shapes(jobs):
    return [jax.ShapeDtypeStruct(a.shape if sc else (N_DEV,) + a.shape, a.dtype) for a, sc in jobs]


def _xchg_scratch(jobs):
    n = len(jobs)
    return [pltpu.SemaphoreType.DMA((n, N_DEV - 1)), pltpu.SemaphoreType.DMA((n, N_DEV - 1)), pltpu.SemaphoreType.DMA((n,))]


def _xchg_copies(scatter, src, out, sems):
    send_sems, recv_sems, local_sems = sems
    x, y, c = lax.axis_index("x"), lax.axis_index("y"), lax.axis_index("c")
    me = 4 * x + 2 * y + c
    local, sends, recvs = [], [], []
    for k, sc in enumerate(scatter):
        local.append(pltpu.make_async_copy(src[k].at[me] if sc else src[k], out[k].at[me], local_sems.at[k]))
    for m in range(1, N_DEV):
        px = lax.rem(x + ((m >> 2) & 1), 2)
        py = lax.rem(y + ((m >> 1) & 1), 2)
        pc = lax.rem(c + (m & 1), 2)
        peer = 4 * px + 2 * py + pc
        for k, sc in enumerate(scatter):
            mine = src[k].at[peer] if sc else src[k]
            for dst, lst in ((out[k].at[me], sends), (out[k].at[peer], recvs)):
                lst.append(pltpu.make_async_remote_copy(
                    src_ref=mine, dst_ref=dst, send_sem=send_sems.at[k, m - 1], recv_sem=recv_sems.at[k, m - 1],
                    device_id=(px, py, pc), device_id_type=MESH))
    return local, sends, recvs


def _xchg_start(scatter, src, out, sems):
    local, sends, _ = _xchg_copies(scatter, src, out, sems)
    for cp in local + sends:
        cp.start()


def _xchg_wait(scatter, src, out, sems):
    local, sends, recvs = _xchg_copies(scatter, src, out, sems)
    for cp in recvs:
        cp.wait_recv()
    for cp in sends:
        cp.wait_send()
    for cp in local:
        cp.wait()


_ANY = pl.BlockSpec(memory_space=pl.ANY)


def _exchange(jobs, name):
    n = len(jobs)
    scatter = [sc for _, sc in jobs]

    def body(*refs):
        src, out, sems = refs[:n], refs[n:2 * n], refs[2 * n:]
        _xchg_start(scatter, src, out, sems)
        _xchg_wait(scatter, src, out, sems)

    return pl.pallas_call(
        body, name=name, in_specs=[_ANY] * n, out_specs=[_ANY] * n, out_shape=_xchg_out_shapes(jobs),
        scratch_shapes=_xchg_scratch(jobs),
        compiler_params=pltpu.CompilerParams(has_side_effects=True))(*[a for a, _ in jobs])


def _carried(body, n_in, n_out, jobs, grid):
    if not jobs:
        return body
    nj = len(jobs)
    scatter = [sc for _, sc in jobs]

    def wrapped(*refs):
        ins, src = refs[:n_in], refs[n_in:n_in + nj]
        outs, got = refs[n_in + nj:n_in + nj + n_out], refs[n_in + nj + n_out:n_in + 2 * nj + n_out]
        rest = refs[n_in + 2 * nj + n_out:]
        scratch, sems = rest[:len(rest) - 3], rest[len(rest) - 3:]
        ids = [pl.program_id(a) for a in range(len(grid))]
        first = functools.reduce(jnp.logical_and, [i == 0 for i in ids])
        last = functools.reduce(jnp.logical_and, [i == g - 1 for i, g in zip(ids, grid)])

        @pl.when(first)
        def _():
            _xchg_start(scatter, src, got, sems)

        body(*ins, *outs, *scratch)

        @pl.when(last)
        def _():
            _xchg_wait(scatter, src, got, sems)

    return wrapped


def _call(body, name, grid, in_specs, out_specs, out_shape, args, scratch=(), sem=None, jobs=()):
    jobs = list(jobs)
    nj = len(jobs)
    sem = ("arbitrary",) * len(grid) if jobs or sem is None else sem
    res = pl.pallas_call(
        _carried(body, len(in_specs), len(out_specs), jobs, grid), name=name, grid=grid,
        in_specs=list(in_specs) + [_ANY] * nj, out_specs=list(out_specs) + [_ANY] * nj,
        out_shape=list(out_shape) + _xchg_out_shapes(jobs),
        scratch_shapes=list(scratch) + (_xchg_scratch(jobs) if jobs else []),
        compiler_params=_params(sem))(*args, *[a for a, _ in jobs])
    return res[:len(out_specs)], res[len(out_specs):]


MATMUL_OPERAND_VMEM = 20 * 1024 * 1024


def _matmul(a, b, mode, out_dtype, name, res=None, jobs=()):
    if mode == "tn":
        kdim, m = a.shape
    else:
        m, kdim = a.shape
    n = b.shape[0] if mode == "nt" else b.shape[1]
    tm, tn = _tile(m, 1024), _tile(n, 1024)
    per_k = 2 * (tm * a.dtype.itemsize + tn * b.dtype.itemsize)
    tk = _tile(kdim, max(LANES, MATMUL_OPERAND_VMEM // per_k))
    nk = kdim // tk
    dims = {"nn": _NN, "nt": _NT, "tn": _TN}[mode]

    def body(*refs):
        a_ref, b_ref = refs[:2]
        r_ref = refs[2] if res is not None else None
        o_ref = refs[3] if res is not None else refs[2]
        acc = refs[-1] if nk > 1 else None

        def write(r):
            if r_ref is not None:
                r = r + r_ref[...].astype(F32)
            o_ref[...] = r.astype(out_dtype)

        prod = _bdot(a_ref[...], b_ref[...], dims)
        if nk == 1:
            write(prod)
        else:
            k = pl.program_id(2)

            @pl.when(k == 0)
            def _():
                acc[...] = prod

            @pl.when(jnp.logical_and(k > 0, k < nk - 1))
            def _():
                acc[...] += prod

            @pl.when(k == nk - 1)
            def _():
                write(acc[...] + prod)

    a_spec = pl.BlockSpec((tk, tm), lambda i, j, k: (k, i)) if mode == "tn" else pl.BlockSpec((tm, tk), lambda i, j, k: (i, k))
    b_spec = pl.BlockSpec((tn, tk), lambda i, j, k: (j, k)) if mode == "nt" else pl.BlockSpec((tk, tn), lambda i, j, k: (k, j))
    o_spec = pl.BlockSpec((tm, tn), lambda i, j, k: (i, j))
    in_specs = [a_spec, b_spec] + ([o_spec] if res is not None else [])
    args = (a, b) + ((res,) if res is not None else ())
    (out,), got = _call(body, name, (m // tm, n // tn, nk), in_specs, [o_spec], [jax.ShapeDtypeStruct((m, n), out_dtype)],
                        args, scratch=[pltpu.VMEM((tm, tn), F32)] if nk > 1 else [],
                        sem=("parallel", "parallel", "arbitrary"), jobs=jobs)
    return (out, got) if jobs else out


def _rms_fwd(x, gain, name):
    t, d = x.shape
    tb = _tile(t, 256, 8)

    def body(x_ref, g_ref, h_ref):
        xv = x_ref[...]
        r = lax.rsqrt(jnp.mean(xv * xv, axis=-1, keepdims=True) + EPS)
        h_ref[...] = (xv * r * g_ref[...]).astype(BF16)

    return pl.pallas_call(
        body, name=name, grid=(t // tb,),
        in_specs=[pl.BlockSpec((tb, d), lambda i: (i, 0)), pl.BlockSpec((1, d), lambda i: (0, 0))],
        out_specs=pl.BlockSpec((tb, d), lambda i: (i, 0)), out_shape=jax.ShapeDtypeStruct((t, d), BF16),
        compiler_params=_params(("parallel",)))(x, gain.reshape(1, d))


def _rms_bwd(x, dh, gain, dres, name):
    t, d = x.shape
    tb = _tile(t, 256, 8)

    def body(x_ref, dh_ref, g_ref, dr_ref, dx_ref, dg_ref):
        @pl.when(pl.program_id(0) == 0)
        def _():
            dg_ref[...] = jnp.zeros_like(dg_ref)

        xv = x_ref[...]
        dy = dh_ref[...].astype(F32)
        r = lax.rsqrt(jnp.mean(xv * xv, axis=-1, keepdims=True) + EPS)
        xh = xv * r
        dxh = dy * g_ref[...]
        dx_ref[...] = dr_ref[...] + r * (dxh - xh * jnp.mean(dxh * xh, axis=-1, keepdims=True))
        dg_ref[...] += jnp.sum(dy * xh, axis=0, keepdims=True)

    row = pl.BlockSpec((tb, d), lambda i: (i, 0))
    vec = pl.BlockSpec((1, d), lambda i: (0, 0))
    return pl.pallas_call(
        body, name=name, grid=(t // tb,), in_specs=[row, row, vec, row], out_specs=[row, vec],
        out_shape=[jax.ShapeDtypeStruct((t, d), F32), jax.ShapeDtypeStruct((1, d), F32)],
        compiler_params=_params(("arbitrary",)))(x, dh, gain.reshape(1, d), dres)


def _loss_head(x, target, gain):
    t, d = x.shape
    tb = _tile(t, 256, 8)

    def body(x_ref, t_ref, g_ref, dx_ref, dg_ref, loss_ref):
        @pl.when(pl.program_id(0) == 0)
        def _():
            dg_ref[...] = jnp.zeros_like(dg_ref)
            loss_ref[...] = jnp.zeros_like(loss_ref)

        xv = x_ref[...]
        r = lax.rsqrt(jnp.mean(xv * xv, axis=-1, keepdims=True) + EPS)
        xh = xv * r
        err = xh * g_ref[...] - t_ref[...]
        per_row = jnp.mean(err * err, axis=-1, keepdims=True)
        loss_ref[...] += 0.5 * jnp.sum(per_row, axis=0, keepdims=True)
        dy = err * (1.0 / d)
        dxh = dy * g_ref[...]
        dx_ref[...] = r * (dxh - xh * jnp.mean(dxh * xh, axis=-1, keepdims=True))
        dg_ref[...] += jnp.sum(dy * xh, axis=0, keepdims=True)

    row = pl.BlockSpec((tb, d), lambda i: (i, 0))
    vec = pl.BlockSpec((1, d), lambda i: (0, 0))
    return pl.pallas_call(
        body, name="loss_head", grid=(t // tb,), in_specs=[row, row, vec],
        out_specs=[row, vec, pl.BlockSpec((1, LANES), lambda i: (0, 0))],
        out_shape=[jax.ShapeDtypeStruct((t, d), F32), jax.ShapeDtypeStruct((1, d), F32),
                   jax.ShapeDtypeStruct((1, LANES), F32)],
        compiler_params=_params(("arbitrary",)))(x, target, gain.reshape(1, d))


def _shift_down(v, s, rows):
    if s == 0:
        return v
    return jnp.where(rows >= s, pltpu.roll(v, s, 0), 0.0)


def _shift_up(v, s, rows):
    if s == 0:
        return v
    t = v.shape[0]
    return jnp.where(rows < t - s, pltpu.roll(v, t - s, 0), 0.0)


def _conv(v, w, rows):
    k = w.shape[0]
    out = v * w[k - 1:k, :]
    for s in range(1, k):
        out = out + _shift_down(v, s, rows) * w[k - 1 - s:k - s, :]
    return out


def _qkv_fwd(proj, conv_w, nq, nk):
    t = proj.shape[0]
    cw = conv_w.shape[1]
    nblk = cw // HEAD

    def body(p_ref, w_ref, o_ref):
        j = pl.program_id(0)
        rows = lax.broadcasted_iota(jnp.int32, (t, HEAD), 0)
        c = _conv(p_ref[...].astype(F32), w_ref[...], rows)
        a = c * _sigmoid(c)
        nrm = a * lax.rsqrt(jnp.sum(a * a, axis=-1, keepdims=True) + EPS)
        nrm = nrm * jnp.where(j < nq, HEAD ** -0.5, 1.0)
        o_ref[...] = jnp.where(j < nq + nk, nrm, a).astype(BF16)

    return pl.pallas_call(
        body, name="qkv_fwd", grid=(nblk,),
        in_specs=[pl.BlockSpec((t, HEAD), lambda j: (0, j)), pl.BlockSpec((conv_w.shape[0], HEAD), lambda j: (0, j))],
        out_specs=pl.BlockSpec((t, HEAD), lambda j: (0, j)), out_shape=jax.ShapeDtypeStruct((t, cw), BF16),
        compiler_params=_params(("parallel",)))(proj, conv_w)


def _qkv_bwd(proj, dqkv, conv_w, nq, nk):
    t = proj.shape[0]
    kw, cw = conv_w.shape
    nblk = cw // HEAD

    def body(p_ref, d_ref, w_ref, dp_ref, dw_ref):
        j = pl.program_id(0)
        rows = lax.broadcasted_iota(jnp.int32, (t, HEAD), 0)
        xv = p_ref[...].astype(F32)
        w = w_ref[...]
        c = _conv(xv, w, rows)
        a = c * _sigmoid(c)
        dy = d_ref[...].astype(F32)
        r = lax.rsqrt(jnp.sum(a * a, axis=-1, keepdims=True) + EPS)
        y = a * r
        scale = jnp.where(j < nq, HEAD ** -0.5, 1.0)
        da_n = scale * r * (dy - y * jnp.sum(dy * y, axis=-1, keepdims=True))
        da = jnp.where(j < nq + nk, da_n, dy)
        dc = da * _silu_grad(c)
        dx = dc * w[kw - 1:kw, :]
        dw_ref[kw - 1:kw, :] = jnp.sum(dc * xv, axis=0, keepdims=True)
        for s in range(1, kw):
            dx = dx + _shift_up(dc, s, rows) * w[kw - 1 - s:kw - s, :]
            dw_ref[kw - 1 - s:kw - s, :] = jnp.sum(dc * _shift_down(xv, s, rows), axis=0, keepdims=True)
        dp_ref[...] = dx.astype(BF16)

    blk = pl.BlockSpec((t, HEAD), lambda j: (0, j))
    wblk = pl.BlockSpec((kw, HEAD), lambda j: (0, j))
    return pl.pallas_call(
        body, name="qkv_bwd", grid=(nblk,), in_specs=[blk, blk, wblk], out_specs=[blk, wblk],
        out_shape=[jax.ShapeDtypeStruct((t, cw), BF16), jax.ShapeDtypeStruct((kw, cw), F32)],
        compiler_params=_params(("parallel",)))(proj, dqkv, conv_w)


def _softplus(v):
    return jnp.where(v < -15.0, jnp.exp(v), jnp.maximum(v, 0.0) + jnp.log(1.0 + jnp.exp(-jnp.abs(v))))


def _gate_fwd(ba, alog_pad, dtb_pad, hv):
    t = ba.shape[0]
    tb = _tile(t, 512, CA)

    def body(ba_ref, al_ref, dt_ref, o_ref):
        v = ba_ref[...]
        beta = _sigmoid(v)
        g = -jnp.exp(al_ref[...]) * _softplus(v + dt_ref[...])
        pos = lax.broadcasted_iota(jnp.int32, (tb, LANES), 0) % CA
        s = 1
        while s < CA:
            g = g + jnp.where(pos >= s, pltpu.roll(g, s, 0), 0.0)
            s *= 2
        lane = lax.broadcasted_iota(jnp.int32, (tb, LANES), 1)
        o_ref[...] = jnp.where(lane < hv, beta, g)

    row = pl.BlockSpec((tb, LANES), lambda i: (i, 0))
    vec = pl.BlockSpec((1, LANES), lambda i: (0, 0))
    return pl.pallas_call(
        body, name="gate_fwd", grid=(t // tb,), in_specs=[row, vec, vec], out_specs=row,
        out_shape=jax.ShapeDtypeStruct((t, LANES), F32), compiler_params=_params(("parallel",)))(ba, alog_pad, dtb_pad)


def _gate_bwd(ba, dbg, alog_pad, dtb_pad, hv):
    t = ba.shape[0]
    tb = _tile(t, 512, CA)

    def body(ba_ref, d_ref, al_ref, dt_ref, dba_ref, dal_ref, ddt_ref):
        @pl.when(pl.program_id(0) == 0)
        def _():
            dal_ref[...] = jnp.zeros_like(dal_ref)
            ddt_ref[...] = jnp.zeros_like(ddt_ref)

        v = ba_ref[...]
        d = d_ref[...]
        pos = lax.broadcasted_iota(jnp.int32, (tb, LANES), 0) % CA
        dg = d
        s = 1
        while s < CA:
            dg = dg + jnp.where(pos < CA - s, pltpu.roll(dg, tb - s, 0), 0.0)
            s *= 2
        beta = _sigmoid(v)
        na = -jnp.exp(al_ref[...])
        z = v + dt_ref[...]
        da = dg * na * _sigmoid(z)
        lane = lax.broadcasted_iota(jnp.int32, (tb, LANES), 1)
        in_a = jnp.logical_and(lane >= hv, lane < 2 * hv)
        da = jnp.where(in_a, da, 0.0)
        dba_ref[...] = jnp.where(lane < hv, d * beta * (1.0 - beta), da)
        ddt_ref[...] += jnp.sum(da, axis=0, keepdims=True)
        dal_ref[...] += jnp.sum(jnp.where(in_a, dg * na * _softplus(z), 0.0), axis=0, keepdims=True)

    row = pl.BlockSpec((tb, LANES), lambda i: (i, 0))
    vec = pl.BlockSpec((1, LANES), lambda i: (0, 0))
    return pl.pallas_call(
        body, name="gate_bwd", grid=(t // tb,), in_specs=[row, row, vec, vec], out_specs=[row, vec, vec],
        out_shape=[jax.ShapeDtypeStruct((t, LANES), F32), jax.ShapeDtypeStruct((1, LANES), F32),
                   jax.ShapeDtypeStruct((1, LANES), F32)],
        compiler_params=_params(("arbitrary",)))(ba, dbg, alog_pad, dtb_pad)


def _chunk_masks():
    r = lax.broadcasted_iota(jnp.int32, (CA, CA), 0)
    c = lax.broadcasted_iota(jnp.int32, (CA, CA), 1)
    return r >= c, r > c, (r == c).astype(F32)


def _inv_unit_lower(a, eye):
    x = eye - a
    ph, plo = _split(a)
    n = 1
    while n < CA // 2:
        ph, plo = _split(_dot3(ph, plo, ph, plo, _BNN))
        x = x + _dot3(*_split(x), ph, plo, _BNN)
        n *= 2
    return x


def _delta_pre(q, k, v, bcol, gc, gr, gl, causal, strict):
    eg = jnp.exp(gc)
    dm = jnp.exp(jnp.where(causal, gc[:, :, :CA] - gr, -jnp.inf))
    kb = k * bcol
    kkb = _bdot(kb, k, _BNT)
    a = jnp.where(strict, kkb * dm, 0.0)
    rhs = jnp.concatenate([v * bcol, kb * eg], axis=2)
    qk = _bdot(q, k, _BNT)
    ekd = jnp.exp(gl - gc)
    return dict(eg=eg, dm=dm, kb=kb, kkb=kkb, a=a, rhs=rhs, p=qk * dm, qd=q * eg, ekd=ekd, kd=k * ekd, cd=jnp.exp(gl))


def _delta_fwd(qkvn, beta_b, gam_b, gam_r, gam_l, hqk, hv, jobs=()):
    t = qkvn.shape[0]
    rep = hv // hqk
    rb = _tile(t, 512, CA)
    nb = t // rb
    ncb = rb // CA
    nc = t // CA

    def body(q_ref, k_ref, v_ref, b_ref, gc_ref, gr_ref, gl_ref, o_ref, s_ref, tm_ref, state, sol_sc, p_sc):
        @pl.when(pl.program_id(1) == 0)
        def _():
            state[...] = jnp.zeros_like(state)

        causal, strict, eye = _chunk_masks()

        def chunks(a):
            return a.astype(F32).reshape(ncb, CA, a.shape[-1])

        q = chunks(q_ref[...])
        k = chunks(k_ref[...])
        for h in range(rep):
            pre = _delta_pre(q, k, chunks(v_ref[:, h * HEAD:(h + 1) * HEAD]), chunks(b_ref[h]), chunks(gc_ref[h]),
                             gr_ref[h], gl_ref[h], causal, strict)
            tm = _inv_unit_lower(pre["a"], eye)
            tm_ref[h] = tm
            sol_sc[h] = _hdot(tm, pre["rhs"], _BNN)
            p_sc[h] = pre["p"]

        def chunk(n, carry):
            rows = pl.ds(pl.multiple_of(n * CA, CA), CA)
            qn = q_ref[rows, :].astype(F32)
            kn = k_ref[rows, :].astype(F32)
            for h in range(rep):
                gc = gc_ref[h, rows, :]
                gl = gl_ref[h, n]
                s = state[h]
                v_new = sol_sc[h, n, :, :HEAD] - _bdot(sol_sc[h, n, :, HEAD:], s)
                o_ref[rows, h * HEAD:(h + 1) * HEAD] = _bdot(qn * jnp.exp(gc), s) + _bdot(p_sc[h, n], v_new)
                s_ref[h, n] = s.astype(BF16)
                state[h] = s * jnp.exp(gl) + _bdot(kn * jnp.exp(gl - gc), v_new, _TN)
            return carry

        lax.fori_loop(0, ncb, chunk, 0)

    qoff, koff, voff = 0, hqk, 2 * hqk // rep
    per_chunk = lambda width: pl.BlockSpec((rep, ncb, 1, width), lambda j, i: (j, i, 0, 0))
    return _call(
        body, "delta_fwd", (hqk, nb),
        [pl.BlockSpec((rb, HEAD), lambda j, i: (i, qoff + j)),
         pl.BlockSpec((rb, HEAD), lambda j, i: (i, koff + j)),
         pl.BlockSpec((rb, rep * HEAD), lambda j, i: (i, voff + j)),
         pl.BlockSpec((rep, rb, LANES), lambda j, i: (j, i, 0)),
         pl.BlockSpec((rep, rb, LANES), lambda j, i: (j, i, 0)),
         per_chunk(CA), per_chunk(LANES)],
        [pl.BlockSpec((rb, rep * HEAD), lambda j, i: (i, j)),
         pl.BlockSpec((rep, ncb, HEAD, HEAD), lambda j, i: (j, i, 0, 0)),
         pl.BlockSpec((rep, ncb, CA, CA), lambda j, i: (j, i, 0, 0))],
        [jax.ShapeDtypeStruct((t, hv * HEAD), F32), jax.ShapeDtypeStruct((hv, nc, HEAD, HEAD), BF16),
         jax.ShapeDtypeStruct((hv, nc, CA, CA), F32)],
        (qkvn, qkvn, qkvn, beta_b, gam_b, gam_r, gam_l),
        scratch=[pltpu.VMEM((rep, HEAD, HEAD), F32), pltpu.VMEM((rep, ncb, CA, 2 * HEAD), F32),
                 pltpu.VMEM((rep, ncb, CA, CA), F32)],
        sem=("parallel", "arbitrary"), jobs=jobs)


def _delta_bwd(qkvn, beta_b, gam_b, gam_r, gam_l, s_all, tm_all, do, hqk, hv, jobs=()):
    t = qkvn.shape[0]
    rep = hv // hqk
    rb = _tile(t, 512, CA)
    nb = t // rb
    ncb = rb // CA

    def body(q_ref, k_ref, v_ref, b_ref, gc_ref, gr_ref, gl_ref, s_ref, tm_ref, do_ref,
             dq_ref, dk_ref, dv_ref, db_ref, dg_ref, dstate, sol_sc, vn_sc, p_sc, kkb_sc, dvn_sc, ds_sc):
        @pl.when(pl.program_id(1) == 0)
        def _():
            dstate[...] = jnp.zeros_like(dstate)

        causal, strict, _ = _chunk_masks()
        ones = jnp.ones((ncb, CA, LANES), BF16)
        last = lax.broadcasted_iota(jnp.int32, (CA, LANES), 0) == CA - 1

        def chunks(a):
            return a.astype(F32).reshape(ncb, CA, a.shape[-1])

        def rows_of(a):
            return a.reshape(rb, a.shape[-1])

        def rowsum(m):
            return jnp.sum(m, axis=2, keepdims=True)

        def colsum(m):
            hi, lo = _split(m)
            return _bdot(hi, ones, _BTN) + _bdot(lo, ones, _BTN)

        q = chunks(q_ref[...])
        k = chunks(k_ref[...])

        def head_inputs(h):
            v = chunks(v_ref[:, h * HEAD:(h + 1) * HEAD])
            bcol = chunks(b_ref[h])
            return v, bcol, _delta_pre(q, k, v, bcol, chunks(gc_ref[h]), gr_ref[h], gl_ref[h], causal, strict)

        for h in range(rep):
            _, _, pre = head_inputs(h)
            sol = _hdot(tm_ref[h], pre["rhs"], _BNN)
            sol_sc[h] = sol
            vn_sc[h] = sol[:, :, :HEAD] - _bdot(sol[:, :, HEAD:], s_ref[h], _BNN)
            p_sc[h] = pre["p"]
            kkb_sc[h] = pre["kkb"]

        def state_step(it, carry):
            n = ncb - 1 - it
            rows = pl.ds(pl.multiple_of(n * CA, CA), CA)
            qn = q_ref[rows, :].astype(F32)
            kn = k_ref[rows, :].astype(F32)
            for h in range(rep):
                gc = gc_ref[h, rows, :]
                gl = gl_ref[h, n]
                ds = dstate[h]
                ds_sc[h, n] = ds
                dov = do_ref[rows, h * HEAD:(h + 1) * HEAD].astype(F32)
                dvn = _bdot(p_sc[h, n], dov, _TN) + _bdot(kn * jnp.exp(gl - gc), ds)
                dvn_sc[h, n] = dvn
                dstate[h] = (ds * jnp.exp(gl) + _bdot(qn * jnp.exp(gc), dov, _TN)
                             - _bdot(sol_sc[h, n, :, HEAD:], dvn, _TN))
            return carry

        lax.fori_loop(0, ncb, state_step, 0)

        kkr = _bdot(k, k, _BNT)
        dq = jnp.zeros((ncb, CA, HEAD), F32)
        dk = jnp.zeros((ncb, CA, HEAD), F32)
        for h in range(rep):
            v, bcol, pre = head_inputs(h)
            eg, dm, kb, qd, kd, cd = pre["eg"], pre["dm"], pre["kb"], pre["qd"], pre["kd"], pre["cd"]
            p = p_sc[h]
            sol = sol_sc[h]
            s = s_ref[h].astype(F32)
            ds = ds_sc[h]
            dov = chunks(do_ref[:, h * HEAD:(h + 1) * HEAD])
            v_new = vn_sc[h]
            dvn = dvn_sc[h]

            dp = jnp.where(causal, _bdot(dov, v_new, _BNT), 0.0)
            dqd = _bdot(dov, s, _BNT)
            dkd = _bdot(v_new, ds, _BNT)
            dcd = jnp.sum(rowsum(s * ds), axis=1, keepdims=True)
            dw = -_bdot(dvn, s, _BNT)

            drhs = _hdot(tm_ref[h], jnp.concatenate([dvn, dw], axis=2), _BTN)
            dbv, dbke = drhs[:, :, :HEAD], drhs[:, :, HEAD:]
            da = -jnp.where(strict, _bdot(drhs, sol, _BNT), 0.0)
            m = da * dm
            e = m * kkb_sc[h] + dp * p
            dgam = rowsum(e) - colsum(e) + rowsum(dbke * kb * eg) + rowsum(dqd * qd)
            r = rowsum(dkd * kd)
            tot = jnp.sum(r, axis=1, keepdims=True) + dcd * cd
            dgam = dgam - r + jnp.where(last, tot, 0.0)
            dbeta = rowsum(m * kkr) + rowsum(dbv * v) + rowsum(dbke * eg * k)
            nm = m * bcol[:, :, :CA]
            dqk = dp * dm
            dq = dq + _bdot(dqk, k, _BNN) + eg * dqd
            dk = (dk + _bdot(nm, k, _BNN) + _bdot(nm, k, _BTN) + _bdot(dqk, q, _BTN) + bcol * eg * dbke
                  + pre["ekd"] * dkd)
            dv_ref[:, h * HEAD:(h + 1) * HEAD] = rows_of(bcol * dbv)
            db_ref[h] = rows_of(jnp.broadcast_to(dbeta, (ncb, CA, LANES)))
            dg_ref[h] = rows_of(jnp.broadcast_to(dgam, (ncb, CA, LANES)))
        dq_ref[...] = rows_of(dq)
        dk_ref[...] = rows_of(dk)

    qoff, koff, voff = 0, hqk, 2 * hqk // rep
    rv = lambda i: nb - 1 - i
    hd = pl.BlockSpec((rep, rb, LANES), lambda j, i: (j, rv(i), 0))
    qk_out = pl.BlockSpec((rb, HEAD), lambda j, i: (rv(i), j))
    v_blk = pl.BlockSpec((rb, rep * HEAD), lambda j, i: (rv(i), j))
    per_chunk = lambda *shape: pl.BlockSpec((rep, ncb) + shape, lambda j, i: (j, rv(i), 0, 0))
    return _call(
        body, "delta_bwd", (hqk, nb),
        [pl.BlockSpec((rb, HEAD), lambda j, i: (rv(i), qoff + j)),
         pl.BlockSpec((rb, HEAD), lambda j, i: (rv(i), koff + j)),
         pl.BlockSpec((rb, rep * HEAD), lambda j, i: (rv(i), voff + j)),
         hd, hd, per_chunk(1, CA), per_chunk(1, LANES), per_chunk(HEAD, HEAD), per_chunk(CA, CA), v_blk],
        [qk_out, qk_out, v_blk, hd, hd],
        [jax.ShapeDtypeStruct((t, hqk * HEAD), F32), jax.ShapeDtypeStruct((t, hqk * HEAD), F32),
         jax.ShapeDtypeStruct((t, hv * HEAD), F32),
         jax.ShapeDtypeStruct((hv, t, LANES), F32), jax.ShapeDtypeStruct((hv, t, LANES), F32)],
        (qkvn, qkvn, qkvn, beta_b, gam_b, gam_r, gam_l, s_all, tm_all, do),
        scratch=[pltpu.VMEM((rep, HEAD, HEAD), F32), pltpu.VMEM((rep, ncb, CA, 2 * HEAD), F32),
                 pltpu.VMEM((rep, ncb, CA, HEAD), F32), pltpu.VMEM((rep, ncb, CA, CA), F32),
                 pltpu.VMEM((rep, ncb, CA, CA), F32), pltpu.VMEM((rep, ncb, CA, HEAD), F32),
                 pltpu.VMEM((rep, ncb, HEAD, HEAD), F32)],
        sem=("parallel", "arbitrary"), jobs=jobs)


def _apost_fwd(o, proj, gain, zoff, hv):
    t = o.shape[0]
    tb = _tile(t, 1024, 8)
    zb = zoff // HEAD

    def body(o_ref, z_ref, g_ref, y_ref):
        ov = o_ref[...]
        z = z_ref[...].astype(F32)
        r = lax.rsqrt(jnp.mean(ov * ov, axis=-1, keepdims=True) + EPS)
        y_ref[...] = (ov * r * g_ref[...] * (z * _sigmoid(z))).astype(BF16)

    blk = pl.BlockSpec((tb, HEAD), lambda i, h: (i, h))
    return pl.pallas_call(
        body, name="apost_fwd", grid=(t // tb, hv),
        in_specs=[blk, pl.BlockSpec((tb, HEAD), lambda i, h: (i, zb + h)), pl.BlockSpec((1, HEAD), lambda i, h: (0, 0))],
        out_specs=blk, out_shape=jax.ShapeDtypeStruct((t, hv * HEAD), BF16),
        compiler_params=_params(("parallel", "parallel")))(o, proj, gain.reshape(1, HEAD))


def _apost_bwd(o, proj, gain, dy, zoff, hv):
    t = o.shape[0]
    tb = _tile(t, 1024, 8)
    zb = zoff // HEAD

    def body(o_ref, z_ref, g_ref, dy_ref, do_ref, dz_ref, dg_ref):
        @pl.when(jnp.logical_and(pl.program_id(0) == 0, pl.program_id(1) == 0))
        def _():
            dg_ref[...] = jnp.zeros_like(dg_ref)

        ov = o_ref[...]
        z = z_ref[...].astype(F32)
        d = dy_ref[...].astype(F32)
        r = lax.rsqrt(jnp.mean(ov * ov, axis=-1, keepdims=True) + EPS)
        oh = ov * r
        sz = z * _sigmoid(z)
        dn = d * sz
        dz_ref[...] = (d * oh * g_ref[...] * _silu_grad(z)).astype(BF16)
        doh = dn * g_ref[...]
        do_ref[...] = r * (doh - oh * jnp.mean(doh * oh, axis=-1, keepdims=True))
        dg_ref[...] += jnp.sum(dn * oh, axis=0, keepdims=True)

    blk = pl.BlockSpec((tb, HEAD), lambda i, h: (i, h))
    vec = pl.BlockSpec((1, HEAD), lambda i, h: (0, 0))
    return pl.pallas_call(
        body, name="apost_bwd", grid=(t // tb, hv),
        in_specs=[blk, pl.BlockSpec((tb, HEAD), lambda i, h: (i, zb + h)), vec, blk],
        out_specs=[blk, blk, vec],
        out_shape=[jax.ShapeDtypeStruct((t, hv * HEAD), F32), jax.ShapeDtypeStruct((t, hv * HEAD), BF16),
                   jax.ShapeDtypeStruct((1, HEAD), F32)],
        compiler_params=_params(("arbitrary", "arbitrary")))(o, proj, gain.reshape(1, HEAD), dy)


def _sgu_fwd(proj, gain, w_s, b_t, uoff, wb):
    t = proj.shape[0]
    ng = wb // HEAD

    def body(u_ref, v_ref, g_ref, w_ref, b_ref, o_ref):
        r_i = lax.broadcasted_iota(jnp.int32, (HEAD, HEAD), 0)
        c_i = lax.broadcasted_iota(jnp.int32, (HEAD, HEAD), 1)
        u = _gelu(u_ref[...].astype(F32))
        vg = _gelu(v_ref[...].astype(F32))
        vn = vg * lax.rsqrt(jnp.mean(vg * vg, axis=-1, keepdims=True) + EPS) * g_ref[...]
        for g in range(ng):
            cols = slice(g * HEAD, (g + 1) * HEAD)
            wg = jnp.where(r_i >= c_i, w_ref[g], 0.0)
            mixed = _bdot(wg, vn[:, cols]) + b_ref[:, g:g + 1]
            o_ref[:, cols] = (u[:, cols] * mixed).astype(BF16)

    ub, vb = uoff // wb, uoff // wb + 1
    return pl.pallas_call(
        body, name="sgu_fwd", grid=(t // HEAD,),
        in_specs=[pl.BlockSpec((HEAD, wb), lambda i: (i, ub)), pl.BlockSpec((HEAD, wb), lambda i: (i, vb)),
                  pl.BlockSpec((1, wb), lambda i: (0, 0)), pl.BlockSpec((ng, HEAD, HEAD), lambda i: (0, 0, 0)),
                  pl.BlockSpec((HEAD, ng), lambda i: (0, 0))],
        out_specs=pl.BlockSpec((HEAD, wb), lambda i: (i, 0)), out_shape=jax.ShapeDtypeStruct((t, wb), BF16),
        compiler_params=_params(("parallel",)))(proj, proj, gain.reshape(1, wb), w_s, b_t)


def _sgu_bwd(proj, gain, w_s, b_t, dout, uoff, wb):
    t = proj.shape[0]
    ng = wb // HEAD

    def body(u_ref, v_ref, g_ref, w_ref, b_ref, d_ref, du_ref, dv_ref, dw_ref, db_ref, dg_ref, dvn_ref):
        @pl.when(pl.program_id(0) == 0)
        def _():
            dw_ref[...] = jnp.zeros_like(dw_ref)
            db_ref[...] = jnp.zeros_like(db_ref)
            dg_ref[...] = jnp.zeros_like(dg_ref)

        r_i = lax.broadcasted_iota(jnp.int32, (HEAD, HEAD), 0)
        c_i = lax.broadcasted_iota(jnp.int32, (HEAD, HEAD), 1)
        tril = r_i >= c_i
        ub = u_ref[...].astype(F32)
        vb = v_ref[...].astype(F32)
        u = _gelu(ub)
        vg = _gelu(vb)
        r = lax.rsqrt(jnp.mean(vg * vg, axis=-1, keepdims=True) + EPS)
        vh = vg * r
        vn = vh * g_ref[...]
        d = d_ref[...].astype(F32)
        for g in range(ng):
            cols = slice(g * HEAD, (g + 1) * HEAD)
            wg = jnp.where(tril, w_ref[g], 0.0)
            mixed = _bdot(wg, vn[:, cols]) + b_ref[:, g:g + 1]
            du_ref[:, cols] = (d[:, cols] * mixed * _gelu_grad(ub[:, cols])).astype(BF16)
            dmix = d[:, cols] * u[:, cols]
            dw_ref[g] += jnp.where(tril, _bdot(dmix, vn[:, cols], _NT), 0.0)
            db_ref[g] += jnp.broadcast_to(jnp.sum(dmix, axis=1, keepdims=True), (HEAD, HEAD))
            dvn_ref[:, cols] = _bdot(wg, dmix, _TN)
        dvn = dvn_ref[...]
        dg_ref[...] += jnp.sum(dvn * vh, axis=0, keepdims=True)
        dvh = dvn * g_ref[...]
        dvg = r * (dvh - vh * jnp.mean(dvh * vh, axis=-1, keepdims=True))
        dv_ref[...] = (dvg * _gelu_grad(vb)).astype(BF16)

    ub_i, vb_i = uoff // wb, uoff // wb + 1
    row = pl.BlockSpec((HEAD, wb), lambda i: (i, 0))
    mat = pl.BlockSpec((ng, HEAD, HEAD), lambda i: (0, 0, 0))
    vec = pl.BlockSpec((1, wb), lambda i: (0, 0))
    return pl.pallas_call(
        body, name="sgu_bwd", grid=(t // HEAD,),
        in_specs=[pl.BlockSpec((HEAD, wb), lambda i: (i, ub_i)), pl.BlockSpec((HEAD, wb), lambda i: (i, vb_i)),
                  vec, mat, pl.BlockSpec((HEAD, ng), lambda i: (0, 0)), row],
        out_specs=[row, row, mat, mat, vec],
        out_shape=[jax.ShapeDtypeStruct((t, wb), BF16), jax.ShapeDtypeStruct((t, wb), BF16),
                   jax.ShapeDtypeStruct((ng, HEAD, HEAD), F32), jax.ShapeDtypeStruct((ng, HEAD, HEAD), F32),
                   jax.ShapeDtypeStruct((1, wb), F32)],
        scratch_shapes=[pltpu.VMEM((HEAD, wb), F32)],
        compiler_params=_params(("arbitrary",)))(proj, proj, gain.reshape(1, wb), w_s, b_t, dout)


def _merge_specs(t, d, goff):
    tb = _tile(t, 512, 8)
    tc = _tile(d, 512)
    gb = goff // tc
    nd = d // tc
    blk = pl.BlockSpec((tb, tc), lambda i, j: (i, j))
    ga = pl.BlockSpec((tb, tc), lambda i, j: (i, gb + j))
    gbs = pl.BlockSpec((tb, tc), lambda i, j: (i, gb + nd + j))
    return (t // tb, nd), blk, ga, gbs


def _merge_fwd(ya, yb, proj, goff):
    t, d = ya.shape
    grid, blk, ga, gbs = _merge_specs(t, d, goff)

    def body(ya_ref, yb_ref, ga_ref, gb_ref, o_ref):
        o_ref[...] = (_sigmoid(ga_ref[...].astype(F32)) * ya_ref[...].astype(F32)
                      + _sigmoid(gb_ref[...].astype(F32)) * yb_ref[...].astype(F32)).astype(BF16)

    return pl.pallas_call(
        body, name="merge_fwd", grid=grid, in_specs=[blk, blk, ga, gbs], out_specs=blk,
        out_shape=jax.ShapeDtypeStruct((t, d), BF16),
        compiler_params=_params(("parallel", "parallel")))(ya, yb, proj, proj)


def _merge_bwd(dm, ya, yb, proj, goff):
    t, d = ya.shape
    grid, blk, ga, gbs = _merge_specs(t, d, goff)

    def body(dm_ref, ya_ref, yb_ref, ga_ref, gb_ref, dya_ref, dyb_ref, dga_ref, dgb_ref):
        dmv = dm_ref[...].astype(F32)
        sa = _sigmoid(ga_ref[...].astype(F32))
        sb = _sigmoid(gb_ref[...].astype(F32))
        dya_ref[...] = (dmv * sa).astype(BF16)
        dyb_ref[...] = (dmv * sb).astype(BF16)
        dga_ref[...] = (dmv * ya_ref[...].astype(F32) * sa * (1.0 - sa)).astype(BF16)
        dgb_ref[...] = (dmv * yb_ref[...].astype(F32) * sb * (1.0 - sb)).astype(BF16)

    shp = jax.ShapeDtypeStruct((t, d), BF16)
    return pl.pallas_call(
        body, name="merge_bwd", grid=grid, in_specs=[blk, blk, blk, ga, gbs], out_specs=[blk] * 4,
        out_shape=[shp] * 4, compiler_params=_params(("parallel", "parallel")))(dm, ya, yb, proj, proj)


def _ffn_act_fwd(up, conv_w, bias, dff):
    t = up.shape[0]
    nblk = dff // HEAD
    kw = conv_w.shape[0]

    def body(g_ref, v_ref, wg_ref, wv_ref, bg_ref, bv_ref, o_ref):
        rows = lax.broadcasted_iota(jnp.int32, (t, HEAD), 0)
        cg = _conv(g_ref[...].astype(F32), wg_ref[...], rows) + bg_ref[...]
        cv = _conv(v_ref[...].astype(F32), wv_ref[...], rows) + bv_ref[...]
        o_ref[...] = (cg * _sigmoid(cg) * cv).astype(BF16)

    return pl.pallas_call(
        body, name="ffn_act_fwd", grid=(nblk,),
        in_specs=[pl.BlockSpec((t, HEAD), lambda j: (0, j)), pl.BlockSpec((t, HEAD), lambda j: (0, nblk + j)),
                  pl.BlockSpec((kw, HEAD), lambda j: (0, j)), pl.BlockSpec((kw, HEAD), lambda j: (0, nblk + j)),
                  pl.BlockSpec((1, HEAD), lambda j: (0, j)), pl.BlockSpec((1, HEAD), lambda j: (0, nblk + j))],
        out_specs=pl.BlockSpec((t, HEAD), lambda j: (0, j)), out_shape=jax.ShapeDtypeStruct((t, dff), BF16),
        compiler_params=_params(("parallel",)))(up, up, conv_w, conv_w, bias, bias)


def _ffn_act_bwd(up, dact, conv_w, bias, dff, jobs=()):
    t = up.shape[0]
    nblk = dff // HEAD
    kw = conv_w.shape[0]

    def body(me_ref, pa_ref, d_ref, wm_ref, wp_ref, bm_ref, bp_ref, dup_ref, dw_ref, db_ref):
        is_gate = pl.program_id(0) < nblk
        rows = lax.broadcasted_iota(jnp.int32, (t, HEAD), 0)
        xv = me_ref[...].astype(F32)
        w = wm_ref[...]
        cm = _conv(xv, w, rows) + bm_ref[...]
        cp = _conv(pa_ref[...].astype(F32), wp_ref[...], rows) + bp_ref[...]
        d = d_ref[...].astype(F32)
        dc = jnp.where(is_gate, d * cp * _silu_grad(cm), d * (cp * _sigmoid(cp)))
        db_ref[...] = jnp.sum(dc, axis=0, keepdims=True)
        dx = dc * w[kw - 1:kw, :]
        dw_ref[kw - 1:kw, :] = jnp.sum(dc * xv, axis=0, keepdims=True)
        for s in range(1, kw):
            dx = dx + _shift_up(dc, s, rows) * w[kw - 1 - s:kw - s, :]
            dw_ref[kw - 1 - s:kw - s, :] = jnp.sum(dc * _shift_down(xv, s, rows), axis=0, keepdims=True)
        dup_ref[...] = dx.astype(BF16)

    part = lambda j: (j + nblk) % (2 * nblk)
    me = pl.BlockSpec((t, HEAD), lambda j: (0, j))
    wme = pl.BlockSpec((kw, HEAD), lambda j: (0, j))
    bme = pl.BlockSpec((1, HEAD), lambda j: (0, j))
    return _call(
        body, "ffn_act_bwd", (2 * nblk,),
        [me, pl.BlockSpec((t, HEAD), lambda j: (0, part(j))), pl.BlockSpec((t, HEAD), lambda j: (0, j % nblk)),
         wme, pl.BlockSpec((kw, HEAD), lambda j: (0, part(j))),
         bme, pl.BlockSpec((1, HEAD), lambda j: (0, part(j)))],
        [me, wme, bme],
        [jax.ShapeDtypeStruct((t, 2 * dff), BF16), jax.ShapeDtypeStruct((kw, 2 * dff), F32),
         jax.ShapeDtypeStruct((1, 2 * dff), F32)],
        (up, up, dact, conv_w, conv_w, bias, bias), sem=("parallel",), jobs=jobs)


def _ple_fwd(x, gt, pp):
    t, d = x.shape
    tb, tc = _tile(t, 512, 8), _tile(d, 1024)

    def body(x_ref, g_ref, p_ref, o_ref):
        o_ref[...] = x_ref[...] + _sigmoid(g_ref[...].astype(F32)) * p_ref[...].astype(F32)

    blk = pl.BlockSpec((tb, tc), lambda i, j: (i, j))
    return pl.pallas_call(
        body, name="ple_fwd", grid=(t // tb, d // tc), in_specs=[blk, blk, blk], out_specs=blk,
        out_shape=jax.ShapeDtypeStruct((t, d), F32), compiler_params=_params(("parallel", "parallel")))(x, gt, pp)


def _ple_bwd(dx, gt, pp):
    t, d = dx.shape
    tb, tc = _tile(t, 512, 8), _tile(d, 1024)

    def body(dx_ref, g_ref, p_ref, dg_ref, dp_ref):
        dv = dx_ref[...]
        s = _sigmoid(g_ref[...].astype(F32))
        dg_ref[...] = (dv * p_ref[...].astype(F32) * s * (1.0 - s)).astype(BF16)
        dp_ref[...] = (dv * s).astype(BF16)

    blk = pl.BlockSpec((tb, tc), lambda i, j: (i, j))
    shp = jax.ShapeDtypeStruct((t, d), BF16)
    return pl.pallas_call(
        body, name="ple_bwd", grid=(t // tb, d // tc), in_specs=[blk, blk, blk], out_specs=[blk, blk],
        out_shape=[shp, shp], compiler_params=_params(("parallel", "parallel")))(dx, gt, pp)


def _adam(pieces, w, m, v, name, jobs=()):
    nq = len(pieces)
    npart, rp, c = pieces[0].shape
    row_bytes = 2 * c * (nq * npart * pieces[0].dtype.itemsize + 7 * 4)
    tr = _tile(rp, max(16, min(512, ADAM_VMEM_BUDGET // row_bytes)), 16)
    nblk = rp // tr
    c1 = 1.0 - ADAM_B1 ** ADAM_STEP
    c2 = 1.0 - ADAM_B2 ** ADAM_STEP

    def body(*refs):
        p_refs = refs[:nq]
        w_ref, m_ref, v_ref, g_ref, d_ref, mo_ref, vo_ref = refs[nq:]
        for q in range(nq):
            @pl.when(pl.program_id(0) == q)
            def _(p_ref=p_refs[q]):
                g = p_ref[0].astype(F32)
                for i in range(1, npart):
                    g = g + p_ref[i].astype(F32)
                mn = ADAM_B1 * m_ref[...] + (1.0 - ADAM_B1) * g
                vn = ADAM_B2 * v_ref[...] + (1.0 - ADAM_B2) * (g * g)
                g_ref[...] = g
                mo_ref[...] = mn
                vo_ref[...] = vn
                d_ref[...] = -ADAM_LR * ((mn / c1) / (jnp.sqrt(vn / c2) + ADAM_EPS) + ADAM_WD * w_ref[...])

    def piece_spec(q):
        return pl.BlockSpec((npart, tr, c), lambda i, r: (0, jnp.where(i == q, r, jnp.where(i < q, 0, nblk - 1)), 0))

    blk = pl.BlockSpec((tr, c), lambda i, r: (i * nblk + r, 0))
    shp = jax.ShapeDtypeStruct((nq * rp, c), F32)
    return _call(body, name, (nq, nblk), [piece_spec(q) for q in range(nq)] + [blk] * 3, [blk] * 4, [shp] * 4,
                 (*pieces, w, m, v), sem=("parallel", "parallel"), jobs=jobs)


_BIG = ("w_in", "w_branch_a", "w_branch_b", "w_out", "w_ffn_up", "w_ffn_down", "w_ple_gate", "w_ple_proj")
_COL_SHARDED = ("w_in", "w_branch_b", "w_ffn_up", "w_ple_proj")
_CONVS = ("conv_qkv", "conv_ffn")
_GATHER_AHEAD = ("w_in", "conv_qkv")
_GATHER_ON_PROJ = ("w_branch_a", "w_branch_b", "w_out", "conv_ffn")
_GATHER_ON_DELTA = ("w_ffn_up",)
_GATHER_ON_UP = ("w_ffn_down",)
_GATHER_ON_DOWN = ("w_ple_gate", "w_ple_proj")
_SCATTER_ON_FFN = ("w_ple_gate", "w_ple_proj", "w_ffn_down")
_SCATTER_ON_DELTA = ("w_ffn_up",)
_SCATTER_ON_DW_MAIN = ("w_out", "w_branch_a", "w_branch_b")
_SMALL = ("norm_mix", "conv_qkv", "a_log", "dt_bias", "head_norm", "sgu_norm", "w_spatial", "b_spatial", "norm_ffn",
          "conv_ffn", "b_conv_ffn", "norm_ple", "norm_final")
_WEIGHTS = ("norm_mix", "w_in", "conv_qkv", "a_log", "dt_bias", "head_norm", "sgu_norm", "w_spatial", "b_spatial",
            "w_branch_a", "w_branch_b", "w_out", "norm_ffn", "w_ffn_up", "conv_ffn", "b_conv_ffn", "w_ffn_down",
            "norm_ple", "w_ple_gate", "w_ple_proj", "norm_final")


def _full_cols(g):
    return jnp.transpose(g, (1, 0, 2)).reshape(g.shape[1], N_DEV * g.shape[2])


def _full_rows(g):
    return g.reshape(N_DEV * g.shape[1], g.shape[2])


def _split_cols(dw):
    k, n = dw.shape
    return jnp.transpose(dw.reshape(k, N_DEV, n // N_DEV), (1, 0, 2))


def _split_rows(dw):
    k, n = dw.shape
    return dw.reshape(N_DEV, k // N_DEV, n)


def _pad_lanes(v, width=LANES, offset=0):
    return jnp.pad(v, ((0, 0), (offset, width - offset - v.shape[1])))


def kernel(x, p, norm_mix, w_in, conv_qkv, a_log, dt_bias, head_norm, sgu_norm, w_spatial, b_spatial, w_branch_a, w_branch_b, w_out, norm_ffn, w_ffn_up, conv_ffn, b_conv_ffn, w_ffn_down, norm_ple, w_ple_gate, w_ple_proj, norm_final, loss_target, m_norm_mix, m_w_in, m_conv_qkv, m_a_log, m_dt_bias, m_head_norm, m_sgu_norm, m_w_spatial, m_b_spatial, m_w_branch_a, m_w_branch_b, m_w_out, m_norm_ffn, m_w_ffn_up, m_conv_ffn, m_b_conv_ffn, m_w_ffn_down, m_norm_ple, m_w_ple_gate, m_w_ple_proj, m_norm_final, v_norm_mix, v_w_in, v_conv_qkv, v_a_log, v_dt_bias, v_head_norm, v_sgu_norm, v_w_spatial, v_b_spatial, v_w_branch_a, v_w_branch_b, v_w_out, v_norm_ffn, v_w_ffn_up, v_conv_ffn, v_b_conv_ffn, v_w_ffn_down, v_norm_ple, v_w_ple_gate, v_w_ple_proj, v_norm_final):
    env = dict(locals())
    wts = {n: env[n] for n in _WEIGHTS}
    mom_m = {n: env["m_" + n] for n in _WEIGHTS}
    mom_v = {n: env["v_" + n] for n in _WEIGHTS}

    xin = x[0]
    tgt = loss_target[0]
    t, d = xin.shape
    depth = w_in.shape[0]
    hv = a_log.shape[1]
    vw = hv * HEAD
    wb = sgu_norm.shape[1]
    ng = w_spatial.shape[1]
    n_in = w_in.shape[2] * N_DEV
    qk = (n_in - 2 * vw - 2 * hv - 2 * wb - 2 * d) // 2
    hqk = qk // HEAD
    dff = w_ffn_down.shape[1] * N_DEV
    cw = 2 * qk + vw
    o_z, o_ba = 2 * qk + vw, 2 * qk + 2 * vw
    o_ub = o_ba
    o_ga = o_ub + 2 * wb
    me = 4 * lax.axis_index("x") + 2 * lax.axis_index("y") + lax.axis_index("c")

    full = [dict() for _ in range(depth)]

    def gather_jobs(i, names):
        return [(wts[n][i].astype(BF16) if n in _BIG else wts[n][i], False) for n in names]

    def take(i, names, results):
        for n, g in zip(names, results):
            full[i][n] = _full_cols(g) if n in _COL_SHARDED or n in _CONVS else _full_rows(g)

    take(0, _GATHER_AHEAD, _exchange(gather_jobs(0, _GATHER_AHEAD), "gather_first"))

    saved = []
    xc = xin
    for i in range(depth):
        fw = full[i]
        w_full = fw["w_in"]
        fw["w_main"] = jnp.concatenate([w_full[:, :o_ba], w_full[:, o_ba + 2 * hv:]], axis=1)
        fw["w_ba"] = _pad_lanes(w_full[:, o_ba:o_ba + 2 * hv])
        s = {"x0": xc}
        s["h1"] = _rms_fwd(xc, norm_mix[i], "rms_fwd")
        s["proj"], got = _matmul(s["h1"], fw["w_main"], "nn", BF16, "mm_proj", jobs=gather_jobs(i, _GATHER_ON_PROJ))
        take(i, _GATHER_ON_PROJ, got)
        s["ba"] = _matmul(s["h1"], fw["w_ba"], "nn", F32, "mm_ba")
        s["qkvn"] = _qkv_fwd(s["proj"], fw["conv_qkv"], hqk, hqk)
        s["alog"] = _pad_lanes(a_log[i][None, :], offset=hv)
        s["dtb"] = _pad_lanes(dt_bias[i][None, :], offset=hv)
        bg = _gate_fwd(s["ba"], s["alog"], s["dtb"], hv)
        beta_t = bg[:, :hv].T
        gam_t = bg[:, hv:2 * hv].T
        s["beta_b"] = jnp.broadcast_to(beta_t[:, :, None], (hv, t, LANES))
        s["gam_b"] = jnp.broadcast_to(gam_t[:, :, None], (hv, t, LANES))
        s["gam_r"] = gam_t.reshape(hv, t // CA, 1, CA)
        s["gam_l"] = jnp.broadcast_to(s["gam_r"][:, :, :, CA - 1:], (hv, t // CA, 1, LANES))
        ahead = gather_jobs(i + 1, _GATHER_AHEAD) if i + 1 < depth else []
        (s["o"], s["s_all"], s["tm_all"]), got = _delta_fwd(
            s["qkvn"], s["beta_b"], s["gam_b"], s["gam_r"], s["gam_l"], hqk, hv,
            jobs=gather_jobs(i, _GATHER_ON_DELTA) + ahead)
        take(i, _GATHER_ON_DELTA, got)
        if ahead:
            take(i + 1, _GATHER_AHEAD, got[len(_GATHER_ON_DELTA):])
        s["outa"] = _apost_fwd(s["o"], s["proj"], head_norm[i], o_z, hv)
        s["b_t"] = b_spatial[i].T
        s["outb"] = _sgu_fwd(s["proj"], sgu_norm[i], w_spatial[i], s["b_t"], o_ub, wb)
        s["ya"] = _matmul(s["outa"], fw["w_branch_a"], "nn", BF16, "mm_ya")
        s["yb"] = _matmul(s["outb"], fw["w_branch_b"], "nn", BF16, "mm_yb")
        s["mg"] = _merge_fwd(s["ya"], s["yb"], s["proj"], o_ga)
        s["x1"] = _matmul(s["mg"], fw["w_out"], "nn", F32, "mm_out", res=xc)
        s["h2"] = _rms_fwd(s["x1"], norm_ffn[i], "rms_fwd")
        s["up"], got = _matmul(s["h2"], fw["w_ffn_up"], "nn", BF16, "mm_up", jobs=gather_jobs(i, _GATHER_ON_UP))
        take(i, _GATHER_ON_UP, got)
        s["bias"] = b_conv_ffn[i][None, :]
        s["act"] = _ffn_act_fwd(s["up"], fw["conv_ffn"], s["bias"], dff)
        s["x2"], got = _matmul(s["act"], fw["w_ffn_down"], "nn", F32, "mm_down", res=s["x1"],
                               jobs=gather_jobs(i, _GATHER_ON_DOWN))
        take(i, _GATHER_ON_DOWN, got)
        s["h3"] = _rms_fwd(s["x2"], norm_ple[i], "rms_fwd")
        s["gt"] = _matmul(s["h3"], fw["w_ple_gate"], "nn", BF16, "mm_gt")
        s["pp"] = _matmul(p[i, 0], fw["w_ple_proj"], "nn", BF16, "mm_pp")
        xc = _ple_fwd(s["x2"], s["gt"], s["pp"])
        saved.append(s)

    dx, g_norm_final, loss_part = _loss_head(xc, tgt, norm_final)

    small = {n: [None] * depth for n in _SMALL if n != "norm_final"}
    recv = {n: [None] * depth for n in _BIG}
    recv["w_in"] = [None] * (2 * depth)

    def scatter_jobs(gw, names):
        return [(_split_cols(gw[n]) if n in _COL_SHARDED else _split_rows(gw[n]), True) for n in names]

    def keep(i, names, results):
        for n, r in zip(names, results):
            recv[n][i] = r

    def carry(jobs, *args):
        return _matmul(*args, jobs=jobs) if jobs else (_matmul(*args), [])

    later = []
    for i in reversed(range(depth)):
        fw, s = full[i], saved[i]
        dgt, dpp = _ple_bwd(dx, s["gt"], s["pp"])
        gw = {"w_ple_gate": _matmul(s["h3"], dgt, "tn", BF16, "mm_dw_gt"),
              "w_ple_proj": _matmul(p[i, 0], dpp, "tn", BF16, "mm_dw_pp")}
        dh3 = _matmul(dgt, fw["w_ple_gate"], "nt", F32, "mm_dh3")
        dx, small["norm_ple"][i] = _rms_bwd(s["x2"], dh3, norm_ple[i], dx, "rms_bwd")

        dact = _matmul(dx, fw["w_ffn_down"], "nt", BF16, "mm_dact")
        gw["w_ffn_down"] = _matmul(s["act"], dx, "tn", BF16, "mm_dw_down")
        (dup, small["conv_ffn"][i], small["b_conv_ffn"][i]), got = _ffn_act_bwd(
            s["up"], dact, fw["conv_ffn"], s["bias"], dff, jobs=scatter_jobs(gw, _SCATTER_ON_FFN))
        keep(i, _SCATTER_ON_FFN, got)
        gw["w_ffn_up"], got = carry(later[:1], s["h2"], dup, "tn", BF16, "mm_dw_up")
        keep(2 * i + 2, ("w_in",), got)
        dh2, got = carry(later[1:], dup, fw["w_ffn_up"], "nt", F32, "mm_dh2")
        keep(2 * i + 3, ("w_in",), got)
        dx, small["norm_ffn"][i] = _rms_bwd(s["x1"], dh2, norm_ffn[i], dx, "rms_bwd")

        dmg = _matmul(dx, fw["w_out"], "nt", BF16, "mm_dmg")
        gw["w_out"] = _matmul(s["mg"], dx, "tn", BF16, "mm_dw_out")
        dya, dyb, dga, dgb = _merge_bwd(dmg, s["ya"], s["yb"], s["proj"], o_ga)
        gw["w_branch_a"] = _matmul(s["outa"], dya, "tn", BF16, "mm_dw_a")
        gw["w_branch_b"] = _matmul(s["outb"], dyb, "tn", BF16, "mm_dw_b")
        douta = _matmul(dya, fw["w_branch_a"], "nt", BF16, "mm_douta")
        doutb = _matmul(dyb, fw["w_branch_b"], "nt", BF16, "mm_doutb")
        dub, dvb, small["w_spatial"][i], db_s, dsg = _sgu_bwd(s["proj"], sgu_norm[i], w_spatial[i], s["b_t"], doutb, o_ub, wb)
        small["b_spatial"][i] = db_s[:, :, 0]
        small["sgu_norm"][i] = dsg
        do, dz, small["head_norm"][i] = _apost_bwd(s["o"], s["proj"], head_norm[i], douta, o_z, hv)
        (dq, dk, dv, db_b, dg_b), got = _delta_bwd(
            s["qkvn"], s["beta_b"], s["gam_b"], s["gam_r"], s["gam_l"], s["s_all"], s["tm_all"], do, hqk, hv,
            jobs=scatter_jobs(gw, _SCATTER_ON_DELTA))
        keep(i, _SCATTER_ON_DELTA, got)
        dbg = _pad_lanes(jnp.concatenate([db_b[:, :, 0].T, dg_b[:, :, 0].T], axis=1))
        dba, dal, ddt = _gate_bwd(s["ba"], dbg, s["alog"], s["dtb"], hv)
        small["a_log"][i] = dal[:, hv:2 * hv]
        small["dt_bias"][i] = ddt[:, hv:2 * hv]
        dqkv_pre, small["conv_qkv"][i] = _qkv_bwd(s["proj"], jnp.concatenate([dq, dk, dv], axis=1), fw["conv_qkv"], hqk, hqk)
        dproj = jnp.concatenate([dqkv_pre, dz, dub, dvb, dga, dgb], axis=1)
        dw_main, got = _matmul(s["h1"], dproj, "tn", BF16, "mm_dw_main", jobs=scatter_jobs(gw, _SCATTER_ON_DW_MAIN))
        keep(i, _SCATTER_ON_DW_MAIN, got)
        dw_ba = _matmul(s["h1"], dba, "tn", BF16, "mm_dw_ba")
        dw_in = jnp.concatenate([dw_main[:, :o_ba], dw_ba[:, :2 * hv], dw_main[:, o_ba:]], axis=1)
        dh1 = _matmul(dproj, fw["w_main"], "nt", F32, "mm_dh1")
        dh1 = _matmul(dba, fw["w_ba"], "nt", F32, "mm_dh1_ba", res=dh1)
        dx, small["norm_mix"][i] = _rms_bwd(s["x0"], dh1, norm_mix[i], dx, "rms_bwd")

        later = [(_split_cols(dw_in[:d // 2]), True), (_split_cols(dw_in[d // 2:]), True)]

    recv["w_in"][0], recv["w_in"][1] = _exchange(later, "scatter_last")

    rep_names = tuple(n for n in _SMALL if n not in _CONVS)
    stacked = {n: jnp.concatenate([jnp.reshape(a, (-1,)) for a in small[n]]) for n in small}
    stacked["norm_final"] = g_norm_final.reshape(-1)

    def padded(parts, mult, axis=0):
        flat = jnp.concatenate(parts, axis=axis)
        pad = -flat.shape[axis] % mult
        return jnp.pad(flat, [(0, 0)] * axis + [(0, pad)])

    rep_flat = padded([stacked[n] for n in rep_names] + [loss_part[0, :1]], 16 * LANES)
    conv_flat = [padded([stacked[n]], 8 * LANES) for n in _CONVS]
    packed = jnp.concatenate([rep_flat] + conv_flat).reshape(-1, LANES)

    outs_g, outs_d, outs_m, outs_v = {}, {}, {}, {}

    for n in _BIG:
        shp = wts[n].shape
        two_d = lambda a: a.reshape(shp[0] * shp[1], shp[2])
        res, got = _adam(recv[n], two_d(wts[n]), two_d(mom_m[n]), two_d(mom_v[n]), "adam_" + n,
                         jobs=[(packed, False)] if n == "w_in" else [])
        if got:
            small_all = got[0].reshape(N_DEV, -1)
        outs_g[n], outs_d[n], outs_m[n], outs_v[n] = [r.reshape(shp) for r in res]

    n_rep = rep_flat.shape[0]
    pk = lambda src: padded([src[n].reshape(-1) for n in rep_names] + [jnp.zeros((1,), F32)], 16 * LANES).reshape(-1, LANES)
    res, _ = _adam([small_all[:, :n_rep].reshape(N_DEV, -1, LANES)], pk(wts), pk(mom_m), pk(mom_v), "adam_small")
    res = [r.reshape(-1) for r in res]
    off = 0
    for n in rep_names:
        shp = wts[n].shape
        size = math.prod(shp)
        outs_g[n], outs_d[n], outs_m[n], outs_v[n] = [r[off:off + size].reshape(shp) for r in res]
        off += size
    loss = res[0][off]

    off = n_rep
    for n, cf in zip(_CONVS, conv_flat):
        _, kw, cl = wts[n].shape
        part = small_all[:, off:off + depth * kw * cl * N_DEV].reshape(N_DEV, depth * kw, N_DEV, cl)
        off += cf.shape[0]
        part = lax.dynamic_index_in_dim(part, me, axis=2, keepdims=False)
        two_d = lambda a: a.reshape(depth * kw, cl)
        res, _ = _adam([part], two_d(wts[n]), two_d(mom_m[n]), two_d(mom_v[n]), "adam_" + n)
        outs_g[n], outs_d[n], outs_m[n], outs_v[n] = [r.reshape(wts[n].shape) for r in res]

    return (loss, dx[None], *[outs_g[n] for n in _WEIGHTS], *[outs_d[n] for n in _WEIGHTS],
            *[outs_m[n] for n in _WEIGHTS], *[outs_v[n] for n in _WEIGHTS])
```

```python
import functools
import math

import jax
import jax.numpy as jnp
from jax import lax
from jax.experimental import pallas as pl
from jax.experimental.pallas import tpu as pltpu

F32 = jnp.float32
BF16 = jnp.bfloat16
EPS = 1e-6
LANES = 128
HEAD = 128
CA = 64
N_DEV = 8
VMEM_LIMIT = 48 * 1024 * 1024
ADAM_VMEM_BUDGET = 16 * 1024 * 1024
MESH = pl.DeviceIdType.MESH

ADAM_LR = 0.001
ADAM_B1 = 0.9
ADAM_B2 = 0.999
ADAM_EPS = 1e-08
ADAM_WD = 0.01
ADAM_STEP = 10


def _tile(n, cap, mult=LANES):
    best = None
    for t in range(mult, min(n, cap) + 1, mult):
        if n % t == 0:
            best = t
    return n if best is None else best


def _params(sem):
    return pltpu.CompilerParams(dimension_semantics=sem, vmem_limit_bytes=VMEM_LIMIT)


def _sigmoid(v):
    return jax.nn.sigmoid(v)


def _silu_grad(c):
    s = _sigmoid(c)
    return s + c * s * (1.0 - s)


_GELU_C = math.sqrt(2.0 / math.pi)


def _gelu(v):
    return 0.5 * v * (1.0 + jnp.tanh(_GELU_C * (v + 0.044715 * v * v * v)))


def _gelu_grad(v):
    t = jnp.tanh(_GELU_C * (v + 0.044715 * v * v * v))
    return 0.5 * (1.0 + t) + 0.5 * v * (1.0 - t * t) * _GELU_C * (1.0 + 3.0 * 0.044715 * v * v)


_NN = (((1,), (0,)), ((), ()))
_NT = (((1,), (1,)), ((), ()))
_TN = (((0,), (0,)), ((), ()))
_BNN = (((2,), (1,)), ((0,), (0,)))
_BNT = (((2,), (2,)), ((0,), (0,)))
_BTN = (((1,), (1,)), ((0,), (0,)))


def _bdot(a, b, dn=_NN):
    return lax.dot_general(a.astype(BF16), b.astype(BF16), dn, preferred_element_type=F32)


def _split(a):
    hi = a.astype(BF16)
    return hi, (a - hi.astype(F32)).astype(BF16)


def _dot3(ah, al, bh, bl, dn=_NN):
    def d(u, v):
        return lax.dot_general(u, v, dn, preferred_element_type=F32)
    return d(ah, bh) + (d(al, bh) + d(ah, bl))


def _hdot(a, b, dn=_NN):
    return _dot3(*_split(a), *_split(b), dn)


_PEERS = {"gather": (1, 2, 3, 4, 5, 6, 7), "scatter": (1, 2, 3, 4, 5, 6, 7), "chips": (1, 2, 4, 6), "relay": (2, 4, 6)}


def _kinds(jobs):
    return [{False: "gather", True: "scatter"}.get(kind, kind) for _, kind in jobs]


def _xchg_out_shapes(jobs):
    return [jax.ShapeDtypeStruct((N_DEV,) + a.shape if kind in ("gather", "chips") else a.shape, a.dtype)
            for (a, _), kind in zip(jobs, _kinds(jobs))]


def _xchg_scratch(jobs):
    n = len(jobs)
    return [pltpu.SemaphoreType.DMA((n, N_DEV - 1)), pltpu.SemaphoreType.DMA((n, N_DEV - 1)), pltpu.SemaphoreType.DMA((n,))]


def _xchg_copies(kinds, src, out, sems):
    send_sems, recv_sems, local_sems = sems
    x, y, c = lax.axis_index("x"), lax.axis_index("y"), lax.axis_index("c")
    me = 4 * x + 2 * y + c
    local, sends, recvs = [], [], []
    for k, kind in enumerate(kinds):
        if kind != "relay":
            local.append(pltpu.make_async_copy(src[k].at[me] if kind == "scatter" else src[k], out[k].at[me],
                                               local_sems.at[k]))
    for m in range(1, N_DEV):
        px = lax.rem(x + ((m >> 2) & 1), 2)
        py = lax.rem(y + ((m >> 1) & 1), 2)
        pc = lax.rem(c + (m & 1), 2)
        peer = 4 * px + 2 * py + pc
        for k, kind in enumerate(kinds):
            if m not in _PEERS[kind]:
                continue
            if kind == "relay":
                to, mine, there, here = (x, y, 1 - c), src[k].at[peer], out[k].at[peer], out[k].at[4 * px + 2 * py + 1 - c]
            else:
                to, mine, there, here = (px, py, pc), src[k].at[peer] if kind == "scatter" else src[k], out[k].at[me], out[k].at[peer]
            for dst, lst in ((there, sends), (here, recvs)):
                lst.append(pltpu.make_async_remote_copy(
                    src_ref=mine, dst_ref=dst, send_sem=send_sems.at[k, m - 1], recv_sem=recv_sems.at[k, m - 1],
                    device_id=to, device_id_type=MESH))
    return local, sends, recvs


def _xchg_start(scatter, src, out, sems):
    local, sends, _ = _xchg_copies(scatter, src, out, sems)
    for cp in local + sends:
        cp.start()


def _xchg_wait(scatter, src, out, sems):
    local, sends, recvs = _xchg_copies(scatter, src, out, sems)
    for cp in recvs:
        cp.wait_recv()
    for cp in sends:
        cp.wait_send()
    for cp in local:
        cp.wait()


_ANY = pl.BlockSpec(memory_space=pl.ANY)


def _xchg_aliases(jobs, n_in, n_out):
    return {n_in + k: n_out + k for k, kind in enumerate(_kinds(jobs)) if kind == "relay"}


def _exchange(jobs, name):
    n = len(jobs)
    kinds = _kinds(jobs)

    def body(*refs):
        src, out, sems = refs[:n], refs[n:2 * n], refs[2 * n:]
        _xchg_start(kinds, src, out, sems)
        _xchg_wait(kinds, src, out, sems)

    return pl.pallas_call(
        body, name=name, in_specs=[_ANY] * n, out_specs=[_ANY] * n, out_shape=_xchg_out_shapes(jobs),
        scratch_shapes=_xchg_scratch(jobs), input_output_aliases=_xchg_aliases(jobs, 0, 0),
        compiler_params=pltpu.CompilerParams(has_side_effects=True))(*[a for a, _ in jobs])


def _carried(body, n_in, n_out, jobs, grid):
    if not jobs:
        return body
    nj = len(jobs)
    scatter = _kinds(jobs)

    def wrapped(*refs):
        ins, src = refs[:n_in], refs[n_in:n_in + nj]
        outs, got = refs[n_in + nj:n_in + nj + n_out], refs[n_in + nj + n_out:n_in + 2 * nj + n_out]
        rest = refs[n_in + 2 * nj + n_out:]
        scratch, sems = rest[:len(rest) - 3], rest[len(rest) - 3:]
        ids = [pl.program_id(a) for a in range(len(grid))]
        first = functools.reduce(jnp.logical_and, [i == 0 for i in ids])
        last = functools.reduce(jnp.logical_and, [i == g - 1 for i, g in zip(ids, grid)])

        @pl.when(first)
        def _():
            _xchg_start(scatter, src, got, sems)

        body(*ins, *outs, *scratch)

        @pl.when(last)
        def _():
            _xchg_wait(scatter, src, got, sems)

    return wrapped


def _call(body, name, grid, in_specs, out_specs, out_shape, args, scratch=(), sem=None, jobs=()):
    jobs = list(jobs)
    nj = len(jobs)
    sem = ("arbitrary",) * len(grid) if jobs or sem is None else sem
    res = pl.pallas_call(
        _carried(body, len(in_specs), len(out_specs), jobs, grid), name=name, grid=grid,
        in_specs=list(in_specs) + [_ANY] * nj, out_specs=list(out_specs) + [_ANY] * nj,
        out_shape=list(out_shape) + _xchg_out_shapes(jobs),
        scratch_shapes=list(scratch) + (_xchg_scratch(jobs) if jobs else []),
        input_output_aliases=_xchg_aliases(jobs, len(in_specs), len(out_specs)),
        compiler_params=_params(sem))(*args, *[a for a, _ in jobs])
    return res[:len(out_specs)], res[len(out_specs):]


MATMUL_OPERAND_VMEM = 20 * 1024 * 1024


def _matmul(a, b, mode, out_dtype, name, res=None, jobs=()):
    if mode == "tn":
        kdim, m = a.shape
    else:
        m, kdim = a.shape
    n = b.shape[0] if mode == "nt" else b.shape[1]
    tm, tn = _tile(m, 1024), _tile(n, 1024)
    per_k = 2 * (tm * a.dtype.itemsize + tn * b.dtype.itemsize)
    tk = _tile(kdim, max(LANES, MATMUL_OPERAND_VMEM // per_k))
    nk = kdim // tk
    dims = {"nn": _NN, "nt": _NT, "tn": _TN}[mode]

    def body(*refs):
        a_ref, b_ref = refs[:2]
        r_ref = refs[2] if res is not None else None
        o_ref = refs[3] if res is not None else refs[2]
        acc = refs[-1] if nk > 1 else None

        def write(r):
            if r_ref is not None:
                r = r + r_ref[...].astype(F32)
            o_ref[...] = r.astype(out_dtype)

        prod = _bdot(a_ref[...], b_ref[...], dims)
        if nk == 1:
            write(prod)
        else:
            k = pl.program_id(2)

            @pl.when(k == 0)
            def _():
                acc[...] = prod

            @pl.when(jnp.logical_and(k > 0, k < nk - 1))
            def _():
                acc[...] += prod

            @pl.when(k == nk - 1)
            def _():
                write(acc[...] + prod)

    a_spec = pl.BlockSpec((tk, tm), lambda i, j, k: (k, i)) if mode == "tn" else pl.BlockSpec((tm, tk), lambda i, j, k: (i, k))
    b_spec = pl.BlockSpec((tn, tk), lambda i, j, k: (j, k)) if mode == "nt" else pl.BlockSpec((tk, tn), lambda i, j, k: (k, j))
    o_spec = pl.BlockSpec((tm, tn), lambda i, j, k: (i, j))
    in_specs = [a_spec, b_spec] + ([o_spec] if res is not None else [])
    args = (a, b) + ((res,) if res is not None else ())
    (out,), got = _call(body, name, (m // tm, n // tn, nk), in_specs, [o_spec], [jax.ShapeDtypeStruct((m, n), out_dtype)],
                        args, scratch=[pltpu.VMEM((tm, tn), F32)] if nk > 1 else [],
                        sem=("parallel", "parallel", "arbitrary"), jobs=jobs)
    return (out, got) if jobs else out


def _rms_fwd(x, gain, name):
    t, d = x.shape
    tb = _tile(t, 256, 8)

    def body(x_ref, g_ref, h_ref):
        xv = x_ref[...]
        r = lax.rsqrt(jnp.mean(xv * xv, axis=-1, keepdims=True) + EPS)
        h_ref[...] = (xv * r * g_ref[...]).astype(BF16)

    return pl.pallas_call(
        body, name=name, grid=(t // tb,),
        in_specs=[pl.BlockSpec((tb, d), lambda i: (i, 0)), pl.BlockSpec((1, d), lambda i: (0, 0))],
        out_specs=pl.BlockSpec((tb, d), lambda i: (i, 0)), out_shape=jax.ShapeDtypeStruct((t, d), BF16),
        compiler_params=_params(("parallel",)))(x, gain.reshape(1, d))


def _rms_bwd(x, dh, gain, dres, name):
    t, d = x.shape
    tb = _tile(t, 256, 8)

    def body(x_ref, dh_ref, g_ref, dr_ref, dx_ref, dg_ref):
        @pl.when(pl.program_id(0) == 0)
        def _():
            dg_ref[...] = jnp.zeros_like(dg_ref)

        xv = x_ref[...]
        dy = dh_ref[...].astype(F32)
        r = lax.rsqrt(jnp.mean(xv * xv, axis=-1, keepdims=True) + EPS)
        xh = xv * r
        dxh = dy * g_ref[...]
        dx_ref[...] = dr_ref[...] + r * (dxh - xh * jnp.mean(dxh * xh, axis=-1, keepdims=True))
        dg_ref[...] += jnp.sum(dy * xh, axis=0, keepdims=True)

    row = pl.BlockSpec((tb, d), lambda i: (i, 0))
    vec = pl.BlockSpec((1, d), lambda i: (0, 0))
    return pl.pallas_call(
        body, name=name, grid=(t // tb,), in_specs=[row, row, vec, row], out_specs=[row, vec],
        out_shape=[jax.ShapeDtypeStruct((t, d), F32), jax.ShapeDtypeStruct((1, d), F32)],
        compiler_params=_params(("arbitrary",)))(x, dh, gain.reshape(1, d), dres)


def _loss_head(x, target, gain):
    t, d = x.shape
    tb = _tile(t, 256, 8)

    def body(x_ref, t_ref, g_ref, dx_ref, dg_ref, loss_ref):
        @pl.when(pl.program_id(0) == 0)
        def _():
            dg_ref[...] = jnp.zeros_like(dg_ref)
            loss_ref[...] = jnp.zeros_like(loss_ref)

        xv = x_ref[...]
        r = lax.rsqrt(jnp.mean(xv * xv, axis=-1, keepdims=True) + EPS)
        xh = xv * r
        err = xh * g_ref[...] - t_ref[...]
        per_row = jnp.mean(err * err, axis=-1, keepdims=True)
        loss_ref[...] += 0.5 * jnp.sum(per_row, axis=0, keepdims=True)
        dy = err * (1.0 / d)
        dxh = dy * g_ref[...]
        dx_ref[...] = r * (dxh - xh * jnp.mean(dxh * xh, axis=-1, keepdims=True))
        dg_ref[...] += jnp.sum(dy * xh, axis=0, keepdims=True)

    row = pl.BlockSpec((tb, d), lambda i: (i, 0))
    vec = pl.BlockSpec((1, d), lambda i: (0, 0))
    return pl.pallas_call(
        body, name="loss_head", grid=(t // tb,), in_specs=[row, row, vec],
        out_specs=[row, vec, pl.BlockSpec((1, LANES), lambda i: (0, 0))],
        out_shape=[jax.ShapeDtypeStruct((t, d), F32), jax.ShapeDtypeStruct((1, d), F32),
                   jax.ShapeDtypeStruct((1, LANES), F32)],
        compiler_params=_params(("arbitrary",)))(x, target, gain.reshape(1, d))


def _shift_down(v, s, rows):
    if s == 0:
        return v
    return jnp.where(rows >= s, pltpu.roll(v, s, 0), 0.0)


def _shift_up(v, s, rows):
    if s == 0:
        return v
    t = v.shape[0]
    return jnp.where(rows < t - s, pltpu.roll(v, t - s, 0), 0.0)


def _conv(v, w, rows):
    k = w.shape[0]
    out = v * w[k - 1:k, :]
    for s in range(1, k):
        out = out + _shift_down(v, s, rows) * w[k - 1 - s:k - s, :]
    return out


def _qkv_fwd(proj, conv_w, nq, nk):
    t = proj.shape[0]
    cw = conv_w.shape[1]
    nblk = cw // HEAD

    def body(p_ref, w_ref, o_ref):
        j = pl.program_id(0)
        rows = lax.broadcasted_iota(jnp.int32, (t, HEAD), 0)
        c = _conv(p_ref[...].astype(F32), w_ref[...], rows)
        a = c * _sigmoid(c)
        nrm = a * lax.rsqrt(jnp.sum(a * a, axis=-1, keepdims=True) + EPS)
        nrm = nrm * jnp.where(j < nq, HEAD ** -0.5, 1.0)
        o_ref[...] = jnp.where(j < nq + nk, nrm, a).astype(BF16)

    return pl.pallas_call(
        body, name="qkv_fwd", grid=(nblk,),
        in_specs=[pl.BlockSpec((t, HEAD), lambda j: (0, j)), pl.BlockSpec((conv_w.shape[0], HEAD), lambda j: (0, j))],
        out_specs=pl.BlockSpec((t, HEAD), lambda j: (0, j)), out_shape=jax.ShapeDtypeStruct((t, cw), BF16),
        compiler_params=_params(("parallel",)))(proj, conv_w)


def _qkv_bwd(proj, dqkv, conv_w, nq, nk):
    t = proj.shape[0]
    kw, cw = conv_w.shape
    nblk = cw // HEAD

    def body(p_ref, d_ref, w_ref, dp_ref, dw_ref):
        j = pl.program_id(0)
        rows = lax.broadcasted_iota(jnp.int32, (t, HEAD), 0)
        xv = p_ref[...].astype(F32)
        w = w_ref[...]
        c = _conv(xv, w, rows)
        a = c * _sigmoid(c)
        dy = d_ref[...].astype(F32)
        r = lax.rsqrt(jnp.sum(a * a, axis=-1, keepdims=True) + EPS)
        y = a * r
        scale = jnp.where(j < nq, HEAD ** -0.5, 1.0)
        da_n = scale * r * (dy - y * jnp.sum(dy * y, axis=-1, keepdims=True))
        da = jnp.where(j < nq + nk, da_n, dy)
        dc = da * _silu_grad(c)
        dx = dc * w[kw - 1:kw, :]
        dw_ref[kw - 1:kw, :] = jnp.sum(dc * xv, axis=0, keepdims=True)
        for s in range(1, kw):
            dx = dx + _shift_up(dc, s, rows) * w[kw - 1 - s:kw - s, :]
            dw_ref[kw - 1 - s:kw - s, :] = jnp.sum(dc * _shift_down(xv, s, rows), axis=0, keepdims=True)
        dp_ref[...] = dx.astype(BF16)

    blk = pl.BlockSpec((t, HEAD), lambda j: (0, j))
    wblk = pl.BlockSpec((kw, HEAD), lambda j: (0, j))
    return pl.pallas_call(
        body, name="qkv_bwd", grid=(nblk,), in_specs=[blk, blk, wblk], out_specs=[blk, wblk],
        out_shape=[jax.ShapeDtypeStruct((t, cw), BF16), jax.ShapeDtypeStruct((kw, cw), F32)],
        compiler_params=_params(("parallel",)))(proj, dqkv, conv_w)


def _softplus(v):
    return jnp.where(v < -15.0, jnp.exp(v), jnp.maximum(v, 0.0) + jnp.log(1.0 + jnp.exp(-jnp.abs(v))))


def _gate_fwd(ba, alog_pad, dtb_pad, hv):
    t = ba.shape[0]
    tb = _tile(t, 512, CA)

    def body(ba_ref, al_ref, dt_ref, o_ref):
        v = ba_ref[...]
        beta = _sigmoid(v)
        g = -jnp.exp(al_ref[...]) * _softplus(v + dt_ref[...])
        pos = lax.broadcasted_iota(jnp.int32, (tb, LANES), 0) % CA
        s = 1
        while s < CA:
            g = g + jnp.where(pos >= s, pltpu.roll(g, s, 0), 0.0)
            s *= 2
        lane = lax.broadcasted_iota(jnp.int32, (tb, LANES), 1)
        o_ref[...] = jnp.where(lane < hv, beta, g)

    row = pl.BlockSpec((tb, LANES), lambda i: (i, 0))
    vec = pl.BlockSpec((1, LANES), lambda i: (0, 0))
    return pl.pallas_call(
        body, name="gate_fwd", grid=(t // tb,), in_specs=[row, vec, vec], out_specs=row,
        out_shape=jax.ShapeDtypeStruct((t, LANES), F32), compiler_params=_params(("parallel",)))(ba, alog_pad, dtb_pad)


def _gate_bwd(ba, dbg, alog_pad, dtb_pad, hv):
    t = ba.shape[0]
    tb = _tile(t, 512, CA)

    def body(ba_ref, d_ref, al_ref, dt_ref, dba_ref, dal_ref, ddt_ref):
        @pl.when(pl.program_id(0) == 0)
        def _():
            dal_ref[...] = jnp.zeros_like(dal_ref)
            ddt_ref[...] = jnp.zeros_like(ddt_ref)

        v = ba_ref[...]
        d = d_ref[...]
        pos = lax.broadcasted_iota(jnp.int32, (tb, LANES), 0) % CA
        dg = d
        s = 1
        while s < CA:
            dg = dg + jnp.where(pos < CA - s, pltpu.roll(dg, tb - s, 0), 0.0)
            s *= 2
        beta = _sigmoid(v)
        na = -jnp.exp(al_ref[...])
        z = v + dt_ref[...]
        da = dg * na * _sigmoid(z)
        lane = lax.broadcasted_iota(jnp.int32, (tb, LANES), 1)
        in_a = jnp.logical_and(lane >= hv, lane < 2 * hv)
        da = jnp.where(in_a, da, 0.0)
        dba_ref[...] = jnp.where(lane < hv, d * beta * (1.0 - beta), da)
        ddt_ref[...] += jnp.sum(da, axis=0, keepdims=True)
        dal_ref[...] += jnp.sum(jnp.where(in_a, dg * na * _softplus(z), 0.0), axis=0, keepdims=True)

    row = pl.BlockSpec((tb, LANES), lambda i: (i, 0))
    vec = pl.BlockSpec((1, LANES), lambda i: (0, 0))
    return pl.pallas_call(
        body, name="gate_bwd", grid=(t // tb,), in_specs=[row, row, vec, vec], out_specs=[row, vec, vec],
        out_shape=[jax.ShapeDtypeStruct((t, LANES), F32), jax.ShapeDtypeStruct((1, LANES), F32),
                   jax.ShapeDtypeStruct((1, LANES), F32)],
        compiler_params=_params(("arbitrary",)))(ba, dbg, alog_pad, dtb_pad)


def _chunk_masks():
    r = lax.broadcasted_iota(jnp.int32, (CA, CA), 0)
    c = lax.broadcasted_iota(jnp.int32, (CA, CA), 1)
    return r >= c, r > c, (r == c).astype(F32)


def _inv_unit_lower(a, eye):
    x = eye - a
    ph, plo = _split(a)
    n = 1
    while n < CA // 2:
        ph, plo = _split(_dot3(ph, plo, ph, plo, _BNN))
        x = x + _dot3(*_split(x), ph, plo, _BNN)
        n *= 2
    return x


def _delta_pre(q, k, v, bcol, gc, gr, gl, causal, strict):
    eg = jnp.exp(gc)
    dm = jnp.exp(jnp.where(causal, gc[:, :, :CA] - gr, -jnp.inf))
    kb = k * bcol
    kkb = _bdot(kb, k, _BNT)
    a = jnp.where(strict, kkb * dm, 0.0)
    rhs = jnp.concatenate([v * bcol, kb * eg], axis=2)
    qk = _bdot(q, k, _BNT)
    ekd = jnp.exp(gl - gc)
    return dict(eg=eg, dm=dm, kb=kb, kkb=kkb, a=a, rhs=rhs, p=qk * dm, qd=q * eg, ekd=ekd, kd=k * ekd, cd=jnp.exp(gl))


def _delta_fwd(qkvn, beta_b, gam_b, gam_r, gam_l, hqk, hv, jobs=()):
    t = qkvn.shape[0]
    rep = hv // hqk
    rb = _tile(t, 512, CA)
    nb = t // rb
    ncb = rb // CA
    nc = t // CA

    def body(q_ref, k_ref, v_ref, b_ref, gc_ref, gr_ref, gl_ref, o_ref, s_ref, tm_ref, state, sol_sc, p_sc):
        @pl.when(pl.program_id(1) == 0)
        def _():
            state[...] = jnp.zeros_like(state)

        causal, strict, eye = _chunk_masks()

        def chunks(a):
            return a.astype(F32).reshape(ncb, CA, a.shape[-1])

        q = chunks(q_ref[...])
        k = chunks(k_ref[...])
        for h in range(rep):
            pre = _delta_pre(q, k, chunks(v_ref[:, h * HEAD:(h + 1) * HEAD]), chunks(b_ref[h]), chunks(gc_ref[h]),
                             gr_ref[h], gl_ref[h], causal, strict)
            tm = _inv_unit_lower(pre["a"], eye)
            tm_ref[h] = tm
            sol_sc[h] = _hdot(tm, pre["rhs"], _BNN)
            p_sc[h] = pre["p"]

        def chunk(n, carry):
            rows = pl.ds(pl.multiple_of(n * CA, CA), CA)
            qn = q_ref[rows, :].astype(F32)
            kn = k_ref[rows, :].astype(F32)
            for h in range(rep):
                gc = gc_ref[h, rows, :]
                gl = gl_ref[h, n]
                s = state[h]
                v_new = sol_sc[h, n, :, :HEAD] - _bdot(sol_sc[h, n, :, HEAD:], s)
                o_ref[rows, h * HEAD:(h + 1) * HEAD] = _bdot(qn * jnp.exp(gc), s) + _bdot(p_sc[h, n], v_new)
                s_ref[h, n] = s.astype(BF16)
                state[h] = s * jnp.exp(gl) + _bdot(kn * jnp.exp(gl - gc), v_new, _TN)
            return carry

        lax.fori_loop(0, ncb, chunk, 0)

    qoff, koff, voff = 0, hqk, 2 * hqk // rep
    per_chunk = lambda width: pl.BlockSpec((rep, ncb, 1, width), lambda j, i: (j, i, 0, 0))
    return _call(
        body, "delta_fwd", (hqk, nb),
        [pl.BlockSpec((rb, HEAD), lambda j, i: (i, qoff + j)),
         pl.BlockSpec((rb, HEAD), lambda j, i: (i, koff + j)),
         pl.BlockSpec((rb, rep * HEAD), lambda j, i: (i, voff + j)),
         pl.BlockSpec((rep, rb, LANES), lambda j, i: (j, i, 0)),
         pl.BlockSpec((rep, rb, LANES), lambda j, i: (j, i, 0)),
         per_chunk(CA), per_chunk(LANES)],
        [pl.BlockSpec((rb, rep * HEAD), lambda j, i: (i, j)),
         pl.BlockSpec((rep, ncb, HEAD, HEAD), lambda j, i: (j, i, 0, 0)),
         pl.BlockSpec((rep, ncb, CA, CA), lambda j, i: (j, i, 0, 0))],
        [jax.ShapeDtypeStruct((t, hv * HEAD), F32), jax.ShapeDtypeStruct((hv, nc, HEAD, HEAD), BF16),
         jax.ShapeDtypeStruct((hv, nc, CA, CA), F32)],
        (qkvn, qkvn, qkvn, beta_b, gam_b, gam_r, gam_l),
        scratch=[pltpu.VMEM((rep, HEAD, HEAD), F32), pltpu.VMEM((rep, ncb, CA, 2 * HEAD), F32),
                 pltpu.VMEM((rep, ncb, CA, CA), F32)],
        sem=("parallel", "arbitrary"), jobs=jobs)


def _delta_bwd(qkvn, beta_b, gam_b, gam_r, gam_l, s_all, tm_all, do, hqk, hv, jobs=()):
    t = qkvn.shape[0]
    rep = hv // hqk
    rb = _tile(t, 512, CA)
    nb = t // rb
    ncb = rb // CA

    def body(q_ref, k_ref, v_ref, b_ref, gc_ref, gr_ref, gl_ref, s_ref, tm_ref, do_ref,
             dq_ref, dk_ref, dv_ref, db_ref, dg_ref, dstate, sol_sc, vn_sc, p_sc, kkb_sc, dvn_sc, ds_sc):
        @pl.when(pl.program_id(1) == 0)
        def _():
            dstate[...] = jnp.zeros_like(dstate)

        causal, strict, _ = _chunk_masks()
        ones = jnp.ones((ncb, CA, LANES), BF16)
        last = lax.broadcasted_iota(jnp.int32, (CA, LANES), 0) == CA - 1

        def chunks(a):
            return a.astype(F32).reshape(ncb, CA, a.shape[-1])

        def rows_of(a):
            return a.reshape(rb, a.shape[-1])

        def rowsum(m):
            return jnp.sum(m, axis=2, keepdims=True)

        def colsum(m):
            hi, lo = _split(m)
            return _bdot(hi, ones, _BTN) + _bdot(lo, ones, _BTN)

        q = chunks(q_ref[...])
        k = chunks(k_ref[...])

        def head_inputs(h):
            v = chunks(v_ref[:, h * HEAD:(h + 1) * HEAD])
            bcol = chunks(b_ref[h])
            return v, bcol, _delta_pre(q, k, v, bcol, chunks(gc_ref[h]), gr_ref[h], gl_ref[h], causal, strict)

        for h in range(rep):
            _, _, pre = head_inputs(h)
            sol = _hdot(tm_ref[h], pre["rhs"], _BNN)
            sol_sc[h] = sol
            vn_sc[h] = sol[:, :, :HEAD] - _bdot(sol[:, :, HEAD:], s_ref[h], _BNN)
            p_sc[h] = pre["p"]
            kkb_sc[h] = pre["kkb"]

        def state_step(it, carry):
            n = ncb - 1 - it
            rows = pl.ds(pl.multiple_of(n * CA, CA), CA)
            qn = q_ref[rows, :].astype(F32)
            kn = k_ref[rows, :].astype(F32)
            for h in range(rep):
                gc = gc_ref[h, rows, :]
                gl = gl_ref[h, n]
                ds = dstate[h]
                ds_sc[h, n] = ds
                dov = do_ref[rows, h * HEAD:(h + 1) * HEAD].astype(F32)
                dvn = _bdot(p_sc[h, n], dov, _TN) + _bdot(kn * jnp.exp(gl - gc), ds)
                dvn_sc[h, n] = dvn
                dstate[h] = (ds * jnp.exp(gl) + _bdot(qn * jnp.exp(gc), dov, _TN)
                             - _bdot(sol_sc[h, n, :, HEAD:], dvn, _TN))
            return carry

        lax.fori_loop(0, ncb, state_step, 0)

        kkr = _bdot(k, k, _BNT)
        dq = jnp.zeros((ncb, CA, HEAD), F32)
        dk = jnp.zeros((ncb, CA, HEAD), F32)
        for h in range(rep):
            v, bcol, pre = head_inputs(h)
            eg, dm, kb, qd, kd, cd = pre["eg"], pre["dm"], pre["kb"], pre["qd"], pre["kd"], pre["cd"]
            p = p_sc[h]
            sol = sol_sc[h]
            s = s_ref[h].astype(F32)
            ds = ds_sc[h]
            dov = chunks(do_ref[:, h * HEAD:(h + 1) * HEAD])
            v_new = vn_sc[h]
            dvn = dvn_sc[h]

            dp = jnp.where(causal, _bdot(dov, v_new, _BNT), 0.0)
            dqd = _bdot(dov, s, _BNT)
            dkd = _bdot(v_new, ds, _BNT)
            dcd = jnp.sum(rowsum(s * ds), axis=1, keepdims=True)
            dw = -_bdot(dvn, s, _BNT)

            drhs = _hdot(tm_ref[h], jnp.concatenate([dvn, dw], axis=2), _BTN)
            dbv, dbke = drhs[:, :, :HEAD], drhs[:, :, HEAD:]
            da = -jnp.where(strict, _bdot(drhs, sol, _BNT), 0.0)
            m = da * dm
            e = m * kkb_sc[h] + dp * p
            dgam = rowsum(e) - colsum(e) + rowsum(dbke * kb * eg) + rowsum(dqd * qd)
            r = rowsum(dkd * kd)
            tot = jnp.sum(r, axis=1, keepdims=True) + dcd * cd
            dgam = dgam - r + jnp.where(last, tot, 0.0)
            dbeta = rowsum(m * kkr) + rowsum(dbv * v) + rowsum(dbke * eg * k)
            nm = m * bcol[:, :, :CA]
            dqk = dp * dm
            dq = dq + _bdot(dqk, k, _BNN) + eg * dqd
            dk = (dk + _bdot(nm, k, _BNN) + _bdot(nm, k, _BTN) + _bdot(dqk, q, _BTN) + bcol * eg * dbke
                  + pre["ekd"] * dkd)
            dv_ref[:, h * HEAD:(h + 1) * HEAD] = rows_of(bcol * dbv)
            db_ref[h] = rows_of(jnp.broadcast_to(dbeta, (ncb, CA, LANES)))
            dg_ref[h] = rows_of(jnp.broadcast_to(dgam, (ncb, CA, LANES)))
        dq_ref[...] = rows_of(dq)
        dk_ref[...] = rows_of(dk)

    qoff, koff, voff = 0, hqk, 2 * hqk // rep
    rv = lambda i: nb - 1 - i
    hd = pl.BlockSpec((rep, rb, LANES), lambda j, i: (j, rv(i), 0))
    qk_out = pl.BlockSpec((rb, HEAD), lambda j, i: (rv(i), j))
    v_blk = pl.BlockSpec((rb, rep * HEAD), lambda j, i: (rv(i), j))
    per_chunk = lambda *shape: pl.BlockSpec((rep, ncb) + shape, lambda j, i: (j, rv(i), 0, 0))
    return _call(
        body, "delta_bwd", (hqk, nb),
        [pl.BlockSpec((rb, HEAD), lambda j, i: (rv(i), qoff + j)),
         pl.BlockSpec((rb, HEAD), lambda j, i: (rv(i), koff + j)),
         pl.BlockSpec((rb, rep * HEAD), lambda j, i: (rv(i), voff + j)),
         hd, hd, per_chunk(1, CA), per_chunk(1, LANES), per_chunk(HEAD, HEAD), per_chunk(CA, CA), v_blk],
        [qk_out, qk_out, v_blk, hd, hd],
        [jax.ShapeDtypeStruct((t, hqk * HEAD), F32), jax.ShapeDtypeStruct((t, hqk * HEAD), F32),
         jax.ShapeDtypeStruct((t, hv * HEAD), F32),
         jax.ShapeDtypeStruct((hv, t, LANES), F32), jax.ShapeDtypeStruct((hv, t, LANES), F32)],
        (qkvn, qkvn, qkvn, beta_b, gam_b, gam_r, gam_l, s_all, tm_all, do),
        scratch=[pltpu.VMEM((rep, HEAD, HEAD), F32), pltpu.VMEM((rep, ncb, CA, 2 * HEAD), F32),
                 pltpu.VMEM((rep, ncb, CA, HEAD), F32), pltpu.VMEM((rep, ncb, CA, CA), F32),
                 pltpu.VMEM((rep, ncb, CA, CA), F32), pltpu.VMEM((rep, ncb, CA, HEAD), F32),
                 pltpu.VMEM((rep, ncb, HEAD, HEAD), F32)],
        sem=("parallel", "arbitrary"), jobs=jobs)


def _apost_fwd(o, proj, gain, zoff, hv):
    t = o.shape[0]
    tb = _tile(t, 1024, 8)
    zb = zoff // HEAD

    def body(o_ref, z_ref, g_ref, y_ref):
        ov = o_ref[...]
        z = z_ref[...].astype(F32)
        r = lax.rsqrt(jnp.mean(ov * ov, axis=-1, keepdims=True) + EPS)
        y_ref[...] = (ov * r * g_ref[...] * (z * _sigmoid(z))).astype(BF16)

    blk = pl.BlockSpec((tb, HEAD), lambda i, h: (i, h))
    return pl.pallas_call(
        body, name="apost_fwd", grid=(t // tb, hv),
        in_specs=[blk, pl.BlockSpec((tb, HEAD), lambda i, h: (i, zb + h)), pl.BlockSpec((1, HEAD), lambda i, h: (0, 0))],
        out_specs=blk, out_shape=jax.ShapeDtypeStruct((t, hv * HEAD), BF16),
        compiler_params=_params(("parallel", "parallel")))(o, proj, gain.reshape(1, HEAD))


def _apost_bwd(o, proj, gain, dy, zoff, hv):
    t = o.shape[0]
    tb = _tile(t, 1024, 8)
    zb = zoff // HEAD

    def body(o_ref, z_ref, g_ref, dy_ref, do_ref, dz_ref, dg_ref):
        @pl.when(jnp.logical_and(pl.program_id(0) == 0, pl.program_id(1) == 0))
        def _():
            dg_ref[...] = jnp.zeros_like(dg_ref)

        ov = o_ref[...]
        z = z_ref[...].astype(F32)
        d = dy_ref[...].astype(F32)
        r = lax.rsqrt(jnp.mean(ov * ov, axis=-1, keepdims=True) + EPS)
        oh = ov * r
        sz = z * _sigmoid(z)
        dn = d * sz
        dz_ref[...] = (d * oh * g_ref[...] * _silu_grad(z)).astype(BF16)
        doh = dn * g_ref[...]
        do_ref[...] = r * (doh - oh * jnp.mean(doh * oh, axis=-1, keepdims=True))
        dg_ref[...] += jnp.sum(dn * oh, axis=0, keepdims=True)

    blk = pl.BlockSpec((tb, HEAD), lambda i, h: (i, h))
    vec = pl.BlockSpec((1, HEAD), lambda i, h: (0, 0))
    return pl.pallas_call(
        body, name="apost_bwd", grid=(t // tb, hv),
        in_specs=[blk, pl.BlockSpec((tb, HEAD), lambda i, h: (i, zb + h)), vec, blk],
        out_specs=[blk, blk, vec],
        out_shape=[jax.ShapeDtypeStruct((t, hv * HEAD), F32), jax.ShapeDtypeStruct((t, hv * HEAD), BF16),
                   jax.ShapeDtypeStruct((1, HEAD), F32)],
        compiler_params=_params(("arbitrary", "arbitrary")))(o, proj, gain.reshape(1, HEAD), dy)


def _sgu_fwd(proj, gain, w_s, b_t, uoff, wb):
    t = proj.shape[0]
    ng = wb // HEAD

    def body(u_ref, v_ref, g_ref, w_ref, b_ref, o_ref):
        r_i = lax.broadcasted_iota(jnp.int32, (HEAD, HEAD), 0)
        c_i = lax.broadcasted_iota(jnp.int32, (HEAD, HEAD), 1)
        u = _gelu(u_ref[...].astype(F32))
        vg = _gelu(v_ref[...].astype(F32))
        vn = vg * lax.rsqrt(jnp.mean(vg * vg, axis=-1, keepdims=True) + EPS) * g_ref[...]
        for g in range(ng):
            cols = slice(g * HEAD, (g + 1) * HEAD)
            wg = jnp.where(r_i >= c_i, w_ref[g], 0.0)
            mixed = _bdot(wg, vn[:, cols]) + b_ref[:, g:g + 1]
            o_ref[:, cols] = (u[:, cols] * mixed).astype(BF16)

    ub, vb = uoff // wb, uoff // wb + 1
    return pl.pallas_call(
        body, name="sgu_fwd", grid=(t // HEAD,),
        in_specs=[pl.BlockSpec((HEAD, wb), lambda i: (i, ub)), pl.BlockSpec((HEAD, wb), lambda i: (i, vb)),
                  pl.BlockSpec((1, wb), lambda i: (0, 0)), pl.BlockSpec((ng, HEAD, HEAD), lambda i: (0, 0, 0)),
                  pl.BlockSpec((HEAD, ng), lambda i: (0, 0))],
        out_specs=pl.BlockSpec((HEAD, wb), lambda i: (i, 0)), out_shape=jax.ShapeDtypeStruct((t, wb), BF16),
        compiler_params=_params(("parallel",)))(proj, proj, gain.reshape(1, wb), w_s, b_t)


def _sgu_bwd(proj, gain, w_s, b_t, dout, uoff, wb):
    t = proj.shape[0]
    ng = wb // HEAD

    def body(u_ref, v_ref, g_ref, w_ref, b_ref, d_ref, du_ref, dv_ref, dw_ref, db_ref, dg_ref, dvn_ref):
        @pl.when(pl.program_id(0) == 0)
        def _():
            dw_ref[...] = jnp.zeros_like(dw_ref)
            db_ref[...] = jnp.zeros_like(db_ref)
            dg_ref[...] = jnp.zeros_like(dg_ref)

        r_i = lax.broadcasted_iota(jnp.int32, (HEAD, HEAD), 0)
        c_i = lax.broadcasted_iota(jnp.int32, (HEAD, HEAD), 1)
        tril = r_i >= c_i
        ub = u_ref[...].astype(F32)
        vb = v_ref[...].astype(F32)
        u = _gelu(ub)
        vg = _gelu(vb)
        r = lax.rsqrt(jnp.mean(vg * vg, axis=-1, keepdims=True) + EPS)
        vh = vg * r
        vn = vh * g_ref[...]
        d = d_ref[...].astype(F32)
        for g in range(ng):
            cols = slice(g * HEAD, (g + 1) * HEAD)
            wg = jnp.where(tril, w_ref[g], 0.0)
            mixed = _bdot(wg, vn[:, cols]) + b_ref[:, g:g + 1]
            du_ref[:, cols] = (d[:, cols] * mixed * _gelu_grad(ub[:, cols])).astype(BF16)
            dmix = d[:, cols] * u[:, cols]
            dw_ref[g] += jnp.where(tril, _bdot(dmix, vn[:, cols], _NT), 0.0)
            db_ref[g] += jnp.broadcast_to(jnp.sum(dmix, axis=1, keepdims=True), (HEAD, HEAD))
            dvn_ref[:, cols] = _bdot(wg, dmix, _TN)
        dvn = dvn_ref[...]
        dg_ref[...] += jnp.sum(dvn * vh, axis=0, keepdims=True)
        dvh = dvn * g_ref[...]
        dvg = r * (dvh - vh * jnp.mean(dvh * vh, axis=-1, keepdims=True))
        dv_ref[...] = (dvg * _gelu_grad(vb)).astype(BF16)

    ub_i, vb_i = uoff // wb, uoff // wb + 1
    row = pl.BlockSpec((HEAD, wb), lambda i: (i, 0))
    mat = pl.BlockSpec((ng, HEAD, HEAD), lambda i: (0, 0, 0))
    vec = pl.BlockSpec((1, wb), lambda i: (0, 0))
    return pl.pallas_call(
        body, name="sgu_bwd", grid=(t // HEAD,),
        in_specs=[pl.BlockSpec((HEAD, wb), lambda i: (i, ub_i)), pl.BlockSpec((HEAD, wb), lambda i: (i, vb_i)),
                  vec, mat, pl.BlockSpec((HEAD, ng), lambda i: (0, 0)), row],
        out_specs=[row, row, mat, mat, vec],
        out_shape=[jax.ShapeDtypeStruct((t, wb), BF16), jax.ShapeDtypeStruct((t, wb), BF16),
                   jax.ShapeDtypeStruct((ng, HEAD, HEAD), F32), jax.ShapeDtypeStruct((ng, HEAD, HEAD), F32),
                   jax.ShapeDtypeStruct((1, wb), F32)],
        scratch_shapes=[pltpu.VMEM((HEAD, wb), F32)],
        compiler_params=_params(("arbitrary",)))(proj, proj, gain.reshape(1, wb), w_s, b_t, dout)


def _merge_specs(t, d, goff):
    tb = _tile(t, 512, 8)
    tc = _tile(d, 512)
    gb = goff // tc
    nd = d // tc
    blk = pl.BlockSpec((tb, tc), lambda i, j: (i, j))
    ga = pl.BlockSpec((tb, tc), lambda i, j: (i, gb + j))
    gbs = pl.BlockSpec((tb, tc), lambda i, j: (i, gb + nd + j))
    return (t // tb, nd), blk, ga, gbs


def _merge_fwd(ya, yb, proj, goff):
    t, d = ya.shape
    grid, blk, ga, gbs = _merge_specs(t, d, goff)

    def body(ya_ref, yb_ref, ga_ref, gb_ref, o_ref):
        o_ref[...] = (_sigmoid(ga_ref[...].astype(F32)) * ya_ref[...].astype(F32)
                      + _sigmoid(gb_ref[...].astype(F32)) * yb_ref[...].astype(F32)).astype(BF16)

    return pl.pallas_call(
        body, name="merge_fwd", grid=grid, in_specs=[blk, blk, ga, gbs], out_specs=blk,
        out_shape=jax.ShapeDtypeStruct((t, d), BF16),
        compiler_params=_params(("parallel", "parallel")))(ya, yb, proj, proj)


def _merge_bwd(dm, ya, yb, proj, goff):
    t, d = ya.shape
    grid, blk, ga, gbs = _merge_specs(t, d, goff)

    def body(dm_ref, ya_ref, yb_ref, ga_ref, gb_ref, dya_ref, dyb_ref, dga_ref, dgb_ref):
        dmv = dm_ref[...].astype(F32)
        sa = _sigmoid(ga_ref[...].astype(F32))
        sb = _sigmoid(gb_ref[...].astype(F32))
        dya_ref[...] = (dmv * sa).astype(BF16)
        dyb_ref[...] = (dmv * sb).astype(BF16)
        dga_ref[...] = (dmv * ya_ref[...].astype(F32) * sa * (1.0 - sa)).astype(BF16)
        dgb_ref[...] = (dmv * yb_ref[...].astype(F32) * sb * (1.0 - sb)).astype(BF16)

    shp = jax.ShapeDtypeStruct((t, d), BF16)
    return pl.pallas_call(
        body, name="merge_bwd", grid=grid, in_specs=[blk, blk, blk, ga, gbs], out_specs=[blk] * 4,
        out_shape=[shp] * 4, compiler_params=_params(("parallel", "parallel")))(dm, ya, yb, proj, proj)


def _ffn_act_fwd(up, conv_w, bias, dff):
    t = up.shape[0]
    nblk = dff // HEAD
    kw = conv_w.shape[0]

    def body(g_ref, v_ref, wg_ref, wv_ref, bg_ref, bv_ref, o_ref):
        rows = lax.broadcasted_iota(jnp.int32, (t, HEAD), 0)
        cg = _conv(g_ref[...].astype(F32), wg_ref[...], rows) + bg_ref[...]
        cv = _conv(v_ref[...].astype(F32), wv_ref[...], rows) + bv_ref[...]
        o_ref[...] = (cg * _sigmoid(cg) * cv).astype(BF16)

    return pl.pallas_call(
        body, name="ffn_act_fwd", grid=(nblk,),
        in_specs=[pl.BlockSpec((t, HEAD), lambda j: (0, j)), pl.BlockSpec((t, HEAD), lambda j: (0, nblk + j)),
                  pl.BlockSpec((kw, HEAD), lambda j: (0, j)), pl.BlockSpec((kw, HEAD), lambda j: (0, nblk + j)),
                  pl.BlockSpec((1, HEAD), lambda j: (0, j)), pl.BlockSpec((1, HEAD), lambda j: (0, nblk + j))],
        out_specs=pl.BlockSpec((t, HEAD), lambda j: (0, j)), out_shape=jax.ShapeDtypeStruct((t, dff), BF16),
        compiler_params=_params(("parallel",)))(up, up, conv_w, conv_w, bias, bias)


def _ffn_act_bwd(up, dact, conv_w, bias, dff, jobs=()):
    t = up.shape[0]
    nblk = dff // HEAD
    kw = conv_w.shape[0]

    def body(me_ref, pa_ref, d_ref, wm_ref, wp_ref, bm_ref, bp_ref, dup_ref, dw_ref, db_ref):
        is_gate = pl.program_id(0) < nblk
        rows = lax.broadcasted_iota(jnp.int32, (t, HEAD), 0)
        xv = me_ref[...].astype(F32)
        w = wm_ref[...]
        cm = _conv(xv, w, rows) + bm_ref[...]
        cp = _conv(pa_ref[...].astype(F32), wp_ref[...], rows) + bp_ref[...]
        d = d_ref[...].astype(F32)
        dc = jnp.where(is_gate, d * cp * _silu_grad(cm), d * (cp * _sigmoid(cp)))
        db_ref[...] = jnp.sum(dc, axis=0, keepdims=True)
        dx = dc * w[kw - 1:kw, :]
        dw_ref[kw - 1:kw, :] = jnp.sum(dc * xv, axis=0, keepdims=True)
        for s in range(1, kw):
            dx = dx + _shift_up(dc, s, rows) * w[kw - 1 - s:kw - s, :]
            dw_ref[kw - 1 - s:kw - s, :] = jnp.sum(dc * _shift_down(xv, s, rows), axis=0, keepdims=True)
        dup_ref[...] = dx.astype(BF16)

    part = lambda j: (j + nblk) % (2 * nblk)
    me = pl.BlockSpec((t, HEAD), lambda j: (0, j))
    wme = pl.BlockSpec((kw, HEAD), lambda j: (0, j))
    bme = pl.BlockSpec((1, HEAD), lambda j: (0, j))
    return _call(
        body, "ffn_act_bwd", (2 * nblk,),
        [me, pl.BlockSpec((t, HEAD), lambda j: (0, part(j))), pl.BlockSpec((t, HEAD), lambda j: (0, j % nblk)),
         wme, pl.BlockSpec((kw, HEAD), lambda j: (0, part(j))),
         bme, pl.BlockSpec((1, HEAD), lambda j: (0, part(j)))],
        [me, wme, bme],
        [jax.ShapeDtypeStruct((t, 2 * dff), BF16), jax.ShapeDtypeStruct((kw, 2 * dff), F32),
         jax.ShapeDtypeStruct((1, 2 * dff), F32)],
        (up, up, dact, conv_w, conv_w, bias, bias), sem=("parallel",), jobs=jobs)


def _ple_fwd(x, gt, pp):
    t, d = x.shape
    tb, tc = _tile(t, 512, 8), _tile(d, 1024)

    def body(x_ref, g_ref, p_ref, o_ref):
        o_ref[...] = x_ref[...] + _sigmoid(g_ref[...].astype(F32)) * p_ref[...].astype(F32)

    blk = pl.BlockSpec((tb, tc), lambda i, j: (i, j))
    return pl.pallas_call(
        body, name="ple_fwd", grid=(t // tb, d // tc), in_specs=[blk, blk, blk], out_specs=blk,
        out_shape=jax.ShapeDtypeStruct((t, d), F32), compiler_params=_params(("parallel", "parallel")))(x, gt, pp)


def _ple_bwd(dx, gt, pp):
    t, d = dx.shape
    tb, tc = _tile(t, 512, 8), _tile(d, 1024)

    def body(dx_ref, g_ref, p_ref, dg_ref, dp_ref):
        dv = dx_ref[...]
        s = _sigmoid(g_ref[...].astype(F32))
        dg_ref[...] = (dv * p_ref[...].astype(F32) * s * (1.0 - s)).astype(BF16)
        dp_ref[...] = (dv * s).astype(BF16)

    blk = pl.BlockSpec((tb, tc), lambda i, j: (i, j))
    shp = jax.ShapeDtypeStruct((t, d), BF16)
    return pl.pallas_call(
        body, name="ple_bwd", grid=(t // tb, d // tc), in_specs=[blk, blk, blk], out_specs=[blk, blk],
        out_shape=[shp, shp], compiler_params=_params(("parallel", "parallel")))(dx, gt, pp)


def _adam(pieces, w, m, v, name, jobs=()):
    nq = len(pieces)
    npart, rp, c = pieces[0].shape
    row_bytes = 2 * c * (nq * npart * pieces[0].dtype.itemsize + 7 * 4)
    tr = _tile(rp, max(16, min(512, ADAM_VMEM_BUDGET // row_bytes)), 16)
    nblk = rp // tr
    c1 = 1.0 - ADAM_B1 ** ADAM_STEP
    c2 = 1.0 - ADAM_B2 ** ADAM_STEP

    def body(*refs):
        p_refs = refs[:nq]
        w_ref, m_ref, v_ref, g_ref, d_ref, mo_ref, vo_ref = refs[nq:]
        for q in range(nq):
            @pl.when(pl.program_id(0) == q)
            def _(p_ref=p_refs[q]):
                g = p_ref[0].astype(F32)
                for i in range(1, npart):
                    g = g + p_ref[i].astype(F32)
                mn = ADAM_B1 * m_ref[...] + (1.0 - ADAM_B1) * g
                vn = ADAM_B2 * v_ref[...] + (1.0 - ADAM_B2) * (g * g)
                g_ref[...] = g
                mo_ref[...] = mn
                vo_ref[...] = vn
                d_ref[...] = -ADAM_LR * ((mn / c1) / (jnp.sqrt(vn / c2) + ADAM_EPS) + ADAM_WD * w_ref[...])

    def piece_spec(q):
        return pl.BlockSpec((npart, tr, c), lambda i, r: (0, jnp.where(i == q, r, jnp.where(i < q, 0, nblk - 1)), 0))

    blk = pl.BlockSpec((tr, c), lambda i, r: (i * nblk + r, 0))
    shp = jax.ShapeDtypeStruct((nq * rp, c), F32)
    return _call(body, name, (nq, nblk), [piece_spec(q) for q in range(nq)] + [blk] * 3, [blk] * 4, [shp] * 4,
                 (*pieces, w, m, v), sem=("parallel", "parallel"), jobs=jobs)


_BIG = ("w_in", "w_branch_a", "w_branch_b", "w_out", "w_ffn_up", "w_ffn_down", "w_ple_gate", "w_ple_proj")
_COL_SHARDED = ("w_in", "w_branch_b", "w_ffn_up", "w_ple_proj")
_CONVS = ("conv_qkv", "conv_ffn")
_GATHER_ON_PROJ = ("w_ffn_up",)
_GATHER_ON_DELTA = ("w_ffn_down",)
_GATHER_AHEAD = ("w_in", "conv_qkv")
_GATHER_ON_UP = ("w_ple_gate", "w_ple_proj")
_GATHER_AHEAD_2 = ("w_branch_a", "w_branch_b", "w_out", "conv_ffn")
_SCATTER_ON_DACT = ("w_ple_gate", "w_ple_proj")
_SCATTER_ON_DELTA = ("w_ffn_up",)
_SCATTER_ON_DW_MAIN = ("w_out", "w_branch_a", "w_branch_b")
_SCATTER_ON_DH1 = ("w_ffn_down",)
_SMALL = ("norm_mix", "conv_qkv", "a_log", "dt_bias", "head_norm", "sgu_norm", "w_spatial", "b_spatial", "norm_ffn",
          "conv_ffn", "b_conv_ffn", "norm_ple", "norm_final")
_WEIGHTS = ("norm_mix", "w_in", "conv_qkv", "a_log", "dt_bias", "head_norm", "sgu_norm", "w_spatial", "b_spatial",
            "w_branch_a", "w_branch_b", "w_out", "norm_ffn", "w_ffn_up", "conv_ffn", "b_conv_ffn", "w_ffn_down",
            "norm_ple", "w_ple_gate", "w_ple_proj", "norm_final")


def _full_cols(g):
    return jnp.transpose(g, (1, 0, 2)).reshape(g.shape[1], N_DEV * g.shape[2])


def _full_rows(g):
    return g.reshape(N_DEV * g.shape[1], g.shape[2])


def _split_cols(dw):
    k, n = dw.shape
    return jnp.transpose(dw.reshape(k, N_DEV, n // N_DEV), (1, 0, 2))


def _split_rows(dw):
    k, n = dw.shape
    return dw.reshape(N_DEV, k // N_DEV, n)


def _pad_lanes(v, width=LANES, offset=0):
    return jnp.pad(v, ((0, 0), (offset, width - offset - v.shape[1])))


def kernel(x, p, norm_mix, w_in, conv_qkv, a_log, dt_bias, head_norm, sgu_norm, w_spatial, b_spatial, w_branch_a, w_branch_b, w_out, norm_ffn, w_ffn_up, conv_ffn, b_conv_ffn, w_ffn_down, norm_ple, w_ple_gate, w_ple_proj, norm_final, loss_target, m_norm_mix, m_w_in, m_conv_qkv, m_a_log, m_dt_bias, m_head_norm, m_sgu_norm, m_w_spatial, m_b_spatial, m_w_branch_a, m_w_branch_b, m_w_out, m_norm_ffn, m_w_ffn_up, m_conv_ffn, m_b_conv_ffn, m_w_ffn_down, m_norm_ple, m_w_ple_gate, m_w_ple_proj, m_norm_final, v_norm_mix, v_w_in, v_conv_qkv, v_a_log, v_dt_bias, v_head_norm, v_sgu_norm, v_w_spatial, v_b_spatial, v_w_branch_a, v_w_branch_b, v_w_out, v_norm_ffn, v_w_ffn_up, v_conv_ffn, v_b_conv_ffn, v_w_ffn_down, v_norm_ple, v_w_ple_gate, v_w_ple_proj, v_norm_final):
    env = dict(locals())
    wts = {n: env[n] for n in _WEIGHTS}
    mom_m = {n: env["m_" + n] for n in _WEIGHTS}
    mom_v = {n: env["v_" + n] for n in _WEIGHTS}

    xin = x[0]
    tgt = loss_target[0]
    t, d = xin.shape
    depth = w_in.shape[0]
    hv = a_log.shape[1]
    vw = hv * HEAD
    wb = sgu_norm.shape[1]
    ng = w_spatial.shape[1]
    n_in = w_in.shape[2] * N_DEV
    qk = (n_in - 2 * vw - 2 * hv - 2 * wb - 2 * d) // 2
    hqk = qk // HEAD
    dff = w_ffn_down.shape[1] * N_DEV
    cw = 2 * qk + vw
    o_z, o_ba = 2 * qk + vw, 2 * qk + 2 * vw
    o_ub = o_ba
    o_ga = o_ub + 2 * wb
    me = 4 * lax.axis_index("x") + 2 * lax.axis_index("y") + lax.axis_index("c")

    full = [dict() for _ in range(depth)]
    staged = {}

    def chips(i, names):
        names = names if i < depth else ()
        return [(i, n) for n in names], [(wts[n][i].astype(BF16) if n in _BIG else wts[n][i], "chips") for n in names]

    def relay(keys):
        return list(keys), [(staged.pop(key), "relay") for key in keys]

    def settle(chip_keys, relay_keys, results):
        for key, g in zip(chip_keys, results):
            staged[key] = g
        for (i, n), g in zip(relay_keys, results[len(chip_keys):]):
            full[i][n] = _full_cols(g) if n in _COL_SHARDED or n in _CONVS else _full_rows(g)

    ck, cj = chips(0, _GATHER_AHEAD + _GATHER_AHEAD_2)
    settle(ck, [], _exchange(cj, "gather_first"))
    rk, rj = relay(ck)
    settle([], rk, _exchange(rj, "relay_first"))

    saved = []
    xc = xin
    for i in range(depth):
        fw = full[i]
        w_full = fw["w_in"]
        fw["w_main"] = jnp.concatenate([w_full[:, :o_ba], w_full[:, o_ba + 2 * hv:]], axis=1)
        fw["w_ba"] = _pad_lanes(w_full[:, o_ba:o_ba + 2 * hv])
        s = {"x0": xc}
        s["h1"] = _rms_fwd(xc, norm_mix[i], "rms_fwd")
        ck, cj = chips(i, _GATHER_ON_PROJ)
        s["proj"], got = _matmul(s["h1"], fw["w_main"], "nn", BF16, "mm_proj", jobs=cj)
        settle(ck, [], got)
        s["ba"] = _matmul(s["h1"], fw["w_ba"], "nn", F32, "mm_ba")
        s["qkvn"] = _qkv_fwd(s["proj"], fw["conv_qkv"], hqk, hqk)
        s["alog"] = _pad_lanes(a_log[i][None, :], offset=hv)
        s["dtb"] = _pad_lanes(dt_bias[i][None, :], offset=hv)
        bg = _gate_fwd(s["ba"], s["alog"], s["dtb"], hv)
        beta_t = bg[:, :hv].T
        gam_t = bg[:, hv:2 * hv].T
        s["beta_b"] = jnp.broadcast_to(beta_t[:, :, None], (hv, t, LANES))
        s["gam_b"] = jnp.broadcast_to(gam_t[:, :, None], (hv, t, LANES))
        s["gam_r"] = gam_t.reshape(hv, t // CA, 1, CA)
        s["gam_l"] = jnp.broadcast_to(s["gam_r"][:, :, :, CA - 1:], (hv, t // CA, 1, LANES))
        ck1, cj1 = chips(i, _GATHER_ON_DELTA)
        ck2, cj2 = chips(i + 1, _GATHER_AHEAD)
        rk, rj = relay([(i, n) for n in _GATHER_ON_PROJ])
        (s["o"], s["s_all"], s["tm_all"]), got = _delta_fwd(
            s["qkvn"], s["beta_b"], s["gam_b"], s["gam_r"], s["gam_l"], hqk, hv, jobs=cj1 + cj2 + rj)
        settle(ck1 + ck2, rk, got)
        s["outa"] = _apost_fwd(s["o"], s["proj"], head_norm[i], o_z, hv)
        s["b_t"] = b_spatial[i].T
        s["outb"] = _sgu_fwd(s["proj"], sgu_norm[i], w_spatial[i], s["b_t"], o_ub, wb)
        s["ya"] = _matmul(s["outa"], fw["w_branch_a"], "nn", BF16, "mm_ya")
        s["yb"] = _matmul(s["outb"], fw["w_branch_b"], "nn", BF16, "mm_yb")
        s["mg"] = _merge_fwd(s["ya"], s["yb"], s["proj"], o_ga)
        s["x1"] = _matmul(s["mg"], fw["w_out"], "nn", F32, "mm_out", res=xc)
        s["h2"] = _rms_fwd(s["x1"], norm_ffn[i], "rms_fwd")
        nxt = i + 1 < depth
        ck1, cj1 = chips(i, _GATHER_ON_UP)
        ck2, cj2 = chips(i + 1, _GATHER_AHEAD_2)
        rk, rj = relay([(i, n) for n in _GATHER_ON_DELTA] + ([(i + 1, n) for n in _GATHER_AHEAD] if nxt else []))
        s["up"], got = _matmul(s["h2"], fw["w_ffn_up"], "nn", BF16, "mm_up", jobs=cj1 + cj2 + rj)
        settle(ck1 + ck2, rk, got)
        s["bias"] = b_conv_ffn[i][None, :]
        s["act"] = _ffn_act_fwd(s["up"], fw["conv_ffn"], s["bias"], dff)
        rk, rj = relay([(i, n) for n in _GATHER_ON_UP] + ([(i + 1, n) for n in _GATHER_AHEAD_2] if nxt else []))
        s["x2"], got = _matmul(s["act"], fw["w_ffn_down"], "nn", F32, "mm_down", res=s["x1"], jobs=rj)
        settle([], rk, got)
        s["h3"] = _rms_fwd(s["x2"], norm_ple[i], "rms_fwd")
        s["gt"] = _matmul(s["h3"], fw["w_ple_gate"], "nn", BF16, "mm_gt")
        s["pp"] = _matmul(p[i, 0], fw["w_ple_proj"], "nn", BF16, "mm_pp")
        xc = _ple_fwd(s["x2"], s["gt"], s["pp"])
        saved.append(s)

    dx, g_norm_final, loss_part = _loss_head(xc, tgt, norm_final)

    small = {n: [None] * depth for n in _SMALL if n != "norm_final"}
    recv = {n: [None] * depth for n in _BIG}
    recv["w_in"] = [None] * (2 * depth)

    def scatter_jobs(gw, names):
        return [(_split_cols(gw[n]) if n in _COL_SHARDED else _split_rows(gw[n]), True) for n in names]

    def keep(i, names, results):
        for n, r in zip(names, results):
            recv[n][i] = r

    def carry(jobs, *args):
        return _matmul(*args, jobs=jobs) if jobs else (_matmul(*args), [])

    later = []
    for i in reversed(range(depth)):
        fw, s = full[i], saved[i]
        dgt, dpp = _ple_bwd(dx, s["gt"], s["pp"])
        gw = {"w_ple_gate": _matmul(s["h3"], dgt, "tn", BF16, "mm_dw_gt"),
              "w_ple_proj": _matmul(p[i, 0], dpp, "tn", BF16, "mm_dw_pp")}
        dh3 = _matmul(dgt, fw["w_ple_gate"], "nt", F32, "mm_dh3")
        dx, small["norm_ple"][i] = _rms_bwd(s["x2"], dh3, norm_ple[i], dx, "rms_bwd")

        dact, got = _matmul(dx, fw["w_ffn_down"], "nt", BF16, "mm_dact", jobs=scatter_jobs(gw, _SCATTER_ON_DACT))
        keep(i, _SCATTER_ON_DACT, got)
        gw["w_ffn_down"] = _matmul(s["act"], dx, "tn", BF16, "mm_dw_down")
        (dup, small["conv_ffn"][i], small["b_conv_ffn"][i]), _ = _ffn_act_bwd(s["up"], dact, fw["conv_ffn"], s["bias"], dff)
        gw["w_ffn_up"], got = carry(later[:1], s["h2"], dup, "tn", BF16, "mm_dw_up")
        keep(2 * i + 2, ("w_in",), got)
        dh2, got = carry(later[1:], dup, fw["w_ffn_up"], "nt", F32, "mm_dh2")
        keep(2 * i + 3, ("w_in",), got)
        dx, small["norm_ffn"][i] = _rms_bwd(s["x1"], dh2, norm_ffn[i], dx, "rms_bwd")

        dmg = _matmul(dx, fw["w_out"], "nt", BF16, "mm_dmg")
        gw["w_out"] = _matmul(s["mg"], dx, "tn", BF16, "mm_dw_out")
        dya, dyb, dga, dgb = _merge_bwd(dmg, s["ya"], s["yb"], s["proj"], o_ga)
        gw["w_branch_a"] = _matmul(s["outa"], dya, "tn", BF16, "mm_dw_a")
        gw["w_branch_b"] = _matmul(s["outb"], dyb, "tn", BF16, "mm_dw_b")
        douta = _matmul(dya, fw["w_branch_a"], "nt", BF16, "mm_douta")
        doutb = _matmul(dyb, fw["w_branch_b"], "nt", BF16, "mm_doutb")
        dub, dvb, small["w_spatial"][i], db_s, dsg = _sgu_bwd(s["proj"], sgu_norm[i], w_spatial[i], s["b_t"], doutb, o_ub, wb)
        small["b_spatial"][i] = db_s[:, :, 0]
        small["sgu_norm"][i] = dsg
        do, dz, small["head_norm"][i] = _apost_bwd(s["o"], s["proj"], head_norm[i], douta, o_z, hv)
        (dq, dk, dv, db_b, dg_b), got = _delta_bwd(
            s["qkvn"], s["beta_b"], s["gam_b"], s["gam_r"], s["gam_l"], s["s_all"], s["tm_all"], do, hqk, hv,
            jobs=scatter_jobs(gw, _SCATTER_ON_DELTA))
        keep(i, _SCATTER_ON_DELTA, got)
        dbg = _pad_lanes(jnp.concatenate([db_b[:, :, 0].T, dg_b[:, :, 0].T], axis=1))
        dba, dal, ddt = _gate_bwd(s["ba"], dbg, s["alog"], s["dtb"], hv)
        small["a_log"][i] = dal[:, hv:2 * hv]
        small["dt_bias"][i] = ddt[:, hv:2 * hv]
        dqkv_pre, small["conv_qkv"][i] = _qkv_bwd(s["proj"], jnp.concatenate([dq, dk, dv], axis=1), fw["conv_qkv"], hqk, hqk)
        dproj = jnp.concatenate([dqkv_pre, dz, dub, dvb, dga, dgb], axis=1)
        dw_main, got = _matmul(s["h1"], dproj, "tn", BF16, "mm_dw_main", jobs=scatter_jobs(gw, _SCATTER_ON_DW_MAIN))
        keep(i, _SCATTER_ON_DW_MAIN, got)
        dw_ba = _matmul(s["h1"], dba, "tn", BF16, "mm_dw_ba")
        dw_in = jnp.concatenate([dw_main[:, :o_ba], dw_ba[:, :2 * hv], dw_main[:, o_ba:]], axis=1)
        dh1, got = _matmul(dproj, fw["w_main"], "nt", F32, "mm_dh1", jobs=scatter_jobs(gw, _SCATTER_ON_DH1))
        keep(i, _SCATTER_ON_DH1, got)
        dh1 = _matmul(dba, fw["w_ba"], "nt", F32, "mm_dh1_ba", res=dh1)
        dx, small["norm_mix"][i] = _rms_bwd(s["x0"], dh1, norm_mix[i], dx, "rms_bwd")

        later = [(_split_cols(dw_in[:d // 2]), True), (_split_cols(dw_in[d // 2:]), True)]

    recv["w_in"][0], recv["w_in"][1] = _exchange(later, "scatter_last")

    rep_names = tuple(n for n in _SMALL if n not in _CONVS)
    stacked = {n: jnp.concatenate([jnp.reshape(a, (-1,)) for a in small[n]]) for n in small}
    stacked["norm_final"] = g_norm_final.reshape(-1)

    def padded(parts, mult, axis=0):
        flat = jnp.concatenate(parts, axis=axis)
        pad = -flat.shape[axis] % mult
        return jnp.pad(flat, [(0, 0)] * axis + [(0, pad)])

    rep_flat = padded([stacked[n] for n in rep_names] + [loss_part[0, :1]], 16 * LANES)
    conv_flat = [padded([stacked[n]], 8 * LANES) for n in _CONVS]
    packed = jnp.concatenate([rep_flat] + conv_flat).reshape(-1, LANES)

    outs_g, outs_d, outs_m, outs_v = {}, {}, {}, {}

    for n in _BIG:
        shp = wts[n].shape
        two_d = lambda a: a.reshape(shp[0] * shp[1], shp[2])
        res, got = _adam(recv[n], two_d(wts[n]), two_d(mom_m[n]), two_d(mom_v[n]), "adam_" + n,
                         jobs=[(packed, False)] if n == "w_in" else [])
        if got:
            small_all = got[0].reshape(N_DEV, -1)
        outs_g[n], outs_d[n], outs_m[n], outs_v[n] = [r.reshape(shp) for r in res]

    n_rep = rep_flat.shape[0]
    pk = lambda src: padded([src[n].reshape(-1) for n in rep_names] + [jnp.zeros((1,), F32)], 16 * LANES).reshape(-1, LANES)
    res, _ = _adam([small_all[:, :n_rep].reshape(N_DEV, -1, LANES)], pk(wts), pk(mom_m), pk(mom_v), "adam_small")
    res = [r.reshape(-1) for r in res]
    off = 0
    for n in rep_names:
        shp = wts[n].shape
        size = math.prod(shp)
        outs_g[n], outs_d[n], outs_m[n], outs_v[n] = [r[off:off + size].reshape(shp) for r in res]
        off += size
    loss = res[0][off]

    off = n_rep
    for n, cf in zip(_CONVS, conv_flat):
        _, kw, cl = wts[n].shape
        part = small_all[:, off:off + depth * kw * cl * N_DEV].reshape(N_DEV, depth * kw, N_DEV, cl)
        off += cf.shape[0]
        part = lax.dynamic_index_in_dim(part, me, axis=2, keepdims=False)
        two_d = lambda a: a.reshape(depth * kw, cl)
        res, _ = _adam([part], two_d(wts[n]), two_d(mom_m[n]), two_d(mom_v[n]), "adam_" + n)
        outs_g[n], outs_d[n], outs_m[n], outs_v[n] = [r.reshape(wts[n].shape) for r in res]

    return (loss, dx[None], *[outs_g[n] for n in _WEIGHTS], *[outs_d[n] for n in _WEIGHTS],
            *[outs_m[n] for n in _WEIGHTS], *[outs_v[n] for n in _WEIGHTS])
```

```python
import functools
import math

import jax
import jax.numpy as jnp
from jax import lax
from jax.experimental import pallas as pl
from jax.experimental.pallas import tpu as pltpu

F32 = jnp.float32
BF16 = jnp.bfloat16
EPS = 1e-6
LANES = 128
HEAD = 128
CA = 64
N_DEV = 8
VMEM_LIMIT = 48 * 1024 * 1024
ADAM_VMEM_BUDGET = 16 * 1024 * 1024
MESH = pl.DeviceIdType.MESH

ADAM_LR = 0.001
ADAM_B1 = 0.9
ADAM_B2 = 0.999
ADAM_EPS = 1e-08
ADAM_WD = 0.01
ADAM_STEP = 10


def _tile(n, cap, mult=LANES):
    best = None
    for t in range(mult, min(n, cap) + 1, mult):
        if n % t == 0:
            best = t
    return n if best is None else best


def _tile_near(n, target, mult=LANES):
    cands = [t for t in range(mult, min(n, target * 3 // 2) + 1, mult) if n % t == 0]
    return min(cands, key=lambda t: abs(t - target)) if cands else n


def _params(sem):
    return pltpu.CompilerParams(dimension_semantics=sem, vmem_limit_bytes=VMEM_LIMIT)


def _sigmoid(v):
    return jax.nn.sigmoid(v)


def _silu_grad(c):
    s = _sigmoid(c)
    return s + c * s * (1.0 - s)


_GELU_C = math.sqrt(2.0 / math.pi)


def _gelu(v):
    return 0.5 * v * (1.0 + jnp.tanh(_GELU_C * (v + 0.044715 * v * v * v)))


def _gelu_grad(v):
    t = jnp.tanh(_GELU_C * (v + 0.044715 * v * v * v))
    return 0.5 * (1.0 + t) + 0.5 * v * (1.0 - t * t) * _GELU_C * (1.0 + 3.0 * 0.044715 * v * v)


_NN = (((1,), (0,)), ((), ()))
_NT = (((1,), (1,)), ((), ()))
_TN = (((0,), (0,)), ((), ()))
_BNN = (((2,), (1,)), ((0,), (0,)))
_BNT = (((2,), (2,)), ((0,), (0,)))
_BTN = (((1,), (1,)), ((0,), (0,)))


def _bdot(a, b, dn=_NN):
    return lax.dot_general(a.astype(BF16), b.astype(BF16), dn, preferred_element_type=F32)


def _split(a):
    hi = a.astype(BF16)
    return hi, (a - hi.astype(F32)).astype(BF16)


def _dot3(ah, al, bh, bl, dn=_NN):
    def d(u, v):
        return lax.dot_general(u, v, dn, preferred_element_type=F32)
    return d(ah, bh) + (d(al, bh) + d(ah, bl))


def _hdot(a, b, dn=_NN):
    return _dot3(*_split(a), *_split(b), dn)


_PEERS = {"gather": (1, 2, 3, 4, 5, 6, 7), "scatter": (1, 2, 3, 4, 5, 6, 7), "chips": (1, 2, 4, 6), "relay": (2, 4, 6)}


def _kinds(jobs):
    return [{False: "gather", True: "scatter"}.get(kind, kind) for _, kind in jobs]


def _xchg_out_shapes(jobs):
    shapes = []
    for (a, _), kind in zip(jobs, _kinds(jobs)):
        shape = {"gather": (N_DEV,) + a.shape, "chips": (N_DEV,) + a.shape, "chips_cols": (a.shape[0], N_DEV * a.shape[1]),
                 "scatter_cols": (N_DEV, a.shape[0], a.shape[1] // N_DEV)}.get(kind, a.shape)
        shapes.append(jax.ShapeDtypeStruct(shape, a.dtype))
    return shapes


def _xchg_scratch(jobs):
    n = len(jobs)
    return [pltpu.SemaphoreType.DMA((n, N_DEV - 1)), pltpu.SemaphoreType.DMA((n, N_DEV - 1)), pltpu.SemaphoreType.DMA((n,))]


def _xchg_copies(kinds, src, out, sems):
    send_sems, recv_sems, local_sems = sems
    x, y, c = lax.axis_index("x"), lax.axis_index("y"), lax.axis_index("c")
    me = 4 * x + 2 * y + c

    def block(ref, idx, as_cols):
        if not as_cols:
            return ref.at[idx]
        width = ref.shape[1] // N_DEV
        return ref.at[:, pl.ds(pl.multiple_of(idx * width, LANES), width)]

    local, sends, recvs = [], [], []
    for m in range(N_DEV):
        px = lax.rem(x + ((m >> 2) & 1), 2)
        py = lax.rem(y + ((m >> 1) & 1), 2)
        pc = lax.rem(c + (m & 1), 2)
        peer = 4 * px + 2 * py + pc
        for k, kind in enumerate(kinds):
            base, cols = kind.split("_")[0], kind.endswith("_cols")
            if m == 0:
                if base != "relay":
                    mine = block(src[k], me, cols) if base == "scatter" else src[k]
                    local.append(pltpu.make_async_copy(mine, block(out[k], me, cols and base != "scatter"), local_sems.at[k]))
                continue
            if m not in _PEERS[base]:
                continue
            if base == "relay":
                to, mine = (x, y, 1 - c), block(src[k], peer, cols)
                there, here = block(out[k], peer, cols), block(out[k], 4 * px + 2 * py + 1 - c, cols)
            else:
                to, mine = (px, py, pc), block(src[k], peer, cols) if base == "scatter" else src[k]
                there, here = block(out[k], me, cols and base != "scatter"), block(out[k], peer, cols and base != "scatter")
            for dst, lst in ((there, sends), (here, recvs)):
                lst.append(pltpu.make_async_remote_copy(
                    src_ref=mine, dst_ref=dst, send_sem=send_sems.at[k, m - 1], recv_sem=recv_sems.at[k, m - 1],
                    device_id=to, device_id_type=MESH))
    return local, sends, recvs


def _xchg_start(scatter, src, out, sems):
    local, sends, _ = _xchg_copies(scatter, src, out, sems)
    for cp in local + sends:
        cp.start()


def _xchg_wait(scatter, src, out, sems):
    local, sends, recvs = _xchg_copies(scatter, src, out, sems)
    for cp in recvs:
        cp.wait_recv()
    for cp in sends:
        cp.wait_send()
    for cp in local:
        cp.wait()


_ANY = pl.BlockSpec(memory_space=pl.ANY)


def _xchg_aliases(jobs, n_in, n_out):
    return {n_in + k: n_out + k for k, kind in enumerate(_kinds(jobs)) if kind.startswith("relay")}


def _exchange(jobs, name):
    n = len(jobs)
    kinds = _kinds(jobs)

    def body(*refs):
        src, out, sems = refs[:n], refs[n:2 * n], refs[2 * n:]
        _xchg_start(kinds, src, out, sems)
        _xchg_wait(kinds, src, out, sems)

    return pl.pallas_call(
        body, name=name, in_specs=[_ANY] * n, out_specs=[_ANY] * n, out_shape=_xchg_out_shapes(jobs),
        scratch_shapes=_xchg_scratch(jobs), input_output_aliases=_xchg_aliases(jobs, 0, 0),
        compiler_params=pltpu.CompilerParams(has_side_effects=True))(*[a for a, _ in jobs])


def _carried(body, n_in, n_out, jobs, grid):
    if not jobs:
        return body
    nj = len(jobs)
    scatter = _kinds(jobs)

    def wrapped(*refs):
        ins, src = refs[:n_in], refs[n_in:n_in + nj]
        outs, got = refs[n_in + nj:n_in + nj + n_out], refs[n_in + nj + n_out:n_in + 2 * nj + n_out]
        rest = refs[n_in + 2 * nj + n_out:]
        scratch, sems = rest[:len(rest) - 3], rest[len(rest) - 3:]
        ids = [pl.program_id(a) for a in range(len(grid))]
        first = functools.reduce(jnp.logical_and, [i == 0 for i in ids])
        last = functools.reduce(jnp.logical_and, [i == g - 1 for i, g in zip(ids, grid)])

        @pl.when(first)
        def _():
            _xchg_start(scatter, src, got, sems)

        body(*ins, *outs, *scratch)

        @pl.when(last)
        def _():
            _xchg_wait(scatter, src, got, sems)

    return wrapped


def _call(body, name, grid, in_specs, out_specs, out_shape, args, scratch=(), sem=None, jobs=()):
    jobs = list(jobs)
    nj = len(jobs)
    sem = ("arbitrary",) * len(grid) if jobs or sem is None else sem
    res = pl.pallas_call(
        _carried(body, len(in_specs), len(out_specs), jobs, grid), name=name, grid=grid,
        in_specs=list(in_specs) + [_ANY] * nj, out_specs=list(out_specs) + [_ANY] * nj,
        out_shape=list(out_shape) + _xchg_out_shapes(jobs),
        scratch_shapes=list(scratch) + (_xchg_scratch(jobs) if jobs else []),
        input_output_aliases=_xchg_aliases(jobs, len(in_specs), len(out_specs)),
        compiler_params=_params(sem))(*args, *[a for a, _ in jobs])
    return res[:len(out_specs)], res[len(out_specs):]


MATMUL_OPERAND_VMEM = 20 * 1024 * 1024
MATMUL_TILE = 1024


def _matmul(a, b, mode, out_dtype, name, res=None, jobs=()):
    if mode == "tn":
        kdim, m = a.shape
    else:
        m, kdim = a.shape
    n = b.shape[0] if mode == "nt" else b.shape[1]
    tm, tn = _tile_near(m, MATMUL_TILE), _tile_near(n, MATMUL_TILE)
    per_k = 2 * (tm * a.dtype.itemsize + tn * b.dtype.itemsize)
    tk = _tile(kdim, max(LANES, MATMUL_OPERAND_VMEM // per_k))
    nk = kdim // tk
    dims = {"nn": _NN, "nt": _NT, "tn": _TN}[mode]

    def body(*refs):
        a_ref, b_ref = refs[:2]
        r_ref = refs[2] if res is not None else None
        o_ref = refs[3] if res is not None else refs[2]
        acc = refs[-1] if nk > 1 else None

        def write(r):
            if r_ref is not None:
                r = r + r_ref[...].astype(F32)
            o_ref[...] = r.astype(out_dtype)

        prod = _bdot(a_ref[...], b_ref[...], dims)
        if nk == 1:
            write(prod)
        else:
            k = pl.program_id(2)

            @pl.when(k == 0)
            def _():
                acc[...] = prod

            @pl.when(jnp.logical_and(k > 0, k < nk - 1))
            def _():
                acc[...] += prod

            @pl.when(k == nk - 1)
            def _():
                write(acc[...] + prod)

    a_spec = pl.BlockSpec((tk, tm), lambda i, j, k: (k, i)) if mode == "tn" else pl.BlockSpec((tm, tk), lambda i, j, k: (i, k))
    b_spec = pl.BlockSpec((tn, tk), lambda i, j, k: (j, k)) if mode == "nt" else pl.BlockSpec((tk, tn), lambda i, j, k: (k, j))
    o_spec = pl.BlockSpec((tm, tn), lambda i, j, k: (i, j))
    in_specs = [a_spec, b_spec] + ([o_spec] if res is not None else [])
    args = (a, b) + ((res,) if res is not None else ())
    (out,), got = _call(body, name, (m // tm, n // tn, nk), in_specs, [o_spec], [jax.ShapeDtypeStruct((m, n), out_dtype)],
                        args, scratch=[pltpu.VMEM((tm, tn), F32)] if nk > 1 else [],
                        sem=("parallel", "parallel", "arbitrary"), jobs=jobs)
    return (out, got) if jobs else out


def _rms_fwd(x, gain, name):
    t, d = x.shape
    tb = _tile(t, 256, 8)

    def body(x_ref, g_ref, h_ref):
        xv = x_ref[...]
        r = lax.rsqrt(jnp.mean(xv * xv, axis=-1, keepdims=True) + EPS)
        h_ref[...] = (xv * r * g_ref[...]).astype(BF16)

    return pl.pallas_call(
        body, name=name, grid=(t // tb,),
        in_specs=[pl.BlockSpec((tb, d), lambda i: (i, 0)), pl.BlockSpec((1, d), lambda i: (0, 0))],
        out_specs=pl.BlockSpec((tb, d), lambda i: (i, 0)), out_shape=jax.ShapeDtypeStruct((t, d), BF16),
        compiler_params=_params(("parallel",)))(x, gain.reshape(1, d))


def _rms_bwd(x, dh, gain, dres, name):
    t, d = x.shape
    tb = _tile(t, 256, 8)

    def body(x_ref, dh_ref, g_ref, dr_ref, dx_ref, dg_ref):
        @pl.when(pl.program_id(0) == 0)
        def _():
            dg_ref[...] = jnp.zeros_like(dg_ref)

        xv = x_ref[...]
        dy = dh_ref[...].astype(F32)
        r = lax.rsqrt(jnp.mean(xv * xv, axis=-1, keepdims=True) + EPS)
        xh = xv * r
        dxh = dy * g_ref[...]
        dx_ref[...] = dr_ref[...] + r * (dxh - xh * jnp.mean(dxh * xh, axis=-1, keepdims=True))
        dg_ref[...] += jnp.sum(dy * xh, axis=0, keepdims=True)

    row = pl.BlockSpec((tb, d), lambda i: (i, 0))
    vec = pl.BlockSpec((1, d), lambda i: (0, 0))
    return pl.pallas_call(
        body, name=name, grid=(t // tb,), in_specs=[row, row, vec, row], out_specs=[row, vec],
        out_shape=[jax.ShapeDtypeStruct((t, d), F32), jax.ShapeDtypeStruct((1, d), F32)],
        compiler_params=_params(("arbitrary",)))(x, dh, gain.reshape(1, d), dres)


def _loss_head(x, target, gain):
    t, d = x.shape
    tb = _tile(t, 256, 8)

    def body(x_ref, t_ref, g_ref, dx_ref, dg_ref, loss_ref):
        @pl.when(pl.program_id(0) == 0)
        def _():
            dg_ref[...] = jnp.zeros_like(dg_ref)
            loss_ref[...] = jnp.zeros_like(loss_ref)

        xv = x_ref[...]
        r = lax.rsqrt(jnp.mean(xv * xv, axis=-1, keepdims=True) + EPS)
        xh = xv * r
        err = xh * g_ref[...] - t_ref[...]
        per_row = jnp.mean(err * err, axis=-1, keepdims=True)
        loss_ref[...] += 0.5 * jnp.sum(per_row, axis=0, keepdims=True)
        dy = err * (1.0 / d)
        dxh = dy * g_ref[...]
        dx_ref[...] = r * (dxh - xh * jnp.mean(dxh * xh, axis=-1, keepdims=True))
        dg_ref[...] += jnp.sum(dy * xh, axis=0, keepdims=True)

    row = pl.BlockSpec((tb, d), lambda i: (i, 0))
    vec = pl.BlockSpec((1, d), lambda i: (0, 0))
    return pl.pallas_call(
        body, name="loss_head", grid=(t // tb,), in_specs=[row, row, vec],
        out_specs=[row, vec, pl.BlockSpec((1, LANES), lambda i: (0, 0))],
        out_shape=[jax.ShapeDtypeStruct((t, d), F32), jax.ShapeDtypeStruct((1, d), F32),
                   jax.ShapeDtypeStruct((1, LANES), F32)],
        compiler_params=_params(("arbitrary",)))(x, target, gain.reshape(1, d))


def _shift_down(v, s, rows):
    if s == 0:
        return v
    return jnp.where(rows >= s, pltpu.roll(v, s, 0), 0.0)


def _shift_up(v, s, rows):
    if s == 0:
        return v
    t = v.shape[0]
    return jnp.where(rows < t - s, pltpu.roll(v, t - s, 0), 0.0)


def _conv(v, w, rows):
    k = w.shape[0]
    out = v * w[k - 1:k, :]
    for s in range(1, k):
        out = out + _shift_down(v, s, rows) * w[k - 1 - s:k - s, :]
    return out


def _qkv_fwd(proj, conv_w, nq, nk):
    t = proj.shape[0]
    cw = conv_w.shape[1]
    nblk = cw // HEAD

    def body(p_ref, w_ref, o_ref):
        j = pl.program_id(0)
        rows = lax.broadcasted_iota(jnp.int32, (t, HEAD), 0)
        c = _conv(p_ref[...].astype(F32), w_ref[...], rows)
        a = c * _sigmoid(c)
        nrm = a * lax.rsqrt(jnp.sum(a * a, axis=-1, keepdims=True) + EPS)
        nrm = nrm * jnp.where(j < nq, HEAD ** -0.5, 1.0)
        o_ref[...] = jnp.where(j < nq + nk, nrm, a).astype(BF16)

    return pl.pallas_call(
        body, name="qkv_fwd", grid=(nblk,),
        in_specs=[pl.BlockSpec((t, HEAD), lambda j: (0, j)), pl.BlockSpec((conv_w.shape[0], HEAD), lambda j: (0, j))],
        out_specs=pl.BlockSpec((t, HEAD), lambda j: (0, j)), out_shape=jax.ShapeDtypeStruct((t, cw), BF16),
        compiler_params=_params(("parallel",)))(proj, conv_w)


def _qkv_bwd(proj, dqkv, conv_w, nq, nk):
    t = proj.shape[0]
    kw, cw = conv_w.shape
    nblk = cw // HEAD

    def body(p_ref, d_ref, w_ref, dp_ref, dw_ref):
        j = pl.program_id(0)
        rows = lax.broadcasted_iota(jnp.int32, (t, HEAD), 0)
        xv = p_ref[...].astype(F32)
        w = w_ref[...]
        c = _conv(xv, w, rows)
        a = c * _sigmoid(c)
        dy = d_ref[...].astype(F32)
        r = lax.rsqrt(jnp.sum(a * a, axis=-1, keepdims=True) + EPS)
        y = a * r
        scale = jnp.where(j < nq, HEAD ** -0.5, 1.0)
        da_n = scale * r * (dy - y * jnp.sum(dy * y, axis=-1, keepdims=True))
        da = jnp.where(j < nq + nk, da_n, dy)
        dc = da * _silu_grad(c)
        dx = dc * w[kw - 1:kw, :]
        dw_ref[kw - 1:kw, :] = jnp.sum(dc * xv, axis=0, keepdims=True)
        for s in range(1, kw):
            dx = dx + _shift_up(dc, s, rows) * w[kw - 1 - s:kw - s, :]
            dw_ref[kw - 1 - s:kw - s, :] = jnp.sum(dc * _shift_down(xv, s, rows), axis=0, keepdims=True)
        dp_ref[...] = dx.astype(BF16)

    blk = pl.BlockSpec((t, HEAD), lambda j: (0, j))
    wblk = pl.BlockSpec((kw, HEAD), lambda j: (0, j))
    return pl.pallas_call(
        body, name="qkv_bwd", grid=(nblk,), in_specs=[blk, blk, wblk], out_specs=[blk, wblk],
        out_shape=[jax.ShapeDtypeStruct((t, cw), BF16), jax.ShapeDtypeStruct((kw, cw), F32)],
        compiler_params=_params(("parallel",)))(proj, dqkv, conv_w)


def _softplus(v):
    return jnp.where(v < -15.0, jnp.exp(v), jnp.maximum(v, 0.0) + jnp.log(1.0 + jnp.exp(-jnp.abs(v))))


def _gate_fwd(ba, alog_pad, dtb_pad, hv):
    t = ba.shape[0]
    tb = _tile(t, 512, CA)

    def body(ba_ref, al_ref, dt_ref, o_ref):
        v = ba_ref[...]
        beta = _sigmoid(v)
        g = -jnp.exp(al_ref[...]) * _softplus(v + dt_ref[...])
        pos = lax.broadcasted_iota(jnp.int32, (tb, LANES), 0) % CA
        s = 1
        while s < CA:
            g = g + jnp.where(pos >= s, pltpu.roll(g, s, 0), 0.0)
            s *= 2
        lane = lax.broadcasted_iota(jnp.int32, (tb, LANES), 1)
        o_ref[...] = jnp.where(lane < hv, beta, g)

    row = pl.BlockSpec((tb, LANES), lambda i: (i, 0))
    vec = pl.BlockSpec((1, LANES), lambda i: (0, 0))
    return pl.pallas_call(
        body, name="gate_fwd", grid=(t // tb,), in_specs=[row, vec, vec], out_specs=row,
        out_shape=jax.ShapeDtypeStruct((t, LANES), F32), compiler_params=_params(("parallel",)))(ba, alog_pad, dtb_pad)


def _gate_bwd(ba, dbg, alog_pad, dtb_pad, hv):
    t = ba.shape[0]
    tb = _tile(t, 512, CA)

    def body(ba_ref, d_ref, al_ref, dt_ref, dba_ref, dal_ref, ddt_ref):
        @pl.when(pl.program_id(0) == 0)
        def _():
            dal_ref[...] = jnp.zeros_like(dal_ref)
            ddt_ref[...] = jnp.zeros_like(ddt_ref)

        v = ba_ref[...]
        d = d_ref[...]
        pos = lax.broadcasted_iota(jnp.int32, (tb, LANES), 0) % CA
        dg = d
        s = 1
        while s < CA:
            dg = dg + jnp.where(pos < CA - s, pltpu.roll(dg, tb - s, 0), 0.0)
            s *= 2
        beta = _sigmoid(v)
        na = -jnp.exp(al_ref[...])
        z = v + dt_ref[...]
        da = dg * na * _sigmoid(z)
        lane = lax.broadcasted_iota(jnp.int32, (tb, LANES), 1)
        in_a = jnp.logical_and(lane >= hv, lane < 2 * hv)
        da = jnp.where(in_a, da, 0.0)
        dba_ref[...] = jnp.where(lane < hv, d * beta * (1.0 - beta), da)
        ddt_ref[...] += jnp.sum(da, axis=0, keepdims=True)
        dal_ref[...] += jnp.sum(jnp.where(in_a, dg * na * _softplus(z), 0.0), axis=0, keepdims=True)

    row = pl.BlockSpec((tb, LANES), lambda i: (i, 0))
    vec = pl.BlockSpec((1, LANES), lambda i: (0, 0))
    return pl.pallas_call(
        body, name="gate_bwd", grid=(t // tb,), in_specs=[row, row, vec, vec], out_specs=[row, vec, vec],
        out_shape=[jax.ShapeDtypeStruct((t, LANES), F32), jax.ShapeDtypeStruct((1, LANES), F32),
                   jax.ShapeDtypeStruct((1, LANES), F32)],
        compiler_params=_params(("arbitrary",)))(ba, dbg, alog_pad, dtb_pad)


def _chunk_masks():
    r = lax.broadcasted_iota(jnp.int32, (CA, CA), 0)
    c = lax.broadcasted_iota(jnp.int32, (CA, CA), 1)
    return r >= c, r > c, (r == c).astype(F32)


def _inv_unit_lower(a, eye):
    x = eye - a
    ph, plo = _split(a)
    n = 1
    while n < CA // 2:
        ph, plo = _split(_dot3(ph, plo, ph, plo, _BNN))
        x = x + _dot3(*_split(x), ph, plo, _BNN)
        n *= 2
    return x


def _delta_pre(q, k, v, bcol, gc, gr, gl, causal, strict):
    eg = jnp.exp(gc)
    dm = jnp.exp(jnp.where(causal, gc[:, :, :CA] - gr, -jnp.inf))
    kb = k * bcol
    kkb = _bdot(kb, k, _BNT)
    a = jnp.where(strict, kkb * dm, 0.0)
    rhs = jnp.concatenate([v * bcol, kb * eg], axis=2)
    qk = _bdot(q, k, _BNT)
    ekd = jnp.exp(gl - gc)
    return dict(eg=eg, dm=dm, kb=kb, kkb=kkb, a=a, rhs=rhs, p=qk * dm, qd=q * eg, ekd=ekd, kd=k * ekd, cd=jnp.exp(gl))


def _delta_fwd(qkvn, beta_b, gam_b, gam_r, gam_l, hqk, hv, jobs=()):
    t = qkvn.shape[0]
    rep = hv // hqk
    rb = _tile(t, 512, CA)
    nb = t // rb
    ncb = rb // CA
    nc = t // CA

    def body(q_ref, k_ref, v_ref, b_ref, gc_ref, gr_ref, gl_ref, o_ref, s_ref, tm_ref, state, sol_sc, p_sc):
        @pl.when(pl.program_id(1) == 0)
        def _():
            state[...] = jnp.zeros_like(state)

        causal, strict, eye = _chunk_masks()

        def chunks(a):
            return a.astype(F32).reshape(ncb, CA, a.shape[-1])

        q = chunks(q_ref[...])
        k = chunks(k_ref[...])
        for h in range(rep):
            pre = _delta_pre(q, k, chunks(v_ref[:, h * HEAD:(h + 1) * HEAD]), chunks(b_ref[h]), chunks(gc_ref[h]),
                             gr_ref[h], gl_ref[h], causal, strict)
            tm = _inv_unit_lower(pre["a"], eye)
            tm_ref[h] = tm
            sol_sc[h] = _hdot(tm, pre["rhs"], _BNN)
            p_sc[h] = pre["p"]

        def chunk(n, carry):
            rows = pl.ds(pl.multiple_of(n * CA, CA), CA)
            qn = q_ref[rows, :].astype(F32)
            kn = k_ref[rows, :].astype(F32)
            for h in range(rep):
                gc = gc_ref[h, rows, :]
                gl = gl_ref[h, n]
                s = state[h]
                v_new = sol_sc[h, n, :, :HEAD] - _bdot(sol_sc[h, n, :, HEAD:], s)
                o_ref[rows, h * HEAD:(h + 1) * HEAD] = _bdot(qn * jnp.exp(gc), s) + _bdot(p_sc[h, n], v_new)
                s_ref[h, n] = s.astype(BF16)
                state[h] = s * jnp.exp(gl) + _bdot(kn * jnp.exp(gl - gc), v_new, _TN)
            return carry

        lax.fori_loop(0, ncb, chunk, 0)

    qoff, koff, voff = 0, hqk, 2 * hqk // rep
    per_chunk = lambda width: pl.BlockSpec((rep, ncb, 1, width), lambda j, i: (j, i, 0, 0))
    return _call(
        body, "delta_fwd", (hqk, nb),
        [pl.BlockSpec((rb, HEAD), lambda j, i: (i, qoff + j)),
         pl.BlockSpec((rb, HEAD), lambda j, i: (i, koff + j)),
         pl.BlockSpec((rb, rep * HEAD), lambda j, i: (i, voff + j)),
         pl.BlockSpec((rep, rb, LANES), lambda j, i: (j, i, 0)),
         pl.BlockSpec((rep, rb, LANES), lambda j, i: (j, i, 0)),
         per_chunk(CA), per_chunk(LANES)],
        [pl.BlockSpec((rb, rep * HEAD), lambda j, i: (i, j)),
         pl.BlockSpec((rep, ncb, HEAD, HEAD), lambda j, i: (j, i, 0, 0)),
         pl.BlockSpec((rep, ncb, CA, CA), lambda j, i: (j, i, 0, 0))],
        [jax.ShapeDtypeStruct((t, hv * HEAD), F32), jax.ShapeDtypeStruct((hv, nc, HEAD, HEAD), BF16),
         jax.ShapeDtypeStruct((hv, nc, CA, CA), F32)],
        (qkvn, qkvn, qkvn, beta_b, gam_b, gam_r, gam_l),
        scratch=[pltpu.VMEM((rep, HEAD, HEAD), F32), pltpu.VMEM((rep, ncb, CA, 2 * HEAD), F32),
                 pltpu.VMEM((rep, ncb, CA, CA), F32)],
        sem=("parallel", "arbitrary"), jobs=jobs)


def _delta_bwd(qkvn, beta_b, gam_b, gam_r, gam_l, s_all, tm_all, do, hqk, hv, jobs=()):
    t = qkvn.shape[0]
    rep = hv // hqk
    rb = _tile(t, 512, CA)
    nb = t // rb
    ncb = rb // CA

    def body(q_ref, k_ref, v_ref, b_ref, gc_ref, gr_ref, gl_ref, s_ref, tm_ref, do_ref,
             dq_ref, dk_ref, dv_ref, db_ref, dg_ref, dstate, sol_sc, vn_sc, p_sc, kkb_sc, dvn_sc, ds_sc):
        @pl.when(pl.program_id(1) == 0)
        def _():
            dstate[...] = jnp.zeros_like(dstate)

        causal, strict, _ = _chunk_masks()
        ones = jnp.ones((ncb, CA, LANES), BF16)
        last = lax.broadcasted_iota(jnp.int32, (CA, LANES), 0) == CA - 1

        def chunks(a):
            return a.astype(F32).reshape(ncb, CA, a.shape[-1])

        def rows_of(a):
            return a.reshape(rb, a.shape[-1])

        def rowsum(m):
            return jnp.sum(m, axis=2, keepdims=True)

        def colsum(m):
            hi, lo = _split(m)
            return _bdot(hi, ones, _BTN) + _bdot(lo, ones, _BTN)

        q = chunks(q_ref[...])
        k = chunks(k_ref[...])

        def head_inputs(h):
            v = chunks(v_ref[:, h * HEAD:(h + 1) * HEAD])
            bcol = chunks(b_ref[h])
            return v, bcol, _delta_pre(q, k, v, bcol, chunks(gc_ref[h]), gr_ref[h], gl_ref[h], causal, strict)

        for h in range(rep):
            _, _, pre = head_inputs(h)
            sol = _hdot(tm_ref[h], pre["rhs"], _BNN)
            sol_sc[h] = sol
            vn_sc[h] = sol[:, :, :HEAD] - _bdot(sol[:, :, HEAD:], s_ref[h], _BNN)
            p_sc[h] = pre["p"]
            kkb_sc[h] = pre["kkb"]

        def state_step(it, carry):
            n = ncb - 1 - it
            rows = pl.ds(pl.multiple_of(n * CA, CA), CA)
            qn = q_ref[rows, :].astype(F32)
            kn = k_ref[rows, :].astype(F32)
            for h in range(rep):
                gc = gc_ref[h, rows, :]
                gl = gl_ref[h, n]
                ds = dstate[h]
                ds_sc[h, n] = ds
                dov = do_ref[rows, h * HEAD:(h + 1) * HEAD].astype(F32)
                dvn = _bdot(p_sc[h, n], dov, _TN) + _bdot(kn * jnp.exp(gl - gc), ds)
                dvn_sc[h, n] = dvn
                dstate[h] = (ds * jnp.exp(gl) + _bdot(qn * jnp.exp(gc), dov, _TN)
                             - _bdot(sol_sc[h, n, :, HEAD:], dvn, _TN))
            return carry

        lax.fori_loop(0, ncb, state_step, 0)

        kkr = _bdot(k, k, _BNT)
        dq = jnp.zeros((ncb, CA, HEAD), F32)
        dk = jnp.zeros((ncb, CA, HEAD), F32)
        for h in range(rep):
            v, bcol, pre = head_inputs(h)
            eg, dm, kb, qd, kd, cd = pre["eg"], pre["dm"], pre["kb"], pre["qd"], pre["kd"], pre["cd"]
            p = p_sc[h]
            sol = sol_sc[h]
            s = s_ref[h].astype(F32)
            ds = ds_sc[h]
            dov = chunks(do_ref[:, h * HEAD:(h + 1) * HEAD])
            v_new = vn_sc[h]
            dvn = dvn_sc[h]

            dp = jnp.where(causal, _bdot(dov, v_new, _BNT), 0.0)
            dqd = _bdot(dov, s, _BNT)
            dkd = _bdot(v_new, ds, _BNT)
            dcd = jnp.sum(rowsum(s * ds), axis=1, keepdims=True)
            dw = -_bdot(dvn, s, _BNT)

            drhs = _hdot(tm_ref[h], jnp.concatenate([dvn, dw], axis=2), _BTN)
            dbv, dbke = drhs[:, :, :HEAD], drhs[:, :, HEAD:]
            da = -jnp.where(strict, _bdot(drhs, sol, _BNT), 0.0)
            m = da * dm
            e = m * kkb_sc[h] + dp * p
            dgam = rowsum(e) - colsum(e) + rowsum(dbke * kb * eg) + rowsum(dqd * qd)
            r = rowsum(dkd * kd)
            tot = jnp.sum(r, axis=1, keepdims=True) + dcd * cd
            dgam = dgam - r + jnp.where(last, tot, 0.0)
            dbeta = rowsum(m * kkr) + rowsum(dbv * v) + rowsum(dbke * eg * k)
            nm = m * bcol[:, :, :CA]
            dqk = dp * dm
            dq = dq + _bdot(dqk, k, _BNN) + eg * dqd
            dk = (dk + _bdot(nm, k, _BNN) + _bdot(nm, k, _BTN) + _bdot(dqk, q, _BTN) + bcol * eg * dbke
                  + pre["ekd"] * dkd)
            dv_ref[:, h * HEAD:(h + 1) * HEAD] = rows_of(bcol * dbv)
            db_ref[h] = rows_of(jnp.broadcast_to(dbeta, (ncb, CA, LANES)))
            dg_ref[h] = rows_of(jnp.broadcast_to(dgam, (ncb, CA, LANES)))
        dq_ref[...] = rows_of(dq)
        dk_ref[...] = rows_of(dk)

    qoff, koff, voff = 0, hqk, 2 * hqk // rep
    rv = lambda i: nb - 1 - i
    hd = pl.BlockSpec((rep, rb, LANES), lambda j, i: (j, rv(i), 0))
    qk_out = pl.BlockSpec((rb, HEAD), lambda j, i: (rv(i), j))
    v_blk = pl.BlockSpec((rb, rep * HEAD), lambda j, i: (rv(i), j))
    per_chunk = lambda *shape: pl.BlockSpec((rep, ncb) + shape, lambda j, i: (j, rv(i), 0, 0))
    return _call(
        body, "delta_bwd", (hqk, nb),
        [pl.BlockSpec((rb, HEAD), lambda j, i: (rv(i), qoff + j)),
         pl.BlockSpec((rb, HEAD), lambda j, i: (rv(i), koff + j)),
         pl.BlockSpec((rb, rep * HEAD), lambda j, i: (rv(i), voff + j)),
         hd, hd, per_chunk(1, CA), per_chunk(1, LANES), per_chunk(HEAD, HEAD), per_chunk(CA, CA), v_blk],
        [qk_out, qk_out, v_blk, hd, hd],
        [jax.ShapeDtypeStruct((t, hqk * HEAD), F32), jax.ShapeDtypeStruct((t, hqk * HEAD), F32),
         jax.ShapeDtypeStruct((t, hv * HEAD), F32),
         jax.ShapeDtypeStruct((hv, t, LANES), F32), jax.ShapeDtypeStruct((hv, t, LANES), F32)],
        (qkvn, qkvn, qkvn, beta_b, gam_b, gam_r, gam_l, s_all, tm_all, do),
        scratch=[pltpu.VMEM((rep, HEAD, HEAD), F32), pltpu.VMEM((rep, ncb, CA, 2 * HEAD), F32),
                 pltpu.VMEM((rep, ncb, CA, HEAD), F32), pltpu.VMEM((rep, ncb, CA, CA), F32),
                 pltpu.VMEM((rep, ncb, CA, CA), F32), pltpu.VMEM((rep, ncb, CA, HEAD), F32),
                 pltpu.VMEM((rep, ncb, HEAD, HEAD), F32)],
        sem=("parallel", "arbitrary"), jobs=jobs)


def _apost_fwd(o, proj, gain, zoff, hv):
    t = o.shape[0]
    tb = _tile(t, 1024, 8)
    zb = zoff // HEAD

    def body(o_ref, z_ref, g_ref, y_ref):
        ov = o_ref[...]
        z = z_ref[...].astype(F32)
        r = lax.rsqrt(jnp.mean(ov * ov, axis=-1, keepdims=True) + EPS)
        y_ref[...] = (ov * r * g_ref[...] * (z * _sigmoid(z))).astype(BF16)

    blk = pl.BlockSpec((tb, HEAD), lambda i, h: (i, h))
    return pl.pallas_call(
        body, name="apost_fwd", grid=(t // tb, hv),
        in_specs=[blk, pl.BlockSpec((tb, HEAD), lambda i, h: (i, zb + h)), pl.BlockSpec((1, HEAD), lambda i, h: (0, 0))],
        out_specs=blk, out_shape=jax.ShapeDtypeStruct((t, hv * HEAD), BF16),
        compiler_params=_params(("parallel", "parallel")))(o, proj, gain.reshape(1, HEAD))


def _apost_bwd(o, proj, gain, dy, zoff, hv):
    t = o.shape[0]
    tb = _tile(t, 1024, 8)
    zb = zoff // HEAD

    def body(o_ref, z_ref, g_ref, dy_ref, do_ref, dz_ref, dg_ref):
        @pl.when(jnp.logical_and(pl.program_id(0) == 0, pl.program_id(1) == 0))
        def _():
            dg_ref[...] = jnp.zeros_like(dg_ref)

        ov = o_ref[...]
        z = z_ref[...].astype(F32)
        d = dy_ref[...].astype(F32)
        r = lax.rsqrt(jnp.mean(ov * ov, axis=-1, keepdims=True) + EPS)
        oh = ov * r
        sz = z * _sigmoid(z)
        dn = d * sz
        dz_ref[...] = (d * oh * g_ref[...] * _silu_grad(z)).astype(BF16)
        doh = dn * g_ref[...]
        do_ref[...] = r * (doh - oh * jnp.mean(doh * oh, axis=-1, keepdims=True))
        dg_ref[...] += jnp.sum(dn * oh, axis=0, keepdims=True)

    blk = pl.BlockSpec((tb, HEAD), lambda i, h: (i, h))
    vec = pl.BlockSpec((1, HEAD), lambda i, h: (0, 0))
    return pl.pallas_call(
        body, name="apost_bwd", grid=(t // tb, hv),
        in_specs=[blk, pl.BlockSpec((tb, HEAD), lambda i, h: (i, zb + h)), vec, blk],
        out_specs=[blk, blk, vec],
        out_shape=[jax.ShapeDtypeStruct((t, hv * HEAD), F32), jax.ShapeDtypeStruct((t, hv * HEAD), BF16),
                   jax.ShapeDtypeStruct((1, HEAD), F32)],
        compiler_params=_params(("arbitrary", "arbitrary")))(o, proj, gain.reshape(1, HEAD), dy)


def _sgu_fwd(proj, gain, w_s, b_t, uoff, wb):
    t = proj.shape[0]
    ng = wb // HEAD

    def body(u_ref, v_ref, g_ref, w_ref, b_ref, o_ref):
        r_i = lax.broadcasted_iota(jnp.int32, (HEAD, HEAD), 0)
        c_i = lax.broadcasted_iota(jnp.int32, (HEAD, HEAD), 1)
        u = _gelu(u_ref[...].astype(F32))
        vg = _gelu(v_ref[...].astype(F32))
        vn = vg * lax.rsqrt(jnp.mean(vg * vg, axis=-1, keepdims=True) + EPS) * g_ref[...]
        for g in range(ng):
            cols = slice(g * HEAD, (g + 1) * HEAD)
            wg = jnp.where(r_i >= c_i, w_ref[g], 0.0)
            mixed = _bdot(wg, vn[:, cols]) + b_ref[:, g:g + 1]
            o_ref[:, cols] = (u[:, cols] * mixed).astype(BF16)

    ub, vb = uoff // wb, uoff // wb + 1
    return pl.pallas_call(
        body, name="sgu_fwd", grid=(t // HEAD,),
        in_specs=[pl.BlockSpec((HEAD, wb), lambda i: (i, ub)), pl.BlockSpec((HEAD, wb), lambda i: (i, vb)),
                  pl.BlockSpec((1, wb), lambda i: (0, 0)), pl.BlockSpec((ng, HEAD, HEAD), lambda i: (0, 0, 0)),
                  pl.BlockSpec((HEAD, ng), lambda i: (0, 0))],
        out_specs=pl.BlockSpec((HEAD, wb), lambda i: (i, 0)), out_shape=jax.ShapeDtypeStruct((t, wb), BF16),
        compiler_params=_params(("parallel",)))(proj, proj, gain.reshape(1, wb), w_s, b_t)


def _sgu_bwd(proj, gain, w_s, b_t, dout, uoff, wb):
    t = proj.shape[0]
    ng = wb // HEAD

    def body(u_ref, v_ref, g_ref, w_ref, b_ref, d_ref, du_ref, dv_ref, dw_ref, db_ref, dg_ref, dvn_ref):
        @pl.when(pl.program_id(0) == 0)
        def _():
            dw_ref[...] = jnp.zeros_like(dw_ref)
            db_ref[...] = jnp.zeros_like(db_ref)
            dg_ref[...] = jnp.zeros_like(dg_ref)

        r_i = lax.broadcasted_iota(jnp.int32, (HEAD, HEAD), 0)
        c_i = lax.broadcasted_iota(jnp.int32, (HEAD, HEAD), 1)
        tril = r_i >= c_i
        ub = u_ref[...].astype(F32)
        vb = v_ref[...].astype(F32)
        u = _gelu(ub)
        vg = _gelu(vb)
        r = lax.rsqrt(jnp.mean(vg * vg, axis=-1, keepdims=True) + EPS)
        vh = vg * r
        vn = vh * g_ref[...]
        d = d_ref[...].astype(F32)
        for g in range(ng):
            cols = slice(g * HEAD, (g + 1) * HEAD)
            wg = jnp.where(tril, w_ref[g], 0.0)
            mixed = _bdot(wg, vn[:, cols]) + b_ref[:, g:g + 1]
            du_ref[:, cols] = (d[:, cols] * mixed * _gelu_grad(ub[:, cols])).astype(BF16)
            dmix = d[:, cols] * u[:, cols]
            dw_ref[g] += jnp.where(tril, _bdot(dmix, vn[:, cols], _NT), 0.0)
            db_ref[g] += jnp.broadcast_to(jnp.sum(dmix, axis=1, keepdims=True), (HEAD, HEAD))
            dvn_ref[:, cols] = _bdot(wg, dmix, _TN)
        dvn = dvn_ref[...]
        dg_ref[...] += jnp.sum(dvn * vh, axis=0, keepdims=True)
        dvh = dvn * g_ref[...]
        dvg = r * (dvh - vh * jnp.mean(dvh * vh, axis=-1, keepdims=True))
        dv_ref[...] = (dvg * _gelu_grad(vb)).astype(BF16)

    ub_i, vb_i = uoff // wb, uoff // wb + 1
    row = pl.BlockSpec((HEAD, wb), lambda i: (i, 0))
    mat = pl.BlockSpec((ng, HEAD, HEAD), lambda i: (0, 0, 0))
    vec = pl.BlockSpec((1, wb), lambda i: (0, 0))
    return pl.pallas_call(
        body, name="sgu_bwd", grid=(t // HEAD,),
        in_specs=[pl.BlockSpec((HEAD, wb), lambda i: (i, ub_i)), pl.BlockSpec((HEAD, wb), lambda i: (i, vb_i)),
                  vec, mat, pl.BlockSpec((HEAD, ng), lambda i: (0, 0)), row],
        out_specs=[row, row, mat, mat, vec],
        out_shape=[jax.ShapeDtypeStruct((t, wb), BF16), jax.ShapeDtypeStruct((t, wb), BF16),
                   jax.ShapeDtypeStruct((ng, HEAD, HEAD), F32), jax.ShapeDtypeStruct((ng, HEAD, HEAD), F32),
                   jax.ShapeDtypeStruct((1, wb), F32)],
        scratch_shapes=[pltpu.VMEM((HEAD, wb), F32)],
        compiler_params=_params(("arbitrary",)))(proj, proj, gain.reshape(1, wb), w_s, b_t, dout)


def _merge_specs(t, d, goff):
    tb = _tile(t, 512, 8)
    tc = _tile(d, 512)
    gb = goff // tc
    nd = d // tc
    blk = pl.BlockSpec((tb, tc), lambda i, j: (i, j))
    ga = pl.BlockSpec((tb, tc), lambda i, j: (i, gb + j))
    gbs = pl.BlockSpec((tb, tc), lambda i, j: (i, gb + nd + j))
    return (t // tb, nd), blk, ga, gbs


def _merge_fwd(ya, yb, proj, goff):
    t, d = ya.shape
    grid, blk, ga, gbs = _merge_specs(t, d, goff)

    def body(ya_ref, yb_ref, ga_ref, gb_ref, o_ref):
        o_ref[...] = (_sigmoid(ga_ref[...].astype(F32)) * ya_ref[...].astype(F32)
                      + _sigmoid(gb_ref[...].astype(F32)) * yb_ref[...].astype(F32)).astype(BF16)

    return pl.pallas_call(
        body, name="merge_fwd", grid=grid, in_specs=[blk, blk, ga, gbs], out_specs=blk,
        out_shape=jax.ShapeDtypeStruct((t, d), BF16),
        compiler_params=_params(("parallel", "parallel")))(ya, yb, proj, proj)


def _merge_bwd(dm, ya, yb, proj, goff):
    t, d = ya.shape
    grid, blk, ga, gbs = _merge_specs(t, d, goff)

    def body(dm_ref, ya_ref, yb_ref, ga_ref, gb_ref, dya_ref, dyb_ref, dga_ref, dgb_ref):
        dmv = dm_ref[...].astype(F32)
        sa = _sigmoid(ga_ref[...].astype(F32))
        sb = _sigmoid(gb_ref[...].astype(F32))
        dya_ref[...] = (dmv * sa).astype(BF16)
        dyb_ref[...] = (dmv * sb).astype(BF16)
        dga_ref[...] = (dmv * ya_ref[...].astype(F32) * sa * (1.0 - sa)).astype(BF16)
        dgb_ref[...] = (dmv * yb_ref[...].astype(F32) * sb * (1.0 - sb)).astype(BF16)

    shp = jax.ShapeDtypeStruct((t, d), BF16)
    return pl.pallas_call(
        body, name="merge_bwd", grid=grid, in_specs=[blk, blk, blk, ga, gbs], out_specs=[blk] * 4,
        out_shape=[shp] * 4, compiler_params=_params(("parallel", "parallel")))(dm, ya, yb, proj, proj)


def _ffn_act_fwd(up, conv_w, bias, dff):
    t = up.shape[0]
    nblk = dff // HEAD
    kw = conv_w.shape[0]

    def body(g_ref, v_ref, wg_ref, wv_ref, bg_ref, bv_ref, o_ref):
        rows = lax.broadcasted_iota(jnp.int32, (t, HEAD), 0)
        cg = _conv(g_ref[...].astype(F32), wg_ref[...], rows) + bg_ref[...]
        cv = _conv(v_ref[...].astype(F32), wv_ref[...], rows) + bv_ref[...]
        o_ref[...] = (cg * _sigmoid(cg) * cv).astype(BF16)

    return pl.pallas_call(
        body, name="ffn_act_fwd", grid=(nblk,),
        in_specs=[pl.BlockSpec((t, HEAD), lambda j: (0, j)), pl.BlockSpec((t, HEAD), lambda j: (0, nblk + j)),
                  pl.BlockSpec((kw, HEAD), lambda j: (0, j)), pl.BlockSpec((kw, HEAD), lambda j: (0, nblk + j)),
                  pl.BlockSpec((1, HEAD), lambda j: (0, j)), pl.BlockSpec((1, HEAD), lambda j: (0, nblk + j))],
        out_specs=pl.BlockSpec((t, HEAD), lambda j: (0, j)), out_shape=jax.ShapeDtypeStruct((t, dff), BF16),
        compiler_params=_params(("parallel",)))(up, up, conv_w, conv_w, bias, bias)


def _ffn_act_bwd(up, dact, conv_w, bias, dff, jobs=()):
    t = up.shape[0]
    nblk = dff // HEAD
    kw = conv_w.shape[0]

    def body(me_ref, pa_ref, d_ref, wm_ref, wp_ref, bm_ref, bp_ref, dup_ref, dw_ref, db_ref):
        is_gate = pl.program_id(0) < nblk
        rows = lax.broadcasted_iota(jnp.int32, (t, HEAD), 0)
        xv = me_ref[...].astype(F32)
        w = wm_ref[...]
        cm = _conv(xv, w, rows) + bm_ref[...]
        cp = _conv(pa_ref[...].astype(F32), wp_ref[...], rows) + bp_ref[...]
        d = d_ref[...].astype(F32)
        dc = jnp.where(is_gate, d * cp * _silu_grad(cm), d * (cp * _sigmoid(cp)))
        db_ref[...] = jnp.sum(dc, axis=0, keepdims=True)
        dx = dc * w[kw - 1:kw, :]
        dw_ref[kw - 1:kw, :] = jnp.sum(dc * xv, axis=0, keepdims=True)
        for s in range(1, kw):
            dx = dx + _shift_up(dc, s, rows) * w[kw - 1 - s:kw - s, :]
            dw_ref[kw - 1 - s:kw - s, :] = jnp.sum(dc * _shift_down(xv, s, rows), axis=0, keepdims=True)
        dup_ref[...] = dx.astype(BF16)

    part = lambda j: (j + nblk) % (2 * nblk)
    me = pl.BlockSpec((t, HEAD), lambda j: (0, j))
    wme = pl.BlockSpec((kw, HEAD), lambda j: (0, j))
    bme = pl.BlockSpec((1, HEAD), lambda j: (0, j))
    return _call(
        body, "ffn_act_bwd", (2 * nblk,),
        [me, pl.BlockSpec((t, HEAD), lambda j: (0, part(j))), pl.BlockSpec((t, HEAD), lambda j: (0, j % nblk)),
         wme, pl.BlockSpec((kw, HEAD), lambda j: (0, part(j))),
         bme, pl.BlockSpec((1, HEAD), lambda j: (0, part(j)))],
        [me, wme, bme],
        [jax.ShapeDtypeStruct((t, 2 * dff), BF16), jax.ShapeDtypeStruct((kw, 2 * dff), F32),
         jax.ShapeDtypeStruct((1, 2 * dff), F32)],
        (up, up, dact, conv_w, conv_w, bias, bias), sem=("parallel",), jobs=jobs)


def _ple_fwd(x, gt, pp):
    t, d = x.shape
    tb, tc = _tile(t, 512, 8), _tile(d, 1024)

    def body(x_ref, g_ref, p_ref, o_ref):
        o_ref[...] = x_ref[...] + _sigmoid(g_ref[...].astype(F32)) * p_ref[...].astype(F32)

    blk = pl.BlockSpec((tb, tc), lambda i, j: (i, j))
    return pl.pallas_call(
        body, name="ple_fwd", grid=(t // tb, d // tc), in_specs=[blk, blk, blk], out_specs=blk,
        out_shape=jax.ShapeDtypeStruct((t, d), F32), compiler_params=_params(("parallel", "parallel")))(x, gt, pp)


def _ple_bwd(dx, gt, pp):
    t, d = dx.shape
    tb, tc = _tile(t, 512, 8), _tile(d, 1024)

    def body(dx_ref, g_ref, p_ref, dg_ref, dp_ref):
        dv = dx_ref[...]
        s = _sigmoid(g_ref[...].astype(F32))
        dg_ref[...] = (dv * p_ref[...].astype(F32) * s * (1.0 - s)).astype(BF16)
        dp_ref[...] = (dv * s).astype(BF16)

    blk = pl.BlockSpec((tb, tc), lambda i, j: (i, j))
    shp = jax.ShapeDtypeStruct((t, d), BF16)
    return pl.pallas_call(
        body, name="ple_bwd", grid=(t // tb, d // tc), in_specs=[blk, blk, blk], out_specs=[blk, blk],
        out_shape=[shp, shp], compiler_params=_params(("parallel", "parallel")))(dx, gt, pp)


def _adam(pieces, w, m, v, name, jobs=()):
    nq = len(pieces)
    npart, rp, c = pieces[0].shape
    row_bytes = 2 * c * (nq * npart * pieces[0].dtype.itemsize + 7 * 4)
    tr = _tile(rp, max(16, min(512, ADAM_VMEM_BUDGET // row_bytes)), 16)
    nblk = rp // tr
    c1 = 1.0 - ADAM_B1 ** ADAM_STEP
    c2 = 1.0 - ADAM_B2 ** ADAM_STEP

    def body(*refs):
        p_refs = refs[:nq]
        w_ref, m_ref, v_ref, g_ref, d_ref, mo_ref, vo_ref = refs[nq:]
        for q in range(nq):
            @pl.when(pl.program_id(0) == q)
            def _(p_ref=p_refs[q]):
                g = p_ref[0].astype(F32)
                for i in range(1, npart):
                    g = g + p_ref[i].astype(F32)
                mn = ADAM_B1 * m_ref[...] + (1.0 - ADAM_B1) * g
                vn = ADAM_B2 * v_ref[...] + (1.0 - ADAM_B2) * (g * g)
                g_ref[...] = g
                mo_ref[...] = mn
                vo_ref[...] = vn
                d_ref[...] = -ADAM_LR * ((mn / c1) / (jnp.sqrt(vn / c2) + ADAM_EPS) + ADAM_WD * w_ref[...])

    def piece_spec(q):
        return pl.BlockSpec((npart, tr, c), lambda i, r: (0, jnp.where(i == q, r, jnp.where(i < q, 0, nblk - 1)), 0))

    blk = pl.BlockSpec((tr, c), lambda i, r: (i * nblk + r, 0))
    shp = jax.ShapeDtypeStruct((nq * rp, c), F32)
    return _call(body, name, (nq, nblk), [piece_spec(q) for q in range(nq)] + [blk] * 3, [blk] * 4, [shp] * 4,
                 (*pieces, w, m, v), sem=("parallel", "parallel"), jobs=jobs)


_BIG = ("w_in", "w_branch_a", "w_branch_b", "w_out", "w_ffn_up", "w_ffn_down", "w_ple_gate", "w_ple_proj")
_COL_SHARDED = ("w_in", "w_branch_b", "w_ffn_up", "w_ple_proj")
_CONVS = ("conv_qkv", "conv_ffn")
_GATHER_ON_PROJ = ("w_ffn_up",)
_GATHER_ON_DELTA = ("w_ffn_down",)
_GATHER_AHEAD = ("w_in", "conv_qkv")
_GATHER_ON_UP = ("w_ple_gate", "w_ple_proj")
_GATHER_AHEAD_2 = ("w_branch_a", "w_branch_b", "w_out", "conv_ffn")
_SCATTER_ON_DACT = ("w_ple_gate", "w_ple_proj")
_SCATTER_ON_DELTA = ("w_ffn_up",)
_SCATTER_ON_DW_MAIN = ("w_out", "w_branch_a", "w_branch_b")
_SCATTER_ON_DH1 = ("w_ffn_down",)
_SMALL = ("norm_mix", "conv_qkv", "a_log", "dt_bias", "head_norm", "sgu_norm", "w_spatial", "b_spatial", "norm_ffn",
          "conv_ffn", "b_conv_ffn", "norm_ple", "norm_final")
_WEIGHTS = ("norm_mix", "w_in", "conv_qkv", "a_log", "dt_bias", "head_norm", "sgu_norm", "w_spatial", "b_spatial",
            "w_branch_a", "w_branch_b", "w_out", "norm_ffn", "w_ffn_up", "conv_ffn", "b_conv_ffn", "w_ffn_down",
            "norm_ple", "w_ple_gate", "w_ple_proj", "norm_final")


def _full_cols(g):
    return jnp.transpose(g, (1, 0, 2)).reshape(g.shape[1], N_DEV * g.shape[2])


def _full_rows(g):
    return g.reshape(N_DEV * g.shape[1], g.shape[2])


def _split_cols(dw):
    k, n = dw.shape
    return jnp.transpose(dw.reshape(k, N_DEV, n // N_DEV), (1, 0, 2))


def _split_rows(dw):
    k, n = dw.shape
    return dw.reshape(N_DEV, k // N_DEV, n)


def _pad_lanes(v, width=LANES, offset=0):
    return jnp.pad(v, ((0, 0), (offset, width - offset - v.shape[1])))


def kernel(x, p, norm_mix, w_in, conv_qkv, a_log, dt_bias, head_norm, sgu_norm, w_spatial, b_spatial, w_branch_a, w_branch_b, w_out, norm_ffn, w_ffn_up, conv_ffn, b_conv_ffn, w_ffn_down, norm_ple, w_ple_gate, w_ple_proj, norm_final, loss_target, m_norm_mix, m_w_in, m_conv_qkv, m_a_log, m_dt_bias, m_head_norm, m_sgu_norm, m_w_spatial, m_b_spatial, m_w_branch_a, m_w_branch_b, m_w_out, m_norm_ffn, m_w_ffn_up, m_conv_ffn, m_b_conv_ffn, m_w_ffn_down, m_norm_ple, m_w_ple_gate, m_w_ple_proj, m_norm_final, v_norm_mix, v_w_in, v_conv_qkv, v_a_log, v_dt_bias, v_head_norm, v_sgu_norm, v_w_spatial, v_b_spatial, v_w_branch_a, v_w_branch_b, v_w_out, v_norm_ffn, v_w_ffn_up, v_conv_ffn, v_b_conv_ffn, v_w_ffn_down, v_norm_ple, v_w_ple_gate, v_w_ple_proj, v_norm_final):
    env = dict(locals())
    wts = {n: env[n] for n in _WEIGHTS}
    mom_m = {n: env["m_" + n] for n in _WEIGHTS}
    mom_v = {n: env["v_" + n] for n in _WEIGHTS}

    xin = x[0]
    tgt = loss_target[0]
    t, d = xin.shape
    depth = w_in.shape[0]
    hv = a_log.shape[1]
    vw = hv * HEAD
    wb = sgu_norm.shape[1]
    ng = w_spatial.shape[1]
    n_in = w_in.shape[2] * N_DEV
    qk = (n_in - 2 * vw - 2 * hv - 2 * wb - 2 * d) // 2
    hqk = qk // HEAD
    dff = w_ffn_down.shape[1] * N_DEV
    cw = 2 * qk + vw
    o_z, o_ba = 2 * qk + vw, 2 * qk + 2 * vw
    o_ub = o_ba
    o_ga = o_ub + 2 * wb
    me = 4 * lax.axis_index("x") + 2 * lax.axis_index("y") + lax.axis_index("c")

    full = [dict() for _ in range(depth)]
    staged = {}

    def as_cols(n):
        return "_cols" if (n in _COL_SHARDED or n in _CONVS) and wts[n].shape[-1] % LANES == 0 else ""

    def chips(i, names):
        names = names if i < depth else ()
        return [(i, n) for n in names], [(wts[n][i].astype(BF16) if n in _BIG else wts[n][i], "chips" + as_cols(n))
                                         for n in names]

    def relay(keys):
        return list(keys), [(staged.pop(key), "relay" + as_cols(key[1])) for key in keys]

    def settle(chip_keys, relay_keys, results):
        for key, g in zip(chip_keys, results):
            staged[key] = g
        for (i, n), g in zip(relay_keys, results[len(chip_keys):]):
            full[i][n] = g if as_cols(n) else _full_cols(g) if n in _COL_SHARDED or n in _CONVS else _full_rows(g)

    ck, cj = chips(0, _GATHER_AHEAD + _GATHER_AHEAD_2)
    settle(ck, [], _exchange(cj, "gather_first"))
    rk, rj = relay(ck)
    settle([], rk, _exchange(rj, "relay_first"))

    saved = []
    xc = xin
    for i in range(depth):
        fw = full[i]
        w_full = fw["w_in"]
        fw["w_main"] = jnp.concatenate([w_full[:, :o_ba], w_full[:, o_ba + 2 * hv:]], axis=1)
        fw["w_ba"] = _pad_lanes(w_full[:, o_ba:o_ba + 2 * hv])
        s = {"x0": xc}
        s["h1"] = _rms_fwd(xc, norm_mix[i], "rms_fwd")
        ck, cj = chips(i, _GATHER_ON_PROJ)
        s["proj"], got = _matmul(s["h1"], fw["w_main"], "nn", BF16, "mm_proj", jobs=cj)
        settle(ck, [], got)
        s["ba"] = _matmul(s["h1"], fw["w_ba"], "nn", F32, "mm_ba")
        s["qkvn"] = _qkv_fwd(s["proj"], fw["conv_qkv"], hqk, hqk)
        s["alog"] = _pad_lanes(a_log[i][None, :], offset=hv)
        s["dtb"] = _pad_lanes(dt_bias[i][None, :], offset=hv)
        bg = _gate_fwd(s["ba"], s["alog"], s["dtb"], hv)
        beta_t = bg[:, :hv].T
        gam_t = bg[:, hv:2 * hv].T
        s["beta_b"] = jnp.broadcast_to(beta_t[:, :, None], (hv, t, LANES))
        s["gam_b"] = jnp.broadcast_to(gam_t[:, :, None], (hv, t, LANES))
        s["gam_r"] = gam_t.reshape(hv, t // CA, 1, CA)
        s["gam_l"] = jnp.broadcast_to(s["gam_r"][:, :, :, CA - 1:], (hv, t // CA, 1, LANES))
        ck1, cj1 = chips(i, _GATHER_ON_DELTA)
        ck2, cj2 = chips(i + 1, _GATHER_AHEAD)
        rk, rj = relay([(i, n) for n in _GATHER_ON_PROJ])
        (s["o"], s["s_all"], s["tm_all"]), got = _delta_fwd(
            s["qkvn"], s["beta_b"], s["gam_b"], s["gam_r"], s["gam_l"], hqk, hv, jobs=cj1 + cj2 + rj)
        settle(ck1 + ck2, rk, got)
        s["outa"] = _apost_fwd(s["o"], s["proj"], head_norm[i], o_z, hv)
        s["b_t"] = b_spatial[i].T
        s["outb"] = _sgu_fwd(s["proj"], sgu_norm[i], w_spatial[i], s["b_t"], o_ub, wb)
        s["ya"] = _matmul(s["outa"], fw["w_branch_a"], "nn", BF16, "mm_ya")
        s["yb"] = _matmul(s["outb"], fw["w_branch_b"], "nn", BF16, "mm_yb")
        s["mg"] = _merge_fwd(s["ya"], s["yb"], s["proj"], o_ga)
        s["x1"] = _matmul(s["mg"], fw["w_out"], "nn", F32, "mm_out", res=xc)
        s["h2"] = _rms_fwd(s["x1"], norm_ffn[i], "rms_fwd")
        nxt = i + 1 < depth
        ck1, cj1 = chips(i, _GATHER_ON_UP)
        ck2, cj2 = chips(i + 1, _GATHER_AHEAD_2)
        rk, rj = relay([(i, n) for n in _GATHER_ON_DELTA] + ([(i + 1, n) for n in _GATHER_AHEAD] if nxt else []))
        s["up"], got = _matmul(s["h2"], fw["w_ffn_up"], "nn", BF16, "mm_up", jobs=cj1 + cj2 + rj)
        settle(ck1 + ck2, rk, got)
        s["bias"] = b_conv_ffn[i][None, :]
        s["act"] = _ffn_act_fwd(s["up"], fw["conv_ffn"], s["bias"], dff)
        rk, rj = relay([(i, n) for n in _GATHER_ON_UP] + ([(i + 1, n) for n in _GATHER_AHEAD_2] if nxt else []))
        s["x2"], got = _matmul(s["act"], fw["w_ffn_down"], "nn", F32, "mm_down", res=s["x1"], jobs=rj)
        settle([], rk, got)
        s["h3"] = _rms_fwd(s["x2"], norm_ple[i], "rms_fwd")
        s["gt"] = _matmul(s["h3"], fw["w_ple_gate"], "nn", BF16, "mm_gt")
        s["pp"] = _matmul(p[i, 0], fw["w_ple_proj"], "nn", BF16, "mm_pp")
        xc = _ple_fwd(s["x2"], s["gt"], s["pp"])
        saved.append(s)

    dx, g_norm_final, loss_part = _loss_head(xc, tgt, norm_final)

    small = {n: [None] * depth for n in _SMALL if n != "norm_final"}
    recv = {n: [None] * depth for n in _BIG}
    recv["w_in"] = [None] * (2 * depth)

    def scatter_jobs(gw, names):
        return [(gw[n], "scatter_cols") if as_cols(n) else
                (_split_cols(gw[n]) if n in _COL_SHARDED else _split_rows(gw[n]), "scatter") for n in names]

    def keep(i, names, results):
        for n, r in zip(names, results):
            recv[n][i] = r

    def carry(jobs, *args):
        return _matmul(*args, jobs=jobs) if jobs else (_matmul(*args), [])

    later = []
    for i in reversed(range(depth)):
        fw, s = full[i], saved[i]
        dgt, dpp = _ple_bwd(dx, s["gt"], s["pp"])
        gw = {"w_ple_gate": _matmul(s["h3"], dgt, "tn", BF16, "mm_dw_gt"),
              "w_ple_proj": _matmul(p[i, 0], dpp, "tn", BF16, "mm_dw_pp")}
        dh3 = _matmul(dgt, fw["w_ple_gate"], "nt", F32, "mm_dh3")
        dx, small["norm_ple"][i] = _rms_bwd(s["x2"], dh3, norm_ple[i], dx, "rms_bwd")

        dact, got = _matmul(dx, fw["w_ffn_down"], "nt", BF16, "mm_dact", jobs=scatter_jobs(gw, _SCATTER_ON_DACT))
        keep(i, _SCATTER_ON_DACT, got)
        gw["w_ffn_down"] = _matmul(s["act"], dx, "tn", BF16, "mm_dw_down")
        (dup, small["conv_ffn"][i], small["b_conv_ffn"][i]), _ = _ffn_act_bwd(s["up"], dact, fw["conv_ffn"], s["bias"], dff)
        gw["w_ffn_up"], got = carry(later[:1], s["h2"], dup, "tn", BF16, "mm_dw_up")
        keep(2 * i + 2, ("w_in",), got)
        dh2, got = carry(later[1:], dup, fw["w_ffn_up"], "nt", F32, "mm_dh2")
        keep(2 * i + 3, ("w_in",), got)
        dx, small["norm_ffn"][i] = _rms_bwd(s["x1"], dh2, norm_ffn[i], dx, "rms_bwd")

        dmg = _matmul(dx, fw["w_out"], "nt", BF16, "mm_dmg")
        gw["w_out"] = _matmul(s["mg"], dx, "tn", BF16, "mm_dw_out")
        dya, dyb, dga, dgb = _merge_bwd(dmg, s["ya"], s["yb"], s["proj"], o_ga)
        gw["w_branch_a"] = _matmul(s["outa"], dya, "tn", BF16, "mm_dw_a")
        gw["w_branch_b"] = _matmul(s["outb"], dyb, "tn", BF16, "mm_dw_b")
        douta = _matmul(dya, fw["w_branch_a"], "nt", BF16, "mm_douta")
        doutb = _matmul(dyb, fw["w_branch_b"], "nt", BF16, "mm_doutb")
        dub, dvb, small["w_spatial"][i], db_s, dsg = _sgu_bwd(s["proj"], sgu_norm[i], w_spatial[i], s["b_t"], doutb, o_ub, wb)
        small["b_spatial"][i] = db_s[:, :, 0]
        small["sgu_norm"][i] = dsg
        do, dz, small["head_norm"][i] = _apost_bwd(s["o"], s["proj"], head_norm[i], douta, o_z, hv)
        (dq, dk, dv, db_b, dg_b), got = _delta_bwd(
            s["qkvn"], s["beta_b"], s["gam_b"], s["gam_r"], s["gam_l"], s["s_all"], s["tm_all"], do, hqk, hv,
            jobs=scatter_jobs(gw, _SCATTER_ON_DELTA))
        keep(i, _SCATTER_ON_DELTA, got)
        dbg = _pad_lanes(jnp.concatenate([db_b[:, :, 0].T, dg_b[:, :, 0].T], axis=1))
        dba, dal, ddt = _gate_bwd(s["ba"], dbg, s["alog"], s["dtb"], hv)
        small["a_log"][i] = dal[:, hv:2 * hv]
        small["dt_bias"][i] = ddt[:, hv:2 * hv]
        dqkv_pre, small["conv_qkv"][i] = _qkv_bwd(s["proj"], jnp.concatenate([dq, dk, dv], axis=1), fw["conv_qkv"], hqk, hqk)
        dproj = jnp.concatenate([dqkv_pre, dz, dub, dvb, dga, dgb], axis=1)
        dw_main, got = _matmul(s["h1"], dproj, "tn", BF16, "mm_dw_main", jobs=scatter_jobs(gw, _SCATTER_ON_DW_MAIN))
        keep(i, _SCATTER_ON_DW_MAIN, got)
        dw_ba = _matmul(s["h1"], dba, "tn", BF16, "mm_dw_ba")
        dw_in = jnp.concatenate([dw_main[:, :o_ba], dw_ba[:, :2 * hv], dw_main[:, o_ba:]], axis=1)
        dh1, got = _matmul(dproj, fw["w_main"], "nt", F32, "mm_dh1", jobs=scatter_jobs(gw, _SCATTER_ON_DH1))
        keep(i, _SCATTER_ON_DH1, got)
        dh1 = _matmul(dba, fw["w_ba"], "nt", F32, "mm_dh1_ba", res=dh1)
        dx, small["norm_mix"][i] = _rms_bwd(s["x0"], dh1, norm_mix[i], dx, "rms_bwd")

        later = [(_split_cols(dw_in[:d // 2]), True), (_split_cols(dw_in[d // 2:]), True)]


    rep_names = tuple(n for n in _SMALL if n not in _CONVS)
    stacked = {n: jnp.concatenate([jnp.reshape(a, (-1,)) for a in small[n]]) for n in small}
    stacked["norm_final"] = g_norm_final.reshape(-1)

    def padded(parts, mult, axis=0):
        flat = jnp.concatenate(parts, axis=axis)
        pad = -flat.shape[axis] % mult
        return jnp.pad(flat, [(0, 0)] * axis + [(0, pad)])

    rep_flat = padded([stacked[n] for n in rep_names] + [loss_part[0, :1]], 16 * LANES)
    conv_flat = [padded([stacked[n]], 8 * LANES) for n in _CONVS]
    packed = jnp.concatenate([rep_flat] + conv_flat).reshape(-1, LANES)

    outs_g, outs_d, outs_m, outs_v = {}, {}, {}, {}

    adam_jobs = {"w_ffn_up": later[:1], "w_ffn_down": later[1:], "w_in": [(packed, False)]}
    for n in sorted(_BIG, key=lambda name: name == "w_in"):
        shp = wts[n].shape
        two_d = lambda a: a.reshape(shp[0] * shp[1], shp[2])
        res, got = _adam(recv[n], two_d(wts[n]), two_d(mom_m[n]), two_d(mom_v[n]), "adam_" + n,
                         jobs=adam_jobs.get(n, []))
        if n == "w_ffn_up":
            recv["w_in"][0] = got[0]
        elif n == "w_ffn_down":
            recv["w_in"][1] = got[0]
        elif n == "w_in":
            small_all = got[0].reshape(N_DEV, -1)
        outs_g[n], outs_d[n], outs_m[n], outs_v[n] = [r.reshape(shp) for r in res]

    n_rep = rep_flat.shape[0]
    pk = lambda src: padded([src[n].reshape(-1) for n in rep_names] + [jnp.zeros((1,), F32)], 16 * LANES).reshape(-1, LANES)
    res, _ = _adam([small_all[:, :n_rep].reshape(N_DEV, -1, LANES)], pk(wts), pk(mom_m), pk(mom_v), "adam_small")
    res = [r.reshape(-1) for r in res]
    off = 0
    for n in rep_names:
        shp = wts[n].shape
        size = math.prod(shp)
        outs_g[n], outs_d[n], outs_m[n], outs_v[n] = [r[off:off + size].reshape(shp) for r in res]
        off += size
    loss = res[0][off]

    off = n_rep
    for n, cf in zip(_CONVS, conv_flat):
        _, kw, cl = wts[n].shape
        part = small_all[:, off:off + depth * kw * cl * N_DEV].reshape(N_DEV, depth * kw, N_DEV, cl)
        off += cf.shape[0]
        part = lax.dynamic_index_in_dim(part, me, axis=2, keepdims=False)
        two_d = lambda a: a.reshape(depth * kw, cl)
        res, _ = _adam([part], two_d(wts[n]), two_d(mom_m[n]), two_d(mom_v[n]), "adam_" + n)
        outs_g[n], outs_d[n], outs_m[n], outs_v[n] = [r.reshape(wts[n].shape) for r in res]

    return (loss, dx[None], *[outs_g[n] for n in _WEIGHTS], *[outs_d[n] for n in _WEIGHTS],
            *[outs_m[n] for n in _WEIGHTS], *[outs_v[n] for n in _WEIGHTS])
```

```python
import functools
import itertools
import math

import jax
import jax.numpy as jnp
from jax import lax
from jax.experimental import pallas as pl
from jax.experimental.pallas import tpu as pltpu

F32 = jnp.float32
BF16 = jnp.bfloat16
EPS = 1e-6
LANES = 128
HEAD = 128
CA = 64
N_DEV = 8
VMEM_LIMIT = 48 * 1024 * 1024
ADAM_VMEM_BUDGET = 16 * 1024 * 1024
MESH = pl.DeviceIdType.MESH

ADAM_LR = 0.001
ADAM_B1 = 0.9
ADAM_B2 = 0.999
ADAM_EPS = 1e-08
ADAM_WD = 0.01
ADAM_STEP = 10


def _tile(n, cap, mult=LANES):
    best = None
    for t in range(mult, min(n, cap) + 1, mult):
        if n % t == 0:
            best = t
    return n if best is None else best


def _tile_near(n, target, mult=LANES):
    cands = [t for t in range(mult, min(n, target * 3 // 2) + 1, mult) if n % t == 0]
    return min(cands, key=lambda t: abs(t - target)) if cands else n


def _params(sem):
    return pltpu.CompilerParams(dimension_semantics=sem, vmem_limit_bytes=VMEM_LIMIT)


def _sigmoid(v):
    return jax.nn.sigmoid(v)


def _silu_grad(c):
    s = _sigmoid(c)
    return s + c * s * (1.0 - s)


_GELU_C = math.sqrt(2.0 / math.pi)


def _gelu(v):
    return 0.5 * v * (1.0 + jnp.tanh(_GELU_C * (v + 0.044715 * v * v * v)))


def _gelu_grad(v):
    t = jnp.tanh(_GELU_C * (v + 0.044715 * v * v * v))
    return 0.5 * (1.0 + t) + 0.5 * v * (1.0 - t * t) * _GELU_C * (1.0 + 3.0 * 0.044715 * v * v)


_NN = (((1,), (0,)), ((), ()))
_NT = (((1,), (1,)), ((), ()))
_TN = (((0,), (0,)), ((), ()))
_BNN = (((2,), (1,)), ((0,), (0,)))
_BNT = (((2,), (2,)), ((0,), (0,)))
_BTN = (((1,), (1,)), ((0,), (0,)))


def _bdot(a, b, dn=_NN):
    return lax.dot_general(a.astype(BF16), b.astype(BF16), dn, preferred_element_type=F32)


def _split(a):
    hi = a.astype(BF16)
    return hi, (a - hi.astype(F32)).astype(BF16)


def _dot3(ah, al, bh, bl, dn=_NN):
    def d(u, v):
        return lax.dot_general(u, v, dn, preferred_element_type=F32)
    return d(ah, bh) + (d(al, bh) + d(ah, bl))


def _hdot(a, b, dn=_NN):
    return _dot3(*_split(a), *_split(b), dn)


_PEERS = {"gather": (1, 2, 3, 4, 5, 6, 7), "scatter": (1, 2, 3, 4, 5, 6, 7), "chips": (1, 2, 4, 6), "relay": (2, 4, 6)}


def _kinds(jobs):
    return [{False: "gather", True: "scatter"}.get(kind, kind) for _, kind in jobs]


def _xchg_out_shapes(jobs):
    shapes = []
    for (a, _), kind in zip(jobs, _kinds(jobs)):
        shape = {"gather": (N_DEV,) + a.shape, "chips": (N_DEV,) + a.shape, "chips_cols": (a.shape[0], N_DEV * a.shape[1]),
                 "scatter_cols": (N_DEV, a.shape[0], a.shape[1] // N_DEV)}.get(kind, a.shape)
        shapes.append(jax.ShapeDtypeStruct(shape, a.dtype))
    return shapes


def _xchg_scratch(jobs):
    n = len(jobs)
    return [pltpu.SemaphoreType.DMA((n, N_DEV - 1)), pltpu.SemaphoreType.DMA((n, N_DEV - 1)), pltpu.SemaphoreType.DMA((n,))]


def _xchg_copies(kinds, src, out, sems):
    send_sems, recv_sems, local_sems = sems
    x, y, c = lax.axis_index("x"), lax.axis_index("y"), lax.axis_index("c")
    me = 4 * x + 2 * y + c

    def block(ref, idx, as_cols):
        if not as_cols:
            return ref.at[idx]
        width = ref.shape[1] // N_DEV
        return ref.at[:, pl.ds(pl.multiple_of(idx * width, LANES), width)]

    local, sends, recvs = [], [], []
    for m in range(N_DEV):
        px = lax.rem(x + ((m >> 2) & 1), 2)
        py = lax.rem(y + ((m >> 1) & 1), 2)
        pc = lax.rem(c + (m & 1), 2)
        peer = 4 * px + 2 * py + pc
        for k, kind in enumerate(kinds):
            base, cols = kind.split("_")[0], kind.endswith("_cols")
            if m == 0:
                if base != "relay":
                    mine = block(src[k], me, cols) if base == "scatter" else src[k]
                    local.append(pltpu.make_async_copy(mine, block(out[k], me, cols and base != "scatter"), local_sems.at[k]))
                continue
            if m not in _PEERS[base]:
                continue
            if base == "relay":
                to, mine = (x, y, 1 - c), block(src[k], peer, cols)
                there, here = block(out[k], peer, cols), block(out[k], 4 * px + 2 * py + 1 - c, cols)
            else:
                to, mine = (px, py, pc), block(src[k], peer, cols) if base == "scatter" else src[k]
                there, here = block(out[k], me, cols and base != "scatter"), block(out[k], peer, cols and base != "scatter")
            for dst, lst in ((there, sends), (here, recvs)):
                lst.append(pltpu.make_async_remote_copy(
                    src_ref=mine, dst_ref=dst, send_sem=send_sems.at[k, m - 1], recv_sem=recv_sems.at[k, m - 1],
                    device_id=to, device_id_type=MESH))
    return local, sends, recvs


def _xchg_start(scatter, src, out, sems):
    local, sends, _ = _xchg_copies(scatter, src, out, sems)
    for cp in local + sends:
        cp.start()


def _xchg_wait(scatter, src, out, sems):
    local, sends, recvs = _xchg_copies(scatter, src, out, sems)
    for cp in recvs:
        cp.wait_recv()
    for cp in sends:
        cp.wait_send()
    for cp in local:
        cp.wait()


_ANY = pl.BlockSpec(memory_space=pl.ANY)


def _xchg_aliases(jobs, n_in, n_out):
    return {n_in + k: n_out + k for k, kind in enumerate(_kinds(jobs)) if kind.startswith("relay")}


def _exchange(jobs, name):
    n = len(jobs)
    kinds = _kinds(jobs)

    def body(*refs):
        src, out, sems = refs[:n], refs[n:2 * n], refs[2 * n:]
        _xchg_start(kinds, src, out, sems)
        _xchg_wait(kinds, src, out, sems)

    return pl.pallas_call(
        body, name=name, in_specs=[_ANY] * n, out_specs=[_ANY] * n, out_shape=_xchg_out_shapes(jobs),
        scratch_shapes=_xchg_scratch(jobs), input_output_aliases=_xchg_aliases(jobs, 0, 0),
        compiler_params=pltpu.CompilerParams(has_side_effects=True))(*[a for a, _ in jobs])


def _carried(body, n_in, n_out, jobs, grid):
    if not jobs:
        return body
    nj = len(jobs)
    scatter = _kinds(jobs)

    def wrapped(*refs):
        ins, src = refs[:n_in], refs[n_in:n_in + nj]
        outs, got = refs[n_in + nj:n_in + nj + n_out], refs[n_in + nj + n_out:n_in + 2 * nj + n_out]
        rest = refs[n_in + 2 * nj + n_out:]
        scratch, sems = rest[:len(rest) - 3], rest[len(rest) - 3:]
        ids = [pl.program_id(a) for a in range(len(grid))]
        first = functools.reduce(jnp.logical_and, [i == 0 for i in ids])
        last = functools.reduce(jnp.logical_and, [i == g - 1 for i, g in zip(ids, grid)])

        @pl.when(first)
        def _():
            _xchg_start(scatter, src, got, sems)

        body(*ins, *outs, *scratch)

        @pl.when(last)
        def _():
            _xchg_wait(scatter, src, got, sems)

    return wrapped


def _call(body, name, grid, in_specs, out_specs, out_shape, args, scratch=(), sem=None, jobs=()):
    jobs = list(jobs)
    nj = len(jobs)
    sem = ("arbitrary",) * len(grid) if jobs or sem is None else sem
    res = pl.pallas_call(
        _carried(body, len(in_specs), len(out_specs), jobs, grid), name=name, grid=grid,
        in_specs=list(in_specs) + [_ANY] * nj, out_specs=list(out_specs) + [_ANY] * nj,
        out_shape=list(out_shape) + _xchg_out_shapes(jobs),
        scratch_shapes=list(scratch) + (_xchg_scratch(jobs) if jobs else []),
        input_output_aliases=_xchg_aliases(jobs, len(in_specs), len(out_specs)),
        compiler_params=_params(sem))(*args, *[a for a, _ in jobs])
    return res[:len(out_specs)], res[len(out_specs):]


MATMUL_OPERAND_VMEM = 20 * 1024 * 1024
MATMUL_TILE = 1024


def _matmul(a, b, mode, out_dtype, name, res=None, jobs=()):
    if mode == "tn":
        kdim, m = a.shape
    else:
        m, kdim = a.shape
    n = b.shape[0] if mode == "nt" else b.shape[1]
    tm, tn = _tile_near(m, MATMUL_TILE), _tile(n, MATMUL_TILE)
    per_k = 2 * (tm * a.dtype.itemsize + tn * b.dtype.itemsize)
    tk = _tile(kdim, max(LANES, MATMUL_OPERAND_VMEM // per_k))
    nk = kdim // tk
    dims = {"nn": _NN, "nt": _NT, "tn": _TN}[mode]

    def body(*refs):
        a_ref, b_ref = refs[:2]
        r_ref = refs[2] if res is not None else None
        o_ref = refs[3] if res is not None else refs[2]
        acc = refs[-1] if nk > 1 else None

        def write(r):
            if r_ref is not None:
                r = r + r_ref[...].astype(F32)
            o_ref[...] = r.astype(out_dtype)

        prod = _bdot(a_ref[...], b_ref[...], dims)
        if nk == 1:
            write(prod)
        else:
            k = pl.program_id(2)

            @pl.when(k == 0)
            def _():
                acc[...] = prod

            @pl.when(jnp.logical_and(k > 0, k < nk - 1))
            def _():
                acc[...] += prod

            @pl.when(k == nk - 1)
            def _():
                write(acc[...] + prod)

    a_spec = pl.BlockSpec((tk, tm), lambda i, j, k: (k, i)) if mode == "tn" else pl.BlockSpec((tm, tk), lambda i, j, k: (i, k))
    b_spec = pl.BlockSpec((tn, tk), lambda i, j, k: (j, k)) if mode == "nt" else pl.BlockSpec((tk, tn), lambda i, j, k: (k, j))
    o_spec = pl.BlockSpec((tm, tn), lambda i, j, k: (i, j))
    in_specs = [a_spec, b_spec] + ([o_spec] if res is not None else [])
    args = (a, b) + ((res,) if res is not None else ())
    (out,), got = _call(body, name, (m // tm, n // tn, nk), in_specs, [o_spec], [jax.ShapeDtypeStruct((m, n), out_dtype)],
                        args, scratch=[pltpu.VMEM((tm, tn), F32)] if nk > 1 else [],
                        sem=("parallel", "parallel", "arbitrary"), jobs=jobs)
    return (out, got) if jobs else out


def _rms_fwd(x, gain, name):
    t, d = x.shape
    tb = _tile(t, 256, 8)

    def body(x_ref, g_ref, h_ref):
        xv = x_ref[...]
        r = lax.rsqrt(jnp.mean(xv * xv, axis=-1, keepdims=True) + EPS)
        h_ref[...] = (xv * r * g_ref[...]).astype(BF16)

    return pl.pallas_call(
        body, name=name, grid=(t // tb,),
        in_specs=[pl.BlockSpec((tb, d), lambda i: (i, 0)), pl.BlockSpec((1, d), lambda i: (0, 0))],
        out_specs=pl.BlockSpec((tb, d), lambda i: (i, 0)), out_shape=jax.ShapeDtypeStruct((t, d), BF16),
        compiler_params=_params(("parallel",)))(x, gain.reshape(1, d))


def _rms_bwd(x, dh, gain, dres, name):
    t, d = x.shape
    tb = _tile(t, 256, 8)

    def body(x_ref, dh_ref, g_ref, dr_ref, dx_ref, dg_ref):
        @pl.when(pl.program_id(0) == 0)
        def _():
            dg_ref[...] = jnp.zeros_like(dg_ref)

        xv = x_ref[...]
        dy = dh_ref[...].astype(F32)
        r = lax.rsqrt(jnp.mean(xv * xv, axis=-1, keepdims=True) + EPS)
        xh = xv * r
        dxh = dy * g_ref[...]
        dx_ref[...] = dr_ref[...] + r * (dxh - xh * jnp.mean(dxh * xh, axis=-1, keepdims=True))
        dg_ref[...] += jnp.sum(dy * xh, axis=0, keepdims=True)

    row = pl.BlockSpec((tb, d), lambda i: (i, 0))
    vec = pl.BlockSpec((1, d), lambda i: (0, 0))
    return pl.pallas_call(
        body, name=name, grid=(t // tb,), in_specs=[row, row, vec, row], out_specs=[row, vec],
        out_shape=[jax.ShapeDtypeStruct((t, d), F32), jax.ShapeDtypeStruct((1, d), F32)],
        compiler_params=_params(("arbitrary",)))(x, dh, gain.reshape(1, d), dres)


def _loss_head(x, target, gain):
    t, d = x.shape
    tb = _tile(t, 256, 8)

    def body(x_ref, t_ref, g_ref, dx_ref, dg_ref, loss_ref):
        @pl.when(pl.program_id(0) == 0)
        def _():
            dg_ref[...] = jnp.zeros_like(dg_ref)
            loss_ref[...] = jnp.zeros_like(loss_ref)

        xv = x_ref[...]
        r = lax.rsqrt(jnp.mean(xv * xv, axis=-1, keepdims=True) + EPS)
        xh = xv * r
        err = xh * g_ref[...] - t_ref[...]
        per_row = jnp.mean(err * err, axis=-1, keepdims=True)
        loss_ref[...] += 0.5 * jnp.sum(per_row, axis=0, keepdims=True)
        dy = err * (1.0 / d)
        dxh = dy * g_ref[...]
        dx_ref[...] = r * (dxh - xh * jnp.mean(dxh * xh, axis=-1, keepdims=True))
        dg_ref[...] += jnp.sum(dy * xh, axis=0, keepdims=True)

    row = pl.BlockSpec((tb, d), lambda i: (i, 0))
    vec = pl.BlockSpec((1, d), lambda i: (0, 0))
    return pl.pallas_call(
        body, name="loss_head", grid=(t // tb,), in_specs=[row, row, vec],
        out_specs=[row, vec, pl.BlockSpec((1, LANES), lambda i: (0, 0))],
        out_shape=[jax.ShapeDtypeStruct((t, d), F32), jax.ShapeDtypeStruct((1, d), F32),
                   jax.ShapeDtypeStruct((1, LANES), F32)],
        compiler_params=_params(("arbitrary",)))(x, target, gain.reshape(1, d))


def _shift_down(v, s, rows):
    if s == 0:
        return v
    return jnp.where(rows >= s, pltpu.roll(v, s, 0), 0.0)


def _shift_up(v, s, rows):
    if s == 0:
        return v
    t = v.shape[0]
    return jnp.where(rows < t - s, pltpu.roll(v, t - s, 0), 0.0)


def _conv(v, w, rows):
    k = w.shape[0]
    out = v * w[k - 1:k, :]
    for s in range(1, k):
        out = out + _shift_down(v, s, rows) * w[k - 1 - s:k - s, :]
    return out


def _qkv_fwd(proj, conv_w, nq, nk):
    t = proj.shape[0]
    cw = conv_w.shape[1]
    nblk = cw // HEAD

    def body(p_ref, w_ref, o_ref):
        j = pl.program_id(0)
        rows = lax.broadcasted_iota(jnp.int32, (t, HEAD), 0)
        c = _conv(p_ref[...].astype(F32), w_ref[...], rows)
        a = c * _sigmoid(c)
        nrm = a * lax.rsqrt(jnp.sum(a * a, axis=-1, keepdims=True) + EPS)
        nrm = nrm * jnp.where(j < nq, HEAD ** -0.5, 1.0)
        o_ref[...] = jnp.where(j < nq + nk, nrm, a).astype(BF16)

    return pl.pallas_call(
        body, name="qkv_fwd", grid=(nblk,),
        in_specs=[pl.BlockSpec((t, HEAD), lambda j: (0, j)), pl.BlockSpec((conv_w.shape[0], HEAD), lambda j: (0, j))],
        out_specs=pl.BlockSpec((t, HEAD), lambda j: (0, j)), out_shape=jax.ShapeDtypeStruct((t, cw), BF16),
        compiler_params=_params(("parallel",)))(proj, conv_w)


def _qkv_bwd(proj, dqkv, conv_w, nq, nk):
    t = proj.shape[0]
    kw, cw = conv_w.shape
    nblk = cw // HEAD

    def body(p_ref, d_ref, w_ref, dp_ref, dw_ref):
        j = pl.program_id(0)
        rows = lax.broadcasted_iota(jnp.int32, (t, HEAD), 0)
        xv = p_ref[...].astype(F32)
        w = w_ref[...]
        c = _conv(xv, w, rows)
        a = c * _sigmoid(c)
        dy = d_ref[...].astype(F32)
        r = lax.rsqrt(jnp.sum(a * a, axis=-1, keepdims=True) + EPS)
        y = a * r
        scale = jnp.where(j < nq, HEAD ** -0.5, 1.0)
        da_n = scale * r * (dy - y * jnp.sum(dy * y, axis=-1, keepdims=True))
        da = jnp.where(j < nq + nk, da_n, dy)
        dc = da * _silu_grad(c)
        dx = dc * w[kw - 1:kw, :]
        dw_ref[kw - 1:kw, :] = jnp.sum(dc * xv, axis=0, keepdims=True)
        for s in range(1, kw):
            dx = dx + _shift_up(dc, s, rows) * w[kw - 1 - s:kw - s, :]
            dw_ref[kw - 1 - s:kw - s, :] = jnp.sum(dc * _shift_down(xv, s, rows), axis=0, keepdims=True)
        dp_ref[...] = dx.astype(BF16)

    blk = pl.BlockSpec((t, HEAD), lambda j: (0, j))
    wblk = pl.BlockSpec((kw, HEAD), lambda j: (0, j))
    return pl.pallas_call(
        body, name="qkv_bwd", grid=(nblk,), in_specs=[blk, blk, wblk], out_specs=[blk, wblk],
        out_shape=[jax.ShapeDtypeStruct((t, cw), BF16), jax.ShapeDtypeStruct((kw, cw), F32)],
        compiler_params=_params(("parallel",)))(proj, dqkv, conv_w)


def _softplus(v):
    return jnp.where(v < -15.0, jnp.exp(v), jnp.maximum(v, 0.0) + jnp.log(1.0 + jnp.exp(-jnp.abs(v))))


def _gate_fwd(ba, alog_pad, dtb_pad, hv):
    t = ba.shape[0]
    tb = _tile(t, 512, CA)

    def body(ba_ref, al_ref, dt_ref, o_ref):
        v = ba_ref[...]
        beta = _sigmoid(v)
        g = -jnp.exp(al_ref[...]) * _softplus(v + dt_ref[...])
        pos = lax.broadcasted_iota(jnp.int32, (tb, LANES), 0) % CA
        s = 1
        while s < CA:
            g = g + jnp.where(pos >= s, pltpu.roll(g, s, 0), 0.0)
            s *= 2
        lane = lax.broadcasted_iota(jnp.int32, (tb, LANES), 1)
        o_ref[...] = jnp.where(lane < hv, beta, g)

    row = pl.BlockSpec((tb, LANES), lambda i: (i, 0))
    vec = pl.BlockSpec((1, LANES), lambda i: (0, 0))
    return pl.pallas_call(
        body, name="gate_fwd", grid=(t // tb,), in_specs=[row, vec, vec], out_specs=row,
        out_shape=jax.ShapeDtypeStruct((t, LANES), F32), compiler_params=_params(("parallel",)))(ba, alog_pad, dtb_pad)


def _gate_bwd(ba, dbg, alog_pad, dtb_pad, hv):
    t = ba.shape[0]
    tb = _tile(t, 512, CA)

    def body(ba_ref, d_ref, al_ref, dt_ref, dba_ref, dal_ref, ddt_ref):
        @pl.when(pl.program_id(0) == 0)
        def _():
            dal_ref[...] = jnp.zeros_like(dal_ref)
            ddt_ref[...] = jnp.zeros_like(ddt_ref)

        v = ba_ref[...]
        d = d_ref[...]
        pos = lax.broadcasted_iota(jnp.int32, (tb, LANES), 0) % CA
        dg = d
        s = 1
        while s < CA:
            dg = dg + jnp.where(pos < CA - s, pltpu.roll(dg, tb - s, 0), 0.0)
            s *= 2
        beta = _sigmoid(v)
        na = -jnp.exp(al_ref[...])
        z = v + dt_ref[...]
        da = dg * na * _sigmoid(z)
        lane = lax.broadcasted_iota(jnp.int32, (tb, LANES), 1)
        in_a = jnp.logical_and(lane >= hv, lane < 2 * hv)
        da = jnp.where(in_a, da, 0.0)
        dba_ref[...] = jnp.where(lane < hv, d * beta * (1.0 - beta), da)
        ddt_ref[...] += jnp.sum(da, axis=0, keepdims=True)
        dal_ref[...] += jnp.sum(jnp.where(in_a, dg * na * _softplus(z), 0.0), axis=0, keepdims=True)

    row = pl.BlockSpec((tb, LANES), lambda i: (i, 0))
    vec = pl.BlockSpec((1, LANES), lambda i: (0, 0))
    return pl.pallas_call(
        body, name="gate_bwd", grid=(t // tb,), in_specs=[row, row, vec, vec], out_specs=[row, vec, vec],
        out_shape=[jax.ShapeDtypeStruct((t, LANES), F32), jax.ShapeDtypeStruct((1, LANES), F32),
                   jax.ShapeDtypeStruct((1, LANES), F32)],
        compiler_params=_params(("arbitrary",)))(ba, dbg, alog_pad, dtb_pad)


def _chunk_masks():
    r = lax.broadcasted_iota(jnp.int32, (CA, CA), 0)
    c = lax.broadcasted_iota(jnp.int32, (CA, CA), 1)
    return r >= c, r > c, (r == c).astype(F32)


def _inv_unit_lower(a, eye):
    x = eye - a
    ph, plo = _split(a)
    n = 1
    while n < CA // 2:
        ph, plo = _split(_dot3(ph, plo, ph, plo, _BNN))
        x = x + _dot3(*_split(x), ph, plo, _BNN)
        n *= 2
    return x


def _delta_pre(q, k, v, bcol, gc, gr, gl, causal, strict):
    eg = jnp.exp(gc)
    dm = jnp.exp(jnp.where(causal, gc[:, :, :CA] - gr, -jnp.inf))
    kb = k * bcol
    kkb = _bdot(kb, k, _BNT)
    a = jnp.where(strict, kkb * dm, 0.0)
    rhs = jnp.concatenate([v * bcol, kb * eg], axis=2)
    qk = _bdot(q, k, _BNT)
    ekd = jnp.exp(gl - gc)
    return dict(eg=eg, dm=dm, kb=kb, kkb=kkb, a=a, rhs=rhs, p=qk * dm, qd=q * eg, ekd=ekd, kd=k * ekd, cd=jnp.exp(gl))


def _delta_fwd(qkvn, beta_b, gam_b, gam_r, gam_l, hqk, hv, jobs=()):
    t = qkvn.shape[0]
    rep = hv // hqk
    rb = _tile(t, 512, CA)
    nb = t // rb
    ncb = rb // CA
    nc = t // CA

    def body(q_ref, k_ref, v_ref, b_ref, gc_ref, gr_ref, gl_ref, o_ref, s_ref, tm_ref, state, sol_sc, p_sc):
        @pl.when(pl.program_id(1) == 0)
        def _():
            state[...] = jnp.zeros_like(state)

        causal, strict, eye = _chunk_masks()

        def chunks(a):
            return a.astype(F32).reshape(ncb, CA, a.shape[-1])

        q = chunks(q_ref[...])
        k = chunks(k_ref[...])
        for h in range(rep):
            pre = _delta_pre(q, k, chunks(v_ref[:, h * HEAD:(h + 1) * HEAD]), chunks(b_ref[h]), chunks(gc_ref[h]),
                             gr_ref[h], gl_ref[h], causal, strict)
            tm = _inv_unit_lower(pre["a"], eye)
            tm_ref[h] = tm
            sol_sc[h] = _hdot(tm, pre["rhs"], _BNN)
            p_sc[h] = pre["p"]

        def chunk(n, carry):
            rows = pl.ds(pl.multiple_of(n * CA, CA), CA)
            qn = q_ref[rows, :].astype(F32)
            kn = k_ref[rows, :].astype(F32)
            for h in range(rep):
                gc = gc_ref[h, rows, :]
                gl = gl_ref[h, n]
                s = state[h]
                v_new = sol_sc[h, n, :, :HEAD] - _bdot(sol_sc[h, n, :, HEAD:], s)
                o_ref[rows, h * HEAD:(h + 1) * HEAD] = _bdot(qn * jnp.exp(gc), s) + _bdot(p_sc[h, n], v_new)
                s_ref[h, n] = s.astype(BF16)
                state[h] = s * jnp.exp(gl) + _bdot(kn * jnp.exp(gl - gc), v_new, _TN)
            return carry

        lax.fori_loop(0, ncb, chunk, 0)

    qoff, koff, voff = 0, hqk, 2 * hqk // rep
    per_chunk = lambda width: pl.BlockSpec((rep, ncb, 1, width), lambda j, i: (j, i, 0, 0))
    return _call(
        body, "delta_fwd", (hqk, nb),
        [pl.BlockSpec((rb, HEAD), lambda j, i: (i, qoff + j)),
         pl.BlockSpec((rb, HEAD), lambda j, i: (i, koff + j)),
         pl.BlockSpec((rb, rep * HEAD), lambda j, i: (i, voff + j)),
         pl.BlockSpec((rep, rb, LANES), lambda j, i: (j, i, 0)),
         pl.BlockSpec((rep, rb, LANES), lambda j, i: (j, i, 0)),
         per_chunk(CA), per_chunk(LANES)],
        [pl.BlockSpec((rb, rep * HEAD), lambda j, i: (i, j)),
         pl.BlockSpec((rep, ncb, HEAD, HEAD), lambda j, i: (j, i, 0, 0)),
         pl.BlockSpec((rep, ncb, CA, CA), lambda j, i: (j, i, 0, 0))],
        [jax.ShapeDtypeStruct((t, hv * HEAD), F32), jax.ShapeDtypeStruct((hv, nc, HEAD, HEAD), BF16),
         jax.ShapeDtypeStruct((hv, nc, CA, CA), F32)],
        (qkvn, qkvn, qkvn, beta_b, gam_b, gam_r, gam_l),
        scratch=[pltpu.VMEM((rep, HEAD, HEAD), F32), pltpu.VMEM((rep, ncb, CA, 2 * HEAD), F32),
                 pltpu.VMEM((rep, ncb, CA, CA), F32)],
        sem=("parallel", "arbitrary"), jobs=jobs)


def _delta_bwd(qkvn, beta_b, gam_b, gam_r, gam_l, s_all, tm_all, do, hqk, hv, jobs=()):
    t = qkvn.shape[0]
    rep = hv // hqk
    rb = _tile(t, 512, CA)
    nb = t // rb
    ncb = rb // CA

    def body(q_ref, k_ref, v_ref, b_ref, gc_ref, gr_ref, gl_ref, s_ref, tm_ref, do_ref,
             dq_ref, dk_ref, dv_ref, db_ref, dg_ref, dstate, sol_sc, vn_sc, p_sc, kkb_sc, dvn_sc, ds_sc):
        @pl.when(pl.program_id(1) == 0)
        def _():
            dstate[...] = jnp.zeros_like(dstate)

        causal, strict, _ = _chunk_masks()
        ones = jnp.ones((ncb, CA, LANES), BF16)
        last = lax.broadcasted_iota(jnp.int32, (CA, LANES), 0) == CA - 1

        def chunks(a):
            return a.astype(F32).reshape(ncb, CA, a.shape[-1])

        def rows_of(a):
            return a.reshape(rb, a.shape[-1])

        def rowsum(m):
            return jnp.sum(m, axis=2, keepdims=True)

        def colsum(m):
            hi, lo = _split(m)
            return _bdot(hi, ones, _BTN) + _bdot(lo, ones, _BTN)

        q = chunks(q_ref[...])
        k = chunks(k_ref[...])

        def head_inputs(h):
            v = chunks(v_ref[:, h * HEAD:(h + 1) * HEAD])
            bcol = chunks(b_ref[h])
            return v, bcol, _delta_pre(q, k, v, bcol, chunks(gc_ref[h]), gr_ref[h], gl_ref[h], causal, strict)

        for h in range(rep):
            _, _, pre = head_inputs(h)
            sol = _hdot(tm_ref[h], pre["rhs"], _BNN)
            sol_sc[h] = sol
            vn_sc[h] = sol[:, :, :HEAD] - _bdot(sol[:, :, HEAD:], s_ref[h], _BNN)
            p_sc[h] = pre["p"]
            kkb_sc[h] = pre["kkb"]

        def state_step(it, carry):
            n = ncb - 1 - it
            rows = pl.ds(pl.multiple_of(n * CA, CA), CA)
            qn = q_ref[rows, :].astype(F32)
            kn = k_ref[rows, :].astype(F32)
            for h in range(rep):
                gc = gc_ref[h, rows, :]
                gl = gl_ref[h, n]
                ds = dstate[h]
                ds_sc[h, n] = ds
                dov = do_ref[rows, h * HEAD:(h + 1) * HEAD].astype(F32)
                dvn = _bdot(p_sc[h, n], dov, _TN) + _bdot(kn * jnp.exp(gl - gc), ds)
                dvn_sc[h, n] = dvn
                dstate[h] = (ds * jnp.exp(gl) + _bdot(qn * jnp.exp(gc), dov, _TN)
                             - _bdot(sol_sc[h, n, :, HEAD:], dvn, _TN))
            return carry

        lax.fori_loop(0, ncb, state_step, 0)

        kkr = _bdot(k, k, _BNT)
        dq = jnp.zeros((ncb, CA, HEAD), F32)
        dk = jnp.zeros((ncb, CA, HEAD), F32)
        for h in range(rep):
            v, bcol, pre = head_inputs(h)
            eg, dm, kb, qd, kd, cd = pre["eg"], pre["dm"], pre["kb"], pre["qd"], pre["kd"], pre["cd"]
            p = p_sc[h]
            sol = sol_sc[h]
            s = s_ref[h].astype(F32)
            ds = ds_sc[h]
            dov = chunks(do_ref[:, h * HEAD:(h + 1) * HEAD])
            v_new = vn_sc[h]
            dvn = dvn_sc[h]

            dp = jnp.where(causal, _bdot(dov, v_new, _BNT), 0.0)
            dqd = _bdot(dov, s, _BNT)
            dkd = _bdot(v_new, ds, _BNT)
            dcd = jnp.sum(rowsum(s * ds), axis=1, keepdims=True)
            dw = -_bdot(dvn, s, _BNT)

            drhs = _hdot(tm_ref[h], jnp.concatenate([dvn, dw], axis=2), _BTN)
            dbv, dbke = drhs[:, :, :HEAD], drhs[:, :, HEAD:]
            da = -jnp.where(strict, _bdot(drhs, sol, _BNT), 0.0)
            m = da * dm
            e = m * kkb_sc[h] + dp * p
            dgam = rowsum(e) - colsum(e) + rowsum(dbke * kb * eg) + rowsum(dqd * qd)
            r = rowsum(dkd * kd)
            tot = jnp.sum(r, axis=1, keepdims=True) + dcd * cd
            dgam = dgam - r + jnp.where(last, tot, 0.0)
            dbeta = rowsum(m * kkr) + rowsum(dbv * v) + rowsum(dbke * eg * k)
            nm = m * bcol[:, :, :CA]
            dqk = dp * dm
            dq = dq + _bdot(dqk, k, _BNN) + eg * dqd
            dk = (dk + _bdot(nm, k, _BNN) + _bdot(nm, k, _BTN) + _bdot(dqk, q, _BTN) + bcol * eg * dbke
                  + pre["ekd"] * dkd)
            dv_ref[:, h * HEAD:(h + 1) * HEAD] = rows_of(bcol * dbv)
            db_ref[h] = rows_of(jnp.broadcast_to(dbeta, (ncb, CA, LANES)))
            dg_ref[h] = rows_of(jnp.broadcast_to(dgam, (ncb, CA, LANES)))
        dq_ref[...] = rows_of(dq)
        dk_ref[...] = rows_of(dk)

    qoff, koff, voff = 0, hqk, 2 * hqk // rep
    rv = lambda i: nb - 1 - i
    hd = pl.BlockSpec((rep, rb, LANES), lambda j, i: (j, rv(i), 0))
    qk_out = pl.BlockSpec((rb, HEAD), lambda j, i: (rv(i), j))
    v_blk = pl.BlockSpec((rb, rep * HEAD), lambda j, i: (rv(i), j))
    per_chunk = lambda *shape: pl.BlockSpec((rep, ncb) + shape, lambda j, i: (j, rv(i), 0, 0))
    return _call(
        body, "delta_bwd", (hqk, nb),
        [pl.BlockSpec((rb, HEAD), lambda j, i: (rv(i), qoff + j)),
         pl.BlockSpec((rb, HEAD), lambda j, i: (rv(i), koff + j)),
         pl.BlockSpec((rb, rep * HEAD), lambda j, i: (rv(i), voff + j)),
         hd, hd, per_chunk(1, CA), per_chunk(1, LANES), per_chunk(HEAD, HEAD), per_chunk(CA, CA), v_blk],
        [qk_out, qk_out, v_blk, hd, hd],
        [jax.ShapeDtypeStruct((t, hqk * HEAD), F32), jax.ShapeDtypeStruct((t, hqk * HEAD), F32),
         jax.ShapeDtypeStruct((t, hv * HEAD), F32),
         jax.ShapeDtypeStruct((hv, t, LANES), F32), jax.ShapeDtypeStruct((hv, t, LANES), F32)],
        (qkvn, qkvn, qkvn, beta_b, gam_b, gam_r, gam_l, s_all, tm_all, do),
        scratch=[pltpu.VMEM((rep, HEAD, HEAD), F32), pltpu.VMEM((rep, ncb, CA, 2 * HEAD), F32),
                 pltpu.VMEM((rep, ncb, CA, HEAD), F32), pltpu.VMEM((rep, ncb, CA, CA), F32),
                 pltpu.VMEM((rep, ncb, CA, CA), F32), pltpu.VMEM((rep, ncb, CA, HEAD), F32),
                 pltpu.VMEM((rep, ncb, HEAD, HEAD), F32)],
        sem=("parallel", "arbitrary"), jobs=jobs)


def _apost_fwd(o, proj, gain, zoff, hv):
    t = o.shape[0]
    tb = _tile(t, 1024, 8)
    zb = zoff // HEAD

    def body(o_ref, z_ref, g_ref, y_ref):
        ov = o_ref[...]
        z = z_ref[...].astype(F32)
        r = lax.rsqrt(jnp.mean(ov * ov, axis=-1, keepdims=True) + EPS)
        y_ref[...] = (ov * r * g_ref[...] * (z * _sigmoid(z))).astype(BF16)

    blk = pl.BlockSpec((tb, HEAD), lambda i, h: (i, h))
    return pl.pallas_call(
        body, name="apost_fwd", grid=(t // tb, hv),
        in_specs=[blk, pl.BlockSpec((tb, HEAD), lambda i, h: (i, zb + h)), pl.BlockSpec((1, HEAD), lambda i, h: (0, 0))],
        out_specs=blk, out_shape=jax.ShapeDtypeStruct((t, hv * HEAD), BF16),
        compiler_params=_params(("parallel", "parallel")))(o, proj, gain.reshape(1, HEAD))


def _apost_bwd(o, proj, gain, dy, zoff, hv):
    t = o.shape[0]
    tb = _tile(t, 1024, 8)
    zb = zoff // HEAD

    def body(o_ref, z_ref, g_ref, dy_ref, do_ref, dz_ref, dg_ref):
        @pl.when(jnp.logical_and(pl.program_id(0) == 0, pl.program_id(1) == 0))
        def _():
            dg_ref[...] = jnp.zeros_like(dg_ref)

        ov = o_ref[...]
        z = z_ref[...].astype(F32)
        d = dy_ref[...].astype(F32)
        r = lax.rsqrt(jnp.mean(ov * ov, axis=-1, keepdims=True) + EPS)
        oh = ov * r
        sz = z * _sigmoid(z)
        dn = d * sz
        dz_ref[...] = (d * oh * g_ref[...] * _silu_grad(z)).astype(BF16)
        doh = dn * g_ref[...]
        do_ref[...] = r * (doh - oh * jnp.mean(doh * oh, axis=-1, keepdims=True))
        dg_ref[...] += jnp.sum(dn * oh, axis=0, keepdims=True)

    blk = pl.BlockSpec((tb, HEAD), lambda i, h: (i, h))
    vec = pl.BlockSpec((1, HEAD), lambda i, h: (0, 0))
    return pl.pallas_call(
        body, name="apost_bwd", grid=(t // tb, hv),
        in_specs=[blk, pl.BlockSpec((tb, HEAD), lambda i, h: (i, zb + h)), vec, blk],
        out_specs=[blk, blk, vec],
        out_shape=[jax.ShapeDtypeStruct((t, hv * HEAD), F32), jax.ShapeDtypeStruct((t, hv * HEAD), BF16),
                   jax.ShapeDtypeStruct((1, HEAD), F32)],
        compiler_params=_params(("arbitrary", "arbitrary")))(o, proj, gain.reshape(1, HEAD), dy)


def _sgu_fwd(proj, gain, w_s, b_t, uoff, wb):
    t = proj.shape[0]
    ng = wb // HEAD

    def body(u_ref, v_ref, g_ref, w_ref, b_ref, o_ref):
        r_i = lax.broadcasted_iota(jnp.int32, (HEAD, HEAD), 0)
        c_i = lax.broadcasted_iota(jnp.int32, (HEAD, HEAD), 1)
        u = _gelu(u_ref[...].astype(F32))
        vg = _gelu(v_ref[...].astype(F32))
        vn = vg * lax.rsqrt(jnp.mean(vg * vg, axis=-1, keepdims=True) + EPS) * g_ref[...]
        for g in range(ng):
            cols = slice(g * HEAD, (g + 1) * HEAD)
            wg = jnp.where(r_i >= c_i, w_ref[g], 0.0)
            mixed = _bdot(wg, vn[:, cols]) + b_ref[:, g:g + 1]
            o_ref[:, cols] = (u[:, cols] * mixed).astype(BF16)

    ub, vb = uoff // wb, uoff // wb + 1
    return pl.pallas_call(
        body, name="sgu_fwd", grid=(t // HEAD,),
        in_specs=[pl.BlockSpec((HEAD, wb), lambda i: (i, ub)), pl.BlockSpec((HEAD, wb), lambda i: (i, vb)),
                  pl.BlockSpec((1, wb), lambda i: (0, 0)), pl.BlockSpec((ng, HEAD, HEAD), lambda i: (0, 0, 0)),
                  pl.BlockSpec((HEAD, ng), lambda i: (0, 0))],
        out_specs=pl.BlockSpec((HEAD, wb), lambda i: (i, 0)), out_shape=jax.ShapeDtypeStruct((t, wb), BF16),
        compiler_params=_params(("parallel",)))(proj, proj, gain.reshape(1, wb), w_s, b_t)


def _sgu_bwd(proj, gain, w_s, b_t, dout, uoff, wb):
    t = proj.shape[0]
    ng = wb // HEAD

    def body(u_ref, v_ref, g_ref, w_ref, b_ref, d_ref, du_ref, dv_ref, dw_ref, db_ref, dg_ref, dvn_ref):
        @pl.when(pl.program_id(0) == 0)
        def _():
            dw_ref[...] = jnp.zeros_like(dw_ref)
            db_ref[...] = jnp.zeros_like(db_ref)
            dg_ref[...] = jnp.zeros_like(dg_ref)

        r_i = lax.broadcasted_iota(jnp.int32, (HEAD, HEAD), 0)
        c_i = lax.broadcasted_iota(jnp.int32, (HEAD, HEAD), 1)
        tril = r_i >= c_i
        ub = u_ref[...].astype(F32)
        vb = v_ref[...].astype(F32)
        u = _gelu(ub)
        vg = _gelu(vb)
        r = lax.rsqrt(jnp.mean(vg * vg, axis=-1, keepdims=True) + EPS)
        vh = vg * r
        vn = vh * g_ref[...]
        d = d_ref[...].astype(F32)
        for g in range(ng):
            cols = slice(g * HEAD, (g + 1) * HEAD)
            wg = jnp.where(tril, w_ref[g], 0.0)
            mixed = _bdot(wg, vn[:, cols]) + b_ref[:, g:g + 1]
            du_ref[:, cols] = (d[:, cols] * mixed * _gelu_grad(ub[:, cols])).astype(BF16)
            dmix = d[:, cols] * u[:, cols]
            dw_ref[g] += jnp.where(tril, _bdot(dmix, vn[:, cols], _NT), 0.0)
            db_ref[g] += jnp.broadcast_to(jnp.sum(dmix, axis=1, keepdims=True), (HEAD, HEAD))
            dvn_ref[:, cols] = _bdot(wg, dmix, _TN)
        dvn = dvn_ref[...]
        dg_ref[...] += jnp.sum(dvn * vh, axis=0, keepdims=True)
        dvh = dvn * g_ref[...]
        dvg = r * (dvh - vh * jnp.mean(dvh * vh, axis=-1, keepdims=True))
        dv_ref[...] = (dvg * _gelu_grad(vb)).astype(BF16)

    ub_i, vb_i = uoff // wb, uoff // wb + 1
    row = pl.BlockSpec((HEAD, wb), lambda i: (i, 0))
    mat = pl.BlockSpec((ng, HEAD, HEAD), lambda i: (0, 0, 0))
    vec = pl.BlockSpec((1, wb), lambda i: (0, 0))
    return pl.pallas_call(
        body, name="sgu_bwd", grid=(t // HEAD,),
        in_specs=[pl.BlockSpec((HEAD, wb), lambda i: (i, ub_i)), pl.BlockSpec((HEAD, wb), lambda i: (i, vb_i)),
                  vec, mat, pl.BlockSpec((HEAD, ng), lambda i: (0, 0)), row],
        out_specs=[row, row, mat, mat, vec],
        out_shape=[jax.ShapeDtypeStruct((t, wb), BF16), jax.ShapeDtypeStruct((t, wb), BF16),
                   jax.ShapeDtypeStruct((ng, HEAD, HEAD), F32), jax.ShapeDtypeStruct((ng, HEAD, HEAD), F32),
                   jax.ShapeDtypeStruct((1, wb), F32)],
        scratch_shapes=[pltpu.VMEM((HEAD, wb), F32)],
        compiler_params=_params(("arbitrary",)))(proj, proj, gain.reshape(1, wb), w_s, b_t, dout)


def _merge_specs(t, d, goff):
    tb = _tile(t, 512, 8)
    tc = _tile(d, 512)
    gb = goff // tc
    nd = d // tc
    blk = pl.BlockSpec((tb, tc), lambda i, j: (i, j))
    ga = pl.BlockSpec((tb, tc), lambda i, j: (i, gb + j))
    gbs = pl.BlockSpec((tb, tc), lambda i, j: (i, gb + nd + j))
    return (t // tb, nd), blk, ga, gbs


def _merge_fwd(ya, yb, proj, goff):
    t, d = ya.shape
    grid, blk, ga, gbs = _merge_specs(t, d, goff)

    def body(ya_ref, yb_ref, ga_ref, gb_ref, o_ref):
        o_ref[...] = (_sigmoid(ga_ref[...].astype(F32)) * ya_ref[...].astype(F32)
                      + _sigmoid(gb_ref[...].astype(F32)) * yb_ref[...].astype(F32)).astype(BF16)

    return pl.pallas_call(
        body, name="merge_fwd", grid=grid, in_specs=[blk, blk, ga, gbs], out_specs=blk,
        out_shape=jax.ShapeDtypeStruct((t, d), BF16),
        compiler_params=_params(("parallel", "parallel")))(ya, yb, proj, proj)


def _merge_bwd(dm, ya, yb, proj, goff):
    t, d = ya.shape
    grid, blk, ga, gbs = _merge_specs(t, d, goff)

    def body(dm_ref, ya_ref, yb_ref, ga_ref, gb_ref, dya_ref, dyb_ref, dga_ref, dgb_ref):
        dmv = dm_ref[...].astype(F32)
        sa = _sigmoid(ga_ref[...].astype(F32))
        sb = _sigmoid(gb_ref[...].astype(F32))
        dya_ref[...] = (dmv * sa).astype(BF16)
        dyb_ref[...] = (dmv * sb).astype(BF16)
        dga_ref[...] = (dmv * ya_ref[...].astype(F32) * sa * (1.0 - sa)).astype(BF16)
        dgb_ref[...] = (dmv * yb_ref[...].astype(F32) * sb * (1.0 - sb)).astype(BF16)

    shp = jax.ShapeDtypeStruct((t, d), BF16)
    return pl.pallas_call(
        body, name="merge_bwd", grid=grid, in_specs=[blk, blk, blk, ga, gbs], out_specs=[blk] * 4,
        out_shape=[shp] * 4, compiler_params=_params(("parallel", "parallel")))(dm, ya, yb, proj, proj)


def _ffn_act_fwd(up, conv_w, bias, dff):
    t = up.shape[0]
    nblk = dff // HEAD
    kw = conv_w.shape[0]

    def body(g_ref, v_ref, wg_ref, wv_ref, bg_ref, bv_ref, o_ref):
        rows = lax.broadcasted_iota(jnp.int32, (t, HEAD), 0)
        cg = _conv(g_ref[...].astype(F32), wg_ref[...], rows) + bg_ref[...]
        cv = _conv(v_ref[...].astype(F32), wv_ref[...], rows) + bv_ref[...]
        o_ref[...] = (cg * _sigmoid(cg) * cv).astype(BF16)

    return pl.pallas_call(
        body, name="ffn_act_fwd", grid=(nblk,),
        in_specs=[pl.BlockSpec((t, HEAD), lambda j: (0, j)), pl.BlockSpec((t, HEAD), lambda j: (0, nblk + j)),
                  pl.BlockSpec((kw, HEAD), lambda j: (0, j)), pl.BlockSpec((kw, HEAD), lambda j: (0, nblk + j)),
                  pl.BlockSpec((1, HEAD), lambda j: (0, j)), pl.BlockSpec((1, HEAD), lambda j: (0, nblk + j))],
        out_specs=pl.BlockSpec((t, HEAD), lambda j: (0, j)), out_shape=jax.ShapeDtypeStruct((t, dff), BF16),
        compiler_params=_params(("parallel",)))(up, up, conv_w, conv_w, bias, bias)


def _ffn_act_bwd(up, dact, conv_w, bias, dff, jobs=()):
    t = up.shape[0]
    nblk = dff // HEAD
    kw = conv_w.shape[0]

    def body(me_ref, pa_ref, d_ref, wm_ref, wp_ref, bm_ref, bp_ref, dup_ref, dw_ref, db_ref):
        is_gate = pl.program_id(0) < nblk
        rows = lax.broadcasted_iota(jnp.int32, (t, HEAD), 0)
        xv = me_ref[...].astype(F32)
        w = wm_ref[...]
        cp = _conv(pa_ref[...].astype(F32), wp_ref[...], rows) + bp_ref[...]
        d = d_ref[...].astype(F32)
        bm = bm_ref[...]
        dc = lax.cond(is_gate, lambda: d * cp * _silu_grad(_conv(xv, w, rows) + bm), lambda: d * (cp * _sigmoid(cp)))
        db_ref[...] = jnp.sum(dc, axis=0, keepdims=True)
        dx = dc * w[kw - 1:kw, :]
        dw_ref[kw - 1:kw, :] = jnp.sum(dc * xv, axis=0, keepdims=True)
        for s in range(1, kw):
            dx = dx + _shift_up(dc, s, rows) * w[kw - 1 - s:kw - s, :]
            dw_ref[kw - 1 - s:kw - s, :] = jnp.sum(dc * _shift_down(xv, s, rows), axis=0, keepdims=True)
        dup_ref[...] = dx.astype(BF16)

    part = lambda j: (j + nblk) % (2 * nblk)
    me = pl.BlockSpec((t, HEAD), lambda j: (0, j))
    wme = pl.BlockSpec((kw, HEAD), lambda j: (0, j))
    bme = pl.BlockSpec((1, HEAD), lambda j: (0, j))
    return _call(
        body, "ffn_act_bwd", (2 * nblk,),
        [me, pl.BlockSpec((t, HEAD), lambda j: (0, part(j))), pl.BlockSpec((t, HEAD), lambda j: (0, j % nblk)),
         wme, pl.BlockSpec((kw, HEAD), lambda j: (0, part(j))),
         bme, pl.BlockSpec((1, HEAD), lambda j: (0, part(j)))],
        [me, wme, bme],
        [jax.ShapeDtypeStruct((t, 2 * dff), BF16), jax.ShapeDtypeStruct((kw, 2 * dff), F32),
         jax.ShapeDtypeStruct((1, 2 * dff), F32)],
        (up, up, dact, conv_w, conv_w, bias, bias), sem=("parallel",), jobs=jobs)


def _ple_fwd(x, gt, pp):
    t, d = x.shape
    tb, tc = _tile(t, 512, 8), _tile(d, 1024)

    def body(x_ref, g_ref, p_ref, o_ref):
        o_ref[...] = x_ref[...] + _sigmoid(g_ref[...].astype(F32)) * p_ref[...].astype(F32)

    blk = pl.BlockSpec((tb, tc), lambda i, j: (i, j))
    return pl.pallas_call(
        body, name="ple_fwd", grid=(t // tb, d // tc), in_specs=[blk, blk, blk], out_specs=blk,
        out_shape=jax.ShapeDtypeStruct((t, d), F32), compiler_params=_params(("parallel", "parallel")))(x, gt, pp)


def _ple_bwd(dx, gt, pp):
    t, d = dx.shape
    tb, tc = _tile(t, 512, 8), _tile(d, 1024)

    def body(dx_ref, g_ref, p_ref, dg_ref, dp_ref):
        dv = dx_ref[...]
        s = _sigmoid(g_ref[...].astype(F32))
        dg_ref[...] = (dv * p_ref[...].astype(F32) * s * (1.0 - s)).astype(BF16)
        dp_ref[...] = (dv * s).astype(BF16)

    blk = pl.BlockSpec((tb, tc), lambda i, j: (i, j))
    shp = jax.ShapeDtypeStruct((t, d), BF16)
    return pl.pallas_call(
        body, name="ple_bwd", grid=(t // tb, d // tc), in_specs=[blk, blk, blk], out_specs=[blk, blk],
        out_shape=[shp, shp], compiler_params=_params(("parallel", "parallel")))(dx, gt, pp)


def _adam(pieces, w, m, v, name, jobs=()):
    nq = len(pieces)
    npart, rp, c = pieces[0].shape
    per_layer = nq // w.shape[0]
    row_bytes = 2 * c * (nq * npart * pieces[0].dtype.itemsize + 7 * 4)
    tr = _tile(rp, max(16, min(512, ADAM_VMEM_BUDGET // row_bytes)), 16)
    nblk = rp // tr
    c1 = 1.0 - ADAM_B1 ** ADAM_STEP
    c2 = 1.0 - ADAM_B2 ** ADAM_STEP

    def body(*refs):
        p_refs = refs[:nq]
        w_ref, m_ref, v_ref, g_ref, d_ref, mo_ref, vo_ref = refs[nq:]
        for q in range(nq):
            @pl.when(pl.program_id(0) == q)
            def _(p_ref=p_refs[q]):
                g = p_ref[0].astype(F32)
                for i in range(1, npart):
                    g = g + p_ref[i].astype(F32)
                mn = ADAM_B1 * m_ref[...] + (1.0 - ADAM_B1) * g
                vn = ADAM_B2 * v_ref[...] + (1.0 - ADAM_B2) * (g * g)
                g_ref[...] = g
                mo_ref[...] = mn
                vo_ref[...] = vn
                d_ref[...] = -ADAM_LR * ((mn / c1) / (jnp.sqrt(vn / c2) + ADAM_EPS) + ADAM_WD * w_ref[...])

    def piece_spec(q):
        return pl.BlockSpec((npart, tr, c), lambda i, r: (0, jnp.where(i == q, r, jnp.where(i < q, 0, nblk - 1)), 0))

    blk = pl.BlockSpec((None, tr, c), lambda i, r: (i // per_layer, (i % per_layer) * nblk + r, 0))
    shp = jax.ShapeDtypeStruct(w.shape, F32)
    return _call(body, name, (nq, nblk), [piece_spec(q) for q in range(nq)] + [blk] * 3, [blk] * 4, [shp] * 4,
                 (*pieces, w, m, v), sem=("parallel", "parallel"), jobs=jobs)


_BIG = ("w_in", "w_branch_a", "w_branch_b", "w_out", "w_ffn_up", "w_ffn_down", "w_ple_gate", "w_ple_proj")
_COL_SHARDED = ("w_in", "w_branch_b", "w_ffn_up", "w_ple_proj")
_CONVS = ("conv_qkv", "conv_ffn")
_GATHER_ON_PROJ = ("w_ffn_up",)
_GATHER_ON_DELTA = ("w_ffn_down",)
_GATHER_AHEAD = ("w_in", "conv_qkv")
_GATHER_ON_UP = ("w_ple_gate", "w_ple_proj")
_GATHER_AHEAD_2 = ("w_branch_a", "w_branch_b", "w_out", "conv_ffn")
_SCATTER_ON_DACT = ("w_ple_gate", "w_ple_proj")
_SCATTER_ON_DELTA = ("w_ffn_up",)
_SCATTER_ON_DW_MAIN = ("w_out", "w_branch_a", "w_branch_b")
_SCATTER_ON_DH1 = ("w_ffn_down",)
_SMALL = ("norm_mix", "conv_qkv", "a_log", "dt_bias", "head_norm", "sgu_norm", "w_spatial", "b_spatial", "norm_ffn",
          "conv_ffn", "b_conv_ffn", "norm_ple", "norm_final")
_WEIGHTS = ("norm_mix", "w_in", "conv_qkv", "a_log", "dt_bias", "head_norm", "sgu_norm", "w_spatial", "b_spatial",
            "w_branch_a", "w_branch_b", "w_out", "norm_ffn", "w_ffn_up", "conv_ffn", "b_conv_ffn", "w_ffn_down",
            "norm_ple", "w_ple_gate", "w_ple_proj", "norm_final")


def _full_cols(g):
    return jnp.transpose(g, (1, 0, 2)).reshape(g.shape[1], N_DEV * g.shape[2])


def _full_rows(g):
    return g.reshape(N_DEV * g.shape[1], g.shape[2])


def _split_cols(dw):
    k, n = dw.shape
    return jnp.transpose(dw.reshape(k, N_DEV, n // N_DEV), (1, 0, 2))


def _split_rows(dw):
    k, n = dw.shape
    return dw.reshape(N_DEV, k // N_DEV, n)


def _pad_lanes(v, width=LANES, offset=0):
    return jnp.pad(v, ((0, 0), (offset, width - offset - v.shape[1])))


def kernel(x, p, norm_mix, w_in, conv_qkv, a_log, dt_bias, head_norm, sgu_norm, w_spatial, b_spatial, w_branch_a, w_branch_b, w_out, norm_ffn, w_ffn_up, conv_ffn, b_conv_ffn, w_ffn_down, norm_ple, w_ple_gate, w_ple_proj, norm_final, loss_target, m_norm_mix, m_w_in, m_conv_qkv, m_a_log, m_dt_bias, m_head_norm, m_sgu_norm, m_w_spatial, m_b_spatial, m_w_branch_a, m_w_branch_b, m_w_out, m_norm_ffn, m_w_ffn_up, m_conv_ffn, m_b_conv_ffn, m_w_ffn_down, m_norm_ple, m_w_ple_gate, m_w_ple_proj, m_norm_final, v_norm_mix, v_w_in, v_conv_qkv, v_a_log, v_dt_bias, v_head_norm, v_sgu_norm, v_w_spatial, v_b_spatial, v_w_branch_a, v_w_branch_b, v_w_out, v_norm_ffn, v_w_ffn_up, v_conv_ffn, v_b_conv_ffn, v_w_ffn_down, v_norm_ple, v_w_ple_gate, v_w_ple_proj, v_norm_final):
    env = dict(locals())
    wts = {n: env[n] for n in _WEIGHTS}
    mom_m = {n: env["m_" + n] for n in _WEIGHTS}
    mom_v = {n: env["v_" + n] for n in _WEIGHTS}

    xin = x[0]
    tgt = loss_target[0]
    t, d = xin.shape
    depth = w_in.shape[0]
    hv = a_log.shape[1]
    vw = hv * HEAD
    wb = sgu_norm.shape[1]
    ng = w_spatial.shape[1]
    n_in = w_in.shape[2] * N_DEV
    qk = (n_in - 2 * vw - 2 * hv - 2 * wb - 2 * d) // 2
    hqk = qk // HEAD
    dff = w_ffn_down.shape[1] * N_DEV
    cw = 2 * qk + vw
    o_z, o_ba = 2 * qk + vw, 2 * qk + 2 * vw
    o_ub = o_ba
    o_ga = o_ub + 2 * wb
    ns = w_in.shape[2]
    in_segments = ((0, 0, o_ba), (1, o_ba, o_ba + 2 * hv), (2, o_ba + 2 * hv, n_in))
    me = 4 * lax.axis_index("x") + 2 * lax.axis_index("y") + lax.axis_index("c")

    full = [dict() for _ in range(depth)]
    staged = {}

    def as_cols(n):
        return "_cols" if (n in _COL_SHARDED or n in _CONVS) and wts[n].shape[-1] % LANES == 0 else ""

    def chips(i, names):
        names = names if i < depth else ()
        return [(i, n) for n in names], [(wts[n][i].astype(BF16) if n in _BIG else wts[n][i], "chips" + as_cols(n))
                                         for n in names]

    def relay(keys):
        return list(keys), [(staged.pop(key), "relay" + as_cols(key[1])) for key in keys]

    def settle(chip_keys, relay_keys, results):
        for key, g in zip(chip_keys, results):
            staged[key] = g
        for (i, n), g in zip(relay_keys, results[len(chip_keys):]):
            keep_blocks = as_cols(n) or n == "w_in"
            full[i][n] = g if keep_blocks else _full_cols(g) if n in _COL_SHARDED or n in _CONVS else _full_rows(g)

    ck, cj = chips(0, _GATHER_AHEAD + _GATHER_AHEAD_2)
    settle(ck, [], _exchange(cj, "gather_first"))
    rk, rj = relay(ck)
    settle([], rk, _exchange(rj, "relay_first"))

    saved = []
    xc = xin
    for i in range(depth):
        fw = full[i]
        main, gates = [], []
        for dev, (seg, lo, hi) in itertools.product(range(N_DEV), in_segments):
            a, b = max(lo, dev * ns), min(hi, (dev + 1) * ns)
            if a < b:
                (gates if seg == 1 else main).append(fw["w_in"][dev][:, a - dev * ns:b - dev * ns])
        fw["w_main"] = jnp.concatenate(main, axis=1)
        fw["w_ba"] = _pad_lanes(jnp.concatenate(gates, axis=1))
        s = {"x0": xc}
        s["h1"] = _rms_fwd(xc, norm_mix[i], "rms_fwd")
        ck, cj = chips(i, _GATHER_ON_PROJ)
        s["proj"], got = _matmul(s["h1"], fw["w_main"], "nn", BF16, "mm_proj", jobs=cj)
        settle(ck, [], got)
        s["ba"] = _matmul(s["h1"], fw["w_ba"], "nn", F32, "mm_ba")
        s["qkvn"] = _qkv_fwd(s["proj"], fw["conv_qkv"], hqk, hqk)
        s["alog"] = _pad_lanes(a_log[i][None, :], offset=hv)
        s["dtb"] = _pad_lanes(dt_bias[i][None, :], offset=hv)
        bg = _gate_fwd(s["ba"], s["alog"], s["dtb"], hv)
        beta_t = bg[:, :hv].T
        gam_t = bg[:, hv:2 * hv].T
        s["beta_b"] = jnp.broadcast_to(beta_t[:, :, None], (hv, t, LANES))
        s["gam_b"] = jnp.broadcast_to(gam_t[:, :, None], (hv, t, LANES))
        s["gam_r"] = gam_t.reshape(hv, t // CA, 1, CA)
        s["gam_l"] = jnp.broadcast_to(s["gam_r"][:, :, :, CA - 1:], (hv, t // CA, 1, LANES))
        ck1, cj1 = chips(i, _GATHER_ON_DELTA)
        ck2, cj2 = chips(i + 1, _GATHER_AHEAD)
        rk, rj = relay([(i, n) for n in _GATHER_ON_PROJ])
        (s["o"], s["s_all"], s["tm_all"]), got = _delta_fwd(
            s["qkvn"], s["beta_b"], s["gam_b"], s["gam_r"], s["gam_l"], hqk, hv, jobs=cj1 + cj2 + rj)
        settle(ck1 + ck2, rk, got)
        s["outa"] = _apost_fwd(s["o"], s["proj"], head_norm[i], o_z, hv)
        s["b_t"] = b_spatial[i].T
        s["outb"] = _sgu_fwd(s["proj"], sgu_norm[i], w_spatial[i], s["b_t"], o_ub, wb)
        s["ya"] = _matmul(s["outa"], fw["w_branch_a"], "nn", BF16, "mm_ya")
        s["yb"] = _matmul(s["outb"], fw["w_branch_b"], "nn", BF16, "mm_yb")
        s["mg"] = _merge_fwd(s["ya"], s["yb"], s["proj"], o_ga)
        s["x1"] = _matmul(s["mg"], fw["w_out"], "nn", F32, "mm_out", res=xc)
        s["h2"] = _rms_fwd(s["x1"], norm_ffn[i], "rms_fwd")
        nxt = i + 1 < depth
        ck1, cj1 = chips(i, _GATHER_ON_UP)
        ck2, cj2 = chips(i + 1, _GATHER_AHEAD_2)
        rk, rj = relay([(i, n) for n in _GATHER_ON_DELTA] + ([(i + 1, n) for n in _GATHER_AHEAD] if nxt else []))
        s["up"], got = _matmul(s["h2"], fw["w_ffn_up"], "nn", BF16, "mm_up", jobs=cj1 + cj2 + rj)
        settle(ck1 + ck2, rk, got)
        s["bias"] = b_conv_ffn[i][None, :]
        s["act"] = _ffn_act_fwd(s["up"], fw["conv_ffn"], s["bias"], dff)
        rk, rj = relay([(i, n) for n in _GATHER_ON_UP] + ([(i + 1, n) for n in _GATHER_AHEAD_2] if nxt else []))
        s["x2"], got = _matmul(s["act"], fw["w_ffn_down"], "nn", F32, "mm_down", res=s["x1"], jobs=rj)
        settle([], rk, got)
        s["h3"] = _rms_fwd(s["x2"], norm_ple[i], "rms_fwd")
        s["gt"] = _matmul(s["h3"], fw["w_ple_gate"], "nn", BF16, "mm_gt")
        s["pp"] = _matmul(p[i, 0], fw["w_ple_proj"], "nn", BF16, "mm_pp")
        xc = _ple_fwd(s["x2"], s["gt"], s["pp"])
        saved.append(s)

    dx, g_norm_final, loss_part = _loss_head(xc, tgt, norm_final)

    small = {n: [None] * depth for n in _SMALL if n != "norm_final"}
    recv = {n: [None] * depth for n in _BIG}
    recv["w_in"] = [None] * (2 * depth)

    def scatter_jobs(gw, names):
        return [(gw[n], "scatter_cols") if as_cols(n) else
                (_split_cols(gw[n]) if n in _COL_SHARDED else _split_rows(gw[n]), "scatter") for n in names]

    def keep(i, names, results):
        for n, r in zip(names, results):
            recv[n][i] = r

    def carry(jobs, *args):
        return _matmul(*args, jobs=jobs) if jobs else (_matmul(*args), [])

    later = []
    for i in reversed(range(depth)):
        fw, s = full[i], saved[i]
        dgt, dpp = _ple_bwd(dx, s["gt"], s["pp"])
        gw = {"w_ple_gate": _matmul(s["h3"], dgt, "tn", BF16, "mm_dw_gt"),
              "w_ple_proj": _matmul(p[i, 0], dpp, "tn", BF16, "mm_dw_pp")}
        dh3 = _matmul(dgt, fw["w_ple_gate"], "nt", F32, "mm_dh3")
        dx, small["norm_ple"][i] = _rms_bwd(s["x2"], dh3, norm_ple[i], dx, "rms_bwd")

        dact, got = _matmul(dx, fw["w_ffn_down"], "nt", BF16, "mm_dact", jobs=scatter_jobs(gw, _SCATTER_ON_DACT))
        keep(i, _SCATTER_ON_DACT, got)
        gw["w_ffn_down"] = _matmul(s["act"], dx, "tn", BF16, "mm_dw_down")
        (dup, small["conv_ffn"][i], small["b_conv_ffn"][i]), _ = _ffn_act_bwd(s["up"], dact, fw["conv_ffn"], s["bias"], dff)
        gw["w_ffn_up"], got = carry(later[:1], s["h2"], dup, "tn", BF16, "mm_dw_up")
        keep(2 * i + 2, ("w_in",), got)
        dh2, got = carry(later[1:], dup, fw["w_ffn_up"], "nt", F32, "mm_dh2")
        keep(2 * i + 3, ("w_in",), got)
        dx, small["norm_ffn"][i] = _rms_bwd(s["x1"], dh2, norm_ffn[i], dx, "rms_bwd")

        dmg = _matmul(dx, fw["w_out"], "nt", BF16, "mm_dmg")
        gw["w_out"] = _matmul(s["mg"], dx, "tn", BF16, "mm_dw_out")
        dya, dyb, dga, dgb = _merge_bwd(dmg, s["ya"], s["yb"], s["proj"], o_ga)
        gw["w_branch_a"] = _matmul(s["outa"], dya, "tn", BF16, "mm_dw_a")
        gw["w_branch_b"] = _matmul(s["outb"], dyb, "tn", BF16, "mm_dw_b")
        douta = _matmul(dya, fw["w_branch_a"], "nt", BF16, "mm_douta")
        doutb = _matmul(dyb, fw["w_branch_b"], "nt", BF16, "mm_doutb")
        dub, dvb, small["w_spatial"][i], db_s, dsg = _sgu_bwd(s["proj"], sgu_norm[i], w_spatial[i], s["b_t"], doutb, o_ub, wb)
        small["b_spatial"][i] = db_s[:, :, 0]
        small["sgu_norm"][i] = dsg
        do, dz, small["head_norm"][i] = _apost_bwd(s["o"], s["proj"], head_norm[i], douta, o_z, hv)
        (dq, dk, dv, db_b, dg_b), got = _delta_bwd(
            s["qkvn"], s["beta_b"], s["gam_b"], s["gam_r"], s["gam_l"], s["s_all"], s["tm_all"], do, hqk, hv,
            jobs=scatter_jobs(gw, _SCATTER_ON_DELTA))
        keep(i, _SCATTER_ON_DELTA, got)
        dbg = _pad_lanes(jnp.concatenate([db_b[:, :, 0].T, dg_b[:, :, 0].T], axis=1))
        dba, dal, ddt = _gate_bwd(s["ba"], dbg, s["alog"], s["dtb"], hv)
        small["a_log"][i] = dal[:, hv:2 * hv]
        small["dt_bias"][i] = ddt[:, hv:2 * hv]
        dqkv_pre, small["conv_qkv"][i] = _qkv_bwd(s["proj"], jnp.concatenate([dq, dk, dv], axis=1), fw["conv_qkv"], hqk, hqk)
        dproj = jnp.concatenate([dqkv_pre, dz, dub, dvb, dga, dgb], axis=1)
        dw_main, got = _matmul(s["h1"], dproj, "tn", BF16, "mm_dw_main", jobs=scatter_jobs(gw, _SCATTER_ON_DW_MAIN))
        keep(i, _SCATTER_ON_DW_MAIN, got)
        dw_ba = _matmul(s["h1"], dba, "tn", BF16, "mm_dw_ba")
        dh1, got = _matmul(dproj, fw["w_main"], "nt", F32, "mm_dh1", jobs=scatter_jobs(gw, _SCATTER_ON_DH1))
        keep(i, _SCATTER_ON_DH1, got)
        dh1 = _matmul(dba, fw["w_ba"], "nt", F32, "mm_dh1_ba", res=dh1)
        dx, small["norm_mix"][i] = _rms_bwd(s["x0"], dh1, norm_mix[i], dx, "rms_bwd")

        sources = (dw_main, dw_ba, dw_main[:, o_ba:])
        later = []
        for rows in (slice(0, d // 2), slice(d // 2, d)):
            shards = []
            for dev in range(N_DEV):
                cuts = [(seg, max(lo, dev * ns), min(hi, (dev + 1) * ns)) for seg, lo, hi in in_segments]
                shards.append(jnp.concatenate([sources[seg][rows, a - lo:b - lo] for (seg, a, b), (_, lo, _)
                                               in zip(cuts, in_segments) if a < b], axis=1))
            later.append((jnp.stack(shards), True))


    rep_names = tuple(n for n in _SMALL if n not in _CONVS)
    stacked = {n: jnp.concatenate([jnp.reshape(a, (-1,)) for a in small[n]]) for n in small}
    stacked["norm_final"] = g_norm_final.reshape(-1)

    def padded(parts, mult, axis=0):
        flat = jnp.concatenate(parts, axis=axis)
        pad = -flat.shape[axis] % mult
        return jnp.pad(flat, [(0, 0)] * axis + [(0, pad)])

    rep_flat = padded([stacked[n] for n in rep_names] + [loss_part[0, :1]], 16 * LANES)
    conv_flat = [padded([stacked[n]], 8 * LANES) for n in _CONVS]
    packed = jnp.concatenate([rep_flat] + conv_flat).reshape(-1, LANES)

    outs_g, outs_d, outs_m, outs_v = {}, {}, {}, {}

    adam_jobs = {"w_ffn_up": later[:1], "w_ffn_down": later[1:], "w_in": [(packed, False)]}
    for n in sorted(_BIG, key=lambda name: name == "w_in"):
        (outs_g[n], outs_d[n], outs_m[n], outs_v[n]), got = _adam(
            recv[n], wts[n], mom_m[n], mom_v[n], "adam_" + n, jobs=adam_jobs.get(n, []))
        if n == "w_ffn_up":
            recv["w_in"][0] = got[0]
        elif n == "w_ffn_down":
            recv["w_in"][1] = got[0]
        elif n == "w_in":
            small_all = got[0].reshape(N_DEV, -1)

    n_rep = rep_flat.shape[0]
    pk = lambda src: padded([src[n].reshape(-1) for n in rep_names] + [jnp.zeros((1,), F32)], 16 * LANES).reshape(1, -1, LANES)
    res, _ = _adam([small_all[:, :n_rep].reshape(N_DEV, -1, LANES)], pk(wts), pk(mom_m), pk(mom_v), "adam_small")
    res = [r.reshape(-1) for r in res]
    off = 0
    for n in rep_names:
        shp = wts[n].shape
        size = math.prod(shp)
        outs_g[n], outs_d[n], outs_m[n], outs_v[n] = [r[off:off + size].reshape(shp) for r in res]
        off += size
    loss = res[0][off]

    off = n_rep
    for n, cf in zip(_CONVS, conv_flat):
        _, kw, cl = wts[n].shape
        part = small_all[:, off:off + depth * kw * cl * N_DEV].reshape(N_DEV, depth * kw, N_DEV, cl)
        off += cf.shape[0]
        part = lax.dynamic_index_in_dim(part, me, axis=2, keepdims=False)
        two_d = lambda a: a.reshape(1, depth * kw, cl)
        res, _ = _adam([part], two_d(wts[n]), two_d(mom_m[n]), two_d(mom_v[n]), "adam_" + n)
        outs_g[n], outs_d[n], outs_m[n], outs_v[n] = [r.reshape(wts[n].shape) for r in res]

    return (loss, dx[None], *[outs_g[n] for n in _WEIGHTS], *[outs_d[n] for n in _WEIGHTS],
            *[outs_m[n] for n in _WEIGHTS], *[outs_v[n] for n in _WEIGHTS])
```

```python
import functools
import itertools
import math

import jax
import jax.numpy as jnp
from jax import lax
from jax.experimental import pallas as pl
from jax.experimental.pallas import tpu as pltpu

F32 = jnp.float32
BF16 = jnp.bfloat16
EPS = 1e-6
LANES = 128
HEAD = 128
CA = 64
N_DEV = 8
VMEM_LIMIT = 48 * 1024 * 1024
ADAM_VMEM_BUDGET = 16 * 1024 * 1024
MESH = pl.DeviceIdType.MESH

ADAM_LR = 0.001
ADAM_B1 = 0.9
ADAM_B2 = 0.999
ADAM_EPS = 1e-08
ADAM_WD = 0.01
ADAM_STEP = 10


def _tile(n, cap, mult=LANES):
    best = None
    for t in range(mult, min(n, cap) + 1, mult):
        if n % t == 0:
            best = t
    return n if best is None else best


def _tile_near(n, target, mult=LANES):
    cands = [t for t in range(mult, min(n, target * 3 // 2) + 1, mult) if n % t == 0]
    return min(cands, key=lambda t: abs(t - target)) if cands else n


def _params(sem):
    return pltpu.CompilerParams(dimension_semantics=sem, vmem_limit_bytes=VMEM_LIMIT)


def _sigmoid(v):
    return jax.nn.sigmoid(v)


def _silu_grad(c):
    s = _sigmoid(c)
    return s + c * s * (1.0 - s)


_GELU_C = math.sqrt(2.0 / math.pi)


def _gelu(v):
    return 0.5 * v * (1.0 + jnp.tanh(_GELU_C * (v + 0.044715 * v * v * v)))


def _gelu_grad(v):
    t = jnp.tanh(_GELU_C * (v + 0.044715 * v * v * v))
    return 0.5 * (1.0 + t) + 0.5 * v * (1.0 - t * t) * _GELU_C * (1.0 + 3.0 * 0.044715 * v * v)


_NN = (((1,), (0,)), ((), ()))
_NT = (((1,), (1,)), ((), ()))
_TN = (((0,), (0,)), ((), ()))
_BNN = (((2,), (1,)), ((0,), (0,)))
_BNT = (((2,), (2,)), ((0,), (0,)))
_BTN = (((1,), (1,)), ((0,), (0,)))


def _bdot(a, b, dn=_NN):
    return lax.dot_general(a.astype(BF16), b.astype(BF16), dn, preferred_element_type=F32)


def _split(a):
    hi = a.astype(BF16)
    return hi, (a - hi.astype(F32)).astype(BF16)


def _dot3(ah, al, bh, bl, dn=_NN):
    def d(u, v):
        return lax.dot_general(u, v, dn, preferred_element_type=F32)
    return d(ah, bh) + (d(al, bh) + d(ah, bl))


def _hdot(a, b, dn=_NN):
    return _dot3(*_split(a), *_split(b), dn)


_PEERS = {"gather": (1, 2, 3, 4, 5, 6, 7), "scatter": (1, 2, 3, 4, 5, 6, 7), "chips": (1, 2, 4, 6), "relay": (2, 4, 6)}


def _kinds(jobs):
    return [{False: "gather", True: "scatter"}.get(kind, kind) for _, kind in jobs]


def _xchg_out_shapes(jobs):
    shapes = []
    for (a, _), kind in zip(jobs, _kinds(jobs)):
        shape = {"gather": (N_DEV,) + a.shape, "chips": (N_DEV,) + a.shape, "chips_cols": (a.shape[0], N_DEV * a.shape[1]),
                 "scatter_cols": (N_DEV, a.shape[0], a.shape[1] // N_DEV)}.get(kind, a.shape)
        shapes.append(jax.ShapeDtypeStruct(shape, a.dtype))
    return shapes


def _xchg_scratch(jobs):
    n = len(jobs)
    return [pltpu.SemaphoreType.DMA((n, N_DEV - 1)), pltpu.SemaphoreType.DMA((n, N_DEV - 1)), pltpu.SemaphoreType.DMA((n,))]


def _xchg_copies(kinds, src, out, sems):
    send_sems, recv_sems, local_sems = sems
    x, y, c = lax.axis_index("x"), lax.axis_index("y"), lax.axis_index("c")
    me = 4 * x + 2 * y + c

    def block(ref, idx, as_cols):
        if not as_cols:
            return ref.at[idx]
        width = ref.shape[1] // N_DEV
        return ref.at[:, pl.ds(pl.multiple_of(idx * width, LANES), width)]

    local, sends, recvs = [], [], []
    for m in range(N_DEV):
        px = lax.rem(x + ((m >> 2) & 1), 2)
        py = lax.rem(y + ((m >> 1) & 1), 2)
        pc = lax.rem(c + (m & 1), 2)
        peer = 4 * px + 2 * py + pc
        for k, kind in enumerate(kinds):
            base, cols = kind.split("_")[0], kind.endswith("_cols")
            if m == 0:
                if base != "relay":
                    mine = block(src[k], me, cols) if base == "scatter" else src[k]
                    local.append(pltpu.make_async_copy(mine, block(out[k], me, cols and base != "scatter"), local_sems.at[k]))
                continue
            if m not in _PEERS[base]:
                continue
            if base == "relay":
                to, mine = (x, y, 1 - c), block(src[k], peer, cols)
                there, here = block(out[k], peer, cols), block(out[k], 4 * px + 2 * py + 1 - c, cols)
            else:
                to, mine = (px, py, pc), block(src[k], peer, cols) if base == "scatter" else src[k]
                there, here = block(out[k], me, cols and base != "scatter"), block(out[k], peer, cols and base != "scatter")
            for dst, lst in ((there, sends), (here, recvs)):
                lst.append(pltpu.make_async_remote_copy(
                    src_ref=mine, dst_ref=dst, send_sem=send_sems.at[k, m - 1], recv_sem=recv_sems.at[k, m - 1],
                    device_id=to, device_id_type=MESH))
    return local, sends, recvs


def _xchg_start(scatter, src, out, sems):
    local, sends, _ = _xchg_copies(scatter, src, out, sems)
    for cp in local + sends:
        cp.start()


def _xchg_wait(scatter, src, out, sems):
    local, sends, recvs = _xchg_copies(scatter, src, out, sems)
    for cp in recvs:
        cp.wait_recv()
    for cp in sends:
        cp.wait_send()
    for cp in local:
        cp.wait()


_ANY = pl.BlockSpec(memory_space=pl.ANY)


def _xchg_aliases(jobs, n_in, n_out):
    return {n_in + k: n_out + k for k, kind in enumerate(_kinds(jobs)) if kind.startswith("relay")}


def _exchange(jobs, name):
    n = len(jobs)
    kinds = _kinds(jobs)

    def body(*refs):
        src, out, sems = refs[:n], refs[n:2 * n], refs[2 * n:]
        _xchg_start(kinds, src, out, sems)
        _xchg_wait(kinds, src, out, sems)

    return pl.pallas_call(
        body, name=name, in_specs=[_ANY] * n, out_specs=[_ANY] * n, out_shape=_xchg_out_shapes(jobs),
        scratch_shapes=_xchg_scratch(jobs), input_output_aliases=_xchg_aliases(jobs, 0, 0),
        compiler_params=pltpu.CompilerParams(has_side_effects=True))(*[a for a, _ in jobs])


def _carried(body, n_in, n_out, jobs, grid):
    if not jobs:
        return body
    nj = len(jobs)
    scatter = _kinds(jobs)

    def wrapped(*refs):
        ins, src = refs[:n_in], refs[n_in:n_in + nj]
        outs, got = refs[n_in + nj:n_in + nj + n_out], refs[n_in + nj + n_out:n_in + 2 * nj + n_out]
        rest = refs[n_in + 2 * nj + n_out:]
        scratch, sems = rest[:len(rest) - 3], rest[len(rest) - 3:]
        ids = [pl.program_id(a) for a in range(len(grid))]
        first = functools.reduce(jnp.logical_and, [i == 0 for i in ids])
        last = functools.reduce(jnp.logical_and, [i == g - 1 for i, g in zip(ids, grid)])

        @pl.when(first)
        def _():
            _xchg_start(scatter, src, got, sems)

        body(*ins, *outs, *scratch)

        @pl.when(last)
        def _():
            _xchg_wait(scatter, src, got, sems)

    return wrapped


def _call(body, name, grid, in_specs, out_specs, out_shape, args, scratch=(), sem=None, jobs=()):
    jobs = list(jobs)
    nj = len(jobs)
    sem = ("arbitrary",) * len(grid) if jobs or sem is None else sem
    res = pl.pallas_call(
        _carried(body, len(in_specs), len(out_specs), jobs, grid), name=name, grid=grid,
        in_specs=list(in_specs) + [_ANY] * nj, out_specs=list(out_specs) + [_ANY] * nj,
        out_shape=list(out_shape) + _xchg_out_shapes(jobs),
        scratch_shapes=list(scratch) + (_xchg_scratch(jobs) if jobs else []),
        input_output_aliases=_xchg_aliases(jobs, len(in_specs), len(out_specs)),
        compiler_params=_params(sem))(*args, *[a for a, _ in jobs])
    return res[:len(out_specs)], res[len(out_specs):]


MATMUL_OPERAND_VMEM = 20 * 1024 * 1024
MATMUL_TILE = 1024


def _matmul(a, b, mode, out_dtype, name, res=None, jobs=()):
    if mode == "tn":
        kdim, m = a.shape
    else:
        m, kdim = a.shape
    n = b.shape[0] if mode == "nt" else b.shape[1]
    tm, tn = _tile_near(m, MATMUL_TILE), _tile(n, MATMUL_TILE)
    per_k = 2 * (tm * a.dtype.itemsize + tn * b.dtype.itemsize)
    tk = _tile(kdim, max(LANES, MATMUL_OPERAND_VMEM // per_k))
    nk = kdim // tk
    dims = {"nn": _NN, "nt": _NT, "tn": _TN}[mode]

    def body(*refs):
        a_ref, b_ref = refs[:2]
        r_ref = refs[2] if res is not None else None
        o_ref = refs[3] if res is not None else refs[2]
        acc = refs[-1] if nk > 1 else None

        def write(r):
            if r_ref is not None:
                r = r + r_ref[...].astype(F32)
            o_ref[...] = r.astype(out_dtype)

        prod = _bdot(a_ref[...], b_ref[...], dims)
        if nk == 1:
            write(prod)
        else:
            k = pl.program_id(2)

            @pl.when(k == 0)
            def _():
                acc[...] = prod

            @pl.when(jnp.logical_and(k > 0, k < nk - 1))
            def _():
                acc[...] += prod

            @pl.when(k == nk - 1)
            def _():
                write(acc[...] + prod)

    a_spec = pl.BlockSpec((tk, tm), lambda i, j, k: (k, i)) if mode == "tn" else pl.BlockSpec((tm, tk), lambda i, j, k: (i, k))
    b_spec = pl.BlockSpec((tn, tk), lambda i, j, k: (j, k)) if mode == "nt" else pl.BlockSpec((tk, tn), lambda i, j, k: (k, j))
    o_spec = pl.BlockSpec((tm, tn), lambda i, j, k: (i, j))
    in_specs = [a_spec, b_spec] + ([o_spec] if res is not None else [])
    args = (a, b) + ((res,) if res is not None else ())
    (out,), got = _call(body, name, (m // tm, n // tn, nk), in_specs, [o_spec], [jax.ShapeDtypeStruct((m, n), out_dtype)],
                        args, scratch=[pltpu.VMEM((tm, tn), F32)] if nk > 1 else [],
                        sem=("parallel", "parallel", "arbitrary"), jobs=jobs)
    return (out, got) if jobs else out


def _rms_fwd(x, gain, name):
    t, d = x.shape
    tb = _tile(t, 256, 8)

    def body(x_ref, g_ref, h_ref):
        xv = x_ref[...]
        r = lax.rsqrt(jnp.mean(xv * xv, axis=-1, keepdims=True) + EPS)
        h_ref[...] = (xv * r * g_ref[...]).astype(BF16)

    return pl.pallas_call(
        body, name=name, grid=(t // tb,),
        in_specs=[pl.BlockSpec((tb, d), lambda i: (i, 0)), pl.BlockSpec((1, d), lambda i: (0, 0))],
        out_specs=pl.BlockSpec((tb, d), lambda i: (i, 0)), out_shape=jax.ShapeDtypeStruct((t, d), BF16),
        compiler_params=_params(("parallel",)))(x, gain.reshape(1, d))


def _rms_bwd(x, dh, gain, dres, name):
    t, d = x.shape
    tb = _tile(t, 256, 8)

    def body(x_ref, dh_ref, g_ref, dr_ref, dx_ref, dg_ref):
        @pl.when(pl.program_id(0) == 0)
        def _():
            dg_ref[...] = jnp.zeros_like(dg_ref)

        xv = x_ref[...]
        dy = dh_ref[...].astype(F32)
        r = lax.rsqrt(jnp.mean(xv * xv, axis=-1, keepdims=True) + EPS)
        xh = xv * r
        dxh = dy * g_ref[...]
        dx_ref[...] = dr_ref[...] + r * (dxh - xh * jnp.mean(dxh * xh, axis=-1, keepdims=True))
        dg_ref[...] += jnp.sum(dy * xh, axis=0, keepdims=True)

    row = pl.BlockSpec((tb, d), lambda i: (i, 0))
    vec = pl.BlockSpec((1, d), lambda i: (0, 0))
    return pl.pallas_call(
        body, name=name, grid=(t // tb,), in_specs=[row, row, vec, row], out_specs=[row, vec],
        out_shape=[jax.ShapeDtypeStruct((t, d), F32), jax.ShapeDtypeStruct((1, d), F32)],
        compiler_params=_params(("arbitrary",)))(x, dh, gain.reshape(1, d), dres)


def _loss_head(x, target, gain):
    t, d = x.shape
    tb = _tile(t, 256, 8)

    def body(x_ref, t_ref, g_ref, dx_ref, dg_ref, loss_ref):
        @pl.when(pl.program_id(0) == 0)
        def _():
            dg_ref[...] = jnp.zeros_like(dg_ref)
            loss_ref[...] = jnp.zeros_like(loss_ref)

        xv = x_ref[...]
        r = lax.rsqrt(jnp.mean(xv * xv, axis=-1, keepdims=True) + EPS)
        xh = xv * r
        err = xh * g_ref[...] - t_ref[...]
        per_row = jnp.mean(err * err, axis=-1, keepdims=True)
        loss_ref[...] += 0.5 * jnp.sum(per_row, axis=0, keepdims=True)
        dy = err * (1.0 / d)
        dxh = dy * g_ref[...]
        dx_ref[...] = r * (dxh - xh * jnp.mean(dxh * xh, axis=-1, keepdims=True))
        dg_ref[...] += jnp.sum(dy * xh, axis=0, keepdims=True)

    row = pl.BlockSpec((tb, d), lambda i: (i, 0))
    vec = pl.BlockSpec((1, d), lambda i: (0, 0))
    return pl.pallas_call(
        body, name="loss_head", grid=(t // tb,), in_specs=[row, row, vec],
        out_specs=[row, vec, pl.BlockSpec((1, LANES), lambda i: (0, 0))],
        out_shape=[jax.ShapeDtypeStruct((t, d), F32), jax.ShapeDtypeStruct((1, d), F32),
                   jax.ShapeDtypeStruct((1, LANES), F32)],
        compiler_params=_params(("arbitrary",)))(x, target, gain.reshape(1, d))


def _shift_down(v, s, rows):
    if s == 0:
        return v
    return jnp.where(rows >= s, pltpu.roll(v, s, 0), 0.0)


def _shift_up(v, s, rows):
    if s == 0:
        return v
    t = v.shape[0]
    return jnp.where(rows < t - s, pltpu.roll(v, t - s, 0), 0.0)


def _conv(v, w, rows):
    k = w.shape[0]
    out = v * w[k - 1:k, :]
    for s in range(1, k):
        out = out + _shift_down(v, s, rows) * w[k - 1 - s:k - s, :]
    return out


def _qkv_fwd(proj, conv_w, nq, nk):
    t = proj.shape[0]
    cw = conv_w.shape[1]
    nblk = cw // HEAD

    def body(p_ref, w_ref, o_ref):
        j = pl.program_id(0)
        rows = lax.broadcasted_iota(jnp.int32, (t, HEAD), 0)
        c = _conv(p_ref[...].astype(F32), w_ref[...], rows)
        a = c * _sigmoid(c)
        nrm = a * lax.rsqrt(jnp.sum(a * a, axis=-1, keepdims=True) + EPS)
        nrm = nrm * jnp.where(j < nq, HEAD ** -0.5, 1.0)
        o_ref[...] = jnp.where(j < nq + nk, nrm, a).astype(BF16)

    return pl.pallas_call(
        body, name="qkv_fwd", grid=(nblk,),
        in_specs=[pl.BlockSpec((t, HEAD), lambda j: (0, j)), pl.BlockSpec((conv_w.shape[0], HEAD), lambda j: (0, j))],
        out_specs=pl.BlockSpec((t, HEAD), lambda j: (0, j)), out_shape=jax.ShapeDtypeStruct((t, cw), BF16),
        compiler_params=_params(("parallel",)))(proj, conv_w)


def _qkv_bwd(proj, dqkv, conv_w, nq, nk):
    t = proj.shape[0]
    kw, cw = conv_w.shape
    nblk = cw // HEAD

    def body(p_ref, d_ref, w_ref, dp_ref, dw_ref):
        j = pl.program_id(0)
        rows = lax.broadcasted_iota(jnp.int32, (t, HEAD), 0)
        xv = p_ref[...].astype(F32)
        w = w_ref[...]
        c = _conv(xv, w, rows)
        a = c * _sigmoid(c)
        dy = d_ref[...].astype(F32)
        r = lax.rsqrt(jnp.sum(a * a, axis=-1, keepdims=True) + EPS)
        y = a * r
        scale = jnp.where(j < nq, HEAD ** -0.5, 1.0)
        da_n = scale * r * (dy - y * jnp.sum(dy * y, axis=-1, keepdims=True))
        da = jnp.where(j < nq + nk, da_n, dy)
        dc = da * _silu_grad(c)
        dx = dc * w[kw - 1:kw, :]
        dw_ref[kw - 1:kw, :] = jnp.sum(dc * xv, axis=0, keepdims=True)
        for s in range(1, kw):
            dx = dx + _shift_up(dc, s, rows) * w[kw - 1 - s:kw - s, :]
            dw_ref[kw - 1 - s:kw - s, :] = jnp.sum(dc * _shift_down(xv, s, rows), axis=0, keepdims=True)
        dp_ref[...] = dx.astype(BF16)

    blk = pl.BlockSpec((t, HEAD), lambda j: (0, j))
    wblk = pl.BlockSpec((kw, HEAD), lambda j: (0, j))
    return pl.pallas_call(
        body, name="qkv_bwd", grid=(nblk,), in_specs=[blk, blk, wblk], out_specs=[blk, wblk],
        out_shape=[jax.ShapeDtypeStruct((t, cw), BF16), jax.ShapeDtypeStruct((kw, cw), F32)],
        compiler_params=_params(("parallel",)))(proj, dqkv, conv_w)


def _softplus(v):
    return jnp.where(v < -15.0, jnp.exp(v), jnp.maximum(v, 0.0) + jnp.log(1.0 + jnp.exp(-jnp.abs(v))))


def _gate_fwd(ba, alog_pad, dtb_pad, hv):
    t = ba.shape[0]
    tb = _tile(t, 512, CA)

    def body(ba_ref, al_ref, dt_ref, o_ref):
        v = ba_ref[...]
        beta = _sigmoid(v)
        g = -jnp.exp(al_ref[...]) * _softplus(v + dt_ref[...])
        pos = lax.broadcasted_iota(jnp.int32, (tb, LANES), 0) % CA
        s = 1
        while s < CA:
            g = g + jnp.where(pos >= s, pltpu.roll(g, s, 0), 0.0)
            s *= 2
        lane = lax.broadcasted_iota(jnp.int32, (tb, LANES), 1)
        o_ref[...] = jnp.where(lane < hv, beta, g)

    row = pl.BlockSpec((tb, LANES), lambda i: (i, 0))
    vec = pl.BlockSpec((1, LANES), lambda i: (0, 0))
    return pl.pallas_call(
        body, name="gate_fwd", grid=(t // tb,), in_specs=[row, vec, vec], out_specs=row,
        out_shape=jax.ShapeDtypeStruct((t, LANES), F32), compiler_params=_params(("parallel",)))(ba, alog_pad, dtb_pad)


def _gate_bwd(ba, dbg, alog_pad, dtb_pad, hv):
    t = ba.shape[0]
    tb = _tile(t, 512, CA)

    def body(ba_ref, d_ref, al_ref, dt_ref, dba_ref, dal_ref, ddt_ref):
        @pl.when(pl.program_id(0) == 0)
        def _():
            dal_ref[...] = jnp.zeros_like(dal_ref)
            ddt_ref[...] = jnp.zeros_like(ddt_ref)

        v = ba_ref[...]
        d = d_ref[...]
        pos = lax.broadcasted_iota(jnp.int32, (tb, LANES), 0) % CA
        dg = d
        s = 1
        while s < CA:
            dg = dg + jnp.where(pos < CA - s, pltpu.roll(dg, tb - s, 0), 0.0)
            s *= 2
        beta = _sigmoid(v)
        na = -jnp.exp(al_ref[...])
        z = v + dt_ref[...]
        da = dg * na * _sigmoid(z)
        lane = lax.broadcasted_iota(jnp.int32, (tb, LANES), 1)
        in_a = jnp.logical_and(lane >= hv, lane < 2 * hv)
        da = jnp.where(in_a, da, 0.0)
        dba_ref[...] = jnp.where(lane < hv, d * beta * (1.0 - beta), da)
        ddt_ref[...] += jnp.sum(da, axis=0, keepdims=True)
        dal_ref[...] += jnp.sum(jnp.where(in_a, dg * na * _softplus(z), 0.0), axis=0, keepdims=True)

    row = pl.BlockSpec((tb, LANES), lambda i: (i, 0))
    vec = pl.BlockSpec((1, LANES), lambda i: (0, 0))
    return pl.pallas_call(
        body, name="gate_bwd", grid=(t // tb,), in_specs=[row, row, vec, vec], out_specs=[row, vec, vec],
        out_shape=[jax.ShapeDtypeStruct((t, LANES), F32), jax.ShapeDtypeStruct((1, LANES), F32),
                   jax.ShapeDtypeStruct((1, LANES), F32)],
        compiler_params=_params(("arbitrary",)))(ba, dbg, alog_pad, dtb_pad)


def _chunk_masks():
    r = lax.broadcasted_iota(jnp.int32, (CA, CA), 0)
    c = lax.broadcasted_iota(jnp.int32, (CA, CA), 1)
    return r >= c, r > c, (r == c).astype(F32)


def _inv_unit_lower(a, eye):
    x = eye - a
    ph, plo = _split(a)
    n = 1
    while n < CA // 2:
        ph, plo = _split(_dot3(ph, plo, ph, plo, _BNN))
        x = x + _dot3(*_split(x), ph, plo, _BNN)
        n *= 2
    return x


def _delta_pre(q, k, v, bcol, gc, gr, gl, causal, strict):
    eg = jnp.exp(gc)
    dm = jnp.exp(jnp.where(causal, gc[:, :, :CA] - gr, -jnp.inf))
    kb = k * bcol
    kkb = _bdot(kb, k, _BNT)
    a = jnp.where(strict, kkb * dm, 0.0)
    rhs = jnp.concatenate([v * bcol, kb * eg], axis=2)
    qk = _bdot(q, k, _BNT)
    ekd = jnp.exp(gl - gc)
    return dict(eg=eg, dm=dm, kb=kb, kkb=kkb, a=a, rhs=rhs, p=qk * dm, qd=q * eg, ekd=ekd, kd=k * ekd, cd=jnp.exp(gl))


DELTA_Q_HEADS_PER_BLOCK = 2


def _delta_fwd(qkvn, beta_b, gam_b, gam_r, gam_l, hqk, hv, jobs=()):
    t = qkvn.shape[0]
    rep = hv // hqk
    qpb = math.gcd(hqk, DELTA_Q_HEADS_PER_BLOCK)
    nh = qpb * rep
    rb = _tile(t, 512, CA)
    nb = t // rb
    ncb = rb // CA
    nc = t // CA

    def body(q_ref, k_ref, v_ref, b_ref, gc_ref, gr_ref, gl_ref, o_ref, s_ref, tm_ref, state, sol_sc, p_sc):
        @pl.when(pl.program_id(1) == 0)
        def _():
            state[...] = jnp.zeros_like(state)

        causal, strict, eye = _chunk_masks()

        def chunks(a):
            return a.astype(F32).reshape(ncb, CA, a.shape[-1])

        def head_cols(hh):
            return slice(hh * HEAD, (hh + 1) * HEAD)

        for hh in range(nh):
            qi = hh // rep
            pre = _delta_pre(chunks(q_ref[:, head_cols(qi)]), chunks(k_ref[:, head_cols(qi)]), chunks(v_ref[:, head_cols(hh)]),
                             chunks(b_ref[hh]), chunks(gc_ref[hh]), gr_ref[hh], gl_ref[hh], causal, strict)
            tm = _inv_unit_lower(pre["a"], eye)
            tm_ref[hh] = tm
            sol_sc[hh] = _hdot(tm, pre["rhs"], _BNN)
            p_sc[hh] = pre["p"]

        def chunk(n, carry):
            rows = pl.ds(pl.multiple_of(n * CA, CA), CA)
            for hh in range(nh):
                qi = hh // rep
                qn = q_ref[rows, head_cols(qi)].astype(F32)
                kn = k_ref[rows, head_cols(qi)].astype(F32)
                gc = gc_ref[hh, rows, :]
                gl = gl_ref[hh, n]
                s = state[hh]
                v_new = sol_sc[hh, n, :, :HEAD] - _bdot(sol_sc[hh, n, :, HEAD:], s)
                o_ref[rows, head_cols(hh)] = _bdot(qn * jnp.exp(gc), s) + _bdot(p_sc[hh, n], v_new)
                s_ref[hh, n] = s.astype(BF16)
                state[hh] = s * jnp.exp(gl) + _bdot(kn * jnp.exp(gl - gc), v_new, _TN)
            return carry

        lax.fori_loop(0, ncb, chunk, 0)

    koff, voff = hqk // qpb, 2 * hqk // nh
    per_chunk = lambda width: pl.BlockSpec((nh, ncb, 1, width), lambda j, i: (j, i, 0, 0))
    return _call(
        body, "delta_fwd", (hqk // qpb, nb),
        [pl.BlockSpec((rb, qpb * HEAD), lambda j, i: (i, j)),
         pl.BlockSpec((rb, qpb * HEAD), lambda j, i: (i, koff + j)),
         pl.BlockSpec((rb, nh * HEAD), lambda j, i: (i, voff + j)),
         pl.BlockSpec((nh, rb, LANES), lambda j, i: (j, i, 0)),
         pl.BlockSpec((nh, rb, LANES), lambda j, i: (j, i, 0)),
         per_chunk(CA), per_chunk(LANES)],
        [pl.BlockSpec((rb, nh * HEAD), lambda j, i: (i, j)),
         pl.BlockSpec((nh, ncb, HEAD, HEAD), lambda j, i: (j, i, 0, 0)),
         pl.BlockSpec((nh, ncb, CA, CA), lambda j, i: (j, i, 0, 0))],
        [jax.ShapeDtypeStruct((t, hv * HEAD), F32), jax.ShapeDtypeStruct((hv, nc, HEAD, HEAD), BF16),
         jax.ShapeDtypeStruct((hv, nc, CA, CA), F32)],
        (qkvn, qkvn, qkvn, beta_b, gam_b, gam_r, gam_l),
        scratch=[pltpu.VMEM((nh, HEAD, HEAD), F32), pltpu.VMEM((nh, ncb, CA, 2 * HEAD), F32),
                 pltpu.VMEM((nh, ncb, CA, CA), F32)],
        sem=("parallel", "arbitrary"), jobs=jobs)


def _delta_bwd(qkvn, beta_b, gam_b, gam_r, gam_l, s_all, tm_all, do, hqk, hv, jobs=()):
    t = qkvn.shape[0]
    rep = hv // hqk
    qpb = math.gcd(hqk, DELTA_Q_HEADS_PER_BLOCK)
    nh = qpb * rep
    rb = _tile(t, 512, CA)
    nb = t // rb
    ncb = rb // CA

    def body(q_ref, k_ref, v_ref, b_ref, gc_ref, gr_ref, gl_ref, s_ref, tm_ref, do_ref,
             dq_ref, dk_ref, dv_ref, db_ref, dg_ref, dstate, sol_sc, vn_sc, p_sc, kkb_sc, dvn_sc, ds_sc):
        @pl.when(pl.program_id(1) == 0)
        def _():
            dstate[...] = jnp.zeros_like(dstate)

        causal, strict, _ = _chunk_masks()
        ones = jnp.ones((ncb, CA, LANES), BF16)
        last = lax.broadcasted_iota(jnp.int32, (CA, LANES), 0) == CA - 1

        def chunks(a):
            return a.astype(F32).reshape(ncb, CA, a.shape[-1])

        def rows_of(a):
            return a.reshape(rb, a.shape[-1])

        def rowsum(m):
            return jnp.sum(m, axis=2, keepdims=True)

        def colsum(m):
            hi, lo = _split(m)
            return _bdot(hi, ones, _BTN) + _bdot(lo, ones, _BTN)

        def head_cols(hh):
            return slice(hh * HEAD, (hh + 1) * HEAD)

        def head_inputs(hh):
            q = chunks(q_ref[:, head_cols(hh // rep)])
            k = chunks(k_ref[:, head_cols(hh // rep)])
            v = chunks(v_ref[:, head_cols(hh)])
            bcol = chunks(b_ref[hh])
            return q, k, v, bcol, _delta_pre(q, k, v, bcol, chunks(gc_ref[hh]), gr_ref[hh], gl_ref[hh], causal, strict)

        for h in range(nh):
            pre = head_inputs(h)[-1]
            sol = _hdot(tm_ref[h], pre["rhs"], _BNN)
            sol_sc[h] = sol
            vn_sc[h] = sol[:, :, :HEAD] - _bdot(sol[:, :, HEAD:], s_ref[h], _BNN)
            p_sc[h] = pre["p"]
            kkb_sc[h] = pre["kkb"]

        def state_step(it, carry):
            n = ncb - 1 - it
            rows = pl.ds(pl.multiple_of(n * CA, CA), CA)
            for h in range(nh):
                qn = q_ref[rows, head_cols(h // rep)].astype(F32)
                kn = k_ref[rows, head_cols(h // rep)].astype(F32)
                gc = gc_ref[h, rows, :]
                gl = gl_ref[h, n]
                ds = dstate[h]
                ds_sc[h, n] = ds
                dov = do_ref[rows, h * HEAD:(h + 1) * HEAD].astype(F32)
                dvn = _bdot(p_sc[h, n], dov, _TN) + _bdot(kn * jnp.exp(gl - gc), ds)
                dvn_sc[h, n] = dvn
                dstate[h] = (ds * jnp.exp(gl) + _bdot(qn * jnp.exp(gc), dov, _TN)
                             - _bdot(sol_sc[h, n, :, HEAD:], dvn, _TN))
            return carry

        lax.fori_loop(0, ncb, state_step, 0)

        for h in range(nh):
            q, k, v, bcol, pre = head_inputs(h)
            if h % rep == 0:
                kkr = _bdot(k, k, _BNT)
                dq = jnp.zeros((ncb, CA, HEAD), F32)
                dk = jnp.zeros((ncb, CA, HEAD), F32)
            eg, dm, kb, qd, kd, cd = pre["eg"], pre["dm"], pre["kb"], pre["qd"], pre["kd"], pre["cd"]
            p = p_sc[h]
            sol = sol_sc[h]
            s = s_ref[h].astype(F32)
            ds = ds_sc[h]
            dov = chunks(do_ref[:, h * HEAD:(h + 1) * HEAD])
            v_new = vn_sc[h]
            dvn = dvn_sc[h]

            dp = jnp.where(causal, _bdot(dov, v_new, _BNT), 0.0)
            dqd = _bdot(dov, s, _BNT)
            dkd = _bdot(v_new, ds, _BNT)
            dcd = jnp.sum(rowsum(s * ds), axis=1, keepdims=True)
            dw = -_bdot(dvn, s, _BNT)

            drhs = _hdot(tm_ref[h], jnp.concatenate([dvn, dw], axis=2), _BTN)
            dbv, dbke = drhs[:, :, :HEAD], drhs[:, :, HEAD:]
            da = -jnp.where(strict, _bdot(drhs, sol, _BNT), 0.0)
            m = da * dm
            e = m * kkb_sc[h] + dp * p
            dgam = rowsum(e) - colsum(e) + rowsum(dbke * kb * eg) + rowsum(dqd * qd)
            r = rowsum(dkd * kd)
            tot = jnp.sum(r, axis=1, keepdims=True) + dcd * cd
            dgam = dgam - r + jnp.where(last, tot, 0.0)
            dbeta = rowsum(m * kkr) + rowsum(dbv * v) + rowsum(dbke * eg * k)
            nm = m * bcol[:, :, :CA]
            dqk = dp * dm
            dq = dq + _bdot(dqk, k, _BNN) + eg * dqd
            dk = (dk + _bdot(nm, k, _BNN) + _bdot(nm, k, _BTN) + _bdot(dqk, q, _BTN) + bcol * eg * dbke
                  + pre["ekd"] * dkd)
            dv_ref[:, h * HEAD:(h + 1) * HEAD] = rows_of(bcol * dbv)
            db_ref[h] = rows_of(jnp.broadcast_to(dbeta, (ncb, CA, LANES)))
            dg_ref[h] = rows_of(jnp.broadcast_to(dgam, (ncb, CA, LANES)))
            if h % rep == rep - 1:
                dq_ref[:, head_cols(h // rep)] = rows_of(dq)
                dk_ref[:, head_cols(h // rep)] = rows_of(dk)

    koff, voff = hqk // qpb, 2 * hqk // nh
    rv = lambda i: nb - 1 - i
    hd = pl.BlockSpec((nh, rb, LANES), lambda j, i: (j, rv(i), 0))
    qk_out = pl.BlockSpec((rb, qpb * HEAD), lambda j, i: (rv(i), j))
    v_blk = pl.BlockSpec((rb, nh * HEAD), lambda j, i: (rv(i), j))
    per_chunk = lambda *shape: pl.BlockSpec((nh, ncb) + shape, lambda j, i: (j, rv(i), 0, 0))
    return _call(
        body, "delta_bwd", (hqk // qpb, nb),
        [pl.BlockSpec((rb, qpb * HEAD), lambda j, i: (rv(i), j)),
         pl.BlockSpec((rb, qpb * HEAD), lambda j, i: (rv(i), koff + j)),
         pl.BlockSpec((rb, nh * HEAD), lambda j, i: (rv(i), voff + j)),
         hd, hd, per_chunk(1, CA), per_chunk(1, LANES), per_chunk(HEAD, HEAD), per_chunk(CA, CA), v_blk],
        [qk_out, qk_out, v_blk, hd, hd],
        [jax.ShapeDtypeStruct((t, hqk * HEAD), F32), jax.ShapeDtypeStruct((t, hqk * HEAD), F32),
         jax.ShapeDtypeStruct((t, hv * HEAD), F32),
         jax.ShapeDtypeStruct((hv, t, LANES), F32), jax.ShapeDtypeStruct((hv, t, LANES), F32)],
        (qkvn, qkvn, qkvn, beta_b, gam_b, gam_r, gam_l, s_all, tm_all, do),
        scratch=[pltpu.VMEM((nh, HEAD, HEAD), F32), pltpu.VMEM((nh, ncb, CA, 2 * HEAD), F32),
                 pltpu.VMEM((nh, ncb, CA, HEAD), F32), pltpu.VMEM((nh, ncb, CA, CA), F32),
                 pltpu.VMEM((nh, ncb, CA, CA), F32), pltpu.VMEM((nh, ncb, CA, HEAD), F32),
                 pltpu.VMEM((nh, ncb, HEAD, HEAD), F32)],
        sem=("parallel", "arbitrary"), jobs=jobs)


def _apost_fwd(o, proj, gain, zoff, hv):
    t = o.shape[0]
    tb = _tile(t, 1024, 8)
    zb = zoff // HEAD

    def body(o_ref, z_ref, g_ref, y_ref):
        ov = o_ref[...]
        z = z_ref[...].astype(F32)
        r = lax.rsqrt(jnp.mean(ov * ov, axis=-1, keepdims=True) + EPS)
        y_ref[...] = (ov * r * g_ref[...] * (z * _sigmoid(z))).astype(BF16)

    blk = pl.BlockSpec((tb, HEAD), lambda i, h: (i, h))
    return pl.pallas_call(
        body, name="apost_fwd", grid=(t // tb, hv),
        in_specs=[blk, pl.BlockSpec((tb, HEAD), lambda i, h: (i, zb + h)), pl.BlockSpec((1, HEAD), lambda i, h: (0, 0))],
        out_specs=blk, out_shape=jax.ShapeDtypeStruct((t, hv * HEAD), BF16),
        compiler_params=_params(("parallel", "parallel")))(o, proj, gain.reshape(1, HEAD))


def _apost_bwd(o, proj, gain, dy, zoff, hv):
    t = o.shape[0]
    tb = _tile(t, 1024, 8)
    zb = zoff // HEAD

    def body(o_ref, z_ref, g_ref, dy_ref, do_ref, dz_ref, dg_ref):
        @pl.when(jnp.logical_and(pl.program_id(0) == 0, pl.program_id(1) == 0))
        def _():
            dg_ref[...] = jnp.zeros_like(dg_ref)

        ov = o_ref[...]
        z = z_ref[...].astype(F32)
        d = dy_ref[...].astype(F32)
        r = lax.rsqrt(jnp.mean(ov * ov, axis=-1, keepdims=True) + EPS)
        oh = ov * r
        sz = z * _sigmoid(z)
        dn = d * sz
        dz_ref[...] = (d * oh * g_ref[...] * _silu_grad(z)).astype(BF16)
        doh = dn * g_ref[...]
        do_ref[...] = r * (doh - oh * jnp.mean(doh * oh, axis=-1, keepdims=True))
        dg_ref[...] += jnp.sum(dn * oh, axis=0, keepdims=True)

    blk = pl.BlockSpec((tb, HEAD), lambda i, h: (i, h))
    vec = pl.BlockSpec((1, HEAD), lambda i, h: (0, 0))
    return pl.pallas_call(
        body, name="apost_bwd", grid=(t // tb, hv),
        in_specs=[blk, pl.BlockSpec((tb, HEAD), lambda i, h: (i, zb + h)), vec, blk],
        out_specs=[blk, blk, vec],
        out_shape=[jax.ShapeDtypeStruct((t, hv * HEAD), F32), jax.ShapeDtypeStruct((t, hv * HEAD), BF16),
                   jax.ShapeDtypeStruct((1, HEAD), F32)],
        compiler_params=_params(("arbitrary", "arbitrary")))(o, proj, gain.reshape(1, HEAD), dy)


def _sgu_fwd(proj, gain, w_s, b_t, uoff, wb):
    t = proj.shape[0]
    ng = wb // HEAD

    def body(u_ref, v_ref, g_ref, w_ref, b_ref, o_ref):
        r_i = lax.broadcasted_iota(jnp.int32, (HEAD, HEAD), 0)
        c_i = lax.broadcasted_iota(jnp.int32, (HEAD, HEAD), 1)
        u = _gelu(u_ref[...].astype(F32))
        vg = _gelu(v_ref[...].astype(F32))
        vn = vg * lax.rsqrt(jnp.mean(vg * vg, axis=-1, keepdims=True) + EPS) * g_ref[...]
        for g in range(ng):
            cols = slice(g * HEAD, (g + 1) * HEAD)
            wg = jnp.where(r_i >= c_i, w_ref[g], 0.0)
            mixed = _bdot(wg, vn[:, cols]) + b_ref[:, g:g + 1]
            o_ref[:, cols] = (u[:, cols] * mixed).astype(BF16)

    ub, vb = uoff // wb, uoff // wb + 1
    return pl.pallas_call(
        body, name="sgu_fwd", grid=(t // HEAD,),
        in_specs=[pl.BlockSpec((HEAD, wb), lambda i: (i, ub)), pl.BlockSpec((HEAD, wb), lambda i: (i, vb)),
                  pl.BlockSpec((1, wb), lambda i: (0, 0)), pl.BlockSpec((ng, HEAD, HEAD), lambda i: (0, 0, 0)),
                  pl.BlockSpec((HEAD, ng), lambda i: (0, 0))],
        out_specs=pl.BlockSpec((HEAD, wb), lambda i: (i, 0)), out_shape=jax.ShapeDtypeStruct((t, wb), BF16),
        compiler_params=_params(("parallel",)))(proj, proj, gain.reshape(1, wb), w_s, b_t)


def _sgu_bwd(proj, gain, w_s, b_t, dout, uoff, wb):
    t = proj.shape[0]
    ng = wb // HEAD

    def body(u_ref, v_ref, g_ref, w_ref, b_ref, d_ref, du_ref, dv_ref, dw_ref, db_ref, dg_ref, dvn_ref):
        @pl.when(pl.program_id(0) == 0)
        def _():
            dw_ref[...] = jnp.zeros_like(dw_ref)
            db_ref[...] = jnp.zeros_like(db_ref)
            dg_ref[...] = jnp.zeros_like(dg_ref)

        r_i = lax.broadcasted_iota(jnp.int32, (HEAD, HEAD), 0)
        c_i = lax.broadcasted_iota(jnp.int32, (HEAD, HEAD), 1)
        tril = r_i >= c_i
        ub = u_ref[...].astype(F32)
        vb = v_ref[...].astype(F32)
        u = _gelu(ub)
        vg = _gelu(vb)
        r = lax.rsqrt(jnp.mean(vg * vg, axis=-1, keepdims=True) + EPS)
        vh = vg * r
        vn = vh * g_ref[...]
        d = d_ref[...].astype(F32)
        for g in range(ng):
            cols = slice(g * HEAD, (g + 1) * HEAD)
            wg = jnp.where(tril, w_ref[g], 0.0)
            mixed = _bdot(wg, vn[:, cols]) + b_ref[:, g:g + 1]
            du_ref[:, cols] = (d[:, cols] * mixed * _gelu_grad(ub[:, cols])).astype(BF16)
            dmix = d[:, cols] * u[:, cols]
            dw_ref[g] += jnp.where(tril, _bdot(dmix, vn[:, cols], _NT), 0.0)
            db_ref[g] += jnp.broadcast_to(jnp.sum(dmix, axis=1, keepdims=True), (HEAD, HEAD))
            dvn_ref[:, cols] = _bdot(wg, dmix, _TN)
        dvn = dvn_ref[...]
        dg_ref[...] += jnp.sum(dvn * vh, axis=0, keepdims=True)
        dvh = dvn * g_ref[...]
        dvg = r * (dvh - vh * jnp.mean(dvh * vh, axis=-1, keepdims=True))
        dv_ref[...] = (dvg * _gelu_grad(vb)).astype(BF16)

    ub_i, vb_i = uoff // wb, uoff // wb + 1
    row = pl.BlockSpec((HEAD, wb), lambda i: (i, 0))
    mat = pl.BlockSpec((ng, HEAD, HEAD), lambda i: (0, 0, 0))
    vec = pl.BlockSpec((1, wb), lambda i: (0, 0))
    return pl.pallas_call(
        body, name="sgu_bwd", grid=(t // HEAD,),
        in_specs=[pl.BlockSpec((HEAD, wb), lambda i: (i, ub_i)), pl.BlockSpec((HEAD, wb), lambda i: (i, vb_i)),
                  vec, mat, pl.BlockSpec((HEAD, ng), lambda i: (0, 0)), row],
        out_specs=[row, row, mat, mat, vec],
        out_shape=[jax.ShapeDtypeStruct((t, wb), BF16), jax.ShapeDtypeStruct((t, wb), BF16),
                   jax.ShapeDtypeStruct((ng, HEAD, HEAD), F32), jax.ShapeDtypeStruct((ng, HEAD, HEAD), F32),
                   jax.ShapeDtypeStruct((1, wb), F32)],
        scratch_shapes=[pltpu.VMEM((HEAD, wb), F32)],
        compiler_params=_params(("arbitrary",)))(proj, proj, gain.reshape(1, wb), w_s, b_t, dout)


def _merge_specs(t, d, goff):
    tb = _tile(t, 512, 8)
    tc = _tile(d, 512)
    gb = goff // tc
    nd = d // tc
    blk = pl.BlockSpec((tb, tc), lambda i, j: (i, j))
    ga = pl.BlockSpec((tb, tc), lambda i, j: (i, gb + j))
    gbs = pl.BlockSpec((tb, tc), lambda i, j: (i, gb + nd + j))
    return (t // tb, nd), blk, ga, gbs


def _merge_fwd(ya, yb, proj, goff):
    t, d = ya.shape
    grid, blk, ga, gbs = _merge_specs(t, d, goff)

    def body(ya_ref, yb_ref, ga_ref, gb_ref, o_ref):
        o_ref[...] = (_sigmoid(ga_ref[...].astype(F32)) * ya_ref[...].astype(F32)
                      + _sigmoid(gb_ref[...].astype(F32)) * yb_ref[...].astype(F32)).astype(BF16)

    return pl.pallas_call(
        body, name="merge_fwd", grid=grid, in_specs=[blk, blk, ga, gbs], out_specs=blk,
        out_shape=jax.ShapeDtypeStruct((t, d), BF16),
        compiler_params=_params(("parallel", "parallel")))(ya, yb, proj, proj)


def _merge_bwd(dm, ya, yb, proj, goff):
    t, d = ya.shape
    grid, blk, ga, gbs = _merge_specs(t, d, goff)

    def body(dm_ref, ya_ref, yb_ref, ga_ref, gb_ref, dya_ref, dyb_ref, dga_ref, dgb_ref):
        dmv = dm_ref[...].astype(F32)
        sa = _sigmoid(ga_ref[...].astype(F32))
        sb = _sigmoid(gb_ref[...].astype(F32))
        dya_ref[...] = (dmv * sa).astype(BF16)
        dyb_ref[...] = (dmv * sb).astype(BF16)
        dga_ref[...] = (dmv * ya_ref[...].astype(F32) * sa * (1.0 - sa)).astype(BF16)
        dgb_ref[...] = (dmv * yb_ref[...].astype(F32) * sb * (1.0 - sb)).astype(BF16)

    shp = jax.ShapeDtypeStruct((t, d), BF16)
    return pl.pallas_call(
        body, name="merge_bwd", grid=grid, in_specs=[blk, blk, blk, ga, gbs], out_specs=[blk] * 4,
        out_shape=[shp] * 4, compiler_params=_params(("parallel", "parallel")))(dm, ya, yb, proj, proj)


def _ffn_act_fwd(up, conv_w, bias, dff):
    t = up.shape[0]
    nblk = dff // HEAD
    kw = conv_w.shape[0]

    def body(g_ref, v_ref, wg_ref, wv_ref, bg_ref, bv_ref, o_ref):
        rows = lax.broadcasted_iota(jnp.int32, (t, HEAD), 0)
        cg = _conv(g_ref[...].astype(F32), wg_ref[...], rows) + bg_ref[...]
        cv = _conv(v_ref[...].astype(F32), wv_ref[...], rows) + bv_ref[...]
        o_ref[...] = (cg * _sigmoid(cg) * cv).astype(BF16)

    return pl.pallas_call(
        body, name="ffn_act_fwd", grid=(nblk,),
        in_specs=[pl.BlockSpec((t, HEAD), lambda j: (0, j)), pl.BlockSpec((t, HEAD), lambda j: (0, nblk + j)),
                  pl.BlockSpec((kw, HEAD), lambda j: (0, j)), pl.BlockSpec((kw, HEAD), lambda j: (0, nblk + j)),
                  pl.BlockSpec((1, HEAD), lambda j: (0, j)), pl.BlockSpec((1, HEAD), lambda j: (0, nblk + j))],
        out_specs=pl.BlockSpec((t, HEAD), lambda j: (0, j)), out_shape=jax.ShapeDtypeStruct((t, dff), BF16),
        compiler_params=_params(("parallel",)))(up, up, conv_w, conv_w, bias, bias)


def _ffn_act_bwd(up, dact, conv_w, bias, dff, jobs=()):
    t = up.shape[0]
    nblk = dff // HEAD
    kw = conv_w.shape[0]

    def body(me_ref, pa_ref, d_ref, wm_ref, wp_ref, bm_ref, bp_ref, dup_ref, dw_ref, db_ref):
        is_gate = pl.program_id(0) < nblk
        rows = lax.broadcasted_iota(jnp.int32, (t, HEAD), 0)
        xv = me_ref[...].astype(F32)
        w = wm_ref[...]
        cm = _conv(xv, w, rows) + bm_ref[...]
        cp = _conv(pa_ref[...].astype(F32), wp_ref[...], rows) + bp_ref[...]
        d = d_ref[...].astype(F32)
        dc = jnp.where(is_gate, d * cp * _silu_grad(cm), d * (cp * _sigmoid(cp)))
        db_ref[...] = jnp.sum(dc, axis=0, keepdims=True)
        dx = dc * w[kw - 1:kw, :]
        dw_ref[kw - 1:kw, :] = jnp.sum(dc * xv, axis=0, keepdims=True)
        for s in range(1, kw):
            dx = dx + _shift_up(dc, s, rows) * w[kw - 1 - s:kw - s, :]
            dw_ref[kw - 1 - s:kw - s, :] = jnp.sum(dc * _shift_down(xv, s, rows), axis=0, keepdims=True)
        dup_ref[...] = dx.astype(BF16)

    part = lambda j: (j + nblk) % (2 * nblk)
    me = pl.BlockSpec((t, HEAD), lambda j: (0, j))
    wme = pl.BlockSpec((kw, HEAD), lambda j: (0, j))
    bme = pl.BlockSpec((1, HEAD), lambda j: (0, j))
    return _call(
        body, "ffn_act_bwd", (2 * nblk,),
        [me, pl.BlockSpec((t, HEAD), lambda j: (0, part(j))), pl.BlockSpec((t, HEAD), lambda j: (0, j % nblk)),
         wme, pl.BlockSpec((kw, HEAD), lambda j: (0, part(j))),
         bme, pl.BlockSpec((1, HEAD), lambda j: (0, part(j)))],
        [me, wme, bme],
        [jax.ShapeDtypeStruct((t, 2 * dff), BF16), jax.ShapeDtypeStruct((kw, 2 * dff), F32),
         jax.ShapeDtypeStruct((1, 2 * dff), F32)],
        (up, up, dact, conv_w, conv_w, bias, bias), sem=("parallel",), jobs=jobs)


def _ple_fwd(x, gt, pp):
    t, d = x.shape
    tb, tc = _tile(t, 512, 8), _tile(d, 1024)

    def body(x_ref, g_ref, p_ref, o_ref):
        o_ref[...] = x_ref[...] + _sigmoid(g_ref[...].astype(F32)) * p_ref[...].astype(F32)

    blk = pl.BlockSpec((tb, tc), lambda i, j: (i, j))
    return pl.pallas_call(
        body, name="ple_fwd", grid=(t // tb, d // tc), in_specs=[blk, blk, blk], out_specs=blk,
        out_shape=jax.ShapeDtypeStruct((t, d), F32), compiler_params=_params(("parallel", "parallel")))(x, gt, pp)


def _ple_bwd(dx, gt, pp):
    t, d = dx.shape
    tb, tc = _tile(t, 512, 8), _tile(d, 1024)

    def body(dx_ref, g_ref, p_ref, dg_ref, dp_ref):
        dv = dx_ref[...]
        s = _sigmoid(g_ref[...].astype(F32))
        dg_ref[...] = (dv * p_ref[...].astype(F32) * s * (1.0 - s)).astype(BF16)
        dp_ref[...] = (dv * s).astype(BF16)

    blk = pl.BlockSpec((tb, tc), lambda i, j: (i, j))
    shp = jax.ShapeDtypeStruct((t, d), BF16)
    return pl.pallas_call(
        body, name="ple_bwd", grid=(t // tb, d // tc), in_specs=[blk, blk, blk], out_specs=[blk, blk],
        out_shape=[shp, shp], compiler_params=_params(("parallel", "parallel")))(dx, gt, pp)


def _adam(pieces, w, m, v, name, jobs=()):
    nq = len(pieces)
    npart, rp, c = pieces[0].shape
    per_layer = nq // w.shape[0]
    row_bytes = 2 * c * (nq * npart * pieces[0].dtype.itemsize + 7 * 4)
    tr = _tile(rp, max(16, min(512, ADAM_VMEM_BUDGET // row_bytes)), 16)
    nblk = rp // tr
    c1 = 1.0 - ADAM_B1 ** ADAM_STEP
    c2 = 1.0 - ADAM_B2 ** ADAM_STEP

    def body(*refs):
        p_refs = refs[:nq]
        w_ref, m_ref, v_ref, g_ref, d_ref, mo_ref, vo_ref = refs[nq:]
        for q in range(nq):
            @pl.when(pl.program_id(0) == q)
            def _(p_ref=p_refs[q]):
                g = p_ref[0].astype(F32)
                for i in range(1, npart):
                    g = g + p_ref[i].astype(F32)
                mn = ADAM_B1 * m_ref[...] + (1.0 - ADAM_B1) * g
                vn = ADAM_B2 * v_ref[...] + (1.0 - ADAM_B2) * (g * g)
                g_ref[...] = g
                mo_ref[...] = mn
                vo_ref[...] = vn
                d_ref[...] = -ADAM_LR * ((mn / c1) / (jnp.sqrt(vn / c2) + ADAM_EPS) + ADAM_WD * w_ref[...])

    def piece_spec(q):
        return pl.BlockSpec((npart, tr, c), lambda i, r: (0, jnp.where(i == q, r, jnp.where(i < q, 0, nblk - 1)), 0))

    blk = pl.BlockSpec((None, tr, c), lambda i, r: (i // per_layer, (i % per_layer) * nblk + r, 0))
    shp = jax.ShapeDtypeStruct(w.shape, F32)
    return _call(body, name, (nq, nblk), [piece_spec(q) for q in range(nq)] + [blk] * 3, [blk] * 4, [shp] * 4,
                 (*pieces, w, m, v), sem=("parallel", "parallel"), jobs=jobs)


_BIG = ("w_in", "w_branch_a", "w_branch_b", "w_out", "w_ffn_up", "w_ffn_down", "w_ple_gate", "w_ple_proj")
_COL_SHARDED = ("w_in", "w_branch_b", "w_ffn_up", "w_ple_proj")
_CONVS = ("conv_qkv", "conv_ffn")
_GATHER_ON_PROJ = ("w_ffn_up",)
_GATHER_ON_DELTA = ("w_ffn_down",)
_GATHER_AHEAD = ("w_in", "conv_qkv")
_GATHER_ON_UP = ("w_ple_gate", "w_ple_proj")
_GATHER_AHEAD_2 = ("w_branch_a", "w_branch_b", "w_out", "conv_ffn")
_SCATTER_ON_DACT = ("w_ple_gate", "w_ple_proj")
_SCATTER_ON_DELTA = ("w_ffn_up",)
_SCATTER_ON_DW_MAIN = ("w_out", "w_branch_a", "w_branch_b")
_SCATTER_ON_DH1 = ("w_ffn_down",)
_SMALL = ("norm_mix", "conv_qkv", "a_log", "dt_bias", "head_norm", "sgu_norm", "w_spatial", "b_spatial", "norm_ffn",
          "conv_ffn", "b_conv_ffn", "norm_ple", "norm_final")
_WEIGHTS = ("norm_mix", "w_in", "conv_qkv", "a_log", "dt_bias", "head_norm", "sgu_norm", "w_spatial", "b_spatial",
            "w_branch_a", "w_branch_b", "w_out", "norm_ffn", "w_ffn_up", "conv_ffn", "b_conv_ffn", "w_ffn_down",
            "norm_ple", "w_ple_gate", "w_ple_proj", "norm_final")


def _full_cols(g):
    return jnp.transpose(g, (1, 0, 2)).reshape(g.shape[1], N_DEV * g.shape[2])


def _full_rows(g):
    return g.reshape(N_DEV * g.shape[1], g.shape[2])


def _split_cols(dw):
    k, n = dw.shape
    return jnp.transpose(dw.reshape(k, N_DEV, n // N_DEV), (1, 0, 2))


def _split_rows(dw):
    k, n = dw.shape
    return dw.reshape(N_DEV, k // N_DEV, n)


def _pad_lanes(v, width=LANES, offset=0):
    return jnp.pad(v, ((0, 0), (offset, width - offset - v.shape[1])))


def kernel(x, p, norm_mix, w_in, conv_qkv, a_log, dt_bias, head_norm, sgu_norm, w_spatial, b_spatial, w_branch_a, w_branch_b, w_out, norm_ffn, w_ffn_up, conv_ffn, b_conv_ffn, w_ffn_down, norm_ple, w_ple_gate, w_ple_proj, norm_final, loss_target, m_norm_mix, m_w_in, m_conv_qkv, m_a_log, m_dt_bias, m_head_norm, m_sgu_norm, m_w_spatial, m_b_spatial, m_w_branch_a, m_w_branch_b, m_w_out, m_norm_ffn, m_w_ffn_up, m_conv_ffn, m_b_conv_ffn, m_w_ffn_down, m_norm_ple, m_w_ple_gate, m_w_ple_proj, m_norm_final, v_norm_mix, v_w_in, v_conv_qkv, v_a_log, v_dt_bias, v_head_norm, v_sgu_norm, v_w_spatial, v_b_spatial, v_w_branch_a, v_w_branch_b, v_w_out, v_norm_ffn, v_w_ffn_up, v_conv_ffn, v_b_conv_ffn, v_w_ffn_down, v_norm_ple, v_w_ple_gate, v_w_ple_proj, v_norm_final):
    env = dict(locals())
    wts = {n: env[n] for n in _WEIGHTS}
    mom_m = {n: env["m_" + n] for n in _WEIGHTS}
    mom_v = {n: env["v_" + n] for n in _WEIGHTS}

    xin = x[0]
    tgt = loss_target[0]
    t, d = xin.shape
    depth = w_in.shape[0]
    hv = a_log.shape[1]
    vw = hv * HEAD
    wb = sgu_norm.shape[1]
    ng = w_spatial.shape[1]
    n_in = w_in.shape[2] * N_DEV
    qk = (n_in - 2 * vw - 2 * hv - 2 * wb - 2 * d) // 2
    hqk = qk // HEAD
    dff = w_ffn_down.shape[1] * N_DEV
    cw = 2 * qk + vw
    o_z, o_ba = 2 * qk + vw, 2 * qk + 2 * vw
    o_ub = o_ba
    o_ga = o_ub + 2 * wb
    ns = w_in.shape[2]
    in_segments = ((0, 0, o_ba), (1, o_ba, o_ba + 2 * hv), (2, o_ba + 2 * hv, n_in))
    me = 4 * lax.axis_index("x") + 2 * lax.axis_index("y") + lax.axis_index("c")

    full = [dict() for _ in range(depth)]
    staged = {}

    def as_cols(n):
        return "_cols" if (n in _COL_SHARDED or n in _CONVS) and wts[n].shape[-1] % LANES == 0 else ""

    def chips(i, names):
        names = names if i < depth else ()
        return [(i, n) for n in names], [(wts[n][i].astype(BF16) if n in _BIG else wts[n][i], "chips" + as_cols(n))
                                         for n in names]

    def relay(keys):
        return list(keys), [(staged.pop(key), "relay" + as_cols(key[1])) for key in keys]

    def settle(chip_keys, relay_keys, results):
        for key, g in zip(chip_keys, results):
            staged[key] = g
        for (i, n), g in zip(relay_keys, results[len(chip_keys):]):
            keep_blocks = as_cols(n) or n == "w_in"
            full[i][n] = g if keep_blocks else _full_cols(g) if n in _COL_SHARDED or n in _CONVS else _full_rows(g)

    ck, cj = chips(0, _GATHER_AHEAD + _GATHER_AHEAD_2)
    settle(ck, [], _exchange(cj, "gather_first"))
    rk, rj = relay(ck)
    settle([], rk, _exchange(rj, "relay_first"))

    saved = []
    xc = xin
    for i in range(depth):
        fw = full[i]
        main, gates = [], []
        for dev, (seg, lo, hi) in itertools.product(range(N_DEV), in_segments):
            a, b = max(lo, dev * ns), min(hi, (dev + 1) * ns)
            if a < b:
                (gates if seg == 1 else main).append(fw["w_in"][dev][:, a - dev * ns:b - dev * ns])
        fw["w_main"] = jnp.concatenate(main, axis=1)
        fw["w_ba"] = _pad_lanes(jnp.concatenate(gates, axis=1))
        s = {"x0": xc}
        s["h1"] = _rms_fwd(xc, norm_mix[i], "rms_fwd")
        ck, cj = chips(i, _GATHER_ON_PROJ)
        s["proj"], got = _matmul(s["h1"], fw["w_main"], "nn", BF16, "mm_proj", jobs=cj)
        settle(ck, [], got)
        s["ba"] = _matmul(s["h1"], fw["w_ba"], "nn", F32, "mm_ba")
        s["qkvn"] = _qkv_fwd(s["proj"], fw["conv_qkv"], hqk, hqk)
        s["alog"] = _pad_lanes(a_log[i][None, :], offset=hv)
        s["dtb"] = _pad_lanes(dt_bias[i][None, :], offset=hv)
        bg = _gate_fwd(s["ba"], s["alog"], s["dtb"], hv)
        beta_t = bg[:, :hv].T
        gam_t = bg[:, hv:2 * hv].T
        s["beta_b"] = jnp.broadcast_to(beta_t[:, :, None], (hv, t, LANES))
        s["gam_b"] = jnp.broadcast_to(gam_t[:, :, None], (hv, t, LANES))
        s["gam_r"] = gam_t.reshape(hv, t // CA, 1, CA)
        s["gam_l"] = jnp.broadcast_to(s["gam_r"][:, :, :, CA - 1:], (hv, t // CA, 1, LANES))
        ck1, cj1 = chips(i, _GATHER_ON_DELTA)
        ck2, cj2 = chips(i + 1, _GATHER_AHEAD)
        rk, rj = relay([(i, n) for n in _GATHER_ON_PROJ])
        (s["o"], s["s_all"], s["tm_all"]), got = _delta_fwd(
            s["qkvn"], s["beta_b"], s["gam_b"], s["gam_r"], s["gam_l"], hqk, hv, jobs=cj1 + cj2 + rj)
        settle(ck1 + ck2, rk, got)
        s["outa"] = _apost_fwd(s["o"], s["proj"], head_norm[i], o_z, hv)
        s["b_t"] = b_spatial[i].T
        s["outb"] = _sgu_fwd(s["proj"], sgu_norm[i], w_spatial[i], s["b_t"], o_ub, wb)
        s["ya"] = _matmul(s["outa"], fw["w_branch_a"], "nn", BF16, "mm_ya")
        s["yb"] = _matmul(s["outb"], fw["w_branch_b"], "nn", BF16, "mm_yb")
        s["mg"] = _merge_fwd(s["ya"], s["yb"], s["proj"], o_ga)
        s["x1"] = _matmul(s["mg"], fw["w_out"], "nn", F32, "mm_out", res=xc)
        s["h2"] = _rms_fwd(s["x1"], norm_ffn[i], "rms_fwd")
        nxt = i + 1 < depth
        ck1, cj1 = chips(i, _GATHER_ON_UP)
        ck2, cj2 = chips(i + 1, _GATHER_AHEAD_2)
        rk, rj = relay([(i, n) for n in _GATHER_ON_DELTA] + ([(i + 1, n) for n in _GATHER_AHEAD] if nxt else []))
        s["up"], got = _matmul(s["h2"], fw["w_ffn_up"], "nn", BF16, "mm_up", jobs=cj1 + cj2 + rj)
        settle(ck1 + ck2, rk, got)
        s["bias"] = b_conv_ffn[i][None, :]
        s["act"] = _ffn_act_fwd(s["up"], fw["conv_ffn"], s["bias"], dff)
        rk, rj = relay([(i, n) for n in _GATHER_ON_UP] + ([(i + 1, n) for n in _GATHER_AHEAD_2] if nxt else []))
        s["x2"], got = _matmul(s["act"], fw["w_ffn_down"], "nn", F32, "mm_down", res=s["x1"], jobs=rj)
        settle([], rk, got)
        s["h3"] = _rms_fwd(s["x2"], norm_ple[i], "rms_fwd")
        s["gt"] = _matmul(s["h3"], fw["w_ple_gate"], "nn", BF16, "mm_gt")
        s["pp"] = _matmul(p[i, 0], fw["w_ple_proj"], "nn", BF16, "mm_pp")
        xc = _ple_fwd(s["x2"], s["gt"], s["pp"])
        saved.append(s)

    dx, g_norm_final, loss_part = _loss_head(xc, tgt, norm_final)

    small = {n: [None] * depth for n in _SMALL if n != "norm_final"}
    recv = {n: [None] * depth for n in _BIG}
    recv["w_in"] = [None] * (2 * depth)

    def scatter_jobs(gw, names):
        return [(gw[n], "scatter_cols") if as_cols(n) else
                (_split_cols(gw[n]) if n in _COL_SHARDED else _split_rows(gw[n]), "scatter") for n in names]

    def keep(i, names, results):
        for n, r in zip(names, results):
            recv[n][i] = r

    def carry(jobs, *args):
        return _matmul(*args, jobs=jobs) if jobs else (_matmul(*args), [])

    later = []
    for i in reversed(range(depth)):
        fw, s = full[i], saved[i]
        dgt, dpp = _ple_bwd(dx, s["gt"], s["pp"])
        gw = {"w_ple_gate": _matmul(s["h3"], dgt, "tn", BF16, "mm_dw_gt"),
              "w_ple_proj": _matmul(p[i, 0], dpp, "tn", BF16, "mm_dw_pp")}
        dh3 = _matmul(dgt, fw["w_ple_gate"], "nt", F32, "mm_dh3")
        dx, small["norm_ple"][i] = _rms_bwd(s["x2"], dh3, norm_ple[i], dx, "rms_bwd")

        dact, got = _matmul(dx, fw["w_ffn_down"], "nt", BF16, "mm_dact", jobs=scatter_jobs(gw, _SCATTER_ON_DACT))
        keep(i, _SCATTER_ON_DACT, got)
        gw["w_ffn_down"] = _matmul(s["act"], dx, "tn", BF16, "mm_dw_down")
        (dup, small["conv_ffn"][i], small["b_conv_ffn"][i]), _ = _ffn_act_bwd(s["up"], dact, fw["conv_ffn"], s["bias"], dff)
        gw["w_ffn_up"], got = carry(later[:1], s["h2"], dup, "tn", BF16, "mm_dw_up")
        keep(2 * i + 2, ("w_in",), got)
        dh2, got = carry(later[1:], dup, fw["w_ffn_up"], "nt", F32, "mm_dh2")
        keep(2 * i + 3, ("w_in",), got)
        dx, small["norm_ffn"][i] = _rms_bwd(s["x1"], dh2, norm_ffn[i], dx, "rms_bwd")

        dmg = _matmul(dx, fw["w_out"], "nt", BF16, "mm_dmg")
        gw["w_out"] = _matmul(s["mg"], dx, "tn", BF16, "mm_dw_out")
        dya, dyb, dga, dgb = _merge_bwd(dmg, s["ya"], s["yb"], s["proj"], o_ga)
        gw["w_branch_a"] = _matmul(s["outa"], dya, "tn", BF16, "mm_dw_a")
        gw["w_branch_b"] = _matmul(s["outb"], dyb, "tn", BF16, "mm_dw_b")
        douta = _matmul(dya, fw["w_branch_a"], "nt", BF16, "mm_douta")
        doutb = _matmul(dyb, fw["w_branch_b"], "nt", BF16, "mm_doutb")
        dub, dvb, small["w_spatial"][i], db_s, dsg = _sgu_bwd(s["proj"], sgu_norm[i], w_spatial[i], s["b_t"], doutb, o_ub, wb)
        small["b_spatial"][i] = db_s[:, :, 0]
        small["sgu_norm"][i] = dsg
        do, dz, small["head_norm"][i] = _apost_bwd(s["o"], s["proj"], head_norm[i], douta, o_z, hv)
        (dq, dk, dv, db_b, dg_b), got = _delta_bwd(
            s["qkvn"], s["beta_b"], s["gam_b"], s["gam_r"], s["gam_l"], s["s_all"], s["tm_all"], do, hqk, hv,
            jobs=scatter_jobs(gw, _SCATTER_ON_DELTA))
        keep(i, _SCATTER_ON_DELTA, got)
        dbg = _pad_lanes(jnp.concatenate([db_b[:, :, 0].T, dg_b[:, :, 0].T], axis=1))
        dba, dal, ddt = _gate_bwd(s["ba"], dbg, s["alog"], s["dtb"], hv)
        small["a_log"][i] = dal[:, hv:2 * hv]
        small["dt_bias"][i] = ddt[:, hv:2 * hv]
        dqkv_pre, small["conv_qkv"][i] = _qkv_bwd(s["proj"], jnp.concatenate([dq, dk, dv], axis=1), fw["conv_qkv"], hqk, hqk)
        dproj = jnp.concatenate([dqkv_pre, dz, dub, dvb, dga, dgb], axis=1)
        dw_main, got = _matmul(s["h1"], dproj, "tn", BF16, "mm_dw_main", jobs=scatter_jobs(gw, _SCATTER_ON_DW_MAIN))
        keep(i, _SCATTER_ON_DW_MAIN, got)
        dw_ba = _matmul(s["h1"], dba, "tn", BF16, "mm_dw_ba")
        dh1, got = _matmul(dproj, fw["w_main"], "nt", F32, "mm_dh1", jobs=scatter_jobs(gw, _SCATTER_ON_DH1))
        keep(i, _SCATTER_ON_DH1, got)
        dh1 = _matmul(dba, fw["w_ba"], "nt", F32, "mm_dh1_ba", res=dh1)
        dx, small["norm_mix"][i] = _rms_bwd(s["x0"], dh1, norm_mix[i], dx, "rms_bwd")

        sources = (dw_main, dw_ba, dw_main[:, o_ba:])
        later = []
        for rows in (slice(0, d // 2), slice(d // 2, d)):
            shards = []
            for dev in range(N_DEV):
                cuts = [(seg, max(lo, dev * ns), min(hi, (dev + 1) * ns)) for seg, lo, hi in in_segments]
                shards.append(jnp.concatenate([sources[seg][rows, a - lo:b - lo] for (seg, a, b), (_, lo, _)
                                               in zip(cuts, in_segments) if a < b], axis=1))
            later.append((jnp.stack(shards), True))


    rep_names = tuple(n for n in _SMALL if n not in _CONVS)
    stacked = {n: jnp.concatenate([jnp.reshape(a, (-1,)) for a in small[n]]) for n in small}
    stacked["norm_final"] = g_norm_final.reshape(-1)

    def padded(parts, mult, axis=0):
        flat = jnp.concatenate(parts, axis=axis)
        pad = -flat.shape[axis] % mult
        return jnp.pad(flat, [(0, 0)] * axis + [(0, pad)])

    rep_flat = padded([stacked[n] for n in rep_names] + [loss_part[0, :1]], 16 * LANES)
    conv_flat = [padded([stacked[n]], 8 * LANES) for n in _CONVS]
    packed = jnp.concatenate([rep_flat] + conv_flat).reshape(-1, LANES)

    outs_g, outs_d, outs_m, outs_v = {}, {}, {}, {}

    adam_jobs = {"w_ffn_up": later[:1], "w_ffn_down": later[1:], "w_in": [(packed, False)]}
    for n in sorted(_BIG, key=lambda name: name == "w_in"):
        (outs_g[n], outs_d[n], outs_m[n], outs_v[n]), got = _adam(
            recv[n], wts[n], mom_m[n], mom_v[n], "adam_" + n, jobs=adam_jobs.get(n, []))
        if n == "w_ffn_up":
            recv["w_in"][0] = got[0]
        elif n == "w_ffn_down":
            recv["w_in"][1] = got[0]
        elif n == "w_in":
            small_all = got[0].reshape(N_DEV, -1)

    n_rep = rep_flat.shape[0]
    pk = lambda src: padded([src[n].reshape(-1) for n in rep_names] + [jnp.zeros((1,), F32)], 16 * LANES).reshape(1, -1, LANES)
    res, _ = _adam([small_all[:, :n_rep].reshape(N_DEV, -1, LANES)], pk(wts), pk(mom_m), pk(mom_v), "adam_small")
    res = [r.reshape(-1) for r in res]
    off = 0
    for n in rep_names:
        shp = wts[n].shape
        size = math.prod(shp)
        outs_g[n], outs_d[n], outs_m[n], outs_v[n] = [r[off:off + size].reshape(shp) for r in res]
        off += size
    loss = res[0][off]

    off = n_rep
    for n, cf in zip(_CONVS, conv_flat):
        _, kw, cl = wts[n].shape
        part = small_all[:, off:off + depth * kw * cl * N_DEV].reshape(N_DEV, depth * kw, N_DEV, cl)
        off += cf.shape[0]
        part = lax.dynamic_index_in_dim(part, me, axis=2, keepdims=False)
        two_d = lambda a: a.reshape(1, depth * kw, cl)
        res, _ = _adam([part], two_d(wts[n]), two_d(mom_m[n]), two_d(mom_v[n]), "adam_" + n)
        outs_g[n], outs_d[n], outs_m[n], outs_v[n] = [r.reshape(wts[n].shape) for r in res]

    return (loss, dx[None], *[outs_g[n] for n in _WEIGHTS], *[outs_d[n] for n in _WEIGHTS],
            *[outs_m[n] for n in _WEIGHTS], *[outs_v[n] for n in _WEIGHTS])
```

```python
import functools
import itertools
import math

import jax
import jax.numpy as jnp
from jax import lax
from jax.experimental import pallas as pl
from jax.experimental.pallas import tpu as pltpu

F32 = jnp.float32
BF16 = jnp.bfloat16
EPS = 1e-6
LANES = 128
HEAD = 128
CA = 64
N_DEV = 8
VMEM_LIMIT = 48 * 1024 * 1024
ADAM_VMEM_BUDGET = 16 * 1024 * 1024
MESH = pl.DeviceIdType.MESH

ADAM_LR = 0.001
ADAM_B1 = 0.9
ADAM_B2 = 0.999
ADAM_EPS = 1e-08
ADAM_WD = 0.01
ADAM_STEP = 10


def _tile(n, cap, mult=LANES):
    best = None
    for t in range(mult, min(n, cap) + 1, mult):
        if n % t == 0:
            best = t
    return n if best is None else best


def _tile_near(n, target, mult=LANES):
    cands = [t for t in range(mult, min(n, target * 3 // 2) + 1, mult) if n % t == 0]
    return min(cands, key=lambda t: abs(t - target)) if cands else n


def _params(sem):
    return pltpu.CompilerParams(dimension_semantics=sem, vmem_limit_bytes=VMEM_LIMIT)


def _sigmoid(v):
    return jax.nn.sigmoid(v)


def _silu_grad(c):
    s = _sigmoid(c)
    return s + c * s * (1.0 - s)


_GELU_C = math.sqrt(2.0 / math.pi)


def _gelu(v):
    return 0.5 * v * (1.0 + jnp.tanh(_GELU_C * (v + 0.044715 * v * v * v)))


def _gelu_grad(v):
    t = jnp.tanh(_GELU_C * (v + 0.044715 * v * v * v))
    return 0.5 * (1.0 + t) + 0.5 * v * (1.0 - t * t) * _GELU_C * (1.0 + 3.0 * 0.044715 * v * v)


_NN = (((1,), (0,)), ((), ()))
_NT = (((1,), (1,)), ((), ()))
_TN = (((0,), (0,)), ((), ()))
_BNN = (((2,), (1,)), ((0,), (0,)))
_BNT = (((2,), (2,)), ((0,), (0,)))
_BTN = (((1,), (1,)), ((0,), (0,)))


def _bdot(a, b, dn=_NN):
    return lax.dot_general(a.astype(BF16), b.astype(BF16), dn, preferred_element_type=F32)


def _split(a):
    hi = a.astype(BF16)
    return hi, (a - hi.astype(F32)).astype(BF16)


def _dot3(ah, al, bh, bl, dn=_NN):
    def d(u, v):
        return lax.dot_general(u, v, dn, preferred_element_type=F32)
    return d(ah, bh) + (d(al, bh) + d(ah, bl))


def _hdot(a, b, dn=_NN):
    return _dot3(*_split(a), *_split(b), dn)


_PEERS = {"gather": (1, 2, 3, 4, 5, 6, 7), "scatter": (1, 2, 3, 4, 5, 6, 7), "chips": (1, 2, 4, 6), "relay": (2, 4, 6)}


def _kinds(jobs):
    return [{False: "gather", True: "scatter"}.get(kind, kind) for _, kind in jobs]


def _xchg_out_shapes(jobs):
    shapes = []
    for (a, _), kind in zip(jobs, _kinds(jobs)):
        shape = {"gather": (N_DEV,) + a.shape, "chips": (N_DEV,) + a.shape, "chips_cols": (a.shape[0], N_DEV * a.shape[1]),
                 "scatter_cols": (N_DEV, a.shape[0], a.shape[1] // N_DEV)}.get(kind, a.shape)
        shapes.append(jax.ShapeDtypeStruct(shape, a.dtype))
    return shapes


def _xchg_scratch(jobs):
    n = len(jobs)
    return [pltpu.SemaphoreType.DMA((n, N_DEV - 1)), pltpu.SemaphoreType.DMA((n, N_DEV - 1)), pltpu.SemaphoreType.DMA((n,))]


def _xchg_copies(kinds, src, out, sems):
    send_sems, recv_sems, local_sems = sems
    x, y, c = lax.axis_index("x"), lax.axis_index("y"), lax.axis_index("c")
    me = 4 * x + 2 * y + c

    def block(ref, idx, as_cols):
        if not as_cols:
            return ref.at[idx]
        width = ref.shape[1] // N_DEV
        return ref.at[:, pl.ds(pl.multiple_of(idx * width, LANES), width)]

    local, sends, recvs = [], [], []
    for m in range(N_DEV):
        px = lax.rem(x + ((m >> 2) & 1), 2)
        py = lax.rem(y + ((m >> 1) & 1), 2)
        pc = lax.rem(c + (m & 1), 2)
        peer = 4 * px + 2 * py + pc
        for k, kind in enumerate(kinds):
            base, cols = kind.split("_")[0], kind.endswith("_cols")
            if m == 0:
                if base != "relay":
                    mine = block(src[k], me, cols) if base == "scatter" else src[k]
                    local.append(pltpu.make_async_copy(mine, block(out[k], me, cols and base != "scatter"), local_sems.at[k]))
                continue
            if m not in _PEERS[base]:
                continue
            if base == "relay":
                to, mine = (x, y, 1 - c), block(src[k], peer, cols)
                there, here = block(out[k], peer, cols), block(out[k], 4 * px + 2 * py + 1 - c, cols)
            else:
                to, mine = (px, py, pc), block(src[k], peer, cols) if base == "scatter" else src[k]
                there, here = block(out[k], me, cols and base != "scatter"), block(out[k], peer, cols and base != "scatter")
            for dst, lst in ((there, sends), (here, recvs)):
                lst.append(pltpu.make_async_remote_copy(
                    src_ref=mine, dst_ref=dst, send_sem=send_sems.at[k, m - 1], recv_sem=recv_sems.at[k, m - 1],
                    device_id=to, device_id_type=MESH))
    return local, sends, recvs


def _xchg_start(scatter, src, out, sems):
    local, sends, _ = _xchg_copies(scatter, src, out, sems)
    for cp in local + sends:
        cp.start()


def _xchg_wait(scatter, src, out, sems):
    local, sends, recvs = _xchg_copies(scatter, src, out, sems)
    for cp in recvs:
        cp.wait_recv()
    for cp in sends:
        cp.wait_send()
    for cp in local:
        cp.wait()


_ANY = pl.BlockSpec(memory_space=pl.ANY)


def _xchg_aliases(jobs, n_in, n_out):
    return {n_in + k: n_out + k for k, kind in enumerate(_kinds(jobs)) if kind.startswith("relay")}


def _exchange(jobs, name):
    n = len(jobs)
    kinds = _kinds(jobs)

    def body(*refs):
        src, out, sems = refs[:n], refs[n:2 * n], refs[2 * n:]
        _xchg_start(kinds, src, out, sems)
        _xchg_wait(kinds, src, out, sems)

    return pl.pallas_call(
        body, name=name, in_specs=[_ANY] * n, out_specs=[_ANY] * n, out_shape=_xchg_out_shapes(jobs),
        scratch_shapes=_xchg_scratch(jobs), input_output_aliases=_xchg_aliases(jobs, 0, 0),
        compiler_params=pltpu.CompilerParams(has_side_effects=True))(*[a for a, _ in jobs])


def _carried(body, n_in, n_out, jobs, grid):
    if not jobs:
        return body
    nj = len(jobs)
    scatter = _kinds(jobs)

    def wrapped(*refs):
        ins, src = refs[:n_in], refs[n_in:n_in + nj]
        outs, got = refs[n_in + nj:n_in + nj + n_out], refs[n_in + nj + n_out:n_in + 2 * nj + n_out]
        rest = refs[n_in + 2 * nj + n_out:]
        scratch, sems = rest[:len(rest) - 3], rest[len(rest) - 3:]
        ids = [pl.program_id(a) for a in range(len(grid))]
        first = functools.reduce(jnp.logical_and, [i == 0 for i in ids])
        last = functools.reduce(jnp.logical_and, [i == g - 1 for i, g in zip(ids, grid)])

        @pl.when(first)
        def _():
            _xchg_start(scatter, src, got, sems)

        body(*ins, *outs, *scratch)

        @pl.when(last)
        def _():
            _xchg_wait(scatter, src, got, sems)

    return wrapped


def _call(body, name, grid, in_specs, out_specs, out_shape, args, scratch=(), sem=None, jobs=()):
    jobs = list(jobs)
    nj = len(jobs)
    sem = ("arbitrary",) * len(grid) if jobs or sem is None else sem
    res = pl.pallas_call(
        _carried(body, len(in_specs), len(out_specs), jobs, grid), name=name, grid=grid,
        in_specs=list(in_specs) + [_ANY] * nj, out_specs=list(out_specs) + [_ANY] * nj,
        out_shape=list(out_shape) + _xchg_out_shapes(jobs),
        scratch_shapes=list(scratch) + (_xchg_scratch(jobs) if jobs else []),
        input_output_aliases=_xchg_aliases(jobs, len(in_specs), len(out_specs)),
        compiler_params=_params(sem))(*args, *[a for a, _ in jobs])
    return res[:len(out_specs)], res[len(out_specs):]


MATMUL_OPERAND_VMEM = 20 * 1024 * 1024
MATMUL_TILE = 1024


def _matmul(a, b, mode, out_dtype, name, res=None, jobs=()):
    if mode == "tn":
        kdim, m = a.shape
    else:
        m, kdim = a.shape
    n = b.shape[0] if mode == "nt" else b.shape[1]
    tm, tn = _tile_near(m, MATMUL_TILE), _tile(n, MATMUL_TILE)
    per_k = 2 * (tm * a.dtype.itemsize + tn * b.dtype.itemsize)
    tk = _tile(kdim, max(LANES, MATMUL_OPERAND_VMEM // per_k))
    nk = kdim // tk
    dims = {"nn": _NN, "nt": _NT, "tn": _TN}[mode]

    def body(*refs):
        a_ref, b_ref = refs[:2]
        r_ref = refs[2] if res is not None else None
        o_ref = refs[3] if res is not None else refs[2]
        acc = refs[-1] if nk > 1 else None

        def write(r):
            if r_ref is not None:
                r = r + r_ref[...].astype(F32)
            o_ref[...] = r.astype(out_dtype)

        prod = _bdot(a_ref[...], b_ref[...], dims)
        if nk == 1:
            write(prod)
        else:
            k = pl.program_id(2)

            @pl.when(k == 0)
            def _():
                acc[...] = prod

            @pl.when(jnp.logical_and(k > 0, k < nk - 1))
            def _():
                acc[...] += prod

            @pl.when(k == nk - 1)
            def _():
                write(acc[...] + prod)

    a_spec = pl.BlockSpec((tk, tm), lambda i, j, k: (k, i)) if mode == "tn" else pl.BlockSpec((tm, tk), lambda i, j, k: (i, k))
    b_spec = pl.BlockSpec((tn, tk), lambda i, j, k: (j, k)) if mode == "nt" else pl.BlockSpec((tk, tn), lambda i, j, k: (k, j))
    o_spec = pl.BlockSpec((tm, tn), lambda i, j, k: (i, j))
    in_specs = [a_spec, b_spec] + ([o_spec] if res is not None else [])
    args = (a, b) + ((res,) if res is not None else ())
    (out,), got = _call(body, name, (m // tm, n // tn, nk), in_specs, [o_spec], [jax.ShapeDtypeStruct((m, n), out_dtype)],
                        args, scratch=[pltpu.VMEM((tm, tn), F32)] if nk > 1 else [],
                        sem=("parallel", "parallel", "arbitrary"), jobs=jobs)
    return (out, got) if jobs else out


def _rms_fwd(x, gain, name):
    t, d = x.shape
    tb = _tile(t, 256, 8)

    def body(x_ref, g_ref, h_ref):
        xv = x_ref[...]
        r = lax.rsqrt(jnp.mean(xv * xv, axis=-1, keepdims=True) + EPS)
        h_ref[...] = (xv * r * g_ref[...]).astype(BF16)

    return pl.pallas_call(
        body, name=name, grid=(t // tb,),
        in_specs=[pl.BlockSpec((tb, d), lambda i: (i, 0)), pl.BlockSpec((1, d), lambda i: (0, 0))],
        out_specs=pl.BlockSpec((tb, d), lambda i: (i, 0)), out_shape=jax.ShapeDtypeStruct((t, d), BF16),
        compiler_params=_params(("parallel",)))(x, gain.reshape(1, d))


def _rms_bwd(x, dh, gain, dres, name):
    t, d = x.shape
    tb = _tile(t, 256, 8)

    def body(x_ref, dh_ref, g_ref, dr_ref, dx_ref, dg_ref):
        @pl.when(pl.program_id(0) == 0)
        def _():
            dg_ref[...] = jnp.zeros_like(dg_ref)

        xv = x_ref[...]
        dy = dh_ref[...].astype(F32)
        r = lax.rsqrt(jnp.mean(xv * xv, axis=-1, keepdims=True) + EPS)
        xh = xv * r
        dxh = dy * g_ref[...]
        dx_ref[...] = dr_ref[...] + r * (dxh - xh * jnp.mean(dxh * xh, axis=-1, keepdims=True))
        dg_ref[...] += jnp.sum(dy * xh, axis=0, keepdims=True)

    row = pl.BlockSpec((tb, d), lambda i: (i, 0))
    vec = pl.BlockSpec((1, d), lambda i: (0, 0))
    return pl.pallas_call(
        body, name=name, grid=(t // tb,), in_specs=[row, row, vec, row], out_specs=[row, vec],
        out_shape=[jax.ShapeDtypeStruct((t, d), F32), jax.ShapeDtypeStruct((1, d), F32)],
        compiler_params=_params(("arbitrary",)))(x, dh, gain.reshape(1, d), dres)


def _loss_head(x, target, gain):
    t, d = x.shape
    tb = _tile(t, 256, 8)

    def body(x_ref, t_ref, g_ref, dx_ref, dg_ref, loss_ref):
        @pl.when(pl.program_id(0) == 0)
        def _():
            dg_ref[...] = jnp.zeros_like(dg_ref)
            loss_ref[...] = jnp.zeros_like(loss_ref)

        xv = x_ref[...]
        r = lax.rsqrt(jnp.mean(xv * xv, axis=-1, keepdims=True) + EPS)
        xh = xv * r
        err = xh * g_ref[...] - t_ref[...]
        per_row = jnp.mean(err * err, axis=-1, keepdims=True)
        loss_ref[...] += 0.5 * jnp.sum(per_row, axis=0, keepdims=True)
        dy = err * (1.0 / d)
        dxh = dy * g_ref[...]
        dx_ref[...] = r * (dxh - xh * jnp.mean(dxh * xh, axis=-1, keepdims=True))
        dg_ref[...] += jnp.sum(dy * xh, axis=0, keepdims=True)

    row = pl.BlockSpec((tb, d), lambda i: (i, 0))
    vec = pl.BlockSpec((1, d), lambda i: (0, 0))
    return pl.pallas_call(
        body, name="loss_head", grid=(t // tb,), in_specs=[row, row, vec],
        out_specs=[row, vec, pl.BlockSpec((1, LANES), lambda i: (0, 0))],
        out_shape=[jax.ShapeDtypeStruct((t, d), F32), jax.ShapeDtypeStruct((1, d), F32),
                   jax.ShapeDtypeStruct((1, LANES), F32)],
        compiler_params=_params(("arbitrary",)))(x, target, gain.reshape(1, d))


def _shift_down(v, s, rows):
    if s == 0:
        return v
    return jnp.where(rows >= s, pltpu.roll(v, s, 0), 0.0)


def _shift_up(v, s, rows):
    if s == 0:
        return v
    t = v.shape[0]
    return jnp.where(rows < t - s, pltpu.roll(v, t - s, 0), 0.0)


def _conv(v, w, rows):
    k = w.shape[0]
    out = v * w[k - 1:k, :]
    for s in range(1, k):
        out = out + _shift_down(v, s, rows) * w[k - 1 - s:k - s, :]
    return out


def _qkv_fwd(proj, conv_w, nq, nk):
    t = proj.shape[0]
    cw = conv_w.shape[1]
    nblk = cw // HEAD

    def body(p_ref, w_ref, o_ref):
        j = pl.program_id(0)
        rows = lax.broadcasted_iota(jnp.int32, (t, HEAD), 0)
        c = _conv(p_ref[...].astype(F32), w_ref[...], rows)
        a = c * _sigmoid(c)
        nrm = a * lax.rsqrt(jnp.sum(a * a, axis=-1, keepdims=True) + EPS)
        nrm = nrm * jnp.where(j < nq, HEAD ** -0.5, 1.0)
        o_ref[...] = jnp.where(j < nq + nk, nrm, a).astype(BF16)

    return pl.pallas_call(
        body, name="qkv_fwd", grid=(nblk,),
        in_specs=[pl.BlockSpec((t, HEAD), lambda j: (0, j)), pl.BlockSpec((conv_w.shape[0], HEAD), lambda j: (0, j))],
        out_specs=pl.BlockSpec((t, HEAD), lambda j: (0, j)), out_shape=jax.ShapeDtypeStruct((t, cw), BF16),
        compiler_params=_params(("parallel",)))(proj, conv_w)


def _qkv_bwd(proj, dqkv, conv_w, nq, nk):
    t = proj.shape[0]
    kw, cw = conv_w.shape
    nblk = cw // HEAD

    def body(p_ref, d_ref, w_ref, dp_ref, dw_ref):
        j = pl.program_id(0)
        rows = lax.broadcasted_iota(jnp.int32, (t, HEAD), 0)
        xv = p_ref[...].astype(F32)
        w = w_ref[...]
        c = _conv(xv, w, rows)
        a = c * _sigmoid(c)
        dy = d_ref[...].astype(F32)
        r = lax.rsqrt(jnp.sum(a * a, axis=-1, keepdims=True) + EPS)
        y = a * r
        scale = jnp.where(j < nq, HEAD ** -0.5, 1.0)
        da_n = scale * r * (dy - y * jnp.sum(dy * y, axis=-1, keepdims=True))
        da = jnp.where(j < nq + nk, da_n, dy)
        dc = da * _silu_grad(c)
        dx = dc * w[kw - 1:kw, :]
        dw_ref[kw - 1:kw, :] = jnp.sum(dc * xv, axis=0, keepdims=True)
        for s in range(1, kw):
            dx = dx + _shift_up(dc, s, rows) * w[kw - 1 - s:kw - s, :]
            dw_ref[kw - 1 - s:kw - s, :] = jnp.sum(dc * _shift_down(xv, s, rows), axis=0, keepdims=True)
        dp_ref[...] = dx.astype(BF16)

    blk = pl.BlockSpec((t, HEAD), lambda j: (0, j))
    wblk = pl.BlockSpec((kw, HEAD), lambda j: (0, j))
    return pl.pallas_call(
        body, name="qkv_bwd", grid=(nblk,), in_specs=[blk, blk, wblk], out_specs=[blk, wblk],
        out_shape=[jax.ShapeDtypeStruct((t, cw), BF16), jax.ShapeDtypeStruct((kw, cw), F32)],
        compiler_params=_params(("parallel",)))(proj, dqkv, conv_w)


def _softplus(v):
    return jnp.where(v < -15.0, jnp.exp(v), jnp.maximum(v, 0.0) + jnp.log(1.0 + jnp.exp(-jnp.abs(v))))


def _gate_fwd(ba, alog_pad, dtb_pad, hv):
    t = ba.shape[0]
    tb = _tile(t, 512, CA)

    def body(ba_ref, al_ref, dt_ref, o_ref):
        v = ba_ref[...]
        beta = _sigmoid(v)
        g = -jnp.exp(al_ref[...]) * _softplus(v + dt_ref[...])
        pos = lax.broadcasted_iota(jnp.int32, (tb, LANES), 0) % CA
        s = 1
        while s < CA:
            g = g + jnp.where(pos >= s, pltpu.roll(g, s, 0), 0.0)
            s *= 2
        lane = lax.broadcasted_iota(jnp.int32, (tb, LANES), 1)
        o_ref[...] = jnp.where(lane < hv, beta, g)

    row = pl.BlockSpec((tb, LANES), lambda i: (i, 0))
    vec = pl.BlockSpec((1, LANES), lambda i: (0, 0))
    return pl.pallas_call(
        body, name="gate_fwd", grid=(t // tb,), in_specs=[row, vec, vec], out_specs=row,
        out_shape=jax.ShapeDtypeStruct((t, LANES), F32), compiler_params=_params(("parallel",)))(ba, alog_pad, dtb_pad)


def _gate_bwd(ba, dbg, alog_pad, dtb_pad, hv):
    t = ba.shape[0]
    tb = _tile(t, 512, CA)

    def body(ba_ref, d_ref, al_ref, dt_ref, dba_ref, dal_ref, ddt_ref):
        @pl.when(pl.program_id(0) == 0)
        def _():
            dal_ref[...] = jnp.zeros_like(dal_ref)
            ddt_ref[...] = jnp.zeros_like(ddt_ref)

        v = ba_ref[...]
        d = d_ref[...]
        pos = lax.broadcasted_iota(jnp.int32, (tb, LANES), 0) % CA
        dg = d
        s = 1
        while s < CA:
            dg = dg + jnp.where(pos < CA - s, pltpu.roll(dg, tb - s, 0), 0.0)
            s *= 2
        beta = _sigmoid(v)
        na = -jnp.exp(al_ref[...])
        z = v + dt_ref[...]
        da = dg * na * _sigmoid(z)
        lane = lax.broadcasted_iota(jnp.int32, (tb, LANES), 1)
        in_a = jnp.logical_and(lane >= hv, lane < 2 * hv)
        da = jnp.where(in_a, da, 0.0)
        dba_ref[...] = jnp.where(lane < hv, d * beta * (1.0 - beta), da)
        ddt_ref[...] += jnp.sum(da, axis=0, keepdims=True)
        dal_ref[...] += jnp.sum(jnp.where(in_a, dg * na * _softplus(z), 0.0), axis=0, keepdims=True)

    row = pl.BlockSpec((tb, LANES), lambda i: (i, 0))
    vec = pl.BlockSpec((1, LANES), lambda i: (0, 0))
    return pl.pallas_call(
        body, name="gate_bwd", grid=(t // tb,), in_specs=[row, row, vec, vec], out_specs=[row, vec, vec],
        out_shape=[jax.ShapeDtypeStruct((t, LANES), F32), jax.ShapeDtypeStruct((1, LANES), F32),
                   jax.ShapeDtypeStruct((1, LANES), F32)],
        compiler_params=_params(("arbitrary",)))(ba, dbg, alog_pad, dtb_pad)


def _chunk_masks():
    r = lax.broadcasted_iota(jnp.int32, (CA, CA), 0)
    c = lax.broadcasted_iota(jnp.int32, (CA, CA), 1)
    return r >= c, r > c, (r == c).astype(F32)


def _inv_unit_lower(a, eye):
    x = eye - a
    ph, plo = _split(a)
    n = 1
    while n < CA // 2:
        ph, plo = _split(_dot3(ph, plo, ph, plo, _BNN))
        x = x + _dot3(*_split(x), ph, plo, _BNN)
        n *= 2
    return x


def _delta_pre(q, k, v, bcol, gc, gr, gl, causal, strict):
    eg = jnp.exp(gc)
    dm = jnp.exp(jnp.where(causal, gc[:, :, :CA] - gr, -jnp.inf))
    kb = k * bcol
    kkb = _bdot(kb, k, _BNT)
    a = jnp.where(strict, kkb * dm, 0.0)
    rhs = jnp.concatenate([v * bcol, kb * eg], axis=2)
    qk = _bdot(q, k, _BNT)
    ekd = jnp.exp(gl - gc)
    return dict(eg=eg, dm=dm, kb=kb, kkb=kkb, a=a, rhs=rhs, p=qk * dm, qd=q * eg, ekd=ekd, kd=k * ekd, cd=jnp.exp(gl))


DELTA_Q_HEADS_PER_BLOCK = 2


def _delta_fwd(qkvn, beta_b, gam_b, gam_r, gam_l, hqk, hv, jobs=()):
    t = qkvn.shape[0]
    rep = hv // hqk
    qpb = math.gcd(hqk, DELTA_Q_HEADS_PER_BLOCK)
    nh = qpb * rep
    rb = _tile(t, 512, CA)
    nb = t // rb
    ncb = rb // CA
    nc = t // CA

    def body(q_ref, k_ref, v_ref, b_ref, gc_ref, gr_ref, gl_ref, o_ref, s_ref, tm_ref, state, sol_sc, p_sc):
        @pl.when(pl.program_id(1) == 0)
        def _():
            state[...] = jnp.zeros_like(state)

        causal, strict, eye = _chunk_masks()

        def chunks(a):
            return a.astype(F32).reshape(ncb, CA, a.shape[-1])

        def head_cols(hh):
            return slice(hh * HEAD, (hh + 1) * HEAD)

        for hh in range(nh):
            qi = hh // rep
            pre = _delta_pre(chunks(q_ref[:, head_cols(qi)]), chunks(k_ref[:, head_cols(qi)]), chunks(v_ref[:, head_cols(hh)]),
                             chunks(b_ref[hh]), chunks(gc_ref[hh]), gr_ref[hh], gl_ref[hh], causal, strict)
            tm = _inv_unit_lower(pre["a"], eye)
            tm_ref[hh] = tm
            sol_sc[hh] = _hdot(tm, pre["rhs"], _BNN)
            p_sc[hh] = pre["p"]

        def chunk(n, carry):
            rows = pl.ds(pl.multiple_of(n * CA, CA), CA)
            for hh in range(nh):
                qi = hh // rep
                qn = q_ref[rows, head_cols(qi)].astype(F32)
                kn = k_ref[rows, head_cols(qi)].astype(F32)
                gc = gc_ref[hh, rows, :]
                gl = gl_ref[hh, n]
                s = state[hh]
                v_new = sol_sc[hh, n, :, :HEAD] - _bdot(sol_sc[hh, n, :, HEAD:], s)
                o_ref[rows, head_cols(hh)] = _bdot(qn * jnp.exp(gc), s) + _bdot(p_sc[hh, n], v_new)
                s_ref[hh, n] = s.astype(BF16)
                state[hh] = s * jnp.exp(gl) + _bdot(kn * jnp.exp(gl - gc), v_new, _TN)
            return carry

        lax.fori_loop(0, ncb, chunk, 0)

    koff, voff = hqk // qpb, 2 * hqk // nh
    per_chunk = lambda width: pl.BlockSpec((nh, ncb, 1, width), lambda j, i: (j, i, 0, 0))
    return _call(
        body, "delta_fwd", (hqk // qpb, nb),
        [pl.BlockSpec((rb, qpb * HEAD), lambda j, i: (i, j)),
         pl.BlockSpec((rb, qpb * HEAD), lambda j, i: (i, koff + j)),
         pl.BlockSpec((rb, nh * HEAD), lambda j, i: (i, voff + j)),
         pl.BlockSpec((nh, rb, LANES), lambda j, i: (j, i, 0)),
         pl.BlockSpec((nh, rb, LANES), lambda j, i: (j, i, 0)),
         per_chunk(CA), per_chunk(LANES)],
        [pl.BlockSpec((rb, nh * HEAD), lambda j, i: (i, j)),
         pl.BlockSpec((nh, ncb, HEAD, HEAD), lambda j, i: (j, i, 0, 0)),
         pl.BlockSpec((nh, ncb, CA, CA), lambda j, i: (j, i, 0, 0))],
        [jax.ShapeDtypeStruct((t, hv * HEAD), F32), jax.ShapeDtypeStruct((hv, nc, HEAD, HEAD), BF16),
         jax.ShapeDtypeStruct((hv, nc, CA, CA), F32)],
        (qkvn, qkvn, qkvn, beta_b, gam_b, gam_r, gam_l),
        scratch=[pltpu.VMEM((nh, HEAD, HEAD), F32), pltpu.VMEM((nh, ncb, CA, 2 * HEAD), F32),
                 pltpu.VMEM((nh, ncb, CA, CA), F32)],
        sem=("parallel", "arbitrary"), jobs=jobs)


def _delta_bwd(qkvn, beta_b, gam_b, gam_r, gam_l, s_all, tm_all, do, hqk, hv, jobs=()):
    t = qkvn.shape[0]
    rep = hv // hqk
    qpb = math.gcd(hqk, DELTA_Q_HEADS_PER_BLOCK)
    nh = qpb * rep
    rb = _tile(t, 512, CA)
    nb = t // rb
    ncb = rb // CA

    def body(q_ref, k_ref, v_ref, b_ref, gc_ref, gr_ref, gl_ref, s_ref, tm_ref, do_ref,
             dq_ref, dk_ref, dv_ref, db_ref, dg_ref, dstate, sol_sc, vn_sc, p_sc, kkb_sc, dvn_sc, ds_sc):
        @pl.when(pl.program_id(1) == 0)
        def _():
            dstate[...] = jnp.zeros_like(dstate)

        causal, strict, _ = _chunk_masks()
        ones = jnp.ones((ncb, CA, LANES), BF16)
        last = lax.broadcasted_iota(jnp.int32, (CA, LANES), 0) == CA - 1

        def chunks(a):
            return a.astype(F32).reshape(ncb, CA, a.shape[-1])

        def rows_of(a):
            return a.reshape(rb, a.shape[-1])

        def rowsum(m):
            return jnp.sum(m, axis=2, keepdims=True)

        def colsum(m):
            hi, lo = _split(m)
            return _bdot(hi, ones, _BTN) + _bdot(lo, ones, _BTN)

        def head_cols(hh):
            return slice(hh * HEAD, (hh + 1) * HEAD)

        def head_inputs(hh):
            q = chunks(q_ref[:, head_cols(hh // rep)])
            k = chunks(k_ref[:, head_cols(hh // rep)])
            v = chunks(v_ref[:, head_cols(hh)])
            bcol = chunks(b_ref[hh])
            return q, k, v, bcol, _delta_pre(q, k, v, bcol, chunks(gc_ref[hh]), gr_ref[hh], gl_ref[hh], causal, strict)

        for h in range(nh):
            pre = head_inputs(h)[-1]
            sol = _hdot(tm_ref[h], pre["rhs"], _BNN)
            sol_sc[h] = sol
            vn_sc[h] = sol[:, :, :HEAD] - _bdot(sol[:, :, HEAD:], s_ref[h], _BNN)
            p_sc[h] = pre["p"]
            kkb_sc[h] = pre["kkb"]

        def state_step(it, carry):
            n = ncb - 1 - it
            rows = pl.ds(pl.multiple_of(n * CA, CA), CA)
            for h in range(nh):
                qn = q_ref[rows, head_cols(h // rep)].astype(F32)
                kn = k_ref[rows, head_cols(h // rep)].astype(F32)
                gc = gc_ref[h, rows, :]
                gl = gl_ref[h, n]
                ds = dstate[h]
                ds_sc[h, n] = ds
                dov = do_ref[rows, h * HEAD:(h + 1) * HEAD].astype(F32)
                dvn = _bdot(p_sc[h, n], dov, _TN) + _bdot(kn * jnp.exp(gl - gc), ds)
                dvn_sc[h, n] = dvn
                dstate[h] = (ds * jnp.exp(gl) + _bdot(qn * jnp.exp(gc), dov, _TN)
                             - _bdot(sol_sc[h, n, :, HEAD:], dvn, _TN))
            return carry

        lax.fori_loop(0, ncb, state_step, 0)

        for h in range(nh):
            q, k, v, bcol, pre = head_inputs(h)
            if h % rep == 0:
                kkr = _bdot(k, k, _BNT)
                dq = jnp.zeros((ncb, CA, HEAD), F32)
                dk = jnp.zeros((ncb, CA, HEAD), F32)
            eg, dm, kb, qd, kd, cd = pre["eg"], pre["dm"], pre["kb"], pre["qd"], pre["kd"], pre["cd"]
            p = p_sc[h]
            sol = sol_sc[h]
            s = s_ref[h].astype(F32)
            ds = ds_sc[h]
            dov = chunks(do_ref[:, h * HEAD:(h + 1) * HEAD])
            v_new = vn_sc[h]
            dvn = dvn_sc[h]

            dp = jnp.where(causal, _bdot(dov, v_new, _BNT), 0.0)
            dqd = _bdot(dov, s, _BNT)
            dkd = _bdot(v_new, ds, _BNT)
            dcd = jnp.sum(rowsum(s * ds), axis=1, keepdims=True)
            dw = -_bdot(dvn, s, _BNT)

            drhs = _hdot(tm_ref[h], jnp.concatenate([dvn, dw], axis=2), _BTN)
            dbv, dbke = drhs[:, :, :HEAD], drhs[:, :, HEAD:]
            da = -jnp.where(strict, _bdot(drhs, sol, _BNT), 0.0)
            m = da * dm
            e = m * kkb_sc[h] + dp * p
            dgam = rowsum(e) - colsum(e) + rowsum(dbke * kb * eg) + rowsum(dqd * qd)
            r = rowsum(dkd * kd)
            tot = jnp.sum(r, axis=1, keepdims=True) + dcd * cd
            dgam = dgam - r + jnp.where(last, tot, 0.0)
            dbeta = rowsum(m * kkr) + rowsum(dbv * v) + rowsum(dbke * eg * k)
            nm = m * bcol[:, :, :CA]
            dqk = dp * dm
            dq = dq + _bdot(dqk, k, _BNN) + eg * dqd
            dk = (dk + _bdot(nm, k, _BNN) + _bdot(nm, k, _BTN) + _bdot(dqk, q, _BTN) + bcol * eg * dbke
                  + pre["ekd"] * dkd)
            dv_ref[:, h * HEAD:(h + 1) * HEAD] = rows_of(bcol * dbv)
            db_ref[h] = rows_of(jnp.broadcast_to(dbeta, (ncb, CA, LANES)))
            dg_ref[h] = rows_of(jnp.broadcast_to(dgam, (ncb, CA, LANES)))
            if h % rep == rep - 1:
                dq_ref[:, head_cols(h // rep)] = rows_of(dq)
                dk_ref[:, head_cols(h // rep)] = rows_of(dk)

    koff, voff = hqk // qpb, 2 * hqk // nh
    rv = lambda i: nb - 1 - i
    hd = pl.BlockSpec((nh, rb, LANES), lambda j, i: (j, rv(i), 0))
    qk_out = pl.BlockSpec((rb, qpb * HEAD), lambda j, i: (rv(i), j))
    v_blk = pl.BlockSpec((rb, nh * HEAD), lambda j, i: (rv(i), j))
    per_chunk = lambda *shape: pl.BlockSpec((nh, ncb) + shape, lambda j, i: (j, rv(i), 0, 0))
    return _call(
        body, "delta_bwd", (hqk // qpb, nb),
        [pl.BlockSpec((rb, qpb * HEAD), lambda j, i: (rv(i), j)),
         pl.BlockSpec((rb, qpb * HEAD), lambda j, i: (rv(i), koff + j)),
         pl.BlockSpec((rb, nh * HEAD), lambda j, i: (rv(i), voff + j)),
         hd, hd, per_chunk(1, CA), per_chunk(1, LANES), per_chunk(HEAD, HEAD), per_chunk(CA, CA), v_blk],
        [qk_out, qk_out, v_blk, hd, hd],
        [jax.ShapeDtypeStruct((t, hqk * HEAD), F32), jax.ShapeDtypeStruct((t, hqk * HEAD), F32),
         jax.ShapeDtypeStruct((t, hv * HEAD), F32),
         jax.ShapeDtypeStruct((hv, t, LANES), F32), jax.ShapeDtypeStruct((hv, t, LANES), F32)],
        (qkvn, qkvn, qkvn, beta_b, gam_b, gam_r, gam_l, s_all, tm_all, do),
        scratch=[pltpu.VMEM((nh, HEAD, HEAD), F32), pltpu.VMEM((nh, ncb, CA, 2 * HEAD), F32),
                 pltpu.VMEM((nh, ncb, CA, HEAD), F32), pltpu.VMEM((nh, ncb, CA, CA), F32),
                 pltpu.VMEM((nh, ncb, CA, CA), F32), pltpu.VMEM((nh, ncb, CA, HEAD), F32),
                 pltpu.VMEM((nh, ncb, HEAD, HEAD), F32)],
        sem=("parallel", "arbitrary"), jobs=jobs)


def _apost_fwd(o, proj, gain, zoff, hv):
    t = o.shape[0]
    tb = _tile(t, 1024, 8)
    zb = zoff // HEAD

    def body(o_ref, z_ref, g_ref, y_ref):
        ov = o_ref[...]
        z = z_ref[...].astype(F32)
        r = lax.rsqrt(jnp.mean(ov * ov, axis=-1, keepdims=True) + EPS)
        y_ref[...] = (ov * r * g_ref[...] * (z * _sigmoid(z))).astype(BF16)

    blk = pl.BlockSpec((tb, HEAD), lambda i, h: (i, h))
    return pl.pallas_call(
        body, name="apost_fwd", grid=(t // tb, hv),
        in_specs=[blk, pl.BlockSpec((tb, HEAD), lambda i, h: (i, zb + h)), pl.BlockSpec((1, HEAD), lambda i, h: (0, 0))],
        out_specs=blk, out_shape=jax.ShapeDtypeStruct((t, hv * HEAD), BF16),
        compiler_params=_params(("parallel", "parallel")))(o, proj, gain.reshape(1, HEAD))


def _apost_bwd(o, proj, gain, dy, zoff, hv):
    t = o.shape[0]
    tb = _tile(t, 1024, 8)
    zb = zoff // HEAD

    def body(o_ref, z_ref, g_ref, dy_ref, do_ref, dz_ref, dg_ref):
        @pl.when(jnp.logical_and(pl.program_id(0) == 0, pl.program_id(1) == 0))
        def _():
            dg_ref[...] = jnp.zeros_like(dg_ref)

        ov = o_ref[...]
        z = z_ref[...].astype(F32)
        d = dy_ref[...].astype(F32)
        r = lax.rsqrt(jnp.mean(ov * ov, axis=-1, keepdims=True) + EPS)
        oh = ov * r
        sz = z * _sigmoid(z)
        dn = d * sz
        dz_ref[...] = (d * oh * g_ref[...] * _silu_grad(z)).astype(BF16)
        doh = dn * g_ref[...]
        do_ref[...] = r * (doh - oh * jnp.mean(doh * oh, axis=-1, keepdims=True))
        dg_ref[...] += jnp.sum(dn * oh, axis=0, keepdims=True)

    blk = pl.BlockSpec((tb, HEAD), lambda i, h: (i, h))
    vec = pl.BlockSpec((1, HEAD), lambda i, h: (0, 0))
    return pl.pallas_call(
        body, name="apost_bwd", grid=(t // tb, hv),
        in_specs=[blk, pl.BlockSpec((tb, HEAD), lambda i, h: (i, zb + h)), vec, blk],
        out_specs=[blk, blk, vec],
        out_shape=[jax.ShapeDtypeStruct((t, hv * HEAD), F32), jax.ShapeDtypeStruct((t, hv * HEAD), BF16),
                   jax.ShapeDtypeStruct((1, HEAD), F32)],
        compiler_params=_params(("arbitrary", "arbitrary")))(o, proj, gain.reshape(1, HEAD), dy)


def _sgu_fwd(proj, gain, w_s, b_t, uoff, wb):
    t = proj.shape[0]
    ng = wb // HEAD

    def body(u_ref, v_ref, g_ref, w_ref, b_ref, o_ref):
        r_i = lax.broadcasted_iota(jnp.int32, (HEAD, HEAD), 0)
        c_i = lax.broadcasted_iota(jnp.int32, (HEAD, HEAD), 1)
        u = _gelu(u_ref[...].astype(F32))
        vg = _gelu(v_ref[...].astype(F32))
        vn = vg * lax.rsqrt(jnp.mean(vg * vg, axis=-1, keepdims=True) + EPS) * g_ref[...]
        for g in range(ng):
            cols = slice(g * HEAD, (g + 1) * HEAD)
            wg = jnp.where(r_i >= c_i, w_ref[g], 0.0)
            mixed = _bdot(wg, vn[:, cols]) + b_ref[:, g:g + 1]
            o_ref[:, cols] = (u[:, cols] * mixed).astype(BF16)

    ub, vb = uoff // wb, uoff // wb + 1
    return pl.pallas_call(
        body, name="sgu_fwd", grid=(t // HEAD,),
        in_specs=[pl.BlockSpec((HEAD, wb), lambda i: (i, ub)), pl.BlockSpec((HEAD, wb), lambda i: (i, vb)),
                  pl.BlockSpec((1, wb), lambda i: (0, 0)), pl.BlockSpec((ng, HEAD, HEAD), lambda i: (0, 0, 0)),
                  pl.BlockSpec((HEAD, ng), lambda i: (0, 0))],
        out_specs=pl.BlockSpec((HEAD, wb), lambda i: (i, 0)), out_shape=jax.ShapeDtypeStruct((t, wb), BF16),
        compiler_params=_params(("parallel",)))(proj, proj, gain.reshape(1, wb), w_s, b_t)


def _sgu_bwd(proj, gain, w_s, b_t, dout, uoff, wb):
    t = proj.shape[0]
    ng = wb // HEAD

    def body(u_ref, v_ref, g_ref, w_ref, b_ref, d_ref, du_ref, dv_ref, dw_ref, db_ref, dg_ref, dvn_ref):
        @pl.when(pl.program_id(0) == 0)
        def _():
            dw_ref[...] = jnp.zeros_like(dw_ref)
            db_ref[...] = jnp.zeros_like(db_ref)
            dg_ref[...] = jnp.zeros_like(dg_ref)

        r_i = lax.broadcasted_iota(jnp.int32, (HEAD, HEAD), 0)
        c_i = lax.broadcasted_iota(jnp.int32, (HEAD, HEAD), 1)
        tril = r_i >= c_i
        ub = u_ref[...].astype(F32)
        vb = v_ref[...].astype(F32)
        u = _gelu(ub)
        vg = _gelu(vb)
        r = lax.rsqrt(jnp.mean(vg * vg, axis=-1, keepdims=True) + EPS)
        vh = vg * r
        vn = vh * g_ref[...]
        d = d_ref[...].astype(F32)
        for g in range(ng):
            cols = slice(g * HEAD, (g + 1) * HEAD)
            wg = jnp.where(tril, w_ref[g], 0.0)
            mixed = _bdot(wg, vn[:, cols]) + b_ref[:, g:g + 1]
            du_ref[:, cols] = (d[:, cols] * mixed * _gelu_grad(ub[:, cols])).astype(BF16)
            dmix = d[:, cols] * u[:, cols]
            dw_ref[g] += jnp.where(tril, _bdot(dmix, vn[:, cols], _NT), 0.0)
            db_ref[g] += jnp.broadcast_to(jnp.sum(dmix, axis=1, keepdims=True), (HEAD, HEAD))
            dvn_ref[:, cols] = _bdot(wg, dmix, _TN)
        dvn = dvn_ref[...]
        dg_ref[...] += jnp.sum(dvn * vh, axis=0, keepdims=True)
        dvh = dvn * g_ref[...]
        dvg = r * (dvh - vh * jnp.mean(dvh * vh, axis=-1, keepdims=True))
        dv_ref[...] = (dvg * _gelu_grad(vb)).astype(BF16)

    ub_i, vb_i = uoff // wb, uoff // wb + 1
    row = pl.BlockSpec((HEAD, wb), lambda i: (i, 0))
    mat = pl.BlockSpec((ng, HEAD, HEAD), lambda i: (0, 0, 0))
    vec = pl.BlockSpec((1, wb), lambda i: (0, 0))
    return pl.pallas_call(
        body, name="sgu_bwd", grid=(t // HEAD,),
        in_specs=[pl.BlockSpec((HEAD, wb), lambda i: (i, ub_i)), pl.BlockSpec((HEAD, wb), lambda i: (i, vb_i)),
                  vec, mat, pl.BlockSpec((HEAD, ng), lambda i: (0, 0)), row],
        out_specs=[row, row, mat, mat, vec],
        out_shape=[jax.ShapeDtypeStruct((t, wb), BF16), jax.ShapeDtypeStruct((t, wb), BF16),
                   jax.ShapeDtypeStruct((ng, HEAD, HEAD), F32), jax.ShapeDtypeStruct((ng, HEAD, HEAD), F32),
                   jax.ShapeDtypeStruct((1, wb), F32)],
        scratch_shapes=[pltpu.VMEM((HEAD, wb), F32)],
        compiler_params=_params(("arbitrary",)))(proj, proj, gain.reshape(1, wb), w_s, b_t, dout)


def _merge_specs(t, d, goff):
    tb = _tile(t, 512, 8)
    tc = _tile(d, 512)
    gb = goff // tc
    nd = d // tc
    blk = pl.BlockSpec((tb, tc), lambda i, j: (i, j))
    ga = pl.BlockSpec((tb, tc), lambda i, j: (i, gb + j))
    gbs = pl.BlockSpec((tb, tc), lambda i, j: (i, gb + nd + j))
    return (t // tb, nd), blk, ga, gbs


def _merge_fwd(ya, yb, proj, goff):
    t, d = ya.shape
    grid, blk, ga, gbs = _merge_specs(t, d, goff)

    def body(ya_ref, yb_ref, ga_ref, gb_ref, o_ref):
        o_ref[...] = (_sigmoid(ga_ref[...].astype(F32)) * ya_ref[...].astype(F32)
                      + _sigmoid(gb_ref[...].astype(F32)) * yb_ref[...].astype(F32)).astype(BF16)

    return pl.pallas_call(
        body, name="merge_fwd", grid=grid, in_specs=[blk, blk, ga, gbs], out_specs=blk,
        out_shape=jax.ShapeDtypeStruct((t, d), BF16),
        compiler_params=_params(("parallel", "parallel")))(ya, yb, proj, proj)


def _merge_bwd(dm, ya, yb, proj, goff):
    t, d = ya.shape
    grid, blk, ga, gbs = _merge_specs(t, d, goff)

    def body(dm_ref, ya_ref, yb_ref, ga_ref, gb_ref, dya_ref, dyb_ref, dga_ref, dgb_ref):
        dmv = dm_ref[...].astype(F32)
        sa = _sigmoid(ga_ref[...].astype(F32))
        sb = _sigmoid(gb_ref[...].astype(F32))
        dya_ref[...] = (dmv * sa).astype(BF16)
        dyb_ref[...] = (dmv * sb).astype(BF16)
        dga_ref[...] = (dmv * ya_ref[...].astype(F32) * sa * (1.0 - sa)).astype(BF16)
        dgb_ref[...] = (dmv * yb_ref[...].astype(F32) * sb * (1.0 - sb)).astype(BF16)

    shp = jax.ShapeDtypeStruct((t, d), BF16)
    return pl.pallas_call(
        body, name="merge_bwd", grid=grid, in_specs=[blk, blk, blk, ga, gbs], out_specs=[blk] * 4,
        out_shape=[shp] * 4, compiler_params=_params(("parallel", "parallel")))(dm, ya, yb, proj, proj)


def _ffn_act_fwd(up, conv_w, bias, dff):
    t = up.shape[0]
    nblk = dff // HEAD
    kw = conv_w.shape[0]

    def body(g_ref, v_ref, wg_ref, wv_ref, bg_ref, bv_ref, o_ref, cg_ref, cv_ref):
        rows = lax.broadcasted_iota(jnp.int32, (t, HEAD), 0)
        cg = _conv(g_ref[...].astype(F32), wg_ref[...], rows) + bg_ref[...]
        cv = _conv(v_ref[...].astype(F32), wv_ref[...], rows) + bv_ref[...]
        o_ref[...] = (cg * _sigmoid(cg) * cv).astype(BF16)
        cg_ref[...] = cg.astype(BF16)
        cv_ref[...] = cv.astype(BF16)

    blk = pl.BlockSpec((t, HEAD), lambda j: (0, j))
    shp = jax.ShapeDtypeStruct((t, dff), BF16)
    return pl.pallas_call(
        body, name="ffn_act_fwd", grid=(nblk,),
        in_specs=[blk, pl.BlockSpec((t, HEAD), lambda j: (0, nblk + j)),
                  pl.BlockSpec((kw, HEAD), lambda j: (0, j)), pl.BlockSpec((kw, HEAD), lambda j: (0, nblk + j)),
                  pl.BlockSpec((1, HEAD), lambda j: (0, j)), pl.BlockSpec((1, HEAD), lambda j: (0, nblk + j))],
        out_specs=[blk, blk, blk], out_shape=[shp, shp, shp],
        compiler_params=_params(("parallel",)))(up, up, conv_w, conv_w, bias, bias)


def _ffn_act_bwd(up, cg, cv, dact, conv_w, dff, jobs=()):
    t = up.shape[0]
    nblk = dff // HEAD
    kw = conv_w.shape[0]

    def body(me_ref, cg_ref, cv_ref, d_ref, wm_ref, dup_ref, dw_ref, db_ref):
        is_gate = pl.program_id(0) < nblk
        rows = lax.broadcasted_iota(jnp.int32, (t, HEAD), 0)
        xv = me_ref[...].astype(F32)
        w = wm_ref[...]
        g = cg_ref[...].astype(F32)
        s = _sigmoid(g)
        d = d_ref[...].astype(F32)
        dc = d * jnp.where(is_gate, cv_ref[...].astype(F32) * (s + g * s * (1.0 - s)), g * s)
        db_ref[...] = jnp.sum(dc, axis=0, keepdims=True)
        dx = dc * w[kw - 1:kw, :]
        dw_ref[kw - 1:kw, :] = jnp.sum(dc * xv, axis=0, keepdims=True)
        for sh in range(1, kw):
            dx = dx + _shift_up(dc, sh, rows) * w[kw - 1 - sh:kw - sh, :]
            dw_ref[kw - 1 - sh:kw - sh, :] = jnp.sum(dc * _shift_down(xv, sh, rows), axis=0, keepdims=True)
        dup_ref[...] = dx.astype(BF16)

    me = pl.BlockSpec((t, HEAD), lambda j: (0, j))
    pair = pl.BlockSpec((t, HEAD), lambda j: (0, j % nblk))
    wme = pl.BlockSpec((kw, HEAD), lambda j: (0, j))
    return _call(
        body, "ffn_act_bwd", (2 * nblk,), [me, pair, pair, pair, wme],
        [me, wme, pl.BlockSpec((1, HEAD), lambda j: (0, j))],
        [jax.ShapeDtypeStruct((t, 2 * dff), BF16), jax.ShapeDtypeStruct((kw, 2 * dff), F32),
         jax.ShapeDtypeStruct((1, 2 * dff), F32)],
        (up, cg, cv, dact, conv_w), sem=("parallel",), jobs=jobs)


def _ple_fwd(x, gt, pp):
    t, d = x.shape
    tb, tc = _tile(t, 512, 8), _tile(d, 1024)

    def body(x_ref, g_ref, p_ref, o_ref):
        o_ref[...] = x_ref[...] + _sigmoid(g_ref[...].astype(F32)) * p_ref[...].astype(F32)

    blk = pl.BlockSpec((tb, tc), lambda i, j: (i, j))
    return pl.pallas_call(
        body, name="ple_fwd", grid=(t // tb, d // tc), in_specs=[blk, blk, blk], out_specs=blk,
        out_shape=jax.ShapeDtypeStruct((t, d), F32), compiler_params=_params(("parallel", "parallel")))(x, gt, pp)


def _ple_bwd(dx, gt, pp):
    t, d = dx.shape
    tb, tc = _tile(t, 512, 8), _tile(d, 1024)

    def body(dx_ref, g_ref, p_ref, dg_ref, dp_ref):
        dv = dx_ref[...]
        s = _sigmoid(g_ref[...].astype(F32))
        dg_ref[...] = (dv * p_ref[...].astype(F32) * s * (1.0 - s)).astype(BF16)
        dp_ref[...] = (dv * s).astype(BF16)

    blk = pl.BlockSpec((tb, tc), lambda i, j: (i, j))
    shp = jax.ShapeDtypeStruct((t, d), BF16)
    return pl.pallas_call(
        body, name="ple_bwd", grid=(t // tb, d // tc), in_specs=[blk, blk, blk], out_specs=[blk, blk],
        out_shape=[shp, shp], compiler_params=_params(("parallel", "parallel")))(dx, gt, pp)


def _adam(pieces, w, m, v, name, jobs=()):
    nq = len(pieces)
    npart, rp, c = pieces[0].shape
    per_layer = nq // w.shape[0]
    row_bytes = 2 * c * (nq * npart * pieces[0].dtype.itemsize + 7 * 4)
    tr = _tile(rp, max(16, min(512, ADAM_VMEM_BUDGET // row_bytes)), 16)
    nblk = rp // tr
    c1 = 1.0 - ADAM_B1 ** ADAM_STEP
    c2 = 1.0 - ADAM_B2 ** ADAM_STEP

    def body(*refs):
        p_refs = refs[:nq]
        w_ref, m_ref, v_ref, g_ref, d_ref, mo_ref, vo_ref = refs[nq:]
        for q in range(nq):
            @pl.when(pl.program_id(0) == q)
            def _(p_ref=p_refs[q]):
                g = p_ref[0].astype(F32)
                for i in range(1, npart):
                    g = g + p_ref[i].astype(F32)
                mn = ADAM_B1 * m_ref[...] + (1.0 - ADAM_B1) * g
                vn = ADAM_B2 * v_ref[...] + (1.0 - ADAM_B2) * (g * g)
                g_ref[...] = g
                mo_ref[...] = mn
                vo_ref[...] = vn
                d_ref[...] = -ADAM_LR * ((mn / c1) / (jnp.sqrt(vn / c2) + ADAM_EPS) + ADAM_WD * w_ref[...])

    def piece_spec(q):
        return pl.BlockSpec((npart, tr, c), lambda i, r: (0, jnp.where(i == q, r, jnp.where(i < q, 0, nblk - 1)), 0))

    blk = pl.BlockSpec((None, tr, c), lambda i, r: (i // per_layer, (i % per_layer) * nblk + r, 0))
    shp = jax.ShapeDtypeStruct(w.shape, F32)
    return _call(body, name, (nq, nblk), [piece_spec(q) for q in range(nq)] + [blk] * 3, [blk] * 4, [shp] * 4,
                 (*pieces, w, m, v), sem=("parallel", "parallel"), jobs=jobs)


_BIG = ("w_in", "w_branch_a", "w_branch_b", "w_out", "w_ffn_up", "w_ffn_down", "w_ple_gate", "w_ple_proj")
_COL_SHARDED = ("w_in", "w_branch_b", "w_ffn_up", "w_ple_proj")
_CONVS = ("conv_qkv", "conv_ffn")
_GATHER_ON_PROJ = ("w_ffn_up",)
_GATHER_ON_DELTA = ("w_ffn_down",)
_GATHER_AHEAD = ("w_in", "conv_qkv")
_GATHER_ON_UP = ("w_ple_gate", "w_ple_proj")
_GATHER_AHEAD_2 = ("w_branch_a", "w_branch_b", "w_out", "conv_ffn")
_SCATTER_ON_DACT = ("w_ple_gate", "w_ple_proj")
_SCATTER_ON_DELTA = ("w_ffn_up",)
_SCATTER_ON_DW_MAIN = ("w_out", "w_branch_a", "w_branch_b")
_SCATTER_ON_DH1 = ("w_ffn_down",)
_SMALL = ("norm_mix", "conv_qkv", "a_log", "dt_bias", "head_norm", "sgu_norm", "w_spatial", "b_spatial", "norm_ffn",
          "conv_ffn", "b_conv_ffn", "norm_ple", "norm_final")
_WEIGHTS = ("norm_mix", "w_in", "conv_qkv", "a_log", "dt_bias", "head_norm", "sgu_norm", "w_spatial", "b_spatial",
            "w_branch_a", "w_branch_b", "w_out", "norm_ffn", "w_ffn_up", "conv_ffn", "b_conv_ffn", "w_ffn_down",
            "norm_ple", "w_ple_gate", "w_ple_proj", "norm_final")


def _full_cols(g):
    return jnp.transpose(g, (1, 0, 2)).reshape(g.shape[1], N_DEV * g.shape[2])


def _full_rows(g):
    return g.reshape(N_DEV * g.shape[1], g.shape[2])


def _split_cols(dw):
    k, n = dw.shape
    return jnp.transpose(dw.reshape(k, N_DEV, n // N_DEV), (1, 0, 2))


def _split_rows(dw):
    k, n = dw.shape
    return dw.reshape(N_DEV, k // N_DEV, n)


def _pad_lanes(v, width=LANES, offset=0):
    return jnp.pad(v, ((0, 0), (offset, width - offset - v.shape[1])))


def kernel(x, p, norm_mix, w_in, conv_qkv, a_log, dt_bias, head_norm, sgu_norm, w_spatial, b_spatial, w_branch_a, w_branch_b, w_out, norm_ffn, w_ffn_up, conv_ffn, b_conv_ffn, w_ffn_down, norm_ple, w_ple_gate, w_ple_proj, norm_final, loss_target, m_norm_mix, m_w_in, m_conv_qkv, m_a_log, m_dt_bias, m_head_norm, m_sgu_norm, m_w_spatial, m_b_spatial, m_w_branch_a, m_w_branch_b, m_w_out, m_norm_ffn, m_w_ffn_up, m_conv_ffn, m_b_conv_ffn, m_w_ffn_down, m_norm_ple, m_w_ple_gate, m_w_ple_proj, m_norm_final, v_norm_mix, v_w_in, v_conv_qkv, v_a_log, v_dt_bias, v_head_norm, v_sgu_norm, v_w_spatial, v_b_spatial, v_w_branch_a, v_w_branch_b, v_w_out, v_norm_ffn, v_w_ffn_up, v_conv_ffn, v_b_conv_ffn, v_w_ffn_down, v_norm_ple, v_w_ple_gate, v_w_ple_proj, v_norm_final):
    env = dict(locals())
    wts = {n: env[n] for n in _WEIGHTS}
    mom_m = {n: env["m_" + n] for n in _WEIGHTS}
    mom_v = {n: env["v_" + n] for n in _WEIGHTS}

    xin = x[0]
    tgt = loss_target[0]
    t, d = xin.shape
    depth = w_in.shape[0]
    hv = a_log.shape[1]
    vw = hv * HEAD
    wb = sgu_norm.shape[1]
    ng = w_spatial.shape[1]
    n_in = w_in.shape[2] * N_DEV
    qk = (n_in - 2 * vw - 2 * hv - 2 * wb - 2 * d) // 2
    hqk = qk // HEAD
    dff = w_ffn_down.shape[1] * N_DEV
    cw = 2 * qk + vw
    o_z, o_ba = 2 * qk + vw, 2 * qk + 2 * vw
    o_ub = o_ba
    o_ga = o_ub + 2 * wb
    ns = w_in.shape[2]
    in_segments = ((0, 0, o_ba), (1, o_ba, o_ba + 2 * hv), (2, o_ba + 2 * hv, n_in))
    me = 4 * lax.axis_index("x") + 2 * lax.axis_index("y") + lax.axis_index("c")

    full = [dict() for _ in range(depth)]
    staged = {}

    def as_cols(n):
        return "_cols" if (n in _COL_SHARDED or n in _CONVS) and wts[n].shape[-1] % LANES == 0 else ""

    def chips(i, names):
        names = names if i < depth else ()
        return [(i, n) for n in names], [(wts[n][i].astype(BF16) if n in _BIG else wts[n][i], "chips" + as_cols(n))
                                         for n in names]

    def relay(keys):
        return list(keys), [(staged.pop(key), "relay" + as_cols(key[1])) for key in keys]

    def settle(chip_keys, relay_keys, results):
        for key, g in zip(chip_keys, results):
            staged[key] = g
        for (i, n), g in zip(relay_keys, results[len(chip_keys):]):
            keep_blocks = as_cols(n) or n == "w_in"
            full[i][n] = g if keep_blocks else _full_cols(g) if n in _COL_SHARDED or n in _CONVS else _full_rows(g)

    ck, cj = chips(0, _GATHER_AHEAD + _GATHER_AHEAD_2)
    settle(ck, [], _exchange(cj, "gather_first"))
    rk, rj = relay(ck)
    settle([], rk, _exchange(rj, "relay_first"))

    saved = []
    xc = xin
    for i in range(depth):
        fw = full[i]
        main, gates = [], []
        for dev, (seg, lo, hi) in itertools.product(range(N_DEV), in_segments):
            a, b = max(lo, dev * ns), min(hi, (dev + 1) * ns)
            if a < b:
                (gates if seg == 1 else main).append(fw["w_in"][dev][:, a - dev * ns:b - dev * ns])
        fw["w_main"] = jnp.concatenate(main, axis=1)
        fw["w_ba"] = _pad_lanes(jnp.concatenate(gates, axis=1))
        s = {"x0": xc}
        s["h1"] = _rms_fwd(xc, norm_mix[i], "rms_fwd")
        ck, cj = chips(i, _GATHER_ON_PROJ)
        s["proj"], got = _matmul(s["h1"], fw["w_main"], "nn", BF16, "mm_proj", jobs=cj)
        settle(ck, [], got)
        s["ba"] = _matmul(s["h1"], fw["w_ba"], "nn", F32, "mm_ba")
        s["qkvn"] = _qkv_fwd(s["proj"], fw["conv_qkv"], hqk, hqk)
        s["alog"] = _pad_lanes(a_log[i][None, :], offset=hv)
        s["dtb"] = _pad_lanes(dt_bias[i][None, :], offset=hv)
        bg = _gate_fwd(s["ba"], s["alog"], s["dtb"], hv)
        beta_t = bg[:, :hv].T
        gam_t = bg[:, hv:2 * hv].T
        s["beta_b"] = jnp.broadcast_to(beta_t[:, :, None], (hv, t, LANES))
        s["gam_b"] = jnp.broadcast_to(gam_t[:, :, None], (hv, t, LANES))
        s["gam_r"] = gam_t.reshape(hv, t // CA, 1, CA)
        s["gam_l"] = jnp.broadcast_to(s["gam_r"][:, :, :, CA - 1:], (hv, t // CA, 1, LANES))
        ck1, cj1 = chips(i, _GATHER_ON_DELTA)
        ck2, cj2 = chips(i + 1, _GATHER_AHEAD)
        rk, rj = relay([(i, n) for n in _GATHER_ON_PROJ])
        (s["o"], s["s_all"], s["tm_all"]), got = _delta_fwd(
            s["qkvn"], s["beta_b"], s["gam_b"], s["gam_r"], s["gam_l"], hqk, hv, jobs=cj1 + cj2 + rj)
        settle(ck1 + ck2, rk, got)
        s["outa"] = _apost_fwd(s["o"], s["proj"], head_norm[i], o_z, hv)
        s["b_t"] = b_spatial[i].T
        s["outb"] = _sgu_fwd(s["proj"], sgu_norm[i], w_spatial[i], s["b_t"], o_ub, wb)
        s["ya"] = _matmul(s["outa"], fw["w_branch_a"], "nn", BF16, "mm_ya")
        s["yb"] = _matmul(s["outb"], fw["w_branch_b"], "nn", BF16, "mm_yb")
        s["mg"] = _merge_fwd(s["ya"], s["yb"], s["proj"], o_ga)
        s["x1"] = _matmul(s["mg"], fw["w_out"], "nn", F32, "mm_out", res=xc)
        s["h2"] = _rms_fwd(s["x1"], norm_ffn[i], "rms_fwd")
        nxt = i + 1 < depth
        ck1, cj1 = chips(i, _GATHER_ON_UP)
        ck2, cj2 = chips(i + 1, _GATHER_AHEAD_2)
        rk, rj = relay([(i, n) for n in _GATHER_ON_DELTA] + ([(i + 1, n) for n in _GATHER_AHEAD] if nxt else []))
        s["up"], got = _matmul(s["h2"], fw["w_ffn_up"], "nn", BF16, "mm_up", jobs=cj1 + cj2 + rj)
        settle(ck1 + ck2, rk, got)
        s["act"], s["cg"], s["cv"] = _ffn_act_fwd(s["up"], fw["conv_ffn"], b_conv_ffn[i][None, :], dff)
        rk, rj = relay([(i, n) for n in _GATHER_ON_UP] + ([(i + 1, n) for n in _GATHER_AHEAD_2] if nxt else []))
        s["x2"], got = _matmul(s["act"], fw["w_ffn_down"], "nn", F32, "mm_down", res=s["x1"], jobs=rj)
        settle([], rk, got)
        s["h3"] = _rms_fwd(s["x2"], norm_ple[i], "rms_fwd")
        s["gt"] = _matmul(s["h3"], fw["w_ple_gate"], "nn", BF16, "mm_gt")
        s["pp"] = _matmul(p[i, 0], fw["w_ple_proj"], "nn", BF16, "mm_pp")
        xc = _ple_fwd(s["x2"], s["gt"], s["pp"])
        saved.append(s)

    dx, g_norm_final, loss_part = _loss_head(xc, tgt, norm_final)

    small = {n: [None] * depth for n in _SMALL if n != "norm_final"}
    recv = {n: [None] * depth for n in _BIG}
    recv["w_in"] = [None] * (2 * depth)

    def scatter_jobs(gw, names):
        return [(gw[n], "scatter_cols") if as_cols(n) else
                (_split_cols(gw[n]) if n in _COL_SHARDED else _split_rows(gw[n]), "scatter") for n in names]

    def keep(i, names, results):
        for n, r in zip(names, results):
            recv[n][i] = r

    def carry(jobs, *args):
        return _matmul(*args, jobs=jobs) if jobs else (_matmul(*args), [])

    later = []
    for i in reversed(range(depth)):
        fw, s = full[i], saved[i]
        dgt, dpp = _ple_bwd(dx, s["gt"], s["pp"])
        gw = {"w_ple_gate": _matmul(s["h3"], dgt, "tn", BF16, "mm_dw_gt"),
              "w_ple_proj": _matmul(p[i, 0], dpp, "tn", BF16, "mm_dw_pp")}
        dh3 = _matmul(dgt, fw["w_ple_gate"], "nt", F32, "mm_dh3")
        dx, small["norm_ple"][i] = _rms_bwd(s["x2"], dh3, norm_ple[i], dx, "rms_bwd")

        dact, got = _matmul(dx, fw["w_ffn_down"], "nt", BF16, "mm_dact", jobs=scatter_jobs(gw, _SCATTER_ON_DACT))
        keep(i, _SCATTER_ON_DACT, got)
        gw["w_ffn_down"] = _matmul(s["act"], dx, "tn", BF16, "mm_dw_down")
        (dup, small["conv_ffn"][i], small["b_conv_ffn"][i]), _ = _ffn_act_bwd(
            s["up"], s["cg"], s["cv"], dact, fw["conv_ffn"], dff)
        gw["w_ffn_up"], got = carry(later[:1], s["h2"], dup, "tn", BF16, "mm_dw_up")
        keep(2 * i + 2, ("w_in",), got)
        dh2, got = carry(later[1:], dup, fw["w_ffn_up"], "nt", F32, "mm_dh2")
        keep(2 * i + 3, ("w_in",), got)
        dx, small["norm_ffn"][i] = _rms_bwd(s["x1"], dh2, norm_ffn[i], dx, "rms_bwd")

        dmg = _matmul(dx, fw["w_out"], "nt", BF16, "mm_dmg")
        gw["w_out"] = _matmul(s["mg"], dx, "tn", BF16, "mm_dw_out")
        dya, dyb, dga, dgb = _merge_bwd(dmg, s["ya"], s["yb"], s["proj"], o_ga)
        gw["w_branch_a"] = _matmul(s["outa"], dya, "tn", BF16, "mm_dw_a")
        gw["w_branch_b"] = _matmul(s["outb"], dyb, "tn", BF16, "mm_dw_b")
        douta = _matmul(dya, fw["w_branch_a"], "nt", BF16, "mm_douta")
        doutb = _matmul(dyb, fw["w_branch_b"], "nt", BF16, "mm_doutb")
        dub, dvb, small["w_spatial"][i], db_s, dsg = _sgu_bwd(s["proj"], sgu_norm[i], w_spatial[i], s["b_t"], doutb, o_ub, wb)
        small["b_spatial"][i] = db_s[:, :, 0]
        small["sgu_norm"][i] = dsg
        do, dz, small["head_norm"][i] = _apost_bwd(s["o"], s["proj"], head_norm[i], douta, o_z, hv)
        (dq, dk, dv, db_b, dg_b), got = _delta_bwd(
            s["qkvn"], s["beta_b"], s["gam_b"], s["gam_r"], s["gam_l"], s["s_all"], s["tm_all"], do, hqk, hv,
            jobs=scatter_jobs(gw, _SCATTER_ON_DELTA))
        keep(i, _SCATTER_ON_DELTA, got)
        dbg = _pad_lanes(jnp.concatenate([db_b[:, :, 0].T, dg_b[:, :, 0].T], axis=1))
        dba, dal, ddt = _gate_bwd(s["ba"], dbg, s["alog"], s["dtb"], hv)
        small["a_log"][i] = dal[:, hv:2 * hv]
        small["dt_bias"][i] = ddt[:, hv:2 * hv]
        dqkv_pre, small["conv_qkv"][i] = _qkv_bwd(s["proj"], jnp.concatenate([dq, dk, dv], axis=1), fw["conv_qkv"], hqk, hqk)
        dproj = jnp.concatenate([dqkv_pre, dz, dub, dvb, dga, dgb], axis=1)
        dw_main, got = _matmul(s["h1"], dproj, "tn", BF16, "mm_dw_main", jobs=scatter_jobs(gw, _SCATTER_ON_DW_MAIN))
        keep(i, _SCATTER_ON_DW_MAIN, got)
        dw_ba = _matmul(s["h1"], dba, "tn", BF16, "mm_dw_ba")
        dh1, got = _matmul(dproj, fw["w_main"], "nt", F32, "mm_dh1", jobs=scatter_jobs(gw, _SCATTER_ON_DH1))
        keep(i, _SCATTER_ON_DH1, got)
        dh1 = _matmul(dba, fw["w_ba"], "nt", F32, "mm_dh1_ba", res=dh1)
        dx, small["norm_mix"][i] = _rms_bwd(s["x0"], dh1, norm_mix[i], dx, "rms_bwd")

        sources = (dw_main, dw_ba, dw_main[:, o_ba:])
        later = []
        for rows in (slice(0, d // 2), slice(d // 2, d)):
            shards = []
            for dev in range(N_DEV):
                cuts = [(seg, max(lo, dev * ns), min(hi, (dev + 1) * ns)) for seg, lo, hi in in_segments]
                shards.append(jnp.concatenate([sources[seg][rows, a - lo:b - lo] for (seg, a, b), (_, lo, _)
                                               in zip(cuts, in_segments) if a < b], axis=1))
            later.append((jnp.stack(shards), True))


    rep_names = tuple(n for n in _SMALL if n not in _CONVS)
    stacked = {n: jnp.concatenate([jnp.reshape(a, (-1,)) for a in small[n]]) for n in small}
    stacked["norm_final"] = g_norm_final.reshape(-1)

    def padded(parts, mult, axis=0):
        flat = jnp.concatenate(parts, axis=axis)
        pad = -flat.shape[axis] % mult
        return jnp.pad(flat, [(0, 0)] * axis + [(0, pad)])

    rep_flat = padded([stacked[n] for n in rep_names] + [loss_part[0, :1]], 16 * LANES)
    conv_flat = [padded([stacked[n]], 8 * LANES) for n in _CONVS]
    packed = jnp.concatenate([rep_flat] + conv_flat).reshape(-1, LANES)

    outs_g, outs_d, outs_m, outs_v = {}, {}, {}, {}

    adam_jobs = {"w_ffn_up": later[:1], "w_ffn_down": later[1:], "w_in": [(packed, False)]}
    for n in sorted(_BIG, key=lambda name: name == "w_in"):
        (outs_g[n], outs_d[n], outs_m[n], outs_v[n]), got = _adam(
            recv[n], wts[n], mom_m[n], mom_v[n], "adam_" + n, jobs=adam_jobs.get(n, []))
        if n == "w_ffn_up":
            recv["w_in"][0] = got[0]
        elif n == "w_ffn_down":
            recv["w_in"][1] = got[0]
        elif n == "w_in":
            small_all = got[0].reshape(N_DEV, -1)

    n_rep = rep_flat.shape[0]
    pk = lambda src: padded([src[n].reshape(-1) for n in rep_names] + [jnp.zeros((1,), F32)], 16 * LANES).reshape(1, -1, LANES)
    res, _ = _adam([small_all[:, :n_rep].reshape(N_DEV, -1, LANES)], pk(wts), pk(mom_m), pk(mom_v), "adam_small")
    res = [r.reshape(-1) for r in res]
    off = 0
    for n in rep_names:
        shp = wts[n].shape
        size = math.prod(shp)
        outs_g[n], outs_d[n], outs_m[n], outs_v[n] = [r[off:off + size].reshape(shp) for r in res]
        off += size
    loss = res[0][off]

    off = n_rep
    for n, cf in zip(_CONVS, conv_flat):
        _, kw, cl = wts[n].shape
        part = small_all[:, off:off + depth * kw * cl * N_DEV].reshape(N_DEV, depth * kw, N_DEV, cl)
        off += cf.shape[0]
        part = lax.dynamic_index_in_dim(part, me, axis=2, keepdims=False)
        two_d = lambda a: a.reshape(1, depth * kw, cl)
        res, _ = _adam([part], two_d(wts[n]), two_d(mom_m[n]), two_d(mom_v[n]), "adam_" + n)
        outs_g[n], outs_d[n], outs_m[n], outs_v[n] = [r.reshape(wts[n].shape) for r in res]

    return (loss, dx[None], *[outs_g[n] for n in _WEIGHTS], *[outs_d[n] for n in _WEIGHTS],
            *[outs_m[n] for n in _WEIGHTS], *[outs_v[n] for n in _WEIGHTS])
```

```python
import functools
import itertools
import math

import jax
import jax.numpy as jnp
from jax import lax
from jax.experimental import pallas as pl
from jax.experimental.pallas import tpu as pltpu

F32 = jnp.float32
BF16 = jnp.bfloat16
EPS = 1e-6
LANES = 128
HEAD = 128
CA = 64
N_DEV = 8
VMEM_LIMIT = 48 * 1024 * 1024
ADAM_VMEM_BUDGET = 16 * 1024 * 1024
MESH = pl.DeviceIdType.MESH

ADAM_LR = 0.001
ADAM_B1 = 0.9
ADAM_B2 = 0.999
ADAM_EPS = 1e-08
ADAM_WD = 0.01
ADAM_STEP = 10


def _tile(n, cap, mult=LANES):
    best = None
    for t in range(mult, min(n, cap) + 1, mult):
        if n % t == 0:
            best = t
    return n if best is None else best


def _tile_near(n, target, mult=LANES):
    cands = [t for t in range(mult, min(n, target * 3 // 2) + 1, mult) if n % t == 0]
    return min(cands, key=lambda t: abs(t - target)) if cands else n


def _params(sem):
    return pltpu.CompilerParams(dimension_semantics=sem, vmem_limit_bytes=VMEM_LIMIT)


def _sigmoid(v):
    return jax.nn.sigmoid(v)


def _silu_grad(c):
    s = _sigmoid(c)
    return s + c * s * (1.0 - s)


_GELU_C = math.sqrt(2.0 / math.pi)


def _gelu(v):
    return 0.5 * v * (1.0 + jnp.tanh(_GELU_C * (v + 0.044715 * v * v * v)))


def _gelu_grad(v):
    t = jnp.tanh(_GELU_C * (v + 0.044715 * v * v * v))
    return 0.5 * (1.0 + t) + 0.5 * v * (1.0 - t * t) * _GELU_C * (1.0 + 3.0 * 0.044715 * v * v)


_NN = (((1,), (0,)), ((), ()))
_NT = (((1,), (1,)), ((), ()))
_TN = (((0,), (0,)), ((), ()))
_BNN = (((2,), (1,)), ((0,), (0,)))
_BNT = (((2,), (2,)), ((0,), (0,)))
_BTN = (((1,), (1,)), ((0,), (0,)))


def _bdot(a, b, dn=_NN):
    return lax.dot_general(a.astype(BF16), b.astype(BF16), dn, preferred_element_type=F32)


def _split(a):
    hi = a.astype(BF16)
    return hi, (a - hi.astype(F32)).astype(BF16)


def _dot3(ah, al, bh, bl, dn=_NN):
    def d(u, v):
        return lax.dot_general(u, v, dn, preferred_element_type=F32)
    return d(ah, bh) + (d(al, bh) + d(ah, bl))


def _hdot(a, b, dn=_NN):
    return _dot3(*_split(a), *_split(b), dn)


_PEERS = {"gather": (1, 2, 3, 4, 5, 6, 7), "scatter": (1, 2, 3, 4, 5, 6, 7), "chips": (1, 2, 4, 6), "relay": (2, 4, 6)}


def _kinds(jobs):
    return [{False: "gather", True: "scatter"}.get(kind, kind) for _, kind in jobs]


def _xchg_out_shapes(jobs):
    shapes = []
    for (a, _), kind in zip(jobs, _kinds(jobs)):
        shape = {"gather": (N_DEV,) + a.shape, "chips": (N_DEV,) + a.shape, "chips_cols": (a.shape[0], N_DEV * a.shape[1]),
                 "scatter_cols": (N_DEV, a.shape[0], a.shape[1] // N_DEV)}.get(kind, a.shape)
        shapes.append(jax.ShapeDtypeStruct(shape, a.dtype))
    return shapes


def _xchg_scratch(jobs):
    n = len(jobs)
    return [pltpu.SemaphoreType.DMA((n, N_DEV - 1)), pltpu.SemaphoreType.DMA((n, N_DEV - 1)), pltpu.SemaphoreType.DMA((n,))]


def _xchg_copies(kinds, src, out, sems):
    send_sems, recv_sems, local_sems = sems
    x, y, c = lax.axis_index("x"), lax.axis_index("y"), lax.axis_index("c")
    me = 4 * x + 2 * y + c

    def block(ref, idx, as_cols):
        if not as_cols:
            return ref.at[idx]
        width = ref.shape[1] // N_DEV
        return ref.at[:, pl.ds(pl.multiple_of(idx * width, LANES), width)]

    local, sends, recvs = [], [], []
    for m in range(N_DEV):
        px = lax.rem(x + ((m >> 2) & 1), 2)
        py = lax.rem(y + ((m >> 1) & 1), 2)
        pc = lax.rem(c + (m & 1), 2)
        peer = 4 * px + 2 * py + pc
        for k, kind in enumerate(kinds):
            base, cols = kind.split("_")[0], kind.endswith("_cols")
            if m == 0:
                if base != "relay":
                    mine = block(src[k], me, cols) if base == "scatter" else src[k]
                    local.append(pltpu.make_async_copy(mine, block(out[k], me, cols and base != "scatter"), local_sems.at[k]))
                continue
            if m not in _PEERS[base]:
                continue
            if base == "relay":
                to, mine = (x, y, 1 - c), block(src[k], peer, cols)
                there, here = block(out[k], peer, cols), block(out[k], 4 * px + 2 * py + 1 - c, cols)
            else:
                to, mine = (px, py, pc), block(src[k], peer, cols) if base == "scatter" else src[k]
                there, here = block(out[k], me, cols and base != "scatter"), block(out[k], peer, cols and base != "scatter")
            for dst, lst in ((there, sends), (here, recvs)):
                lst.append(pltpu.make_async_remote_copy(
                    src_ref=mine, dst_ref=dst, send_sem=send_sems.at[k, m - 1], recv_sem=recv_sems.at[k, m - 1],
                    device_id=to, device_id_type=MESH))
    return local, sends, recvs


def _xchg_start(scatter, src, out, sems):
    local, sends, _ = _xchg_copies(scatter, src, out, sems)
    for cp in local + sends:
        cp.start()


def _xchg_wait(scatter, src, out, sems):
    local, sends, recvs = _xchg_copies(scatter, src, out, sems)
    for cp in recvs:
        cp.wait_recv()
    for cp in sends:
        cp.wait_send()
    for cp in local:
        cp.wait()


_ANY = pl.BlockSpec(memory_space=pl.ANY)


def _xchg_aliases(jobs, n_in, n_out):
    return {n_in + k: n_out + k for k, kind in enumerate(_kinds(jobs)) if kind.startswith("relay")}


def _exchange(jobs, name):
    n = len(jobs)
    kinds = _kinds(jobs)

    def body(*refs):
        src, out, sems = refs[:n], refs[n:2 * n], refs[2 * n:]
        _xchg_start(kinds, src, out, sems)
        _xchg_wait(kinds, src, out, sems)

    return pl.pallas_call(
        body, name=name, in_specs=[_ANY] * n, out_specs=[_ANY] * n, out_shape=_xchg_out_shapes(jobs),
        scratch_shapes=_xchg_scratch(jobs), input_output_aliases=_xchg_aliases(jobs, 0, 0),
        compiler_params=pltpu.CompilerParams(has_side_effects=True))(*[a for a, _ in jobs])


def _carried(body, n_in, n_out, jobs, grid):
    if not jobs:
        return body
    nj = len(jobs)
    scatter = _kinds(jobs)

    def wrapped(*refs):
        ins, src = refs[:n_in], refs[n_in:n_in + nj]
        outs, got = refs[n_in + nj:n_in + nj + n_out], refs[n_in + nj + n_out:n_in + 2 * nj + n_out]
        rest = refs[n_in + 2 * nj + n_out:]
        scratch, sems = rest[:len(rest) - 3], rest[len(rest) - 3:]
        ids = [pl.program_id(a) for a in range(len(grid))]
        first = functools.reduce(jnp.logical_and, [i == 0 for i in ids])
        last = functools.reduce(jnp.logical_and, [i == g - 1 for i, g in zip(ids, grid)])

        @pl.when(first)
        def _():
            _xchg_start(scatter, src, got, sems)

        body(*ins, *outs, *scratch)

        @pl.when(last)
        def _():
            _xchg_wait(scatter, src, got, sems)

    return wrapped


def _call(body, name, grid, in_specs, out_specs, out_shape, args, scratch=(), sem=None, jobs=()):
    jobs = list(jobs)
    nj = len(jobs)
    sem = ("arbitrary",) * len(grid) if jobs or sem is None else sem
    res = pl.pallas_call(
        _carried(body, len(in_specs), len(out_specs), jobs, grid), name=name, grid=grid,
        in_specs=list(in_specs) + [_ANY] * nj, out_specs=list(out_specs) + [_ANY] * nj,
        out_shape=list(out_shape) + _xchg_out_shapes(jobs),
        scratch_shapes=list(scratch) + (_xchg_scratch(jobs) if jobs else []),
        input_output_aliases=_xchg_aliases(jobs, len(in_specs), len(out_specs)),
        compiler_params=_params(sem))(*args, *[a for a, _ in jobs])
    return res[:len(out_specs)], res[len(out_specs):]


MATMUL_OPERAND_VMEM = 20 * 1024 * 1024
MATMUL_TILE = 1024


def _matmul(a, b, mode, out_dtype, name, res=None, jobs=()):
    if mode == "tn":
        kdim, m = a.shape
    else:
        m, kdim = a.shape
    n = b.shape[0] if mode == "nt" else b.shape[1]
    tm, tn = _tile_near(m, MATMUL_TILE), _tile(n, MATMUL_TILE)
    per_k = 2 * (tm * a.dtype.itemsize + tn * b.dtype.itemsize)
    tk = _tile(kdim, max(LANES, MATMUL_OPERAND_VMEM // per_k))
    nk = kdim // tk
    dims = {"nn": _NN, "nt": _NT, "tn": _TN}[mode]

    def body(*refs):
        a_ref, b_ref = refs[:2]
        r_ref = refs[2] if res is not None else None
        o_ref = refs[3] if res is not None else refs[2]
        acc = refs[-1] if nk > 1 else None

        def write(r):
            if r_ref is not None:
                r = r + r_ref[...].astype(F32)
            o_ref[...] = r.astype(out_dtype)

        prod = _bdot(a_ref[...], b_ref[...], dims)
        if nk == 1:
            write(prod)
        else:
            k = pl.program_id(2)

            @pl.when(k == 0)
            def _():
                acc[...] = prod

            @pl.when(jnp.logical_and(k > 0, k < nk - 1))
            def _():
                acc[...] += prod

            @pl.when(k == nk - 1)
            def _():
                write(acc[...] + prod)

    a_spec = pl.BlockSpec((tk, tm), lambda i, j, k: (k, i)) if mode == "tn" else pl.BlockSpec((tm, tk), lambda i, j, k: (i, k))
    b_spec = pl.BlockSpec((tn, tk), lambda i, j, k: (j, k)) if mode == "nt" else pl.BlockSpec((tk, tn), lambda i, j, k: (k, j))
    o_spec = pl.BlockSpec((tm, tn), lambda i, j, k: (i, j))
    in_specs = [a_spec, b_spec] + ([o_spec] if res is not None else [])
    args = (a, b) + ((res,) if res is not None else ())
    (out,), got = _call(body, name, (m // tm, n // tn, nk), in_specs, [o_spec], [jax.ShapeDtypeStruct((m, n), out_dtype)],
                        args, scratch=[pltpu.VMEM((tm, tn), F32)] if nk > 1 else [],
                        sem=("parallel", "parallel", "arbitrary"), jobs=jobs)
    return (out, got) if jobs else out


def _rms_fwd(x, gain, name):
    t, d = x.shape
    tb = _tile(t, 256, 8)

    def body(x_ref, g_ref, h_ref):
        xv = x_ref[...]
        r = lax.rsqrt(jnp.mean(xv * xv, axis=-1, keepdims=True) + EPS)
        h_ref[...] = (xv * r * g_ref[...]).astype(BF16)

    return pl.pallas_call(
        body, name=name, grid=(t // tb,),
        in_specs=[pl.BlockSpec((tb, d), lambda i: (i, 0)), pl.BlockSpec((1, d), lambda i: (0, 0))],
        out_specs=pl.BlockSpec((tb, d), lambda i: (i, 0)), out_shape=jax.ShapeDtypeStruct((t, d), BF16),
        compiler_params=_params(("parallel",)))(x, gain.reshape(1, d))


def _rms_bwd(x, dh, gain, dres, name):
    t, d = x.shape
    tb = _tile(t, 256, 8)

    def body(x_ref, dh_ref, g_ref, dr_ref, dx_ref, dg_ref):
        @pl.when(pl.program_id(0) == 0)
        def _():
            dg_ref[...] = jnp.zeros_like(dg_ref)

        xv = x_ref[...]
        dy = dh_ref[...].astype(F32)
        r = lax.rsqrt(jnp.mean(xv * xv, axis=-1, keepdims=True) + EPS)
        xh = xv * r
        dxh = dy * g_ref[...]
        dx_ref[...] = dr_ref[...] + r * (dxh - xh * jnp.mean(dxh * xh, axis=-1, keepdims=True))
        dg_ref[...] += jnp.sum(dy * xh, axis=0, keepdims=True)

    row = pl.BlockSpec((tb, d), lambda i: (i, 0))
    vec = pl.BlockSpec((1, d), lambda i: (0, 0))
    return pl.pallas_call(
        body, name=name, grid=(t // tb,), in_specs=[row, row, vec, row], out_specs=[row, vec],
        out_shape=[jax.ShapeDtypeStruct((t, d), F32), jax.ShapeDtypeStruct((1, d), F32)],
        compiler_params=_params(("arbitrary",)))(x, dh, gain.reshape(1, d), dres)


def _loss_head(x, target, gain):
    t, d = x.shape
    tb = _tile(t, 256, 8)

    def body(x_ref, t_ref, g_ref, dx_ref, dg_ref, loss_ref):
        @pl.when(pl.program_id(0) == 0)
        def _():
            dg_ref[...] = jnp.zeros_like(dg_ref)
            loss_ref[...] = jnp.zeros_like(loss_ref)

        xv = x_ref[...]
        r = lax.rsqrt(jnp.mean(xv * xv, axis=-1, keepdims=True) + EPS)
        xh = xv * r
        err = xh * g_ref[...] - t_ref[...]
        per_row = jnp.mean(err * err, axis=-1, keepdims=True)
        loss_ref[...] += 0.5 * jnp.sum(per_row, axis=0, keepdims=True)
        dy = err * (1.0 / d)
        dxh = dy * g_ref[...]
        dx_ref[...] = r * (dxh - xh * jnp.mean(dxh * xh, axis=-1, keepdims=True))
        dg_ref[...] += jnp.sum(dy * xh, axis=0, keepdims=True)

    row = pl.BlockSpec((tb, d), lambda i: (i, 0))
    vec = pl.BlockSpec((1, d), lambda i: (0, 0))
    return pl.pallas_call(
        body, name="loss_head", grid=(t // tb,), in_specs=[row, row, vec],
        out_specs=[row, vec, pl.BlockSpec((1, LANES), lambda i: (0, 0))],
        out_shape=[jax.ShapeDtypeStruct((t, d), F32), jax.ShapeDtypeStruct((1, d), F32),
                   jax.ShapeDtypeStruct((1, LANES), F32)],
        compiler_params=_params(("arbitrary",)))(x, target, gain.reshape(1, d))


def _shift_down(v, s, rows):
    if s == 0:
        return v
    return jnp.where(rows >= s, pltpu.roll(v, s, 0), 0.0)


def _shift_up(v, s, rows):
    if s == 0:
        return v
    t = v.shape[0]
    return jnp.where(rows < t - s, pltpu.roll(v, t - s, 0), 0.0)


def _conv(v, w, rows):
    k = w.shape[0]
    out = v * w[k - 1:k, :]
    for s in range(1, k):
        out = out + _shift_down(v, s, rows) * w[k - 1 - s:k - s, :]
    return out


def _qkv_fwd(proj, conv_w, nq, nk):
    t = proj.shape[0]
    cw = conv_w.shape[1]
    nblk = cw // HEAD

    def body(p_ref, w_ref, o_ref):
        j = pl.program_id(0)
        rows = lax.broadcasted_iota(jnp.int32, (t, HEAD), 0)
        c = _conv(p_ref[...].astype(F32), w_ref[...], rows)
        a = c * _sigmoid(c)
        nrm = a * lax.rsqrt(jnp.sum(a * a, axis=-1, keepdims=True) + EPS)
        nrm = nrm * jnp.where(j < nq, HEAD ** -0.5, 1.0)
        o_ref[...] = jnp.where(j < nq + nk, nrm, a).astype(BF16)

    return pl.pallas_call(
        body, name="qkv_fwd", grid=(nblk,),
        in_specs=[pl.BlockSpec((t, HEAD), lambda j: (0, j)), pl.BlockSpec((conv_w.shape[0], HEAD), lambda j: (0, j))],
        out_specs=pl.BlockSpec((t, HEAD), lambda j: (0, j)), out_shape=jax.ShapeDtypeStruct((t, cw), BF16),
        compiler_params=_params(("parallel",)))(proj, conv_w)


def _qkv_bwd(proj, dqkv, conv_w, nq, nk):
    t = proj.shape[0]
    kw, cw = conv_w.shape
    nblk = cw // HEAD

    def body(p_ref, d_ref, w_ref, dp_ref, dw_ref):
        j = pl.program_id(0)
        rows = lax.broadcasted_iota(jnp.int32, (t, HEAD), 0)
        xv = p_ref[...].astype(F32)
        w = w_ref[...]
        c = _conv(xv, w, rows)
        a = c * _sigmoid(c)
        dy = d_ref[...].astype(F32)
        r = lax.rsqrt(jnp.sum(a * a, axis=-1, keepdims=True) + EPS)
        y = a * r
        scale = jnp.where(j < nq, HEAD ** -0.5, 1.0)
        da_n = scale * r * (dy - y * jnp.sum(dy * y, axis=-1, keepdims=True))
        da = jnp.where(j < nq + nk, da_n, dy)
        dc = da * _silu_grad(c)
        dx = dc * w[kw - 1:kw, :]
        dw_ref[kw - 1:kw, :] = jnp.sum(dc * xv, axis=0, keepdims=True)
        for s in range(1, kw):
            dx = dx + _shift_up(dc, s, rows) * w[kw - 1 - s:kw - s, :]
            dw_ref[kw - 1 - s:kw - s, :] = jnp.sum(dc * _shift_down(xv, s, rows), axis=0, keepdims=True)
        dp_ref[...] = dx.astype(BF16)

    blk = pl.BlockSpec((t, HEAD), lambda j: (0, j))
    wblk = pl.BlockSpec((kw, HEAD), lambda j: (0, j))
    return pl.pallas_call(
        body, name="qkv_bwd", grid=(nblk,), in_specs=[blk, blk, wblk], out_specs=[blk, wblk],
        out_shape=[jax.ShapeDtypeStruct((t, cw), BF16), jax.ShapeDtypeStruct((kw, cw), F32)],
        compiler_params=_params(("parallel",)))(proj, dqkv, conv_w)


def _softplus(v):
    return jnp.where(v < -15.0, jnp.exp(v), jnp.maximum(v, 0.0) + jnp.log(1.0 + jnp.exp(-jnp.abs(v))))


def _gate_fwd(ba, alog_pad, dtb_pad, hv):
    t = ba.shape[0]
    tb = _tile(t, 512, CA)

    def body(ba_ref, al_ref, dt_ref, o_ref):
        v = ba_ref[...]
        beta = _sigmoid(v)
        g = -jnp.exp(al_ref[...]) * _softplus(v + dt_ref[...])
        pos = lax.broadcasted_iota(jnp.int32, (tb, LANES), 0) % CA
        s = 1
        while s < CA:
            g = g + jnp.where(pos >= s, pltpu.roll(g, s, 0), 0.0)
            s *= 2
        lane = lax.broadcasted_iota(jnp.int32, (tb, LANES), 1)
        o_ref[...] = jnp.where(lane < hv, beta, g)

    row = pl.BlockSpec((tb, LANES), lambda i: (i, 0))
    vec = pl.BlockSpec((1, LANES), lambda i: (0, 0))
    return pl.pallas_call(
        body, name="gate_fwd", grid=(t // tb,), in_specs=[row, vec, vec], out_specs=row,
        out_shape=jax.ShapeDtypeStruct((t, LANES), F32), compiler_params=_params(("parallel",)))(ba, alog_pad, dtb_pad)


def _gate_bwd(ba, dbg, alog_pad, dtb_pad, hv):
    t = ba.shape[0]
    tb = _tile(t, 512, CA)

    def body(ba_ref, d_ref, al_ref, dt_ref, dba_ref, dal_ref, ddt_ref):
        @pl.when(pl.program_id(0) == 0)
        def _():
            dal_ref[...] = jnp.zeros_like(dal_ref)
            ddt_ref[...] = jnp.zeros_like(ddt_ref)

        v = ba_ref[...]
        d = d_ref[...]
        pos = lax.broadcasted_iota(jnp.int32, (tb, LANES), 0) % CA
        dg = d
        s = 1
        while s < CA:
            dg = dg + jnp.where(pos < CA - s, pltpu.roll(dg, tb - s, 0), 0.0)
            s *= 2
        beta = _sigmoid(v)
        na = -jnp.exp(al_ref[...])
        z = v + dt_ref[...]
        da = dg * na * _sigmoid(z)
        lane = lax.broadcasted_iota(jnp.int32, (tb, LANES), 1)
        in_a = jnp.logical_and(lane >= hv, lane < 2 * hv)
        da = jnp.where(in_a, da, 0.0)
        dba_ref[...] = jnp.where(lane < hv, d * beta * (1.0 - beta), da)
        ddt_ref[...] += jnp.sum(da, axis=0, keepdims=True)
        dal_ref[...] += jnp.sum(jnp.where(in_a, dg * na * _softplus(z), 0.0), axis=0, keepdims=True)

    row = pl.BlockSpec((tb, LANES), lambda i: (i, 0))
    vec = pl.BlockSpec((1, LANES), lambda i: (0, 0))
    return pl.pallas_call(
        body, name="gate_bwd", grid=(t // tb,), in_specs=[row, row, vec, vec], out_specs=[row, vec, vec],
        out_shape=[jax.ShapeDtypeStruct((t, LANES), F32), jax.ShapeDtypeStruct((1, LANES), F32),
                   jax.ShapeDtypeStruct((1, LANES), F32)],
        compiler_params=_params(("arbitrary",)))(ba, dbg, alog_pad, dtb_pad)


def _chunk_masks():
    r = lax.broadcasted_iota(jnp.int32, (CA, CA), 0)
    c = lax.broadcasted_iota(jnp.int32, (CA, CA), 1)
    return r >= c, r > c, (r == c).astype(F32)


def _inv_unit_lower(a, eye):
    x = eye - a
    ph, plo = _split(a)
    n = 1
    while n < CA // 2:
        ph, plo = _split(_dot3(ph, plo, ph, plo, _BNN))
        x = x + _dot3(*_split(x), ph, plo, _BNN)
        n *= 2
    return x


def _delta_pre(q, k, v, bcol, gc, gr, gl, causal, strict):
    eg = jnp.exp(gc)
    dm = jnp.exp(jnp.where(causal, gc[:, :, :CA] - gr, -jnp.inf))
    kb = k * bcol
    kkb = _bdot(kb, k, _BNT)
    a = jnp.where(strict, kkb * dm, 0.0)
    rhs = jnp.concatenate([v * bcol, kb * eg], axis=2)
    qk = _bdot(q, k, _BNT)
    ekd = jnp.exp(gl - gc)
    return dict(eg=eg, dm=dm, kb=kb, kkb=kkb, a=a, rhs=rhs, p=qk * dm, qd=q * eg, ekd=ekd, kd=k * ekd, cd=jnp.exp(gl))


DELTA_Q_HEADS_PER_BLOCK = 2


def _delta_fwd(qkvn, beta_b, gam_b, gam_r, gam_l, hqk, hv, jobs=()):
    t = qkvn.shape[0]
    rep = hv // hqk
    qpb = math.gcd(hqk, DELTA_Q_HEADS_PER_BLOCK)
    nh = qpb * rep
    rb = _tile(t, 512, CA)
    nb = t // rb
    ncb = rb // CA
    nc = t // CA

    def body(q_ref, k_ref, v_ref, b_ref, gc_ref, gr_ref, gl_ref, o_ref, s_ref, tm_ref, state, sol_sc, p_sc):
        @pl.when(pl.program_id(1) == 0)
        def _():
            state[...] = jnp.zeros_like(state)

        causal, strict, eye = _chunk_masks()

        def chunks(a):
            return a.astype(F32).reshape(ncb, CA, a.shape[-1])

        def head_cols(hh):
            return slice(hh * HEAD, (hh + 1) * HEAD)

        for qi in range(qpb):
            hs = range(qi * rep, (qi + 1) * rep)
            stack = lambda per_head: jnp.concatenate([per_head(hh) for hh in hs], axis=0)
            pre = _delta_pre(stack(lambda hh: chunks(q_ref[:, head_cols(qi)])), stack(lambda hh: chunks(k_ref[:, head_cols(qi)])),
                             stack(lambda hh: chunks(v_ref[:, head_cols(hh)])), stack(lambda hh: chunks(b_ref[hh])),
                             stack(lambda hh: chunks(gc_ref[hh])), stack(lambda hh: gr_ref[hh]), stack(lambda hh: gl_ref[hh]),
                             causal, strict)
            tm = _inv_unit_lower(pre["a"], eye)
            sol = _hdot(tm, pre["rhs"], _BNN)
            for idx, hh in enumerate(hs):
                part = slice(idx * ncb, (idx + 1) * ncb)
                tm_ref[hh] = tm[part]
                sol_sc[hh] = sol[part]
                p_sc[hh] = pre["p"][part]

        def chunk(n, carry):
            rows = pl.ds(pl.multiple_of(n * CA, CA), CA)
            for hh in range(nh):
                qi = hh // rep
                qn = q_ref[rows, head_cols(qi)].astype(F32)
                kn = k_ref[rows, head_cols(qi)].astype(F32)
                gc = gc_ref[hh, rows, :]
                gl = gl_ref[hh, n]
                s = state[hh]
                v_new = sol_sc[hh, n, :, :HEAD] - _bdot(sol_sc[hh, n, :, HEAD:], s)
                o_ref[rows, head_cols(hh)] = _bdot(qn * jnp.exp(gc), s) + _bdot(p_sc[hh, n], v_new)
                s_ref[hh, n] = s.astype(BF16)
                state[hh] = s * jnp.exp(gl) + _bdot(kn * jnp.exp(gl - gc), v_new, _TN)
            return carry

        lax.fori_loop(0, ncb, chunk, 0)

    koff, voff = hqk // qpb, 2 * hqk // nh
    per_chunk = lambda width: pl.BlockSpec((nh, ncb, 1, width), lambda j, i: (j, i, 0, 0))
    return _call(
        body, "delta_fwd", (hqk // qpb, nb),
        [pl.BlockSpec((rb, qpb * HEAD), lambda j, i: (i, j)),
         pl.BlockSpec((rb, qpb * HEAD), lambda j, i: (i, koff + j)),
         pl.BlockSpec((rb, nh * HEAD), lambda j, i: (i, voff + j)),
         pl.BlockSpec((nh, rb, LANES), lambda j, i: (j, i, 0)),
         pl.BlockSpec((nh, rb, LANES), lambda j, i: (j, i, 0)),
         per_chunk(CA), per_chunk(LANES)],
        [pl.BlockSpec((rb, nh * HEAD), lambda j, i: (i, j)),
         pl.BlockSpec((nh, ncb, HEAD, HEAD), lambda j, i: (j, i, 0, 0)),
         pl.BlockSpec((nh, ncb, CA, CA), lambda j, i: (j, i, 0, 0))],
        [jax.ShapeDtypeStruct((t, hv * HEAD), F32), jax.ShapeDtypeStruct((hv, nc, HEAD, HEAD), BF16),
         jax.ShapeDtypeStruct((hv, nc, CA, CA), F32)],
        (qkvn, qkvn, qkvn, beta_b, gam_b, gam_r, gam_l),
        scratch=[pltpu.VMEM((nh, HEAD, HEAD), F32), pltpu.VMEM((nh, ncb, CA, 2 * HEAD), F32),
                 pltpu.VMEM((nh, ncb, CA, CA), F32)],
        sem=("parallel", "arbitrary"), jobs=jobs)


def _delta_bwd(qkvn, beta_b, gam_b, gam_r, gam_l, s_all, tm_all, do, hqk, hv, jobs=()):
    t = qkvn.shape[0]
    rep = hv // hqk
    qpb = math.gcd(hqk, DELTA_Q_HEADS_PER_BLOCK)
    nh = qpb * rep
    rb = _tile(t, 512, CA)
    nb = t // rb
    ncb = rb // CA

    def body(q_ref, k_ref, v_ref, b_ref, gc_ref, gr_ref, gl_ref, s_ref, tm_ref, do_ref,
             dq_ref, dk_ref, dv_ref, db_ref, dg_ref, dstate, sol_sc, vn_sc, p_sc, kkb_sc, dvn_sc, ds_sc):
        @pl.when(pl.program_id(1) == 0)
        def _():
            dstate[...] = jnp.zeros_like(dstate)

        causal, strict, _ = _chunk_masks()
        ones = jnp.ones((rep * ncb, CA, LANES), BF16)
        last = lax.broadcasted_iota(jnp.int32, (CA, LANES), 0) == CA - 1

        def chunks(a):
            return a.astype(F32).reshape(ncb, CA, a.shape[-1])

        def rows_of(a):
            return a.reshape(rb, a.shape[-1])

        def rowsum(m):
            return jnp.sum(m, axis=2, keepdims=True)

        def colsum(m):
            hi, lo = _split(m)
            return _bdot(hi, ones, _BTN) + _bdot(lo, ones, _BTN)

        def head_cols(hh):
            return slice(hh * HEAD, (hh + 1) * HEAD)

        def heads_of(qi):
            return range(qi * rep, (qi + 1) * rep)

        def stack(qi, per_head):
            return jnp.concatenate([per_head(hh) for hh in heads_of(qi)], axis=0)

        def parts(qi, a):
            return [(hh, a[idx * ncb:(idx + 1) * ncb]) for idx, hh in enumerate(heads_of(qi))]

        def head_inputs(qi):
            q = stack(qi, lambda hh: chunks(q_ref[:, head_cols(qi)]))
            k = stack(qi, lambda hh: chunks(k_ref[:, head_cols(qi)]))
            v = stack(qi, lambda hh: chunks(v_ref[:, head_cols(hh)]))
            bcol = stack(qi, lambda hh: chunks(b_ref[hh]))
            pre = _delta_pre(q, k, v, bcol, stack(qi, lambda hh: chunks(gc_ref[hh])), stack(qi, lambda hh: gr_ref[hh]),
                             stack(qi, lambda hh: gl_ref[hh]), causal, strict)
            return q, k, v, bcol, pre

        for qi in range(qpb):
            pre = head_inputs(qi)[-1]
            sol = _hdot(stack(qi, lambda hh: tm_ref[hh]), pre["rhs"], _BNN)
            vn = sol[:, :, :HEAD] - _bdot(sol[:, :, HEAD:], stack(qi, lambda hh: s_ref[hh]), _BNN)
            for sc, a in ((sol_sc, sol), (vn_sc, vn), (p_sc, pre["p"]), (kkb_sc, pre["kkb"])):
                for hh, part in parts(qi, a):
                    sc[hh] = part

        def state_step(it, carry):
            n = ncb - 1 - it
            rows = pl.ds(pl.multiple_of(n * CA, CA), CA)
            for h in range(nh):
                qn = q_ref[rows, head_cols(h // rep)].astype(F32)
                kn = k_ref[rows, head_cols(h // rep)].astype(F32)
                gc = gc_ref[h, rows, :]
                gl = gl_ref[h, n]
                ds = dstate[h]
                ds_sc[h, n] = ds
                dov = do_ref[rows, h * HEAD:(h + 1) * HEAD].astype(F32)
                dvn = _bdot(p_sc[h, n], dov, _TN) + _bdot(kn * jnp.exp(gl - gc), ds)
                dvn_sc[h, n] = dvn
                dstate[h] = (ds * jnp.exp(gl) + _bdot(qn * jnp.exp(gc), dov, _TN)
                             - _bdot(sol_sc[h, n, :, HEAD:], dvn, _TN))
            return carry

        lax.fori_loop(0, ncb, state_step, 0)

        for qi in range(qpb):
            q, k, v, bcol, pre = head_inputs(qi)
            kkr = _bdot(k, k, _BNT)
            eg, dm, kb, qd, kd, cd = pre["eg"], pre["dm"], pre["kb"], pre["qd"], pre["kd"], pre["cd"]
            p = stack(qi, lambda hh: p_sc[hh])
            sol = stack(qi, lambda hh: sol_sc[hh])
            s = stack(qi, lambda hh: s_ref[hh]).astype(F32)
            ds = stack(qi, lambda hh: ds_sc[hh])
            dov = stack(qi, lambda hh: chunks(do_ref[:, head_cols(hh)]))
            v_new = stack(qi, lambda hh: vn_sc[hh])
            dvn = stack(qi, lambda hh: dvn_sc[hh])

            dp = jnp.where(causal, _bdot(dov, v_new, _BNT), 0.0)
            dqd = _bdot(dov, s, _BNT)
            dkd = _bdot(v_new, ds, _BNT)
            dcd = jnp.sum(rowsum(s * ds), axis=1, keepdims=True)
            dw = -_bdot(dvn, s, _BNT)

            drhs = _hdot(stack(qi, lambda hh: tm_ref[hh]), jnp.concatenate([dvn, dw], axis=2), _BTN)
            dbv, dbke = drhs[:, :, :HEAD], drhs[:, :, HEAD:]
            da = -jnp.where(strict, _bdot(drhs, sol, _BNT), 0.0)
            m = da * dm
            e = m * stack(qi, lambda hh: kkb_sc[hh]) + dp * p
            dgam = rowsum(e) - colsum(e) + rowsum(dbke * kb * eg) + rowsum(dqd * qd)
            r = rowsum(dkd * kd)
            tot = jnp.sum(r, axis=1, keepdims=True) + dcd * cd
            dgam = dgam - r + jnp.where(last, tot, 0.0)
            dbeta = rowsum(m * kkr) + rowsum(dbv * v) + rowsum(dbke * eg * k)
            nm = m * bcol[:, :, :CA]
            dqk = dp * dm
            dq = _bdot(dqk, k, _BNN) + eg * dqd
            dk = _bdot(nm, k, _BNN) + _bdot(nm, k, _BTN) + _bdot(dqk, q, _BTN) + bcol * eg * dbke + pre["ekd"] * dkd
            for ref, a in ((dv_ref, bcol * dbv),):
                for hh, part in parts(qi, a):
                    ref[:, head_cols(hh)] = rows_of(part)
            for ref, a in ((db_ref, dbeta), (dg_ref, dgam)):
                for hh, part in parts(qi, a):
                    ref[hh] = rows_of(jnp.broadcast_to(part, (ncb, CA, LANES)))
            dq_ref[:, head_cols(qi)] = rows_of(sum(part for _, part in parts(qi, dq)))
            dk_ref[:, head_cols(qi)] = rows_of(sum(part for _, part in parts(qi, dk)))

    koff, voff = hqk // qpb, 2 * hqk // nh
    rv = lambda i: nb - 1 - i
    hd = pl.BlockSpec((nh, rb, LANES), lambda j, i: (j, rv(i), 0))
    qk_out = pl.BlockSpec((rb, qpb * HEAD), lambda j, i: (rv(i), j))
    v_blk = pl.BlockSpec((rb, nh * HEAD), lambda j, i: (rv(i), j))
    per_chunk = lambda *shape: pl.BlockSpec((nh, ncb) + shape, lambda j, i: (j, rv(i), 0, 0))
    return _call(
        body, "delta_bwd", (hqk // qpb, nb),
        [pl.BlockSpec((rb, qpb * HEAD), lambda j, i: (rv(i), j)),
         pl.BlockSpec((rb, qpb * HEAD), lambda j, i: (rv(i), koff + j)),
         pl.BlockSpec((rb, nh * HEAD), lambda j, i: (rv(i), voff + j)),
         hd, hd, per_chunk(1, CA), per_chunk(1, LANES), per_chunk(HEAD, HEAD), per_chunk(CA, CA), v_blk],
        [qk_out, qk_out, v_blk, hd, hd],
        [jax.ShapeDtypeStruct((t, hqk * HEAD), F32), jax.ShapeDtypeStruct((t, hqk * HEAD), F32),
         jax.ShapeDtypeStruct((t, hv * HEAD), F32),
         jax.ShapeDtypeStruct((hv, t, LANES), F32), jax.ShapeDtypeStruct((hv, t, LANES), F32)],
        (qkvn, qkvn, qkvn, beta_b, gam_b, gam_r, gam_l, s_all, tm_all, do),
        scratch=[pltpu.VMEM((nh, HEAD, HEAD), F32), pltpu.VMEM((nh, ncb, CA, 2 * HEAD), F32),
                 pltpu.VMEM((nh, ncb, CA, HEAD), F32), pltpu.VMEM((nh, ncb, CA, CA), F32),
                 pltpu.VMEM((nh, ncb, CA, CA), F32), pltpu.VMEM((nh, ncb, CA, HEAD), F32),
                 pltpu.VMEM((nh, ncb, HEAD, HEAD), F32)],
        sem=("parallel", "arbitrary"), jobs=jobs)


def _apost_fwd(o, proj, gain, zoff, hv):
    t = o.shape[0]
    tb = _tile(t, 1024, 8)
    zb = zoff // HEAD

    def body(o_ref, z_ref, g_ref, y_ref):
        ov = o_ref[...]
        z = z_ref[...].astype(F32)
        r = lax.rsqrt(jnp.mean(ov * ov, axis=-1, keepdims=True) + EPS)
        y_ref[...] = (ov * r * g_ref[...] * (z * _sigmoid(z))).astype(BF16)

    blk = pl.BlockSpec((tb, HEAD), lambda i, h: (i, h))
    return pl.pallas_call(
        body, name="apost_fwd", grid=(t // tb, hv),
        in_specs=[blk, pl.BlockSpec((tb, HEAD), lambda i, h: (i, zb + h)), pl.BlockSpec((1, HEAD), lambda i, h: (0, 0))],
        out_specs=blk, out_shape=jax.ShapeDtypeStruct((t, hv * HEAD), BF16),
        compiler_params=_params(("parallel", "parallel")))(o, proj, gain.reshape(1, HEAD))


def _apost_bwd(o, proj, gain, dy, zoff, hv):
    t = o.shape[0]
    tb = _tile(t, 1024, 8)
    zb = zoff // HEAD

    def body(o_ref, z_ref, g_ref, dy_ref, do_ref, dz_ref, dg_ref):
        @pl.when(jnp.logical_and(pl.program_id(0) == 0, pl.program_id(1) == 0))
        def _():
            dg_ref[...] = jnp.zeros_like(dg_ref)

        ov = o_ref[...]
        z = z_ref[...].astype(F32)
        d = dy_ref[...].astype(F32)
        r = lax.rsqrt(jnp.mean(ov * ov, axis=-1, keepdims=True) + EPS)
        oh = ov * r
        sz = z * _sigmoid(z)
        dn = d * sz
        dz_ref[...] = (d * oh * g_ref[...] * _silu_grad(z)).astype(BF16)
        doh = dn * g_ref[...]
        do_ref[...] = r * (doh - oh * jnp.mean(doh * oh, axis=-1, keepdims=True))
        dg_ref[...] += jnp.sum(dn * oh, axis=0, keepdims=True)

    blk = pl.BlockSpec((tb, HEAD), lambda i, h: (i, h))
    vec = pl.BlockSpec((1, HEAD), lambda i, h: (0, 0))
    return pl.pallas_call(
        body, name="apost_bwd", grid=(t // tb, hv),
        in_specs=[blk, pl.BlockSpec((tb, HEAD), lambda i, h: (i, zb + h)), vec, blk],
        out_specs=[blk, blk, vec],
        out_shape=[jax.ShapeDtypeStruct((t, hv * HEAD), F32), jax.ShapeDtypeStruct((t, hv * HEAD), BF16),
                   jax.ShapeDtypeStruct((1, HEAD), F32)],
        compiler_params=_params(("arbitrary", "arbitrary")))(o, proj, gain.reshape(1, HEAD), dy)


def _sgu_fwd(proj, gain, w_s, b_t, uoff, wb):
    t = proj.shape[0]
    ng = wb // HEAD

    def body(u_ref, v_ref, g_ref, w_ref, b_ref, o_ref):
        r_i = lax.broadcasted_iota(jnp.int32, (HEAD, HEAD), 0)
        c_i = lax.broadcasted_iota(jnp.int32, (HEAD, HEAD), 1)
        u = _gelu(u_ref[...].astype(F32))
        vg = _gelu(v_ref[...].astype(F32))
        vn = vg * lax.rsqrt(jnp.mean(vg * vg, axis=-1, keepdims=True) + EPS) * g_ref[...]
        for g in range(ng):
            cols = slice(g * HEAD, (g + 1) * HEAD)
            wg = jnp.where(r_i >= c_i, w_ref[g], 0.0)
            mixed = _bdot(wg, vn[:, cols]) + b_ref[:, g:g + 1]
            o_ref[:, cols] = (u[:, cols] * mixed).astype(BF16)

    ub, vb = uoff // wb, uoff // wb + 1
    return pl.pallas_call(
        body, name="sgu_fwd", grid=(t // HEAD,),
        in_specs=[pl.BlockSpec((HEAD, wb), lambda i: (i, ub)), pl.BlockSpec((HEAD, wb), lambda i: (i, vb)),
                  pl.BlockSpec((1, wb), lambda i: (0, 0)), pl.BlockSpec((ng, HEAD, HEAD), lambda i: (0, 0, 0)),
                  pl.BlockSpec((HEAD, ng), lambda i: (0, 0))],
        out_specs=pl.BlockSpec((HEAD, wb), lambda i: (i, 0)), out_shape=jax.ShapeDtypeStruct((t, wb), BF16),
        compiler_params=_params(("parallel",)))(proj, proj, gain.reshape(1, wb), w_s, b_t)


def _sgu_bwd(proj, gain, w_s, b_t, dout, uoff, wb):
    t = proj.shape[0]
    ng = wb // HEAD

    def body(u_ref, v_ref, g_ref, w_ref, b_ref, d_ref, du_ref, dv_ref, dw_ref, db_ref, dg_ref, dvn_ref):
        @pl.when(pl.program_id(0) == 0)
        def _():
            dw_ref[...] = jnp.zeros_like(dw_ref)
            db_ref[...] = jnp.zeros_like(db_ref)
            dg_ref[...] = jnp.zeros_like(dg_ref)

        r_i = lax.broadcasted_iota(jnp.int32, (HEAD, HEAD), 0)
        c_i = lax.broadcasted_iota(jnp.int32, (HEAD, HEAD), 1)
        tril = r_i >= c_i
        ub = u_ref[...].astype(F32)
        vb = v_ref[...].astype(F32)
        u = _gelu(ub)
        vg = _gelu(vb)
        r = lax.rsqrt(jnp.mean(vg * vg, axis=-1, keepdims=True) + EPS)
        vh = vg * r
        vn = vh * g_ref[...]
        d = d_ref[...].astype(F32)
        for g in range(ng):
            cols = slice(g * HEAD, (g + 1) * HEAD)
            wg = jnp.where(tril, w_ref[g], 0.0)
            mixed = _bdot(wg, vn[:, cols]) + b_ref[:, g:g + 1]
            du_ref[:, cols] = (d[:, cols] * mixed * _gelu_grad(ub[:, cols])).astype(BF16)
            dmix = d[:, cols] * u[:, cols]
            dw_ref[g] += jnp.where(tril, _bdot(dmix, vn[:, cols], _NT), 0.0)
            db_ref[g] += jnp.broadcast_to(jnp.sum(dmix, axis=1, keepdims=True), (HEAD, HEAD))
            dvn_ref[:, cols] = _bdot(wg, dmix, _TN)
        dvn = dvn_ref[...]
        dg_ref[...] += jnp.sum(dvn * vh, axis=0, keepdims=True)
        dvh = dvn * g_ref[...]
        dvg = r * (dvh - vh * jnp.mean(dvh * vh, axis=-1, keepdims=True))
        dv_ref[...] = (dvg * _gelu_grad(vb)).astype(BF16)

    ub_i, vb_i = uoff // wb, uoff // wb + 1
    row = pl.BlockSpec((HEAD, wb), lambda i: (i, 0))
    mat = pl.BlockSpec((ng, HEAD, HEAD), lambda i: (0, 0, 0))
    vec = pl.BlockSpec((1, wb), lambda i: (0, 0))
    return pl.pallas_call(
        body, name="sgu_bwd", grid=(t // HEAD,),
        in_specs=[pl.BlockSpec((HEAD, wb), lambda i: (i, ub_i)), pl.BlockSpec((HEAD, wb), lambda i: (i, vb_i)),
                  vec, mat, pl.BlockSpec((HEAD, ng), lambda i: (0, 0)), row],
        out_specs=[row, row, mat, mat, vec],
        out_shape=[jax.ShapeDtypeStruct((t, wb), BF16), jax.ShapeDtypeStruct((t, wb), BF16),
                   jax.ShapeDtypeStruct((ng, HEAD, HEAD), F32), jax.ShapeDtypeStruct((ng, HEAD, HEAD), F32),
                   jax.ShapeDtypeStruct((1, wb), F32)],
        scratch_shapes=[pltpu.VMEM((HEAD, wb), F32)],
        compiler_params=_params(("arbitrary",)))(proj, proj, gain.reshape(1, wb), w_s, b_t, dout)


def _merge_specs(t, d, goff):
    tb = _tile(t, 512, 8)
    tc = _tile(d, 512)
    gb = goff // tc
    nd = d // tc
    blk = pl.BlockSpec((tb, tc), lambda i, j: (i, j))
    ga = pl.BlockSpec((tb, tc), lambda i, j: (i, gb + j))
    gbs = pl.BlockSpec((tb, tc), lambda i, j: (i, gb + nd + j))
    return (t // tb, nd), blk, ga, gbs


def _merge_fwd(ya, yb, proj, goff):
    t, d = ya.shape
    grid, blk, ga, gbs = _merge_specs(t, d, goff)

    def body(ya_ref, yb_ref, ga_ref, gb_ref, o_ref):
        o_ref[...] = (_sigmoid(ga_ref[...].astype(F32)) * ya_ref[...].astype(F32)
                      + _sigmoid(gb_ref[...].astype(F32)) * yb_ref[...].astype(F32)).astype(BF16)

    return pl.pallas_call(
        body, name="merge_fwd", grid=grid, in_specs=[blk, blk, ga, gbs], out_specs=blk,
        out_shape=jax.ShapeDtypeStruct((t, d), BF16),
        compiler_params=_params(("parallel", "parallel")))(ya, yb, proj, proj)


def _merge_bwd(dm, ya, yb, proj, goff):
    t, d = ya.shape
    grid, blk, ga, gbs = _merge_specs(t, d, goff)

    def body(dm_ref, ya_ref, yb_ref, ga_ref, gb_ref, dya_ref, dyb_ref, dga_ref, dgb_ref):
        dmv = dm_ref[...].astype(F32)
        sa = _sigmoid(ga_ref[...].astype(F32))
        sb = _sigmoid(gb_ref[...].astype(F32))
        dya_ref[...] = (dmv * sa).astype(BF16)
        dyb_ref[...] = (dmv * sb).astype(BF16)
        dga_ref[...] = (dmv * ya_ref[...].astype(F32) * sa * (1.0 - sa)).astype(BF16)
        dgb_ref[...] = (dmv * yb_ref[...].astype(F32) * sb * (1.0 - sb)).astype(BF16)

    shp = jax.ShapeDtypeStruct((t, d), BF16)
    return pl.pallas_call(
        body, name="merge_bwd", grid=grid, in_specs=[blk, blk, blk, ga, gbs], out_specs=[blk] * 4,
        out_shape=[shp] * 4, compiler_params=_params(("parallel", "parallel")))(dm, ya, yb, proj, proj)


def _ffn_act_fwd(up, conv_w, bias, dff):
    t = up.shape[0]
    nblk = dff // HEAD
    kw = conv_w.shape[0]

    def body(g_ref, v_ref, wg_ref, wv_ref, bg_ref, bv_ref, o_ref, cg_ref, cv_ref):
        rows = lax.broadcasted_iota(jnp.int32, (t, HEAD), 0)
        cg = _conv(g_ref[...].astype(F32), wg_ref[...], rows) + bg_ref[...]
        cv = _conv(v_ref[...].astype(F32), wv_ref[...], rows) + bv_ref[...]
        o_ref[...] = (cg * _sigmoid(cg) * cv).astype(BF16)
        cg_ref[...] = cg.astype(BF16)
        cv_ref[...] = cv.astype(BF16)

    blk = pl.BlockSpec((t, HEAD), lambda j: (0, j))
    shp = jax.ShapeDtypeStruct((t, dff), BF16)
    return pl.pallas_call(
        body, name="ffn_act_fwd", grid=(nblk,),
        in_specs=[blk, pl.BlockSpec((t, HEAD), lambda j: (0, nblk + j)),
                  pl.BlockSpec((kw, HEAD), lambda j: (0, j)), pl.BlockSpec((kw, HEAD), lambda j: (0, nblk + j)),
                  pl.BlockSpec((1, HEAD), lambda j: (0, j)), pl.BlockSpec((1, HEAD), lambda j: (0, nblk + j))],
        out_specs=[blk, blk, blk], out_shape=[shp, shp, shp],
        compiler_params=_params(("parallel",)))(up, up, conv_w, conv_w, bias, bias)


def _ffn_act_bwd(up, cg, cv, dact, conv_w, dff, jobs=()):
    t = up.shape[0]
    nblk = dff // HEAD
    kw = conv_w.shape[0]

    def body(me_ref, cg_ref, cv_ref, d_ref, wm_ref, dup_ref, dw_ref, db_ref):
        is_gate = pl.program_id(0) < nblk
        rows = lax.broadcasted_iota(jnp.int32, (t, HEAD), 0)
        xv = me_ref[...].astype(F32)
        w = wm_ref[...]
        g = cg_ref[...].astype(F32)
        s = _sigmoid(g)
        d = d_ref[...].astype(F32)
        dc = d * jnp.where(is_gate, cv_ref[...].astype(F32) * (s + g * s * (1.0 - s)), g * s)
        db_ref[...] = jnp.sum(dc, axis=0, keepdims=True)
        dx = dc * w[kw - 1:kw, :]
        dw_ref[kw - 1:kw, :] = jnp.sum(dc * xv, axis=0, keepdims=True)
        for sh in range(1, kw):
            dx = dx + _shift_up(dc, sh, rows) * w[kw - 1 - sh:kw - sh, :]
            dw_ref[kw - 1 - sh:kw - sh, :] = jnp.sum(dc * _shift_down(xv, sh, rows), axis=0, keepdims=True)
        dup_ref[...] = dx.astype(BF16)

    me = pl.BlockSpec((t, HEAD), lambda j: (0, j))
    pair = pl.BlockSpec((t, HEAD), lambda j: (0, j % nblk))
    wme = pl.BlockSpec((kw, HEAD), lambda j: (0, j))
    return _call(
        body, "ffn_act_bwd", (2 * nblk,), [me, pair, pair, pair, wme],
        [me, wme, pl.BlockSpec((1, HEAD), lambda j: (0, j))],
        [jax.ShapeDtypeStruct((t, 2 * dff), BF16), jax.ShapeDtypeStruct((kw, 2 * dff), F32),
         jax.ShapeDtypeStruct((1, 2 * dff), F32)],
        (up, cg, cv, dact, conv_w), sem=("parallel",), jobs=jobs)


def _ple_fwd(x, gt, pp):
    t, d = x.shape
    tb, tc = _tile(t, 512, 8), _tile(d, 1024)

    def body(x_ref, g_ref, p_ref, o_ref):
        o_ref[...] = x_ref[...] + _sigmoid(g_ref[...].astype(F32)) * p_ref[...].astype(F32)

    blk = pl.BlockSpec((tb, tc), lambda i, j: (i, j))
    return pl.pallas_call(
        body, name="ple_fwd", grid=(t // tb, d // tc), in_specs=[blk, blk, blk], out_specs=blk,
        out_shape=jax.ShapeDtypeStruct((t, d), F32), compiler_params=_params(("parallel", "parallel")))(x, gt, pp)


def _ple_bwd(dx, gt, pp):
    t, d = dx.shape
    tb, tc = _tile(t, 512, 8), _tile(d, 1024)

    def body(dx_ref, g_ref, p_ref, dg_ref, dp_ref):
        dv = dx_ref[...]
        s = _sigmoid(g_ref[...].astype(F32))
        dg_ref[...] = (dv * p_ref[...].astype(F32) * s * (1.0 - s)).astype(BF16)
        dp_ref[...] = (dv * s).astype(BF16)

    blk = pl.BlockSpec((tb, tc), lambda i, j: (i, j))
    shp = jax.ShapeDtypeStruct((t, d), BF16)
    return pl.pallas_call(
        body, name="ple_bwd", grid=(t // tb, d // tc), in_specs=[blk, blk, blk], out_specs=[blk, blk],
        out_shape=[shp, shp], compiler_params=_params(("parallel", "parallel")))(dx, gt, pp)


def _adam(pieces, w, m, v, name, jobs=()):
    nq = len(pieces)
    npart, rp, c = pieces[0].shape
    per_layer = nq // w.shape[0]
    row_bytes = 2 * c * (nq * npart * pieces[0].dtype.itemsize + 7 * 4)
    tr = _tile(rp, max(16, min(512, ADAM_VMEM_BUDGET // row_bytes)), 16)
    nblk = rp // tr
    c1 = 1.0 - ADAM_B1 ** ADAM_STEP
    c2 = 1.0 - ADAM_B2 ** ADAM_STEP

    def body(*refs):
        p_refs = refs[:nq]
        w_ref, m_ref, v_ref, g_ref, d_ref, mo_ref, vo_ref = refs[nq:]
        for q in range(nq):
            @pl.when(pl.program_id(0) == q)
            def _(p_ref=p_refs[q]):
                g = p_ref[0].astype(F32)
                for i in range(1, npart):
                    g = g + p_ref[i].astype(F32)
                mn = ADAM_B1 * m_ref[...] + (1.0 - ADAM_B1) * g
                vn = ADAM_B2 * v_ref[...] + (1.0 - ADAM_B2) * (g * g)
                g_ref[...] = g
                mo_ref[...] = mn
                vo_ref[...] = vn
                d_ref[...] = -ADAM_LR * ((mn / c1) / (jnp.sqrt(vn / c2) + ADAM_EPS) + ADAM_WD * w_ref[...])

    def piece_spec(q):
        return pl.BlockSpec((npart, tr, c), lambda i, r: (0, jnp.where(i == q, r, jnp.where(i < q, 0, nblk - 1)), 0))

    blk = pl.BlockSpec((None, tr, c), lambda i, r: (i // per_layer, (i % per_layer) * nblk + r, 0))
    shp = jax.ShapeDtypeStruct(w.shape, F32)
    return _call(body, name, (nq, nblk), [piece_spec(q) for q in range(nq)] + [blk] * 3, [blk] * 4, [shp] * 4,
                 (*pieces, w, m, v), sem=("parallel", "parallel"), jobs=jobs)


_BIG = ("w_in", "w_branch_a", "w_branch_b", "w_out", "w_ffn_up", "w_ffn_down", "w_ple_gate", "w_ple_proj")
_COL_SHARDED = ("w_in", "w_branch_b", "w_ffn_up", "w_ple_proj")
_CONVS = ("conv_qkv", "conv_ffn")
_GATHER_ON_PROJ = ("w_ffn_up",)
_GATHER_ON_DELTA = ("w_ffn_down",)
_GATHER_AHEAD = ("w_in", "conv_qkv")
_GATHER_ON_UP = ("w_ple_gate", "w_ple_proj")
_GATHER_AHEAD_2 = ("w_branch_a", "w_branch_b", "w_out", "conv_ffn")
_SCATTER_ON_DACT = ("w_ple_gate", "w_ple_proj")
_SCATTER_ON_DELTA = ("w_ffn_up",)
_SCATTER_ON_DW_MAIN = ("w_out", "w_branch_a", "w_branch_b")
_SCATTER_ON_DH1 = ("w_ffn_down",)
_SMALL = ("norm_mix", "conv_qkv", "a_log", "dt_bias", "head_norm", "sgu_norm", "w_spatial", "b_spatial", "norm_ffn",
          "conv_ffn", "b_conv_ffn", "norm_ple", "norm_final")
_WEIGHTS = ("norm_mix", "w_in", "conv_qkv", "a_log", "dt_bias", "head_norm", "sgu_norm", "w_spatial", "b_spatial",
            "w_branch_a", "w_branch_b", "w_out", "norm_ffn", "w_ffn_up", "conv_ffn", "b_conv_ffn", "w_ffn_down",
            "norm_ple", "w_ple_gate", "w_ple_proj", "norm_final")


def _full_cols(g):
    return jnp.transpose(g, (1, 0, 2)).reshape(g.shape[1], N_DEV * g.shape[2])


def _full_rows(g):
    return g.reshape(N_DEV * g.shape[1], g.shape[2])


def _split_cols(dw):
    k, n = dw.shape
    return jnp.transpose(dw.reshape(k, N_DEV, n // N_DEV), (1, 0, 2))


def _split_rows(dw):
    k, n = dw.shape
    return dw.reshape(N_DEV, k // N_DEV, n)


def _pad_lanes(v, width=LANES, offset=0):
    return jnp.pad(v, ((0, 0), (offset, width - offset - v.shape[1])))


def kernel(x, p, norm_mix, w_in, conv_qkv, a_log, dt_bias, head_norm, sgu_norm, w_spatial, b_spatial, w_branch_a, w_branch_b, w_out, norm_ffn, w_ffn_up, conv_ffn, b_conv_ffn, w_ffn_down, norm_ple, w_ple_gate, w_ple_proj, norm_final, loss_target, m_norm_mix, m_w_in, m_conv_qkv, m_a_log, m_dt_bias, m_head_norm, m_sgu_norm, m_w_spatial, m_b_spatial, m_w_branch_a, m_w_branch_b, m_w_out, m_norm_ffn, m_w_ffn_up, m_conv_ffn, m_b_conv_ffn, m_w_ffn_down, m_norm_ple, m_w_ple_gate, m_w_ple_proj, m_norm_final, v_norm_mix, v_w_in, v_conv_qkv, v_a_log, v_dt_bias, v_head_norm, v_sgu_norm, v_w_spatial, v_b_spatial, v_w_branch_a, v_w_branch_b, v_w_out, v_norm_ffn, v_w_ffn_up, v_conv_ffn, v_b_conv_ffn, v_w_ffn_down, v_norm_ple, v_w_ple_gate, v_w_ple_proj, v_norm_final):
    env = dict(locals())
    wts = {n: env[n] for n in _WEIGHTS}
    mom_m = {n: env["m_" + n] for n in _WEIGHTS}
    mom_v = {n: env["v_" + n] for n in _WEIGHTS}

    xin = x[0]
    tgt = loss_target[0]
    t, d = xin.shape
    depth = w_in.shape[0]
    hv = a_log.shape[1]
    vw = hv * HEAD
    wb = sgu_norm.shape[1]
    ng = w_spatial.shape[1]
    n_in = w_in.shape[2] * N_DEV
    qk = (n_in - 2 * vw - 2 * hv - 2 * wb - 2 * d) // 2
    hqk = qk // HEAD
    dff = w_ffn_down.shape[1] * N_DEV
    cw = 2 * qk + vw
    o_z, o_ba = 2 * qk + vw, 2 * qk + 2 * vw
    o_ub = o_ba
    o_ga = o_ub + 2 * wb
    ns = w_in.shape[2]
    in_segments = ((0, 0, o_ba), (1, o_ba, o_ba + 2 * hv), (2, o_ba + 2 * hv, n_in))
    me = 4 * lax.axis_index("x") + 2 * lax.axis_index("y") + lax.axis_index("c")

    full = [dict() for _ in range(depth)]
    staged = {}

    def as_cols(n):
        return "_cols" if (n in _COL_SHARDED or n in _CONVS) and wts[n].shape[-1] % LANES == 0 else ""

    def chips(i, names):
        names = names if i < depth else ()
        return [(i, n) for n in names], [(wts[n][i].astype(BF16) if n in _BIG else wts[n][i], "chips" + as_cols(n))
                                         for n in names]

    def relay(keys):
        return list(keys), [(staged.pop(key), "relay" + as_cols(key[1])) for key in keys]

    def settle(chip_keys, relay_keys, results):
        for key, g in zip(chip_keys, results):
            staged[key] = g
        for (i, n), g in zip(relay_keys, results[len(chip_keys):]):
            keep_blocks = as_cols(n) or n == "w_in"
            full[i][n] = g if keep_blocks else _full_cols(g) if n in _COL_SHARDED or n in _CONVS else _full_rows(g)

    ck, cj = chips(0, _GATHER_AHEAD + _GATHER_AHEAD_2)
    settle(ck, [], _exchange(cj, "gather_first"))
    rk, rj = relay(ck)
    settle([], rk, _exchange(rj, "relay_first"))

    saved = []
    xc = xin
    for i in range(depth):
        fw = full[i]
        main, gates = [], []
        for dev, (seg, lo, hi) in itertools.product(range(N_DEV), in_segments):
            a, b = max(lo, dev * ns), min(hi, (dev + 1) * ns)
            if a < b:
                (gates if seg == 1 else main).append(fw["w_in"][dev][:, a - dev * ns:b - dev * ns])
        fw["w_main"] = jnp.concatenate(main, axis=1)
        fw["w_ba"] = _pad_lanes(jnp.concatenate(gates, axis=1))
        s = {"x0": xc}
        s["h1"] = _rms_fwd(xc, norm_mix[i], "rms_fwd")
        ck, cj = chips(i, _GATHER_ON_PROJ)
        s["proj"], got = _matmul(s["h1"], fw["w_main"], "nn", BF16, "mm_proj", jobs=cj)
        settle(ck, [], got)
        s["ba"] = _matmul(s["h1"], fw["w_ba"], "nn", F32, "mm_ba")
        s["qkvn"] = _qkv_fwd(s["proj"], fw["conv_qkv"], hqk, hqk)
        s["alog"] = _pad_lanes(a_log[i][None, :], offset=hv)
        s["dtb"] = _pad_lanes(dt_bias[i][None, :], offset=hv)
        bg = _gate_fwd(s["ba"], s["alog"], s["dtb"], hv)
        beta_t = bg[:, :hv].T
        gam_t = bg[:, hv:2 * hv].T
        s["beta_b"] = jnp.broadcast_to(beta_t[:, :, None], (hv, t, LANES))
        s["gam_b"] = jnp.broadcast_to(gam_t[:, :, None], (hv, t, LANES))
        s["gam_r"] = gam_t.reshape(hv, t // CA, 1, CA)
        s["gam_l"] = jnp.broadcast_to(s["gam_r"][:, :, :, CA - 1:], (hv, t // CA, 1, LANES))
        ck1, cj1 = chips(i, _GATHER_ON_DELTA)
        ck2, cj2 = chips(i + 1, _GATHER_AHEAD)
        rk, rj = relay([(i, n) for n in _GATHER_ON_PROJ])
        (s["o"], s["s_all"], s["tm_all"]), got = _delta_fwd(
            s["qkvn"], s["beta_b"], s["gam_b"], s["gam_r"], s["gam_l"], hqk, hv, jobs=cj1 + cj2 + rj)
        settle(ck1 + ck2, rk, got)
        s["outa"] = _apost_fwd(s["o"], s["proj"], head_norm[i], o_z, hv)
        s["b_t"] = b_spatial[i].T
        s["outb"] = _sgu_fwd(s["proj"], sgu_norm[i], w_spatial[i], s["b_t"], o_ub, wb)
        s["ya"] = _matmul(s["outa"], fw["w_branch_a"], "nn", BF16, "mm_ya")
        s["yb"] = _matmul(s["outb"], fw["w_branch_b"], "nn", BF16, "mm_yb")
        s["mg"] = _merge_fwd(s["ya"], s["yb"], s["proj"], o_ga)
        s["x1"] = _matmul(s["mg"], fw["w_out"], "nn", F32, "mm_out", res=xc)
        s["h2"] = _rms_fwd(s["x1"], norm_ffn[i], "rms_fwd")
        nxt = i + 1 < depth
        ck1, cj1 = chips(i, _GATHER_ON_UP)
        ck2, cj2 = chips(i + 1, _GATHER_AHEAD_2)
        rk, rj = relay([(i, n) for n in _GATHER_ON_DELTA] + ([(i + 1, n) for n in _GATHER_AHEAD] if nxt else []))
        s["up"], got = _matmul(s["h2"], fw["w_ffn_up"], "nn", BF16, "mm_up", jobs=cj1 + cj2 + rj)
        settle(ck1 + ck2, rk, got)
        s["act"], s["cg"], s["cv"] = _ffn_act_fwd(s["up"], fw["conv_ffn"], b_conv_ffn[i][None, :], dff)
        rk, rj = relay([(i, n) for n in _GATHER_ON_UP] + ([(i + 1, n) for n in _GATHER_AHEAD_2] if nxt else []))
        s["x2"], got = _matmul(s["act"], fw["w_ffn_down"], "nn", F32, "mm_down", res=s["x1"], jobs=rj)
        settle([], rk, got)
        s["h3"] = _rms_fwd(s["x2"], norm_ple[i], "rms_fwd")
        s["gt"] = _matmul(s["h3"], fw["w_ple_gate"], "nn", BF16, "mm_gt")
        s["pp"] = _matmul(p[i, 0], fw["w_ple_proj"], "nn", BF16, "mm_pp")
        xc = _ple_fwd(s["x2"], s["gt"], s["pp"])
        saved.append(s)

    dx, g_norm_final, loss_part = _loss_head(xc, tgt, norm_final)

    small = {n: [None] * depth for n in _SMALL if n != "norm_final"}
    recv = {n: [None] * depth for n in _BIG}
    recv["w_in"] = [None] * (2 * depth)

    def scatter_jobs(gw, names):
        return [(gw[n], "scatter_cols") if as_cols(n) else
                (_split_cols(gw[n]) if n in _COL_SHARDED else _split_rows(gw[n]), "scatter") for n in names]

    def keep(i, names, results):
        for n, r in zip(names, results):
            recv[n][i] = r

    def carry(jobs, *args):
        return _matmul(*args, jobs=jobs) if jobs else (_matmul(*args), [])

    later = []
    for i in reversed(range(depth)):
        fw, s = full[i], saved[i]
        dgt, dpp = _ple_bwd(dx, s["gt"], s["pp"])
        gw = {"w_ple_gate": _matmul(s["h3"], dgt, "tn", BF16, "mm_dw_gt"),
              "w_ple_proj": _matmul(p[i, 0], dpp, "tn", BF16, "mm_dw_pp")}
        dh3 = _matmul(dgt, fw["w_ple_gate"], "nt", F32, "mm_dh3")
        dx, small["norm_ple"][i] = _rms_bwd(s["x2"], dh3, norm_ple[i], dx, "rms_bwd")

        dact, got = _matmul(dx, fw["w_ffn_down"], "nt", BF16, "mm_dact", jobs=scatter_jobs(gw, _SCATTER_ON_DACT))
        keep(i, _SCATTER_ON_DACT, got)
        gw["w_ffn_down"] = _matmul(s["act"], dx, "tn", BF16, "mm_dw_down")
        (dup, small["conv_ffn"][i], small["b_conv_ffn"][i]), _ = _ffn_act_bwd(
            s["up"], s["cg"], s["cv"], dact, fw["conv_ffn"], dff)
        gw["w_ffn_up"], got = carry(later[:1], s["h2"], dup, "tn", BF16, "mm_dw_up")
        keep(2 * i + 2, ("w_in",), got)
        dh2, got = carry(later[1:], dup, fw["w_ffn_up"], "nt", F32, "mm_dh2")
        keep(2 * i + 3, ("w_in",), got)
        dx, small["norm_ffn"][i] = _rms_bwd(s["x1"], dh2, norm_ffn[i], dx, "rms_bwd")

        dmg = _matmul(dx, fw["w_out"], "nt", BF16, "mm_dmg")
        gw["w_out"] = _matmul(s["mg"], dx, "tn", BF16, "mm_dw_out")
        dya, dyb, dga, dgb = _merge_bwd(dmg, s["ya"], s["yb"], s["proj"], o_ga)
        gw["w_branch_a"] = _matmul(s["outa"], dya, "tn", BF16, "mm_dw_a")
        gw["w_branch_b"] = _matmul(s["outb"], dyb, "tn", BF16, "mm_dw_b")
        douta = _matmul(dya, fw["w_branch_a"], "nt", BF16, "mm_douta")
        doutb = _matmul(dyb, fw["w_branch_b"], "nt", BF16, "mm_doutb")
        dub, dvb, small["w_spatial"][i], db_s, dsg = _sgu_bwd(s["proj"], sgu_norm[i], w_spatial[i], s["b_t"], doutb, o_ub, wb)
        small["b_spatial"][i] = db_s[:, :, 0]
        small["sgu_norm"][i] = dsg
        do, dz, small["head_norm"][i] = _apost_bwd(s["o"], s["proj"], head_norm[i], douta, o_z, hv)
        (dq, dk, dv, db_b, dg_b), got = _delta_bwd(
            s["qkvn"], s["beta_b"], s["gam_b"], s["gam_r"], s["gam_l"], s["s_all"], s["tm_all"], do, hqk, hv,
            jobs=scatter_jobs(gw, _SCATTER_ON_DELTA))
        keep(i, _SCATTER_ON_DELTA, got)
        dbg = _pad_lanes(jnp.concatenate([db_b[:, :, 0].T, dg_b[:, :, 0].T], axis=1))
        dba, dal, ddt = _gate_bwd(s["ba"], dbg, s["alog"], s["dtb"], hv)
        small["a_log"][i] = dal[:, hv:2 * hv]
        small["dt_bias"][i] = ddt[:, hv:2 * hv]
        dqkv_pre, small["conv_qkv"][i] = _qkv_bwd(s["proj"], jnp.concatenate([dq, dk, dv], axis=1), fw["conv_qkv"], hqk, hqk)
        dproj = jnp.concatenate([dqkv_pre, dz, dub, dvb, dga, dgb], axis=1)
        dw_main, got = _matmul(s["h1"], dproj, "tn", BF16, "mm_dw_main", jobs=scatter_jobs(gw, _SCATTER_ON_DW_MAIN))
        keep(i, _SCATTER_ON_DW_MAIN, got)
        dw_ba = _matmul(s["h1"], dba, "tn", BF16, "mm_dw_ba")
        dh1, got = _matmul(dproj, fw["w_main"], "nt", F32, "mm_dh1", jobs=scatter_jobs(gw, _SCATTER_ON_DH1))
        keep(i, _SCATTER_ON_DH1, got)
        dh1 = _matmul(dba, fw["w_ba"], "nt", F32, "mm_dh1_ba", res=dh1)
        dx, small["norm_mix"][i] = _rms_bwd(s["x0"], dh1, norm_mix[i], dx, "rms_bwd")

        sources = (dw_main, dw_ba, dw_main[:, o_ba:])
        later = []
        for rows in (slice(0, d // 2), slice(d // 2, d)):
            shards = []
            for dev in range(N_DEV):
                cuts = [(seg, max(lo, dev * ns), min(hi, (dev + 1) * ns)) for seg, lo, hi in in_segments]
                shards.append(jnp.concatenate([sources[seg][rows, a - lo:b - lo] for (seg, a, b), (_, lo, _)
                                               in zip(cuts, in_segments) if a < b], axis=1))
            later.append((jnp.stack(shards), True))


    rep_names = tuple(n for n in _SMALL if n not in _CONVS)
    stacked = {n: jnp.concatenate([jnp.reshape(a, (-1,)) for a in small[n]]) for n in small}
    stacked["norm_final"] = g_norm_final.reshape(-1)

    def padded(parts, mult, axis=0):
        flat = jnp.concatenate(parts, axis=axis)
        pad = -flat.shape[axis] % mult
        return jnp.pad(flat, [(0, 0)] * axis + [(0, pad)])

    rep_flat = padded([stacked[n] for n in rep_names] + [loss_part[0, :1]], 16 * LANES)
    conv_flat = [padded([stacked[n]], 8 * LANES) for n in _CONVS]
    packed = jnp.concatenate([rep_flat] + conv_flat).reshape(-1, LANES)

    outs_g, outs_d, outs_m, outs_v = {}, {}, {}, {}

    adam_jobs = {"w_ffn_up": later[:1], "w_ffn_down": later[1:], "w_in": [(packed, False)]}
    for n in sorted(_BIG, key=lambda name: name == "w_in"):
        (outs_g[n], outs_d[n], outs_m[n], outs_v[n]), got = _adam(
            recv[n], wts[n], mom_m[n], mom_v[n], "adam_" + n, jobs=adam_jobs.get(n, []))
        if n == "w_ffn_up":
            recv["w_in"][0] = got[0]
        elif n == "w_ffn_down":
            recv["w_in"][1] = got[0]
        elif n == "w_in":
            small_all = got[0].reshape(N_DEV, -1)

    n_rep = rep_flat.shape[0]
    pk = lambda src: padded([src[n].reshape(-1) for n in rep_names] + [jnp.zeros((1,), F32)], 16 * LANES).reshape(1, -1, LANES)
    res, _ = _adam([small_all[:, :n_rep].reshape(N_DEV, -1, LANES)], pk(wts), pk(mom_m), pk(mom_v), "adam_small")
    res = [r.reshape(-1) for r in res]
    off = 0
    for n in rep_names:
        shp = wts[n].shape
        size = math.prod(shp)
        outs_g[n], outs_d[n], outs_m[n], outs_v[n] = [r[off:off + size].reshape(shp) for r in res]
        off += size
    loss = res[0][off]

    off = n_rep
    for n, cf in zip(_CONVS, conv_flat):
        _, kw, cl = wts[n].shape
        part = small_all[:, off:off + depth * kw * cl * N_DEV].reshape(N_DEV, depth * kw, N_DEV, cl)
        off += cf.shape[0]
        part = lax.dynamic_index_in_dim(part, me, axis=2, keepdims=False)
        two_d = lambda a: a.reshape(1, depth * kw, cl)
        res, _ = _adam([part], two_d(wts[n]), two_d(mom_m[n]), two_d(mom_v[n]), "adam_" + n)
        outs_g[n], outs_d[n], outs_m[n], outs_v[n] = [r.reshape(wts[n].shape) for r in res]

    return (loss, dx[None], *[outs_g[n] for n in _WEIGHTS], *[outs_d[n] for n in _WEIGHTS],
            *[outs_m[n] for n in _WEIGHTS], *[outs_v[n] for n in _WEIGHTS])
```

```python
import functools
import itertools
import math

import jax
import jax.numpy as jnp
from jax import lax
from jax.experimental import pallas as pl
from jax.experimental.pallas import tpu as pltpu

F32 = jnp.float32
BF16 = jnp.bfloat16
EPS = 1e-6
LANES = 128
HEAD = 128
CA = 64
N_DEV = 8
VMEM_LIMIT = 48 * 1024 * 1024
ADAM_VMEM_BUDGET = 16 * 1024 * 1024
MESH = pl.DeviceIdType.MESH

ADAM_LR = 0.001
ADAM_B1 = 0.9
ADAM_B2 = 0.999
ADAM_EPS = 1e-08
ADAM_WD = 0.01
ADAM_STEP = 10


def _tile(n, cap, mult=LANES):
    best = None
    for t in range(mult, min(n, cap) + 1, mult):
        if n % t == 0:
            best = t
    return n if best is None else best


def _tile_near(n, target, mult=LANES):
    cands = [t for t in range(mult, min(n, target * 3 // 2) + 1, mult) if n % t == 0]
    return min(cands, key=lambda t: abs(t - target)) if cands else n


def _params(sem):
    return pltpu.CompilerParams(dimension_semantics=sem, vmem_limit_bytes=VMEM_LIMIT)


def _sigmoid(v):
    return jax.nn.sigmoid(v)


def _silu_grad(c):
    s = _sigmoid(c)
    return s + c * s * (1.0 - s)


_GELU_C = math.sqrt(2.0 / math.pi)


def _gelu(v):
    return 0.5 * v * (1.0 + jnp.tanh(_GELU_C * (v + 0.044715 * v * v * v)))


def _gelu_grad(v):
    t = jnp.tanh(_GELU_C * (v + 0.044715 * v * v * v))
    return 0.5 * (1.0 + t) + 0.5 * v * (1.0 - t * t) * _GELU_C * (1.0 + 3.0 * 0.044715 * v * v)


_NN = (((1,), (0,)), ((), ()))
_NT = (((1,), (1,)), ((), ()))
_TN = (((0,), (0,)), ((), ()))
_BNN = (((2,), (1,)), ((0,), (0,)))
_BNT = (((2,), (2,)), ((0,), (0,)))
_BTN = (((1,), (1,)), ((0,), (0,)))


def _bdot(a, b, dn=_NN):
    return lax.dot_general(a.astype(BF16), b.astype(BF16), dn, preferred_element_type=F32)


def _split(a):
    hi = a.astype(BF16)
    return hi, (a - hi.astype(F32)).astype(BF16)


def _dot3(ah, al, bh, bl, dn=_NN):
    def d(u, v):
        return lax.dot_general(u, v, dn, preferred_element_type=F32)
    return d(ah, bh) + (d(al, bh) + d(ah, bl))


def _hdot(a, b, dn=_NN):
    return _dot3(*_split(a), *_split(b), dn)


_PEERS = {"gather": (1, 2, 3, 4, 5, 6, 7), "scatter": (1, 2, 3, 4, 5, 6, 7), "chips": (1, 2, 4, 6), "relay": (2, 4, 6)}


def _kinds(jobs):
    return [{False: "gather", True: "scatter"}.get(kind, kind) for _, kind in jobs]


def _xchg_out_shapes(jobs):
    shapes = []
    for (a, _), kind in zip(jobs, _kinds(jobs)):
        shape = {"gather": (N_DEV,) + a.shape, "chips": (N_DEV,) + a.shape, "chips_cols": (a.shape[0], N_DEV * a.shape[1]),
                 "scatter_cols": (N_DEV, a.shape[0], a.shape[1] // N_DEV)}.get(kind, a.shape)
        shapes.append(jax.ShapeDtypeStruct(shape, a.dtype))
    return shapes


def _xchg_scratch(jobs):
    n = len(jobs)
    return [pltpu.SemaphoreType.DMA((n, N_DEV - 1)), pltpu.SemaphoreType.DMA((n, N_DEV - 1)), pltpu.SemaphoreType.DMA((n,))]


def _xchg_copies(kinds, src, out, sems):
    send_sems, recv_sems, local_sems = sems
    x, y, c = lax.axis_index("x"), lax.axis_index("y"), lax.axis_index("c")
    me = 4 * x + 2 * y + c

    def block(ref, idx, as_cols):
        if not as_cols:
            return ref.at[idx]
        width = ref.shape[1] // N_DEV
        return ref.at[:, pl.ds(pl.multiple_of(idx * width, LANES), width)]

    local, sends, recvs = [], [], []
    for m in range(N_DEV):
        px = lax.rem(x + ((m >> 2) & 1), 2)
        py = lax.rem(y + ((m >> 1) & 1), 2)
        pc = lax.rem(c + (m & 1), 2)
        peer = 4 * px + 2 * py + pc
        for k, kind in enumerate(kinds):
            base, cols = kind.split("_")[0], kind.endswith("_cols")
            if m == 0:
                if base != "relay":
                    mine = block(src[k], me, cols) if base == "scatter" else src[k]
                    local.append(pltpu.make_async_copy(mine, block(out[k], me, cols and base != "scatter"), local_sems.at[k]))
                continue
            if m not in _PEERS[base]:
                continue
            if base == "relay":
                to, mine = (x, y, 1 - c), block(src[k], peer, cols)
                there, here = block(out[k], peer, cols), block(out[k], 4 * px + 2 * py + 1 - c, cols)
            else:
                to, mine = (px, py, pc), block(src[k], peer, cols) if base == "scatter" else src[k]
                there, here = block(out[k], me, cols and base != "scatter"), block(out[k], peer, cols and base != "scatter")
            for dst, lst in ((there, sends), (here, recvs)):
                lst.append(pltpu.make_async_remote_copy(
                    src_ref=mine, dst_ref=dst, send_sem=send_sems.at[k, m - 1], recv_sem=recv_sems.at[k, m - 1],
                    device_id=to, device_id_type=MESH))
    return local, sends, recvs


def _xchg_start(scatter, src, out, sems):
    local, sends, _ = _xchg_copies(scatter, src, out, sems)
    for cp in local + sends:
        cp.start()


def _xchg_wait(scatter, src, out, sems):
    local, sends, recvs = _xchg_copies(scatter, src, out, sems)
    for cp in recvs:
        cp.wait_recv()
    for cp in sends:
        cp.wait_send()
    for cp in local:
        cp.wait()


_ANY = pl.BlockSpec(memory_space=pl.ANY)


def _xchg_aliases(jobs, n_in, n_out):
    return {n_in + k: n_out + k for k, kind in enumerate(_kinds(jobs)) if kind.startswith("relay")}


def _exchange(jobs, name):
    n = len(jobs)
    kinds = _kinds(jobs)

    def body(*refs):
        src, out, sems = refs[:n], refs[n:2 * n], refs[2 * n:]
        _xchg_start(kinds, src, out, sems)
        _xchg_wait(kinds, src, out, sems)

    return pl.pallas_call(
        body, name=name, in_specs=[_ANY] * n, out_specs=[_ANY] * n, out_shape=_xchg_out_shapes(jobs),
        scratch_shapes=_xchg_scratch(jobs), input_output_aliases=_xchg_aliases(jobs, 0, 0),
        compiler_params=pltpu.CompilerParams(has_side_effects=True))(*[a for a, _ in jobs])


def _carried(body, n_in, n_out, jobs, grid):
    if not jobs:
        return body
    nj = len(jobs)
    scatter = _kinds(jobs)

    def wrapped(*refs):
        ins, src = refs[:n_in], refs[n_in:n_in + nj]
        outs, got = refs[n_in + nj:n_in + nj + n_out], refs[n_in + nj + n_out:n_in + 2 * nj + n_out]
        rest = refs[n_in + 2 * nj + n_out:]
        scratch, sems = rest[:len(rest) - 3], rest[len(rest) - 3:]
        ids = [pl.program_id(a) for a in range(len(grid))]
        first = functools.reduce(jnp.logical_and, [i == 0 for i in ids])
        last = functools.reduce(jnp.logical_and, [i == g - 1 for i, g in zip(ids, grid)])

        @pl.when(first)
        def _():
            _xchg_start(scatter, src, got, sems)

        body(*ins, *outs, *scratch)

        @pl.when(last)
        def _():
            _xchg_wait(scatter, src, got, sems)

    return wrapped


def _call(body, name, grid, in_specs, out_specs, out_shape, args, scratch=(), sem=None, jobs=()):
    jobs = list(jobs)
    nj = len(jobs)
    sem = ("arbitrary",) * len(grid) if jobs or sem is None else sem
    res = pl.pallas_call(
        _carried(body, len(in_specs), len(out_specs), jobs, grid), name=name, grid=grid,
        in_specs=list(in_specs) + [_ANY] * nj, out_specs=list(out_specs) + [_ANY] * nj,
        out_shape=list(out_shape) + _xchg_out_shapes(jobs),
        scratch_shapes=list(scratch) + (_xchg_scratch(jobs) if jobs else []),
        input_output_aliases=_xchg_aliases(jobs, len(in_specs), len(out_specs)),
        compiler_params=_params(sem))(*args, *[a for a, _ in jobs])
    return res[:len(out_specs)], res[len(out_specs):]


MATMUL_OPERAND_VMEM = 20 * 1024 * 1024
MATMUL_TILE = 1024


def _matmul(a, b, mode, out_dtype, name, res=None, jobs=()):
    if mode == "tn":
        kdim, m = a.shape
    else:
        m, kdim = a.shape
    n = b.shape[0] if mode == "nt" else b.shape[1]
    tm, tn = _tile_near(m, MATMUL_TILE), _tile(n, MATMUL_TILE)
    per_k = 2 * (tm * a.dtype.itemsize + tn * b.dtype.itemsize)
    tk = _tile(kdim, max(LANES, MATMUL_OPERAND_VMEM // per_k))
    nk = kdim // tk
    dims = {"nn": _NN, "nt": _NT, "tn": _TN}[mode]

    def body(*refs):
        a_ref, b_ref = refs[:2]
        r_ref = refs[2] if res is not None else None
        o_ref = refs[3] if res is not None else refs[2]
        acc = refs[-1] if nk > 1 else None

        def write(r):
            if r_ref is not None:
                r = r + r_ref[...].astype(F32)
            o_ref[...] = r.astype(out_dtype)

        prod = _bdot(a_ref[...], b_ref[...], dims)
        if nk == 1:
            write(prod)
        else:
            k = pl.program_id(2)

            @pl.when(k == 0)
            def _():
                acc[...] = prod

            @pl.when(jnp.logical_and(k > 0, k < nk - 1))
            def _():
                acc[...] += prod

            @pl.when(k == nk - 1)
            def _():
                write(acc[...] + prod)

    a_spec = pl.BlockSpec((tk, tm), lambda i, j, k: (k, i)) if mode == "tn" else pl.BlockSpec((tm, tk), lambda i, j, k: (i, k))
    b_spec = pl.BlockSpec((tn, tk), lambda i, j, k: (j, k)) if mode == "nt" else pl.BlockSpec((tk, tn), lambda i, j, k: (k, j))
    o_spec = pl.BlockSpec((tm, tn), lambda i, j, k: (i, j))
    in_specs = [a_spec, b_spec] + ([o_spec] if res is not None else [])
    args = (a, b) + ((res,) if res is not None else ())
    (out,), got = _call(body, name, (m // tm, n // tn, nk), in_specs, [o_spec], [jax.ShapeDtypeStruct((m, n), out_dtype)],
                        args, scratch=[pltpu.VMEM((tm, tn), F32)] if nk > 1 else [],
                        sem=("parallel", "parallel", "arbitrary"), jobs=jobs)
    return (out, got) if jobs else out


def _rms_fwd(x, gain, name):
    t, d = x.shape
    tb = _tile(t, 256, 8)

    def body(x_ref, g_ref, h_ref):
        xv = x_ref[...]
        r = lax.rsqrt(jnp.mean(xv * xv, axis=-1, keepdims=True) + EPS)
        h_ref[...] = (xv * r * g_ref[...]).astype(BF16)

    return pl.pallas_call(
        body, name=name, grid=(t // tb,),
        in_specs=[pl.BlockSpec((tb, d), lambda i: (i, 0)), pl.BlockSpec((1, d), lambda i: (0, 0))],
        out_specs=pl.BlockSpec((tb, d), lambda i: (i, 0)), out_shape=jax.ShapeDtypeStruct((t, d), BF16),
        compiler_params=_params(("parallel",)))(x, gain.reshape(1, d))


def _rms_bwd(x, dh, gain, dres, name):
    t, d = x.shape
    tb = _tile(t, 256, 8)

    def body(x_ref, dh_ref, g_ref, dr_ref, dx_ref, dxb_ref, dg_ref):
        @pl.when(pl.program_id(0) == 0)
        def _():
            dg_ref[...] = jnp.zeros_like(dg_ref)

        xv = x_ref[...]
        dy = dh_ref[...].astype(F32)
        r = lax.rsqrt(jnp.mean(xv * xv, axis=-1, keepdims=True) + EPS)
        xh = xv * r
        dxh = dy * g_ref[...]
        dx = dr_ref[...] + r * (dxh - xh * jnp.mean(dxh * xh, axis=-1, keepdims=True))
        dx_ref[...] = dx
        dxb_ref[...] = dx.astype(BF16)
        dg_ref[...] += jnp.sum(dy * xh, axis=0, keepdims=True)

    row = pl.BlockSpec((tb, d), lambda i: (i, 0))
    vec = pl.BlockSpec((1, d), lambda i: (0, 0))
    return pl.pallas_call(
        body, name=name, grid=(t // tb,), in_specs=[row, row, vec, row], out_specs=[row, row, vec],
        out_shape=[jax.ShapeDtypeStruct((t, d), F32), jax.ShapeDtypeStruct((t, d), BF16), jax.ShapeDtypeStruct((1, d), F32)],
        compiler_params=_params(("arbitrary",)))(x, dh, gain.reshape(1, d), dres)


def _loss_head(x, target, gain):
    t, d = x.shape
    tb = _tile(t, 256, 8)

    def body(x_ref, t_ref, g_ref, dx_ref, dg_ref, loss_ref):
        @pl.when(pl.program_id(0) == 0)
        def _():
            dg_ref[...] = jnp.zeros_like(dg_ref)
            loss_ref[...] = jnp.zeros_like(loss_ref)

        xv = x_ref[...]
        r = lax.rsqrt(jnp.mean(xv * xv, axis=-1, keepdims=True) + EPS)
        xh = xv * r
        err = xh * g_ref[...] - t_ref[...]
        per_row = jnp.mean(err * err, axis=-1, keepdims=True)
        loss_ref[...] += 0.5 * jnp.sum(per_row, axis=0, keepdims=True)
        dy = err * (1.0 / d)
        dxh = dy * g_ref[...]
        dx_ref[...] = r * (dxh - xh * jnp.mean(dxh * xh, axis=-1, keepdims=True))
        dg_ref[...] += jnp.sum(dy * xh, axis=0, keepdims=True)

    row = pl.BlockSpec((tb, d), lambda i: (i, 0))
    vec = pl.BlockSpec((1, d), lambda i: (0, 0))
    return pl.pallas_call(
        body, name="loss_head", grid=(t // tb,), in_specs=[row, row, vec],
        out_specs=[row, vec, pl.BlockSpec((1, LANES), lambda i: (0, 0))],
        out_shape=[jax.ShapeDtypeStruct((t, d), F32), jax.ShapeDtypeStruct((1, d), F32),
                   jax.ShapeDtypeStruct((1, LANES), F32)],
        compiler_params=_params(("arbitrary",)))(x, target, gain.reshape(1, d))


def _shift_down(v, s, rows):
    if s == 0:
        return v
    return jnp.where(rows >= s, pltpu.roll(v, s, 0), 0.0)


def _shift_up(v, s, rows):
    if s == 0:
        return v
    t = v.shape[0]
    return jnp.where(rows < t - s, pltpu.roll(v, t - s, 0), 0.0)


def _conv(v, w, rows):
    k = w.shape[0]
    out = v * w[k - 1:k, :]
    for s in range(1, k):
        out = out + _shift_down(v, s, rows) * w[k - 1 - s:k - s, :]
    return out


def _qkv_fwd(proj, conv_w, nq, nk):
    t = proj.shape[0]
    cw = conv_w.shape[1]
    nblk = cw // HEAD

    def body(p_ref, w_ref, o_ref):
        j = pl.program_id(0)
        rows = lax.broadcasted_iota(jnp.int32, (t, HEAD), 0)
        c = _conv(p_ref[...].astype(F32), w_ref[...], rows)
        a = c * _sigmoid(c)
        nrm = a * lax.rsqrt(jnp.sum(a * a, axis=-1, keepdims=True) + EPS)
        nrm = nrm * jnp.where(j < nq, HEAD ** -0.5, 1.0)
        o_ref[...] = jnp.where(j < nq + nk, nrm, a).astype(BF16)

    return pl.pallas_call(
        body, name="qkv_fwd", grid=(nblk,),
        in_specs=[pl.BlockSpec((t, HEAD), lambda j: (0, j)), pl.BlockSpec((conv_w.shape[0], HEAD), lambda j: (0, j))],
        out_specs=pl.BlockSpec((t, HEAD), lambda j: (0, j)), out_shape=jax.ShapeDtypeStruct((t, cw), BF16),
        compiler_params=_params(("parallel",)))(proj, conv_w)


def _qkv_bwd(proj, dqkv, conv_w, nq, nk):
    t = proj.shape[0]
    kw, cw = conv_w.shape
    nblk = cw // HEAD

    def body(p_ref, d_ref, w_ref, dp_ref, dw_ref):
        j = pl.program_id(0)
        rows = lax.broadcasted_iota(jnp.int32, (t, HEAD), 0)
        xv = p_ref[...].astype(F32)
        w = w_ref[...]
        c = _conv(xv, w, rows)
        a = c * _sigmoid(c)
        dy = d_ref[...].astype(F32)
        r = lax.rsqrt(jnp.sum(a * a, axis=-1, keepdims=True) + EPS)
        y = a * r
        scale = jnp.where(j < nq, HEAD ** -0.5, 1.0)
        da_n = scale * r * (dy - y * jnp.sum(dy * y, axis=-1, keepdims=True))
        da = jnp.where(j < nq + nk, da_n, dy)
        dc = da * _silu_grad(c)
        dx = dc * w[kw - 1:kw, :]
        dw_ref[kw - 1:kw, :] = jnp.sum(dc * xv, axis=0, keepdims=True)
        for s in range(1, kw):
            dx = dx + _shift_up(dc, s, rows) * w[kw - 1 - s:kw - s, :]
            dw_ref[kw - 1 - s:kw - s, :] = jnp.sum(dc * _shift_down(xv, s, rows), axis=0, keepdims=True)
        dp_ref[...] = dx.astype(BF16)

    blk = pl.BlockSpec((t, HEAD), lambda j: (0, j))
    wblk = pl.BlockSpec((kw, HEAD), lambda j: (0, j))
    return pl.pallas_call(
        body, name="qkv_bwd", grid=(nblk,), in_specs=[blk, blk, wblk], out_specs=[blk, wblk],
        out_shape=[jax.ShapeDtypeStruct((t, cw), BF16), jax.ShapeDtypeStruct((kw, cw), F32)],
        compiler_params=_params(("parallel",)))(proj, dqkv, conv_w)


def _softplus(v):
    return jnp.where(v < -15.0, jnp.exp(v), jnp.maximum(v, 0.0) + jnp.log(1.0 + jnp.exp(-jnp.abs(v))))


def _gate_fwd(ba, alog_pad, dtb_pad, hv):
    t = ba.shape[0]
    tb = _tile(t, 512, CA)

    def body(ba_ref, al_ref, dt_ref, o_ref):
        v = ba_ref[...]
        beta = _sigmoid(v)
        g = -jnp.exp(al_ref[...]) * _softplus(v + dt_ref[...])
        pos = lax.broadcasted_iota(jnp.int32, (tb, LANES), 0) % CA
        s = 1
        while s < CA:
            g = g + jnp.where(pos >= s, pltpu.roll(g, s, 0), 0.0)
            s *= 2
        lane = lax.broadcasted_iota(jnp.int32, (tb, LANES), 1)
        o_ref[...] = jnp.where(lane < hv, beta, g)

    row = pl.BlockSpec((tb, LANES), lambda i: (i, 0))
    vec = pl.BlockSpec((1, LANES), lambda i: (0, 0))
    return pl.pallas_call(
        body, name="gate_fwd", grid=(t // tb,), in_specs=[row, vec, vec], out_specs=row,
        out_shape=jax.ShapeDtypeStruct((t, LANES), F32), compiler_params=_params(("parallel",)))(ba, alog_pad, dtb_pad)


def _gate_bwd(ba, dbg, alog_pad, dtb_pad, hv):
    t = ba.shape[0]
    tb = _tile(t, 512, CA)

    def body(ba_ref, d_ref, al_ref, dt_ref, dba_ref, dal_ref, ddt_ref):
        @pl.when(pl.program_id(0) == 0)
        def _():
            dal_ref[...] = jnp.zeros_like(dal_ref)
            ddt_ref[...] = jnp.zeros_like(ddt_ref)

        v = ba_ref[...]
        d = d_ref[...]
        pos = lax.broadcasted_iota(jnp.int32, (tb, LANES), 0) % CA
        dg = d
        s = 1
        while s < CA:
            dg = dg + jnp.where(pos < CA - s, pltpu.roll(dg, tb - s, 0), 0.0)
            s *= 2
        beta = _sigmoid(v)
        na = -jnp.exp(al_ref[...])
        z = v + dt_ref[...]
        da = dg * na * _sigmoid(z)
        lane = lax.broadcasted_iota(jnp.int32, (tb, LANES), 1)
        in_a = jnp.logical_and(lane >= hv, lane < 2 * hv)
        da = jnp.where(in_a, da, 0.0)
        dba_ref[...] = jnp.where(lane < hv, d * beta * (1.0 - beta), da)
        ddt_ref[...] += jnp.sum(da, axis=0, keepdims=True)
        dal_ref[...] += jnp.sum(jnp.where(in_a, dg * na * _softplus(z), 0.0), axis=0, keepdims=True)

    row = pl.BlockSpec((tb, LANES), lambda i: (i, 0))
    vec = pl.BlockSpec((1, LANES), lambda i: (0, 0))
    return pl.pallas_call(
        body, name="gate_bwd", grid=(t // tb,), in_specs=[row, row, vec, vec], out_specs=[row, vec, vec],
        out_shape=[jax.ShapeDtypeStruct((t, LANES), F32), jax.ShapeDtypeStruct((1, LANES), F32),
                   jax.ShapeDtypeStruct((1, LANES), F32)],
        compiler_params=_params(("arbitrary",)))(ba, dbg, alog_pad, dtb_pad)


def _chunk_masks():
    r = lax.broadcasted_iota(jnp.int32, (CA, CA), 0)
    c = lax.broadcasted_iota(jnp.int32, (CA, CA), 1)
    return r >= c, r > c, (r == c).astype(F32)


def _inv_unit_lower(a, eye):
    x = eye - a
    ph, plo = _split(a)
    n = 1
    while n < CA // 2:
        ph, plo = _split(_dot3(ph, plo, ph, plo, _BNN))
        x = x + _dot3(*_split(x), ph, plo, _BNN)
        n *= 2
    return x


def _delta_pre(q, k, v, bcol, gc, gr, gl, causal, strict):
    eg = jnp.exp(gc)
    dm = jnp.exp(jnp.where(causal, gc[:, :, :CA] - gr, -jnp.inf))
    kb = k * bcol
    kkb = _bdot(kb, k, _BNT)
    a = jnp.where(strict, kkb * dm, 0.0)
    rhs = jnp.concatenate([v * bcol, kb * eg], axis=2)
    qk = _bdot(q, k, _BNT)
    ekd = jnp.exp(gl - gc)
    return dict(eg=eg, dm=dm, kb=kb, kkb=kkb, a=a, rhs=rhs, p=qk * dm, qd=q * eg, ekd=ekd, kd=k * ekd, cd=jnp.exp(gl))


DELTA_Q_HEADS_PER_BLOCK = 2


def _delta_fwd(qkvn, beta_b, gam_b, gam_r, gam_l, hqk, hv, jobs=()):
    t = qkvn.shape[0]
    rep = hv // hqk
    qpb = math.gcd(hqk, DELTA_Q_HEADS_PER_BLOCK)
    nh = qpb * rep
    rb = _tile(t, 512, CA)
    nb = t // rb
    ncb = rb // CA
    nc = t // CA

    def body(q_ref, k_ref, v_ref, b_ref, gc_ref, gr_ref, gl_ref, o_ref, s_ref, tm_ref, state, sol_sc, p_sc):
        @pl.when(pl.program_id(1) == 0)
        def _():
            state[...] = jnp.zeros_like(state)

        causal, strict, eye = _chunk_masks()

        def chunks(a):
            return a.astype(F32).reshape(ncb, CA, a.shape[-1])

        def head_cols(hh):
            return slice(hh * HEAD, (hh + 1) * HEAD)

        for qi in range(qpb):
            hs = range(qi * rep, (qi + 1) * rep)
            stack = lambda per_head: jnp.concatenate([per_head(hh) for hh in hs], axis=0)
            pre = _delta_pre(stack(lambda hh: chunks(q_ref[:, head_cols(qi)])), stack(lambda hh: chunks(k_ref[:, head_cols(qi)])),
                             stack(lambda hh: chunks(v_ref[:, head_cols(hh)])), stack(lambda hh: chunks(b_ref[hh])),
                             stack(lambda hh: chunks(gc_ref[hh])), stack(lambda hh: gr_ref[hh]), stack(lambda hh: gl_ref[hh]),
                             causal, strict)
            tm = _inv_unit_lower(pre["a"], eye)
            sol = _hdot(tm, pre["rhs"], _BNN)
            for idx, hh in enumerate(hs):
                part = slice(idx * ncb, (idx + 1) * ncb)
                tm_ref[hh] = tm[part]
                sol_sc[hh] = sol[part]
                p_sc[hh] = pre["p"][part]

        def chunk(n, carry):
            rows = pl.ds(pl.multiple_of(n * CA, CA), CA)
            for hh in range(nh):
                qi = hh // rep
                qn = q_ref[rows, head_cols(qi)].astype(F32)
                kn = k_ref[rows, head_cols(qi)].astype(F32)
                gc = gc_ref[hh, rows, :]
                gl = gl_ref[hh, n]
                s = state[hh]
                v_new = sol_sc[hh, n, :, :HEAD] - _bdot(sol_sc[hh, n, :, HEAD:], s)
                o_ref[rows, head_cols(hh)] = _bdot(qn * jnp.exp(gc), s) + _bdot(p_sc[hh, n], v_new)
                s_ref[hh, n] = s.astype(BF16)
                state[hh] = s * jnp.exp(gl) + _bdot(kn * jnp.exp(gl - gc), v_new, _TN)
            return carry

        lax.fori_loop(0, ncb, chunk, 0)

    koff, voff = hqk // qpb, 2 * hqk // nh
    per_chunk = lambda width: pl.BlockSpec((nh, ncb, 1, width), lambda j, i: (j, i, 0, 0))
    return _call(
        body, "delta_fwd", (hqk // qpb, nb),
        [pl.BlockSpec((rb, qpb * HEAD), lambda j, i: (i, j)),
         pl.BlockSpec((rb, qpb * HEAD), lambda j, i: (i, koff + j)),
         pl.BlockSpec((rb, nh * HEAD), lambda j, i: (i, voff + j)),
         pl.BlockSpec((nh, rb, LANES), lambda j, i: (j, i, 0)),
         pl.BlockSpec((nh, rb, LANES), lambda j, i: (j, i, 0)),
         per_chunk(CA), per_chunk(LANES)],
        [pl.BlockSpec((rb, nh * HEAD), lambda j, i: (i, j)),
         pl.BlockSpec((nh, ncb, HEAD, HEAD), lambda j, i: (j, i, 0, 0)),
         pl.BlockSpec((nh, ncb, CA, CA), lambda j, i: (j, i, 0, 0))],
        [jax.ShapeDtypeStruct((t, hv * HEAD), F32), jax.ShapeDtypeStruct((hv, nc, HEAD, HEAD), BF16),
         jax.ShapeDtypeStruct((hv, nc, CA, CA), F32)],
        (qkvn, qkvn, qkvn, beta_b, gam_b, gam_r, gam_l),
        scratch=[pltpu.VMEM((nh, HEAD, HEAD), F32), pltpu.VMEM((nh, ncb, CA, 2 * HEAD), F32),
                 pltpu.VMEM((nh, ncb, CA, CA), F32)],
        sem=("parallel", "arbitrary"), jobs=jobs)


def _delta_bwd(qkvn, beta_b, gam_b, gam_r, gam_l, s_all, tm_all, do, hqk, hv, jobs=()):
    t = qkvn.shape[0]
    rep = hv // hqk
    qpb = math.gcd(hqk, DELTA_Q_HEADS_PER_BLOCK)
    nh = qpb * rep
    rb = _tile(t, 512, CA)
    nb = t // rb
    ncb = rb // CA

    def body(q_ref, k_ref, v_ref, b_ref, gc_ref, gr_ref, gl_ref, s_ref, tm_ref, do_ref,
             dq_ref, dk_ref, dv_ref, db_ref, dg_ref, dstate, sol_sc, vn_sc, p_sc, kkb_sc, dvn_sc, ds_sc):
        @pl.when(pl.program_id(1) == 0)
        def _():
            dstate[...] = jnp.zeros_like(dstate)

        causal, strict, _ = _chunk_masks()
        ones = jnp.ones((rep * ncb, CA, LANES), BF16)
        last = lax.broadcasted_iota(jnp.int32, (CA, LANES), 0) == CA - 1

        def chunks(a):
            return a.astype(F32).reshape(ncb, CA, a.shape[-1])

        def rows_of(a):
            return a.reshape(rb, a.shape[-1])

        def rowsum(m):
            return jnp.sum(m, axis=2, keepdims=True)

        def colsum(m):
            hi, lo = _split(m)
            return _bdot(hi, ones, _BTN) + _bdot(lo, ones, _BTN)

        def head_cols(hh):
            return slice(hh * HEAD, (hh + 1) * HEAD)

        def heads_of(qi):
            return range(qi * rep, (qi + 1) * rep)

        def stack(qi, per_head):
            return jnp.concatenate([per_head(hh) for hh in heads_of(qi)], axis=0)

        def parts(qi, a):
            return [(hh, a[idx * ncb:(idx + 1) * ncb]) for idx, hh in enumerate(heads_of(qi))]

        def head_inputs(qi):
            q = stack(qi, lambda hh: chunks(q_ref[:, head_cols(qi)]))
            k = stack(qi, lambda hh: chunks(k_ref[:, head_cols(qi)]))
            v = stack(qi, lambda hh: chunks(v_ref[:, head_cols(hh)]))
            bcol = stack(qi, lambda hh: chunks(b_ref[hh]))
            pre = _delta_pre(q, k, v, bcol, stack(qi, lambda hh: chunks(gc_ref[hh])), stack(qi, lambda hh: gr_ref[hh]),
                             stack(qi, lambda hh: gl_ref[hh]), causal, strict)
            return q, k, v, bcol, pre

        for qi in range(qpb):
            pre = head_inputs(qi)[-1]
            sol = _hdot(stack(qi, lambda hh: tm_ref[hh]), pre["rhs"], _BNN)
            vn = sol[:, :, :HEAD] - _bdot(sol[:, :, HEAD:], stack(qi, lambda hh: s_ref[hh]), _BNN)
            for sc, a in ((sol_sc, sol), (vn_sc, vn), (p_sc, pre["p"]), (kkb_sc, pre["kkb"])):
                for hh, part in parts(qi, a):
                    sc[hh] = part

        def state_step(it, carry):
            n = ncb - 1 - it
            rows = pl.ds(pl.multiple_of(n * CA, CA), CA)
            for h in range(nh):
                qn = q_ref[rows, head_cols(h // rep)].astype(F32)
                kn = k_ref[rows, head_cols(h // rep)].astype(F32)
                gc = gc_ref[h, rows, :]
                gl = gl_ref[h, n]
                ds = dstate[h]
                ds_sc[h, n] = ds
                dov = do_ref[rows, h * HEAD:(h + 1) * HEAD].astype(F32)
                dvn = _bdot(p_sc[h, n], dov, _TN) + _bdot(kn * jnp.exp(gl - gc), ds)
                dvn_sc[h, n] = dvn
                dstate[h] = (ds * jnp.exp(gl) + _bdot(qn * jnp.exp(gc), dov, _TN)
                             - _bdot(sol_sc[h, n, :, HEAD:], dvn, _TN))
            return carry

        lax.fori_loop(0, ncb, state_step, 0)

        for qi in range(qpb):
            q, k, v, bcol, pre = head_inputs(qi)
            kkr = _bdot(k, k, _BNT)
            eg, dm, kb, qd, kd, cd = pre["eg"], pre["dm"], pre["kb"], pre["qd"], pre["kd"], pre["cd"]
            p = stack(qi, lambda hh: p_sc[hh])
            sol = stack(qi, lambda hh: sol_sc[hh])
            s = stack(qi, lambda hh: s_ref[hh]).astype(F32)
            ds = stack(qi, lambda hh: ds_sc[hh])
            dov = stack(qi, lambda hh: chunks(do_ref[:, head_cols(hh)]))
            v_new = stack(qi, lambda hh: vn_sc[hh])
            dvn = stack(qi, lambda hh: dvn_sc[hh])

            dp = jnp.where(causal, _bdot(dov, v_new, _BNT), 0.0)
            dqd = _bdot(dov, s, _BNT)
            dkd = _bdot(v_new, ds, _BNT)
            dcd = jnp.sum(rowsum(s * ds), axis=1, keepdims=True)
            dw = -_bdot(dvn, s, _BNT)

            drhs = _hdot(stack(qi, lambda hh: tm_ref[hh]), jnp.concatenate([dvn, dw], axis=2), _BTN)
            dbv, dbke = drhs[:, :, :HEAD], drhs[:, :, HEAD:]
            da = -jnp.where(strict, _bdot(drhs, sol, _BNT), 0.0)
            m = da * dm
            e = m * stack(qi, lambda hh: kkb_sc[hh]) + dp * p
            dgam = rowsum(e) - colsum(e) + rowsum(dbke * kb * eg) + rowsum(dqd * qd)
            r = rowsum(dkd * kd)
            tot = jnp.sum(r, axis=1, keepdims=True) + dcd * cd
            dgam = dgam - r + jnp.where(last, tot, 0.0)
            dbeta = rowsum(m * kkr) + rowsum(dbv * v) + rowsum(dbke * eg * k)
            nm = m * bcol[:, :, :CA]
            dqk = dp * dm
            dq = _bdot(dqk, k, _BNN) + eg * dqd
            dk = _bdot(nm, k, _BNN) + _bdot(nm, k, _BTN) + _bdot(dqk, q, _BTN) + bcol * eg * dbke + pre["ekd"] * dkd
            for ref, a in ((dv_ref, bcol * dbv),):
                for hh, part in parts(qi, a):
                    ref[:, head_cols(hh)] = rows_of(part)
            for ref, a in ((db_ref, dbeta), (dg_ref, dgam)):
                for hh, part in parts(qi, a):
                    ref[hh] = rows_of(jnp.broadcast_to(part, (ncb, CA, LANES)))
            dq_ref[:, head_cols(qi)] = rows_of(sum(part for _, part in parts(qi, dq)))
            dk_ref[:, head_cols(qi)] = rows_of(sum(part for _, part in parts(qi, dk)))

    koff, voff = hqk // qpb, 2 * hqk // nh
    rv = lambda i: nb - 1 - i
    hd = pl.BlockSpec((nh, rb, LANES), lambda j, i: (j, rv(i), 0))
    qk_out = pl.BlockSpec((rb, qpb * HEAD), lambda j, i: (rv(i), j))
    v_blk = pl.BlockSpec((rb, nh * HEAD), lambda j, i: (rv(i), j))
    per_chunk = lambda *shape: pl.BlockSpec((nh, ncb) + shape, lambda j, i: (j, rv(i), 0, 0))
    return _call(
        body, "delta_bwd", (hqk // qpb, nb),
        [pl.BlockSpec((rb, qpb * HEAD), lambda j, i: (rv(i), j)),
         pl.BlockSpec((rb, qpb * HEAD), lambda j, i: (rv(i), koff + j)),
         pl.BlockSpec((rb, nh * HEAD), lambda j, i: (rv(i), voff + j)),
         hd, hd, per_chunk(1, CA), per_chunk(1, LANES), per_chunk(HEAD, HEAD), per_chunk(CA, CA), v_blk],
        [qk_out, qk_out, v_blk, hd, hd],
        [jax.ShapeDtypeStruct((t, hqk * HEAD), F32), jax.ShapeDtypeStruct((t, hqk * HEAD), F32),
         jax.ShapeDtypeStruct((t, hv * HEAD), F32),
         jax.ShapeDtypeStruct((hv, t, LANES), F32), jax.ShapeDtypeStruct((hv, t, LANES), F32)],
        (qkvn, qkvn, qkvn, beta_b, gam_b, gam_r, gam_l, s_all, tm_all, do),
        scratch=[pltpu.VMEM((nh, HEAD, HEAD), F32), pltpu.VMEM((nh, ncb, CA, 2 * HEAD), F32),
                 pltpu.VMEM((nh, ncb, CA, HEAD), F32), pltpu.VMEM((nh, ncb, CA, CA), F32),
                 pltpu.VMEM((nh, ncb, CA, CA), F32), pltpu.VMEM((nh, ncb, CA, HEAD), F32),
                 pltpu.VMEM((nh, ncb, HEAD, HEAD), F32)],
        sem=("parallel", "arbitrary"), jobs=jobs)


def _apost_fwd(o, proj, gain, zoff, hv):
    t = o.shape[0]
    tb = _tile(t, 1024, 8)
    zb = zoff // HEAD

    def body(o_ref, z_ref, g_ref, y_ref):
        ov = o_ref[...]
        z = z_ref[...].astype(F32)
        r = lax.rsqrt(jnp.mean(ov * ov, axis=-1, keepdims=True) + EPS)
        y_ref[...] = (ov * r * g_ref[...] * (z * _sigmoid(z))).astype(BF16)

    blk = pl.BlockSpec((tb, HEAD), lambda i, h: (i, h))
    return pl.pallas_call(
        body, name="apost_fwd", grid=(t // tb, hv),
        in_specs=[blk, pl.BlockSpec((tb, HEAD), lambda i, h: (i, zb + h)), pl.BlockSpec((1, HEAD), lambda i, h: (0, 0))],
        out_specs=blk, out_shape=jax.ShapeDtypeStruct((t, hv * HEAD), BF16),
        compiler_params=_params(("parallel", "parallel")))(o, proj, gain.reshape(1, HEAD))


def _apost_bwd(o, proj, gain, dy, zoff, hv):
    t = o.shape[0]
    tb = _tile(t, 1024, 8)
    zb = zoff // HEAD

    def body(o_ref, z_ref, g_ref, dy_ref, do_ref, dz_ref, dg_ref):
        @pl.when(jnp.logical_and(pl.program_id(0) == 0, pl.program_id(1) == 0))
        def _():
            dg_ref[...] = jnp.zeros_like(dg_ref)

        ov = o_ref[...]
        z = z_ref[...].astype(F32)
        d = dy_ref[...].astype(F32)
        r = lax.rsqrt(jnp.mean(ov * ov, axis=-1, keepdims=True) + EPS)
        oh = ov * r
        sz = z * _sigmoid(z)
        dn = d * sz
        dz_ref[...] = (d * oh * g_ref[...] * _silu_grad(z)).astype(BF16)
        doh = dn * g_ref[...]
        do_ref[...] = r * (doh - oh * jnp.mean(doh * oh, axis=-1, keepdims=True))
        dg_ref[...] += jnp.sum(dn * oh, axis=0, keepdims=True)

    blk = pl.BlockSpec((tb, HEAD), lambda i, h: (i, h))
    vec = pl.BlockSpec((1, HEAD), lambda i, h: (0, 0))
    return pl.pallas_call(
        body, name="apost_bwd", grid=(t // tb, hv),
        in_specs=[blk, pl.BlockSpec((tb, HEAD), lambda i, h: (i, zb + h)), vec, blk],
        out_specs=[blk, blk, vec],
        out_shape=[jax.ShapeDtypeStruct((t, hv * HEAD), F32), jax.ShapeDtypeStruct((t, hv * HEAD), BF16),
                   jax.ShapeDtypeStruct((1, HEAD), F32)],
        compiler_params=_params(("arbitrary", "arbitrary")))(o, proj, gain.reshape(1, HEAD), dy)


def _sgu_fwd(proj, gain, w_s, b_t, uoff, wb):
    t = proj.shape[0]
    ng = wb // HEAD

    def body(u_ref, v_ref, g_ref, w_ref, b_ref, o_ref):
        r_i = lax.broadcasted_iota(jnp.int32, (HEAD, HEAD), 0)
        c_i = lax.broadcasted_iota(jnp.int32, (HEAD, HEAD), 1)
        u = _gelu(u_ref[...].astype(F32))
        vg = _gelu(v_ref[...].astype(F32))
        vn = vg * lax.rsqrt(jnp.mean(vg * vg, axis=-1, keepdims=True) + EPS) * g_ref[...]
        for g in range(ng):
            cols = slice(g * HEAD, (g + 1) * HEAD)
            wg = jnp.where(r_i >= c_i, w_ref[g], 0.0)
            mixed = _bdot(wg, vn[:, cols]) + b_ref[:, g:g + 1]
            o_ref[:, cols] = (u[:, cols] * mixed).astype(BF16)

    ub, vb = uoff // wb, uoff // wb + 1
    return pl.pallas_call(
        body, name="sgu_fwd", grid=(t // HEAD,),
        in_specs=[pl.BlockSpec((HEAD, wb), lambda i: (i, ub)), pl.BlockSpec((HEAD, wb), lambda i: (i, vb)),
                  pl.BlockSpec((1, wb), lambda i: (0, 0)), pl.BlockSpec((ng, HEAD, HEAD), lambda i: (0, 0, 0)),
                  pl.BlockSpec((HEAD, ng), lambda i: (0, 0))],
        out_specs=pl.BlockSpec((HEAD, wb), lambda i: (i, 0)), out_shape=jax.ShapeDtypeStruct((t, wb), BF16),
        compiler_params=_params(("parallel",)))(proj, proj, gain.reshape(1, wb), w_s, b_t)


def _sgu_bwd(proj, gain, w_s, b_t, dout, uoff, wb):
    t = proj.shape[0]
    ng = wb // HEAD

    def body(u_ref, v_ref, g_ref, w_ref, b_ref, d_ref, du_ref, dv_ref, dw_ref, db_ref, dg_ref, dvn_ref):
        @pl.when(pl.program_id(0) == 0)
        def _():
            dw_ref[...] = jnp.zeros_like(dw_ref)
            db_ref[...] = jnp.zeros_like(db_ref)
            dg_ref[...] = jnp.zeros_like(dg_ref)

        r_i = lax.broadcasted_iota(jnp.int32, (HEAD, HEAD), 0)
        c_i = lax.broadcasted_iota(jnp.int32, (HEAD, HEAD), 1)
        tril = r_i >= c_i
        ub = u_ref[...].astype(F32)
        vb = v_ref[...].astype(F32)
        u = _gelu(ub)
        vg = _gelu(vb)
        r = lax.rsqrt(jnp.mean(vg * vg, axis=-1, keepdims=True) + EPS)
        vh = vg * r
        vn = vh * g_ref[...]
        d = d_ref[...].astype(F32)
        for g in range(ng):
            cols = slice(g * HEAD, (g + 1) * HEAD)
            wg = jnp.where(tril, w_ref[g], 0.0)
            mixed = _bdot(wg, vn[:, cols]) + b_ref[:, g:g + 1]
            du_ref[:, cols] = (d[:, cols] * mixed * _gelu_grad(ub[:, cols])).astype(BF16)
            dmix = d[:, cols] * u[:, cols]
            dw_ref[g] += jnp.where(tril, _bdot(dmix, vn[:, cols], _NT), 0.0)
            db_ref[g] += jnp.broadcast_to(jnp.sum(dmix, axis=1, keepdims=True), (HEAD, HEAD))
            dvn_ref[:, cols] = _bdot(wg, dmix, _TN)
        dvn = dvn_ref[...]
        dg_ref[...] += jnp.sum(dvn * vh, axis=0, keepdims=True)
        dvh = dvn * g_ref[...]
        dvg = r * (dvh - vh * jnp.mean(dvh * vh, axis=-1, keepdims=True))
        dv_ref[...] = (dvg * _gelu_grad(vb)).astype(BF16)

    ub_i, vb_i = uoff // wb, uoff // wb + 1
    row = pl.BlockSpec((HEAD, wb), lambda i: (i, 0))
    mat = pl.BlockSpec((ng, HEAD, HEAD), lambda i: (0, 0, 0))
    vec = pl.BlockSpec((1, wb), lambda i: (0, 0))
    return pl.pallas_call(
        body, name="sgu_bwd", grid=(t // HEAD,),
        in_specs=[pl.BlockSpec((HEAD, wb), lambda i: (i, ub_i)), pl.BlockSpec((HEAD, wb), lambda i: (i, vb_i)),
                  vec, mat, pl.BlockSpec((HEAD, ng), lambda i: (0, 0)), row],
        out_specs=[row, row, mat, mat, vec],
        out_shape=[jax.ShapeDtypeStruct((t, wb), BF16), jax.ShapeDtypeStruct((t, wb), BF16),
                   jax.ShapeDtypeStruct((ng, HEAD, HEAD), F32), jax.ShapeDtypeStruct((ng, HEAD, HEAD), F32),
                   jax.ShapeDtypeStruct((1, wb), F32)],
        scratch_shapes=[pltpu.VMEM((HEAD, wb), F32)],
        compiler_params=_params(("arbitrary",)))(proj, proj, gain.reshape(1, wb), w_s, b_t, dout)


def _merge_specs(t, d, goff):
    tb = _tile(t, 512, 8)
    tc = _tile(d, 512)
    gb = goff // tc
    nd = d // tc
    blk = pl.BlockSpec((tb, tc), lambda i, j: (i, j))
    ga = pl.BlockSpec((tb, tc), lambda i, j: (i, gb + j))
    gbs = pl.BlockSpec((tb, tc), lambda i, j: (i, gb + nd + j))
    return (t // tb, nd), blk, ga, gbs


def _merge_fwd(ya, yb, proj, goff):
    t, d = ya.shape
    grid, blk, ga, gbs = _merge_specs(t, d, goff)

    def body(ya_ref, yb_ref, ga_ref, gb_ref, o_ref):
        o_ref[...] = (_sigmoid(ga_ref[...].astype(F32)) * ya_ref[...].astype(F32)
                      + _sigmoid(gb_ref[...].astype(F32)) * yb_ref[...].astype(F32)).astype(BF16)

    return pl.pallas_call(
        body, name="merge_fwd", grid=grid, in_specs=[blk, blk, ga, gbs], out_specs=blk,
        out_shape=jax.ShapeDtypeStruct((t, d), BF16),
        compiler_params=_params(("parallel", "parallel")))(ya, yb, proj, proj)


def _merge_bwd(dm, ya, yb, proj, goff):
    t, d = ya.shape
    grid, blk, ga, gbs = _merge_specs(t, d, goff)

    def body(dm_ref, ya_ref, yb_ref, ga_ref, gb_ref, dya_ref, dyb_ref, dga_ref, dgb_ref):
        dmv = dm_ref[...].astype(F32)
        sa = _sigmoid(ga_ref[...].astype(F32))
        sb = _sigmoid(gb_ref[...].astype(F32))
        dya_ref[...] = (dmv * sa).astype(BF16)
        dyb_ref[...] = (dmv * sb).astype(BF16)
        dga_ref[...] = (dmv * ya_ref[...].astype(F32) * sa * (1.0 - sa)).astype(BF16)
        dgb_ref[...] = (dmv * yb_ref[...].astype(F32) * sb * (1.0 - sb)).astype(BF16)

    shp = jax.ShapeDtypeStruct((t, d), BF16)
    return pl.pallas_call(
        body, name="merge_bwd", grid=grid, in_specs=[blk, blk, blk, ga, gbs], out_specs=[blk] * 4,
        out_shape=[shp] * 4, compiler_params=_params(("parallel", "parallel")))(dm, ya, yb, proj, proj)


def _ffn_act_fwd(up, conv_w, bias, dff):
    t = up.shape[0]
    nblk = dff // HEAD
    kw = conv_w.shape[0]

    def body(g_ref, v_ref, wg_ref, wv_ref, bg_ref, bv_ref, o_ref, cg_ref, cv_ref):
        rows = lax.broadcasted_iota(jnp.int32, (t, HEAD), 0)
        cg = _conv(g_ref[...].astype(F32), wg_ref[...], rows) + bg_ref[...]
        cv = _conv(v_ref[...].astype(F32), wv_ref[...], rows) + bv_ref[...]
        o_ref[...] = (cg * _sigmoid(cg) * cv).astype(BF16)
        cg_ref[...] = cg.astype(BF16)
        cv_ref[...] = cv.astype(BF16)

    blk = pl.BlockSpec((t, HEAD), lambda j: (0, j))
    shp = jax.ShapeDtypeStruct((t, dff), BF16)
    return pl.pallas_call(
        body, name="ffn_act_fwd", grid=(nblk,),
        in_specs=[blk, pl.BlockSpec((t, HEAD), lambda j: (0, nblk + j)),
                  pl.BlockSpec((kw, HEAD), lambda j: (0, j)), pl.BlockSpec((kw, HEAD), lambda j: (0, nblk + j)),
                  pl.BlockSpec((1, HEAD), lambda j: (0, j)), pl.BlockSpec((1, HEAD), lambda j: (0, nblk + j))],
        out_specs=[blk, blk, blk], out_shape=[shp, shp, shp],
        compiler_params=_params(("parallel",)))(up, up, conv_w, conv_w, bias, bias)


def _ffn_act_bwd(up, cg, cv, dact, conv_w, dff, jobs=()):
    t = up.shape[0]
    nblk = dff // HEAD
    kw = conv_w.shape[0]

    def body(me_ref, cg_ref, cv_ref, d_ref, wm_ref, dup_ref, dw_ref, db_ref):
        is_gate = pl.program_id(0) < nblk
        rows = lax.broadcasted_iota(jnp.int32, (t, HEAD), 0)
        xv = me_ref[...].astype(F32)
        w = wm_ref[...]
        g = cg_ref[...].astype(F32)
        s = _sigmoid(g)
        d = d_ref[...].astype(F32)
        dc = d * jnp.where(is_gate, cv_ref[...].astype(F32) * (s + g * s * (1.0 - s)), g * s)
        db_ref[...] = jnp.sum(dc, axis=0, keepdims=True)
        dx = dc * w[kw - 1:kw, :]
        dw_ref[kw - 1:kw, :] = jnp.sum(dc * xv, axis=0, keepdims=True)
        for sh in range(1, kw):
            dx = dx + _shift_up(dc, sh, rows) * w[kw - 1 - sh:kw - sh, :]
            dw_ref[kw - 1 - sh:kw - sh, :] = jnp.sum(dc * _shift_down(xv, sh, rows), axis=0, keepdims=True)
        dup_ref[...] = dx.astype(BF16)

    me = pl.BlockSpec((t, HEAD), lambda j: (0, j))
    pair = pl.BlockSpec((t, HEAD), lambda j: (0, j % nblk))
    wme = pl.BlockSpec((kw, HEAD), lambda j: (0, j))
    return _call(
        body, "ffn_act_bwd", (2 * nblk,), [me, pair, pair, pair, wme],
        [me, wme, pl.BlockSpec((1, HEAD), lambda j: (0, j))],
        [jax.ShapeDtypeStruct((t, 2 * dff), BF16), jax.ShapeDtypeStruct((kw, 2 * dff), F32),
         jax.ShapeDtypeStruct((1, 2 * dff), F32)],
        (up, cg, cv, dact, conv_w), sem=("parallel",), jobs=jobs)


def _ple_fwd(x, gt, pp):
    t, d = x.shape
    tb, tc = _tile(t, 512, 8), _tile(d, 1024)

    def body(x_ref, g_ref, p_ref, o_ref):
        o_ref[...] = x_ref[...] + _sigmoid(g_ref[...].astype(F32)) * p_ref[...].astype(F32)

    blk = pl.BlockSpec((tb, tc), lambda i, j: (i, j))
    return pl.pallas_call(
        body, name="ple_fwd", grid=(t // tb, d // tc), in_specs=[blk, blk, blk], out_specs=blk,
        out_shape=jax.ShapeDtypeStruct((t, d), F32), compiler_params=_params(("parallel", "parallel")))(x, gt, pp)


def _ple_bwd(dx, gt, pp):
    t, d = dx.shape
    tb, tc = _tile(t, 512, 8), _tile(d, 1024)

    def body(dx_ref, g_ref, p_ref, dg_ref, dp_ref):
        dv = dx_ref[...]
        s = _sigmoid(g_ref[...].astype(F32))
        dg_ref[...] = (dv * p_ref[...].astype(F32) * s * (1.0 - s)).astype(BF16)
        dp_ref[...] = (dv * s).astype(BF16)

    blk = pl.BlockSpec((tb, tc), lambda i, j: (i, j))
    shp = jax.ShapeDtypeStruct((t, d), BF16)
    return pl.pallas_call(
        body, name="ple_bwd", grid=(t // tb, d // tc), in_specs=[blk, blk, blk], out_specs=[blk, blk],
        out_shape=[shp, shp], compiler_params=_params(("parallel", "parallel")))(dx, gt, pp)


def _adam(pieces, w, m, v, name, jobs=()):
    nq = len(pieces)
    npart, rp, c = pieces[0].shape
    per_layer = nq // w.shape[0]
    row_bytes = 2 * c * (nq * npart * pieces[0].dtype.itemsize + 7 * 4)
    tr = _tile(rp, max(16, min(512, ADAM_VMEM_BUDGET // row_bytes)), 16)
    nblk = rp // tr
    c1 = 1.0 - ADAM_B1 ** ADAM_STEP
    c2 = 1.0 - ADAM_B2 ** ADAM_STEP

    def body(*refs):
        p_refs = refs[:nq]
        w_ref, m_ref, v_ref, g_ref, d_ref, mo_ref, vo_ref = refs[nq:]
        for q in range(nq):
            @pl.when(pl.program_id(0) == q)
            def _(p_ref=p_refs[q]):
                g = p_ref[0].astype(F32)
                for i in range(1, npart):
                    g = g + p_ref[i].astype(F32)
                mn = ADAM_B1 * m_ref[...] + (1.0 - ADAM_B1) * g
                vn = ADAM_B2 * v_ref[...] + (1.0 - ADAM_B2) * (g * g)
                g_ref[...] = g
                mo_ref[...] = mn
                vo_ref[...] = vn
                d_ref[...] = -ADAM_LR * ((mn / c1) / (jnp.sqrt(vn / c2) + ADAM_EPS) + ADAM_WD * w_ref[...])

    def piece_spec(q):
        return pl.BlockSpec((npart, tr, c), lambda i, r: (0, jnp.where(i == q, r, jnp.where(i < q, 0, nblk - 1)), 0))

    blk = pl.BlockSpec((None, tr, c), lambda i, r: (i // per_layer, (i % per_layer) * nblk + r, 0))
    shp = jax.ShapeDtypeStruct(w.shape, F32)
    return _call(body, name, (nq, nblk), [piece_spec(q) for q in range(nq)] + [blk] * 3, [blk] * 4, [shp] * 4,
                 (*pieces, w, m, v), sem=("parallel", "parallel"), jobs=jobs)


_BIG = ("w_in", "w_branch_a", "w_branch_b", "w_out", "w_ffn_up", "w_ffn_down", "w_ple_gate", "w_ple_proj")
_COL_SHARDED = ("w_in", "w_branch_b", "w_ffn_up", "w_ple_proj")
_CONVS = ("conv_qkv", "conv_ffn")
_GATHER_ON_PROJ = ("w_ffn_up",)
_GATHER_ON_DELTA = ("w_ffn_down",)
_GATHER_AHEAD = ("w_in", "conv_qkv")
_GATHER_ON_UP = ("w_ple_gate", "w_ple_proj")
_GATHER_AHEAD_2 = ("w_branch_a", "w_branch_b", "w_out", "conv_ffn")
_SCATTER_ON_DACT = ("w_ple_gate", "w_ple_proj")
_SCATTER_ON_DELTA = ("w_ffn_up",)
_SCATTER_ON_DW_MAIN = ("w_out", "w_branch_a", "w_branch_b")
_SCATTER_ON_DH1 = ("w_ffn_down",)
_SMALL = ("norm_mix", "conv_qkv", "a_log", "dt_bias", "head_norm", "sgu_norm", "w_spatial", "b_spatial", "norm_ffn",
          "conv_ffn", "b_conv_ffn", "norm_ple", "norm_final")
_WEIGHTS = ("norm_mix", "w_in", "conv_qkv", "a_log", "dt_bias", "head_norm", "sgu_norm", "w_spatial", "b_spatial",
            "w_branch_a", "w_branch_b", "w_out", "norm_ffn", "w_ffn_up", "conv_ffn", "b_conv_ffn", "w_ffn_down",
            "norm_ple", "w_ple_gate", "w_ple_proj", "norm_final")


def _full_cols(g):
    return jnp.transpose(g, (1, 0, 2)).reshape(g.shape[1], N_DEV * g.shape[2])


def _full_rows(g):
    return g.reshape(N_DEV * g.shape[1], g.shape[2])


def _split_cols(dw):
    k, n = dw.shape
    return jnp.transpose(dw.reshape(k, N_DEV, n // N_DEV), (1, 0, 2))


def _split_rows(dw):
    k, n = dw.shape
    return dw.reshape(N_DEV, k // N_DEV, n)


def _pad_lanes(v, width=LANES, offset=0):
    return jnp.pad(v, ((0, 0), (offset, width - offset - v.shape[1])))


def kernel(x, p, norm_mix, w_in, conv_qkv, a_log, dt_bias, head_norm, sgu_norm, w_spatial, b_spatial, w_branch_a, w_branch_b, w_out, norm_ffn, w_ffn_up, conv_ffn, b_conv_ffn, w_ffn_down, norm_ple, w_ple_gate, w_ple_proj, norm_final, loss_target, m_norm_mix, m_w_in, m_conv_qkv, m_a_log, m_dt_bias, m_head_norm, m_sgu_norm, m_w_spatial, m_b_spatial, m_w_branch_a, m_w_branch_b, m_w_out, m_norm_ffn, m_w_ffn_up, m_conv_ffn, m_b_conv_ffn, m_w_ffn_down, m_norm_ple, m_w_ple_gate, m_w_ple_proj, m_norm_final, v_norm_mix, v_w_in, v_conv_qkv, v_a_log, v_dt_bias, v_head_norm, v_sgu_norm, v_w_spatial, v_b_spatial, v_w_branch_a, v_w_branch_b, v_w_out, v_norm_ffn, v_w_ffn_up, v_conv_ffn, v_b_conv_ffn, v_w_ffn_down, v_norm_ple, v_w_ple_gate, v_w_ple_proj, v_norm_final):
    env = dict(locals())
    wts = {n: env[n] for n in _WEIGHTS}
    mom_m = {n: env["m_" + n] for n in _WEIGHTS}
    mom_v = {n: env["v_" + n] for n in _WEIGHTS}

    xin = x[0]
    tgt = loss_target[0]
    t, d = xin.shape
    depth = w_in.shape[0]
    hv = a_log.shape[1]
    vw = hv * HEAD
    wb = sgu_norm.shape[1]
    ng = w_spatial.shape[1]
    n_in = w_in.shape[2] * N_DEV
    qk = (n_in - 2 * vw - 2 * hv - 2 * wb - 2 * d) // 2
    hqk = qk // HEAD
    dff = w_ffn_down.shape[1] * N_DEV
    cw = 2 * qk + vw
    o_z, o_ba = 2 * qk + vw, 2 * qk + 2 * vw
    o_ub = o_ba
    o_ga = o_ub + 2 * wb
    ns = w_in.shape[2]
    in_segments = ((0, 0, o_ba), (1, o_ba, o_ba + 2 * hv), (2, o_ba + 2 * hv, n_in))
    me = 4 * lax.axis_index("x") + 2 * lax.axis_index("y") + lax.axis_index("c")

    full = [dict() for _ in range(depth)]
    staged = {}

    def as_cols(n):
        return "_cols" if (n in _COL_SHARDED or n in _CONVS) and wts[n].shape[-1] % LANES == 0 else ""

    def chips(i, names):
        names = names if i < depth else ()
        return [(i, n) for n in names], [(wts[n][i].astype(BF16) if n in _BIG else wts[n][i], "chips" + as_cols(n))
                                         for n in names]

    def relay(keys):
        return list(keys), [(staged.pop(key), "relay" + as_cols(key[1])) for key in keys]

    def settle(chip_keys, relay_keys, results):
        for key, g in zip(chip_keys, results):
            staged[key] = g
        for (i, n), g in zip(relay_keys, results[len(chip_keys):]):
            keep_blocks = as_cols(n) or n == "w_in"
            full[i][n] = g if keep_blocks else _full_cols(g) if n in _COL_SHARDED or n in _CONVS else _full_rows(g)

    ck, cj = chips(0, _GATHER_AHEAD + _GATHER_AHEAD_2)
    settle(ck, [], _exchange(cj, "gather_first"))
    rk, rj = relay(ck)
    settle([], rk, _exchange(rj, "relay_first"))

    saved = []
    xc = xin
    for i in range(depth):
        fw = full[i]
        main, gates = [], []
        for dev, (seg, lo, hi) in itertools.product(range(N_DEV), in_segments):
            a, b = max(lo, dev * ns), min(hi, (dev + 1) * ns)
            if a < b:
                (gates if seg == 1 else main).append(fw["w_in"][dev][:, a - dev * ns:b - dev * ns])
        fw["w_main"] = jnp.concatenate(main, axis=1)
        fw["w_ba"] = _pad_lanes(jnp.concatenate(gates, axis=1))
        s = {"x0": xc}
        s["h1"] = _rms_fwd(xc, norm_mix[i], "rms_fwd")
        ck, cj = chips(i, _GATHER_ON_PROJ)
        s["proj"], got = _matmul(s["h1"], fw["w_main"], "nn", BF16, "mm_proj", jobs=cj)
        settle(ck, [], got)
        s["ba"] = _matmul(s["h1"], fw["w_ba"], "nn", F32, "mm_ba")
        s["qkvn"] = _qkv_fwd(s["proj"], fw["conv_qkv"], hqk, hqk)
        s["alog"] = _pad_lanes(a_log[i][None, :], offset=hv)
        s["dtb"] = _pad_lanes(dt_bias[i][None, :], offset=hv)
        bg = _gate_fwd(s["ba"], s["alog"], s["dtb"], hv)
        beta_t = bg[:, :hv].T
        gam_t = bg[:, hv:2 * hv].T
        s["beta_b"] = jnp.broadcast_to(beta_t[:, :, None], (hv, t, LANES))
        s["gam_b"] = jnp.broadcast_to(gam_t[:, :, None], (hv, t, LANES))
        s["gam_r"] = gam_t.reshape(hv, t // CA, 1, CA)
        s["gam_l"] = jnp.broadcast_to(s["gam_r"][:, :, :, CA - 1:], (hv, t // CA, 1, LANES))
        ck1, cj1 = chips(i, _GATHER_ON_DELTA)
        ck2, cj2 = chips(i + 1, _GATHER_AHEAD)
        rk, rj = relay([(i, n) for n in _GATHER_ON_PROJ])
        (s["o"], s["s_all"], s["tm_all"]), got = _delta_fwd(
            s["qkvn"], s["beta_b"], s["gam_b"], s["gam_r"], s["gam_l"], hqk, hv, jobs=cj1 + cj2 + rj)
        settle(ck1 + ck2, rk, got)
        s["outa"] = _apost_fwd(s["o"], s["proj"], head_norm[i], o_z, hv)
        s["b_t"] = b_spatial[i].T
        s["outb"] = _sgu_fwd(s["proj"], sgu_norm[i], w_spatial[i], s["b_t"], o_ub, wb)
        s["ya"] = _matmul(s["outa"], fw["w_branch_a"], "nn", BF16, "mm_ya")
        s["yb"] = _matmul(s["outb"], fw["w_branch_b"], "nn", BF16, "mm_yb")
        s["mg"] = _merge_fwd(s["ya"], s["yb"], s["proj"], o_ga)
        s["x1"] = _matmul(s["mg"], fw["w_out"], "nn", F32, "mm_out", res=xc)
        s["h2"] = _rms_fwd(s["x1"], norm_ffn[i], "rms_fwd")
        nxt = i + 1 < depth
        ck1, cj1 = chips(i, _GATHER_ON_UP)
        ck2, cj2 = chips(i + 1, _GATHER_AHEAD_2)
        rk, rj = relay([(i, n) for n in _GATHER_ON_DELTA] + ([(i + 1, n) for n in _GATHER_AHEAD] if nxt else []))
        s["up"], got = _matmul(s["h2"], fw["w_ffn_up"], "nn", BF16, "mm_up", jobs=cj1 + cj2 + rj)
        settle(ck1 + ck2, rk, got)
        s["act"], s["cg"], s["cv"] = _ffn_act_fwd(s["up"], fw["conv_ffn"], b_conv_ffn[i][None, :], dff)
        rk, rj = relay([(i, n) for n in _GATHER_ON_UP] + ([(i + 1, n) for n in _GATHER_AHEAD_2] if nxt else []))
        s["x2"], got = _matmul(s["act"], fw["w_ffn_down"], "nn", F32, "mm_down", res=s["x1"], jobs=rj)
        settle([], rk, got)
        s["h3"] = _rms_fwd(s["x2"], norm_ple[i], "rms_fwd")
        s["gt"] = _matmul(s["h3"], fw["w_ple_gate"], "nn", BF16, "mm_gt")
        s["pp"] = _matmul(p[i, 0], fw["w_ple_proj"], "nn", BF16, "mm_pp")
        xc = _ple_fwd(s["x2"], s["gt"], s["pp"])
        saved.append(s)

    dx, g_norm_final, loss_part = _loss_head(xc, tgt, norm_final)

    small = {n: [None] * depth for n in _SMALL if n != "norm_final"}
    recv = {n: [None] * depth for n in _BIG}
    recv["w_in"] = [None] * (2 * depth)

    def scatter_jobs(gw, names):
        return [(gw[n], "scatter_cols") if as_cols(n) else
                (_split_cols(gw[n]) if n in _COL_SHARDED else _split_rows(gw[n]), "scatter") for n in names]

    def keep(i, names, results):
        for n, r in zip(names, results):
            recv[n][i] = r

    def carry(jobs, *args):
        return _matmul(*args, jobs=jobs) if jobs else (_matmul(*args), [])

    later = []
    for i in reversed(range(depth)):
        fw, s = full[i], saved[i]
        dgt, dpp = _ple_bwd(dx, s["gt"], s["pp"])
        gw = {"w_ple_gate": _matmul(s["h3"], dgt, "tn", BF16, "mm_dw_gt"),
              "w_ple_proj": _matmul(p[i, 0], dpp, "tn", BF16, "mm_dw_pp")}
        dh3 = _matmul(dgt, fw["w_ple_gate"], "nt", F32, "mm_dh3")
        dx, dxb, small["norm_ple"][i] = _rms_bwd(s["x2"], dh3, norm_ple[i], dx, "rms_bwd")

        dact, got = _matmul(dxb, fw["w_ffn_down"], "nt", BF16, "mm_dact", jobs=scatter_jobs(gw, _SCATTER_ON_DACT))
        keep(i, _SCATTER_ON_DACT, got)
        gw["w_ffn_down"] = _matmul(s["act"], dxb, "tn", BF16, "mm_dw_down")
        (dup, small["conv_ffn"][i], small["b_conv_ffn"][i]), _ = _ffn_act_bwd(
            s["up"], s["cg"], s["cv"], dact, fw["conv_ffn"], dff)
        gw["w_ffn_up"], got = carry(later[:1], s["h2"], dup, "tn", BF16, "mm_dw_up")
        keep(2 * i + 2, ("w_in",), got)
        dh2, got = carry(later[1:], dup, fw["w_ffn_up"], "nt", F32, "mm_dh2")
        keep(2 * i + 3, ("w_in",), got)
        dx, dxb, small["norm_ffn"][i] = _rms_bwd(s["x1"], dh2, norm_ffn[i], dx, "rms_bwd")

        dmg = _matmul(dxb, fw["w_out"], "nt", BF16, "mm_dmg")
        gw["w_out"] = _matmul(s["mg"], dxb, "tn", BF16, "mm_dw_out")
        dya, dyb, dga, dgb = _merge_bwd(dmg, s["ya"], s["yb"], s["proj"], o_ga)
        gw["w_branch_a"] = _matmul(s["outa"], dya, "tn", BF16, "mm_dw_a")
        gw["w_branch_b"] = _matmul(s["outb"], dyb, "tn", BF16, "mm_dw_b")
        douta = _matmul(dya, fw["w_branch_a"], "nt", BF16, "mm_douta")
        doutb = _matmul(dyb, fw["w_branch_b"], "nt", BF16, "mm_doutb")
        dub, dvb, small["w_spatial"][i], db_s, dsg = _sgu_bwd(s["proj"], sgu_norm[i], w_spatial[i], s["b_t"], doutb, o_ub, wb)
        small["b_spatial"][i] = db_s[:, :, 0]
        small["sgu_norm"][i] = dsg
        do, dz, small["head_norm"][i] = _apost_bwd(s["o"], s["proj"], head_norm[i], douta, o_z, hv)
        (dq, dk, dv, db_b, dg_b), got = _delta_bwd(
            s["qkvn"], s["beta_b"], s["gam_b"], s["gam_r"], s["gam_l"], s["s_all"], s["tm_all"], do, hqk, hv,
            jobs=scatter_jobs(gw, _SCATTER_ON_DELTA))
        keep(i, _SCATTER_ON_DELTA, got)
        dbg = _pad_lanes(jnp.concatenate([db_b[:, :, 0].T, dg_b[:, :, 0].T], axis=1))
        dba, dal, ddt = _gate_bwd(s["ba"], dbg, s["alog"], s["dtb"], hv)
        small["a_log"][i] = dal[:, hv:2 * hv]
        small["dt_bias"][i] = ddt[:, hv:2 * hv]
        dqkv_pre, small["conv_qkv"][i] = _qkv_bwd(s["proj"], jnp.concatenate([dq, dk, dv], axis=1), fw["conv_qkv"], hqk, hqk)
        dproj = jnp.concatenate([dqkv_pre, dz, dub, dvb, dga, dgb], axis=1)
        dw_main, got = _matmul(s["h1"], dproj, "tn", BF16, "mm_dw_main", jobs=scatter_jobs(gw, _SCATTER_ON_DW_MAIN))
        keep(i, _SCATTER_ON_DW_MAIN, got)
        dw_ba = _matmul(s["h1"], dba, "tn", BF16, "mm_dw_ba")
        dh1, got = _matmul(dproj, fw["w_main"], "nt", F32, "mm_dh1", jobs=scatter_jobs(gw, _SCATTER_ON_DH1))
        keep(i, _SCATTER_ON_DH1, got)
        dh1 = _matmul(dba, fw["w_ba"], "nt", F32, "mm_dh1_ba", res=dh1)
        dx, _, small["norm_mix"][i] = _rms_bwd(s["x0"], dh1, norm_mix[i], dx, "rms_bwd")

        sources = (dw_main, dw_ba, dw_main[:, o_ba:])
        later = []
        for rows in (slice(0, d // 2), slice(d // 2, d)):
            shards = []
            for dev in range(N_DEV):
                cuts = [(seg, max(lo, dev * ns), min(hi, (dev + 1) * ns)) for seg, lo, hi in in_segments]
                shards.append(jnp.concatenate([sources[seg][rows, a - lo:b - lo] for (seg, a, b), (_, lo, _)
                                               in zip(cuts, in_segments) if a < b], axis=1))
            later.append((jnp.stack(shards), True))


    rep_names = tuple(n for n in _SMALL if n not in _CONVS)
    stacked = {n: jnp.concatenate([jnp.reshape(a, (-1,)) for a in small[n]]) for n in small}
    stacked["norm_final"] = g_norm_final.reshape(-1)

    def padded(parts, mult, axis=0):
        flat = jnp.concatenate(parts, axis=axis)
        pad = -flat.shape[axis] % mult
        return jnp.pad(flat, [(0, 0)] * axis + [(0, pad)])

    rep_flat = padded([stacked[n] for n in rep_names] + [loss_part[0, :1]], 16 * LANES)
    conv_flat = [padded([stacked[n]], 8 * LANES) for n in _CONVS]
    packed = jnp.concatenate([rep_flat] + conv_flat).reshape(-1, LANES)

    outs_g, outs_d, outs_m, outs_v = {}, {}, {}, {}

    adam_jobs = {"w_ffn_up": later[:1], "w_ffn_down": later[1:], "w_in": [(packed, False)]}
    for n in sorted(_BIG, key=lambda name: name == "w_in"):
        (outs_g[n], outs_d[n], outs_m[n], outs_v[n]), got = _adam(
            recv[n], wts[n], mom_m[n], mom_v[n], "adam_" + n, jobs=adam_jobs.get(n, []))
        if n == "w_ffn_up":
            recv["w_in"][0] = got[0]
        elif n == "w_ffn_down":
            recv["w_in"][1] = got[0]
        elif n == "w_in":
            small_all = got[0].reshape(N_DEV, -1)

    n_rep = rep_flat.shape[0]
    pk = lambda src: padded([src[n].reshape(-1) for n in rep_names] + [jnp.zeros((1,), F32)], 16 * LANES).reshape(1, -1, LANES)
    res, _ = _adam([small_all[:, :n_rep].reshape(N_DEV, -1, LANES)], pk(wts), pk(mom_m), pk(mom_v), "adam_small")
    res = [r.reshape(-1) for r in res]
    off = 0
    for n in rep_names:
        shp = wts[n].shape
        size = math.prod(shp)
        outs_g[n], outs_d[n], outs_m[n], outs_v[n] = [r[off:off + size].reshape(shp) for r in res]
        off += size
    loss = res[0][off]

    off = n_rep
    for n, cf in zip(_CONVS, conv_flat):
        _, kw, cl = wts[n].shape
        part = small_all[:, off:off + depth * kw * cl * N_DEV].reshape(N_DEV, depth * kw, N_DEV, cl)
        off += cf.shape[0]
        part = lax.dynamic_index_in_dim(part, me, axis=2, keepdims=False)
        two_d = lambda a: a.reshape(1, depth * kw, cl)
        res, _ = _adam([part], two_d(wts[n]), two_d(mom_m[n]), two_d(mom_v[n]), "adam_" + n)
        outs_g[n], outs_d[n], outs_m[n], outs_v[n] = [r.reshape(wts[n].shape) for r in res]

    return (loss, dx[None], *[outs_g[n] for n in _WEIGHTS], *[outs_d[n] for n in _WEIGHTS],
            *[outs_m[n] for n in _WEIGHTS], *[outs_v[n] for n in _WEIGHTS])
```
